```python
import jax, jax.numpy as jnp
from jax import lax
import numpy as np

D_MODEL = 2048
BATCH = 4
SEQ = 2048
DEPTH = 1

N_HEADS_A = 16
QK_NOPE = 64
QK_ROPE = 32
V_HEAD = 64
Q_LORA = 512
KV_LORA = 256
IDX_HEADS = 16
IDX_DIM = 64
IDX_ROPE = 32
TOPK_MAX = 256
Q_BLOCK = 128
ATT_WIDTH = N_HEADS_A * V_HEAD
CHUNK = 128
SGU_GROUPS = 8
SGU_GROUP_DIM = 128
SGU_WIDTH = SGU_GROUPS * SGU_GROUP_DIM
D_FF = -(-8 * D_MODEL // (3 * 256)) * 256
PLE_DIM = 256
ROPE_THETA = 10000.0
EPS = 1e-6

IN_SIZES = (Q_LORA, KV_LORA, QK_ROPE, IDX_DIM, IDX_HEADS, 2 * SGU_WIDTH, 2 * D_MODEL)
IN_COLS = sum(IN_SIZES)
IN_SPLITS = [int(c) for c in np.cumsum(IN_SIZES)[:-1]]

kernel_name = "hybrid_dsa_mla_gmlp_gated_block"


def rmsnorm(x, g):
    xf = x.astype(jnp.float32)
    y = xf * lax.rsqrt(jnp.mean(xf * xf, axis=-1, keepdims=True) + EPS)
    return (y * g.astype(jnp.float32)).astype(x.dtype)


def rope_tables(seq, dim):
    inv = ROPE_THETA ** (-jnp.arange(0, dim, 2, dtype=jnp.float32) / dim)
    ang = jnp.arange(seq, dtype=jnp.float32)[:, None] * inv[None, :]
    return jnp.cos(ang), jnp.sin(ang)


def apply_rope(x, cos, sin):
    half = x.shape[-1] // 2
    x1, x2 = x[..., :half], x[..., half:]
    c = cos.astype(x.dtype)
    s = sin.astype(x.dtype)
    return jnp.concatenate([x1 * c - x2 * s, x2 * c + x1 * s], axis=-1)


def rope_partial(x, cos, sin, r):
    return jnp.concatenate([apply_rope(x[..., :r], cos, sin), x[..., r:]], axis=-1)


def dsa_mla_attention(c_q, c_kv, k_rope_raw, k_idx_raw, w_idx_raw, w_uq, w_uk, w_iq):
    B, S, _ = c_q.shape
    cos_r, sin_r = rope_tables(S, QK_ROPE)
    cos_i, sin_i = rope_tables(S, IDX_ROPE)
    q = (c_q @ w_uq).reshape(B, S, N_HEADS_A, QK_NOPE + QK_ROPE)
    q_nope = q[..., :QK_NOPE]
    q_rope = apply_rope(q[..., QK_NOPE:], cos_r[:, None, :], sin_r[:, None, :])
    k_rope = apply_rope(k_rope_raw, cos_r, sin_r)
    q_lat = jnp.einsum('bshn,chn->bshc', q_nope, w_uk)
    q_idx = rope_partial((c_q @ w_iq).reshape(B, S, IDX_HEADS, IDX_DIM),
                         cos_i[:, None, :], sin_i[:, None, :], IDX_ROPE)
    k_idx = rope_partial(k_idx_raw, cos_i, sin_i, IDX_ROPE)
    w_idx = w_idx_raw * (IDX_HEADS ** -0.5 * IDX_DIM ** -0.5)
    topk = min(TOPK_MAX, S // 4)
    scale = (QK_NOPE + QK_ROPE) ** -0.5
    kpos = jnp.arange(S)

    def block(j):
        q0 = j * Q_BLOCK
        tpos = q0 + jnp.arange(Q_BLOCK)
        qi = lax.dynamic_slice_in_dim(q_idx, q0, Q_BLOCK, axis=1)
        wi = lax.dynamic_slice_in_dim(w_idx, q0, Q_BLOCK, axis=1)
        rel = jax.nn.relu(jnp.einsum('bqhd,bkd->bqhk', qi, k_idx))
        isc = jnp.einsum('bqhk,bqh->bqk', rel, wi).astype(jnp.float32)
        causal = kpos[None, :] <= tpos[:, None]
        isc = jnp.where(causal[None], isc, -jnp.inf)
        _, idx = lax.top_k(isc, topk)
        valid = idx <= tpos[None, :, None]
        c_sel = jax.vmap(lambda a, i: a[i])(c_kv, idx)
        r_sel = jax.vmap(lambda a, i: a[i])(k_rope, idx)
        ql = lax.dynamic_slice_in_dim(q_lat, q0, Q_BLOCK, axis=1)
        qr = lax.dynamic_slice_in_dim(q_rope, q0, Q_BLOCK, axis=1)
        s = (jnp.einsum('bqhc,bqkc->bqhk', ql, c_sel)
             + jnp.einsum('bqhr,bqkr->bqhk', qr, r_sel)).astype(jnp.float32) * scale
        s = jnp.where(valid[:, :, None, :], s, -jnp.inf)
        pr = jax.nn.softmax(s, axis=-1).astype(c_sel.dtype)
        return jnp.einsum('bqhk,bqkc->bqhc', pr, c_sel)

    o = lax.map(block, jnp.arange(S // Q_BLOCK))
    return jnp.moveaxis(o, 0, 1).reshape(B, S, N_HEADS_A, KV_LORA)


def chunked_sgu(uv, g_sgu, w_s, b_s):
    B, S, _ = uv.shape
    z = jax.nn.gelu(uv, approximate=False)
    u, v = z[..., :SGU_WIDTH], z[..., SGU_WIDTH:]
    v = rmsnorm(v, g_sgu).reshape(B, S // CHUNK, CHUNK, SGU_GROUPS, SGU_GROUP_DIM)
    mask = jnp.tril(jnp.ones((CHUNK, CHUNK), dtype=bool))
    w = jnp.where(mask[None], w_s, jnp.zeros_like(w_s))
    mixed = jnp.einsum('gts,bnsgc->bntgc', w, v) + jnp.transpose(b_s)[None, None, :, :, None]
    return u * mixed.reshape(B, S, SGU_WIDTH)


def setup_inputs(seed: int = 0) -> dict:
    key = jax.random.key(seed)
    ks = iter(jax.random.split(key, 32))
    f32 = jnp.float32

    def nrm(shape, fan_in):
        return jax.random.normal(next(ks), shape, f32) * (fan_in ** -0.5)

    def gain(shape):
        return 1.0 + 0.01 * jax.random.normal(next(ks), shape, f32)

    L = DEPTH
    return {
        "x": jax.random.normal(next(ks), (BATCH, SEQ, D_MODEL), f32),
        "p": jax.random.normal(next(ks), (DEPTH, BATCH, SEQ, PLE_DIM), f32),
        "g_mix": gain((L, D_MODEL)),
        "w_in": nrm((L, D_MODEL, IN_COLS), D_MODEL),
        "g_cq": gain((L, Q_LORA)),
        "g_ckv": gain((L, KV_LORA)),
        "w_uq": nrm((L, Q_LORA, N_HEADS_A * (QK_NOPE + QK_ROPE)), Q_LORA),
        "w_uk": nrm((L, KV_LORA, N_HEADS_A, QK_NOPE), KV_LORA),
        "w_uv": nrm((L, KV_LORA, N_HEADS_A, V_HEAD), KV_LORA),
        "w_iq": nrm((L, Q_LORA, IDX_HEADS * IDX_DIM), Q_LORA),
        "w_a_proj": nrm((L, ATT_WIDTH, D_MODEL), ATT_WIDTH),
        "g_sgu": gain((L, SGU_WIDTH)),
        "w_spatial": nrm((L, SGU_GROUPS, CHUNK, CHUNK), CHUNK),
        "b_spatial": gain((L, SGU_GROUPS, CHUNK)),
        "w_b_proj": nrm((L, SGU_WIDTH, D_MODEL), SGU_WIDTH),
        "w_o": nrm((L, D_MODEL, D_MODEL), D_MODEL),
        "g_ffn": gain((L, D_MODEL)),
        "w_gu": nrm((L, D_MODEL, 2 * D_FF), D_MODEL),
        "w_down": nrm((L, D_FF, D_MODEL), D_FF),
        "g_ple": gain((L, D_MODEL)),
        "w_ple_gate": nrm((L, D_MODEL, D_MODEL), D_MODEL),
        "w_ple_proj": nrm((L, PLE_DIM, D_MODEL), PLE_DIM),
        "g_final": gain((D_MODEL,)),
    }


def reference(x, p, g_mix, w_in, g_cq, g_ckv, w_uq, w_uk, w_uv, w_iq, w_a_proj,
              g_sgu, w_spatial, b_spatial, w_b_proj, w_o, g_ffn, w_gu, w_down,
              g_ple, w_ple_gate, w_ple_proj, g_final):
    B, S, _ = x.shape
    h = x
    for i in range(DEPTH):
        n = rmsnorm(h, g_mix[i])
        proj = n @ w_in[i]
        cq, ckv, kr, kidx, widx, uv, gates = jnp.split(proj, IN_SPLITS, axis=-1)
        c_q = rmsnorm(cq, g_cq[i])
        c_kv = rmsnorm(ckv, g_ckv[i])
        o_lat = dsa_mla_attention(c_q, c_kv, kr, kidx, widx, w_uq[i], w_uk[i], w_iq[i])
        o_a = jnp.einsum('bshc,chv->bshv', o_lat, w_uv[i]).reshape(B, S, ATT_WIDTH)
        y_a = o_a @ w_a_proj[i]
        y_b = chunked_sgu(uv, g_sgu[i], w_spatial[i], b_spatial[i]) @ w_b_proj[i]
        gg = jax.nn.sigmoid(gates)
        merged = gg[..., :D_MODEL] * y_a + gg[..., D_MODEL:] * y_b
        h = h + merged @ w_o[i]
        gu = rmsnorm(h, g_ffn[i]) @ w_gu[i]
        h = h + (jax.nn.silu(gu[..., :D_FF]) * gu[..., D_FF:]) @ w_down[i]
        ple_gate = jax.nn.sigmoid(rmsnorm(h, g_ple[i]) @ w_ple_gate[i])
        h = h + ple_gate * (p[i] @ w_ple_proj[i])
    return rmsnorm(h, g_final)
```

```python
import functools
import math

import jax
import jax.numpy as jnp
from jax import lax
from jax.experimental import pallas as pl
from jax.experimental.pallas import tpu as pltpu

F32 = jnp.float32
BF16 = jnp.bfloat16

N_HEADS = 16
QK_NOPE = 64
QK_ROPE = 32
V_HEAD = 64
Q_LORA = 512
KV_LORA = 256
IDX_HEADS = 16
IDX_DIM = 64
IDX_ROPE = 32
TOPK_MAX = 256
SGU_CHUNK = 128
SGU_GROUPS = 8
SGU_GROUP_DIM = 128
SGU_WIDTH = SGU_GROUPS * SGU_GROUP_DIM
ROPE_THETA = 10000.0
EPS = 1e-6

LANES = 128
SMALL_COLS = 1024
COL_CKV = Q_LORA
COL_KIDX = Q_LORA + KV_LORA
COL_MISC = COL_KIDX + LANES
MISC_WIDX = 64
VMEM_LIMIT_BYTES = 56 * 1024 * 1024
NEG_BIAS = -1e30
F32_MAX = 3.4028234663852886e38


def _cparams(semantics):
    return pltpu.CompilerParams(dimension_semantics=semantics, vmem_limit_bytes=VMEM_LIMIT_BYTES)


def _rms(x, g):
    return x * lax.rsqrt(jnp.mean(x * x, axis=-1, keepdims=True) + EPS) * g


def _dot(a, b):
    return jnp.dot(a, b, preferred_element_type=F32)


def _dot_nt(a, b):
    return lax.dot_general(a, b, (((1,), (1,)), ((), ())), preferred_element_type=F32)


def _gelu_exact(x):
    return 0.5 * x * (1.0 + lax.erf(x * (1.0 / math.sqrt(2.0))))


def _sigmoid(x):
    return 1.0 / (1.0 + jnp.exp(-x))


def _in_proj_kernel(x_ref, g_ref, w_ref, small_ref, z_ref, gate_ref, n_scr, *, n_small, n_z):
    j = pl.program_id(1)

    @pl.when(j == 0)
    def _():
        n_scr[...] = _rms(x_ref[...], g_ref[...]).astype(BF16)

    acc = _dot(n_scr[...], w_ref[...])

    @pl.when(j < n_small)
    def _():
        small_ref[...] = acc

    @pl.when((j >= n_small) & (j < n_small + n_z))
    def _():
        z_ref[...] = _gelu_exact(acc).astype(BF16)

    @pl.when(j >= n_small + n_z)
    def _():
        gate_ref[...] = _sigmoid(acc).astype(BF16)


def _in_proj(x2, g_mix, w1, *, tm, tn):
    T, D = x2.shape
    n_small = SMALL_COLS // tn
    n_z = (2 * SGU_WIDTH) // tn
    n_gate = (2 * D) // tn
    assert w1.shape[1] == (n_small + n_z + n_gate) * tn
    grid = (T // tm, n_small + n_z + n_gate)
    kern = functools.partial(_in_proj_kernel, n_small=n_small, n_z=n_z)
    return pl.pallas_call(
        kern,
        grid=grid,
        in_specs=[
            pl.BlockSpec((tm, D), lambda i, j: (i, 0)),
            pl.BlockSpec((1, D), lambda i, j: (0, 0)),
            pl.BlockSpec((D, tn), lambda i, j: (0, j)),
        ],
        out_specs=[
            pl.BlockSpec((tm, tn), lambda i, j: (i, jnp.minimum(j, n_small - 1))),
            pl.BlockSpec((tm, tn), lambda i, j: (i, jnp.clip(j - n_small, 0, n_z - 1))),
            pl.BlockSpec((tm, tn), lambda i, j: (i, jnp.clip(j - n_small - n_z, 0, n_gate - 1))),
        ],
        out_shape=[
            jax.ShapeDtypeStruct((T, SMALL_COLS), F32),
            jax.ShapeDtypeStruct((T, 2 * SGU_WIDTH), BF16),
            jax.ShapeDtypeStruct((T, 2 * D), BF16),
        ],
        scratch_shapes=[pltpu.VMEM((tm, D), BF16)],
        compiler_params=_cparams(("parallel", "arbitrary")),
        name="in_proj",
    )(x2, g_mix, w1)


def _rope_block(x, cos, sin_a, sin_b):
    return x * cos + pltpu.roll(x, LANES - 16, 1) * sin_a + pltpu.roll(x, 16, 1) * sin_b


def _a_proj_kernel(small_ref, gcq_ref, gckv_ref, wq_ref, wiq_ref, wk_ref, wv_ref,
                   tq_ref, ti_ref, tk_ref,
                   q_ref, k_ref, v_ref, qi_ref, kilo_ref, kihi_ref, wt_ref):
    c_q = _rms(small_ref[:, 0:Q_LORA], gcq_ref[...]).astype(BF16)
    c_kv = _rms(small_ref[:, COL_CKV:COL_CKV + KV_LORA], gckv_ref[...]).astype(BF16)

    q = _dot(c_q, wq_ref[...])
    cq, sqa, sqb = tq_ref[0], tq_ref[1], tq_ref[2]
    for h in range(N_HEADS):
        blk = q[:, h * LANES:(h + 1) * LANES]
        q_ref[0, h] = _rope_block(blk, cq, sqa, sqb).astype(BF16)

    qi = _dot(c_q, wiq_ref[...])
    ci, sia, sib = ti_ref[0], ti_ref[1], ti_ref[2]
    for hp in range(IDX_HEADS // 2):
        blk = qi[:, hp * LANES:(hp + 1) * LANES]
        qi_ref[:, hp * LANES:(hp + 1) * LANES] = _rope_block(blk, ci, sia, sib).astype(BF16)

    kib = small_ref[:, COL_KIDX:COL_KIDX + LANES]
    ki_lo = _rope_block(kib, ci, sia, sib)
    kilo_ref[...] = ki_lo.astype(BF16)
    kihi_ref[...] = pltpu.roll(ki_lo, IDX_DIM, 1).astype(BF16)

    misc = small_ref[:, COL_MISC:COL_MISC + LANES]
    ck, ska, skb = tk_ref[0], tk_ref[1], tk_ref[2]
    k_rope = pltpu.roll(_rope_block(misc, ck, ska, skb), QK_NOPE, 1)
    k_nope = _dot(c_kv, wk_ref[...])
    for h in range(N_HEADS):
        k_ref[0, h] = (k_nope[:, h * LANES:(h + 1) * LANES] + k_rope).astype(BF16)

    v = _dot(c_kv, wv_ref[...])
    for p in range(N_HEADS // 2):
        v_ref[0, p] = v[:, p * LANES:(p + 1) * LANES].astype(BF16)

    w_scale = IDX_HEADS ** -0.5 * IDX_DIM ** -0.5
    wt_ref[...] = misc.T[MISC_WIDX:MISC_WIDX + IDX_HEADS, :] * w_scale


def _a_proj(small, g_cq, g_ckv, wq, wiq, wk, wv, tq, ti, tk, *, B, S, tm):
    T = B * S
    nt = S // tm
    const2 = lambda i: (0, 0)
    tab = pl.BlockSpec((3, tm, LANES), lambda i: (0, i % nt, 0))
    head_spec = lambda nh: pl.BlockSpec((1, nh, tm, LANES), lambda i: (i // nt, 0, i % nt, 0))
    return pl.pallas_call(
        _a_proj_kernel,
        grid=(T // tm,),
        in_specs=[
            pl.BlockSpec((tm, SMALL_COLS), lambda i: (i, 0)),
            pl.BlockSpec((1, Q_LORA), const2),
            pl.BlockSpec((1, KV_LORA), const2),
            pl.BlockSpec(wq.shape, const2),
            pl.BlockSpec(wiq.shape, const2),
            pl.BlockSpec(wk.shape, const2),
            pl.BlockSpec(wv.shape, const2),
            tab, tab, tab,
        ],
        out_specs=[
            head_spec(N_HEADS),
            head_spec(N_HEADS),
            head_spec(N_HEADS // 2),
            pl.BlockSpec((tm, IDX_HEADS * IDX_DIM), lambda i: (i, 0)),
            pl.BlockSpec((tm, LANES), lambda i: (i, 0)),
            pl.BlockSpec((tm, LANES), lambda i: (i, 0)),
            pl.BlockSpec((IDX_HEADS, tm), lambda i: (0, i)),
        ],
        out_shape=[
            jax.ShapeDtypeStruct((B, N_HEADS, S, LANES), BF16),
            jax.ShapeDtypeStruct((B, N_HEADS, S, LANES), BF16),
            jax.ShapeDtypeStruct((B, N_HEADS // 2, S, LANES), BF16),
            jax.ShapeDtypeStruct((T, IDX_HEADS * IDX_DIM), BF16),
            jax.ShapeDtypeStruct((T, LANES), BF16),
            jax.ShapeDtypeStruct((T, LANES), BF16),
            jax.ShapeDtypeStruct((IDX_HEADS, T), F32),
        ],
        compiler_params=_cparams(("parallel",)),
        name="a_proj",
    )(small, g_cq, g_ckv, wq, wiq, wk, wv, tq, ti, tk)


def _dsa_index_kernel(kilo_ref, kihi_ref, qi_ref, wt_ref, bias_ref, isc_ref, mm_ref, js_ref,
                      *, S, TQ, KCH, topk):
    j = pl.program_id(1)
    q0 = j * TQ
    nkeys = q0 + TQ
    nch = j + 1
    qidx = q0 + lax.broadcasted_iota(jnp.int32, (1, TQ), 1)
    kf = float(topk)

    mm_ref[0:8, :] = jnp.full((8, TQ), jnp.inf, F32)
    mm_ref[8:16, :] = jnp.full((8, TQ), -jnp.inf, F32)
    for c in range(S // KCH):
        @pl.when(c * KCH < nkeys)
        def _(c=c):
            klo = kilo_ref[c * KCH:(c + 1) * KCH, :]
            khi = kihi_ref[c * KCH:(c + 1) * KCH, :]
            acc = jnp.zeros((KCH, TQ), F32)
            for hp in range(IDX_HEADS // 2):
                qp = qi_ref[:, hp * LANES:(hp + 1) * LANES]
                s0 = _dot_nt(klo, qp)
                s1 = _dot_nt(khi, qp)
                acc = acc + jnp.maximum(s0, 0.0) * wt_ref[2 * hp:2 * hp + 1, :]
                acc = acc + jnp.maximum(s1, 0.0) * wt_ref[2 * hp + 1:2 * hp + 2, :]
            kidx = c * KCH + lax.broadcasted_iota(jnp.int32, (KCH, TQ), 0)
            causal = kidx <= qidx
            isc_ref[c * KCH:(c + 1) * KCH, :] = jnp.where(causal, acc, -jnp.inf)
            lo_part = jnp.where(causal, acc, jnp.inf).reshape(KCH // 8, 8, TQ).min(axis=0)
            hi_part = jnp.where(causal, acc, -jnp.inf).reshape(KCH // 8, 8, TQ).max(axis=0)
            mm_ref[0:8, :] = jnp.minimum(mm_ref[0:8, :], lo_part)
            mm_ref[8:16, :] = jnp.maximum(mm_ref[8:16, :], hi_part)

    def count(pred):
        def body(c, acc):
            k0 = pl.multiple_of(c * TQ, TQ)
            blk = isc_ref[pl.ds(k0, TQ), :]
            ones = jnp.where(pred(blk, k0), 1.0, 0.0)
            return acc + ones.reshape(TQ // 8, 8, TQ).sum(axis=0)
        acc = lax.fori_loop(0, nch, body, jnp.zeros((8, TQ), F32))
        return acc.sum(axis=0, keepdims=True)

    row_min = mm_ref[0:8, :].min(axis=0, keepdims=True)
    row_max = mm_ref[8:16, :].max(axis=0, keepdims=True)
    full = (qidx + 1) <= topk
    c_max = count(lambda blk, k0: blk >= row_max)
    exact0 = c_max == kf
    tie0 = c_max > kf
    settled0 = full | exact0 | tie0
    lo0 = jnp.where(full, -F32_MAX, jnp.where(settled0, row_max, row_min))
    hi0 = jnp.where(full, -F32_MAX, jnp.where(tie0, jnp.inf, row_max))
    act0 = jnp.where(settled0, 0.0, 1.0)

    def cond(st):
        return st[3] > 0

    def body(st):
        lo, hi, act, _ = st
        mid = lo * 0.5 + hi * 0.5
        inside = (mid > lo) & (mid < hi)
        cnt = count(lambda blk, k0: blk >= mid)
        upd = (act > 0.0) & inside
        found = upd & (cnt == kf)
        lo2 = jnp.where(upd & (cnt >= kf), mid, lo)
        hi2 = jnp.where(upd & (cnt <= kf), mid, hi)
        act2 = jnp.where(upd & jnp.logical_not(found), 1.0, 0.0)
        return lo2, hi2, act2, (jnp.max(act2) > 0.0).astype(jnp.int32)

    lo, hi, _, _ = lax.while_loop(cond, body, (lo0, hi0, act0, (jnp.max(act0) > 0.0).astype(jnp.int32)))

    tie = lo < hi
    js_ref[...] = jnp.full((8, TQ), -1.0, F32)

    @pl.when(jnp.max(jnp.where(tie, 1.0, 0.0)) > 0.0)
    def _():
        need = kf - count(lambda blk, k0: blk >= hi)

        def kpos(k0):
            return (k0 + lax.broadcasted_iota(jnp.int32, (TQ, TQ), 0)).astype(F32)

        def step(_, st):
            ilo, ihi = st
            imid = jnp.floor((ilo + ihi) * 0.5)
            cnt = count(lambda blk, k0: (blk >= lo) & (blk < hi) & (kpos(k0) <= imid))
            ge = cnt >= need
            return jnp.where(ge, ilo, imid), jnp.where(ge, imid, ihi)

        nsteps = int(math.ceil(math.log2(S))) + 1
        _, ihi = lax.fori_loop(0, nsteps, step,
                               (jnp.full((1, TQ), -1.0, F32), jnp.full((1, TQ), S - 1.0, F32)))
        js_ref[0:1, :] = jnp.where(tie, ihi, -1.0)

    jstar = js_ref[0:1, :]

    for c in range(S // TQ):
        @pl.when(c < nch)
        def _(c=c):
            blk = isc_ref[c * TQ:(c + 1) * TQ, :]
            kpos = (c * TQ + lax.broadcasted_iota(jnp.int32, (TQ, TQ), 0)).astype(F32)
            sel = (blk >= hi) | ((blk >= lo) & (kpos <= jstar))
            bias_ref[:, c * TQ:(c + 1) * TQ] = jnp.where(sel, 0.0, NEG_BIAS).T

        @pl.when(c >= nch)
        def _(c=c):
            bias_ref[:, c * TQ:(c + 1) * TQ] = jnp.full((TQ, TQ), NEG_BIAS, F32)


def _dsa_index(ki_lo, ki_hi, qi, wt, *, B, S, TQ, KCH, topk):
    T = B * S
    nq = S // TQ
    kern = functools.partial(_dsa_index_kernel, S=S, TQ=TQ, KCH=KCH, topk=topk)
    return pl.pallas_call(
        kern,
        grid=(B, nq),
        in_specs=[
            pl.BlockSpec((S, LANES), lambda b, j: (b, 0)),
            pl.BlockSpec((S, LANES), lambda b, j: (b, 0)),
            pl.BlockSpec((TQ, IDX_HEADS * IDX_DIM), lambda b, j: (b * nq + j, 0)),
            pl.BlockSpec((IDX_HEADS, TQ), lambda b, j: (0, b * nq + j)),
        ],
        out_specs=pl.BlockSpec((TQ, S), lambda b, j: (b * nq + j, 0)),
        out_shape=jax.ShapeDtypeStruct((T, S), F32),
        scratch_shapes=[
            pltpu.VMEM((S, TQ), F32),
            pltpu.VMEM((16, TQ), F32),
            pltpu.VMEM((8, TQ), F32),
        ],
        compiler_params=_cparams(("parallel", "arbitrary")),
        name="dsa_index",
    )(ki_lo, ki_hi, qi, wt)


def _dsa_attn_kernel(q_ref, k_ref, v_ref, bias_ref, o_ref, *, S, TQ):
    j = pl.program_id(1)
    lane = lax.broadcasted_iota(jnp.int32, (TQ, LANES), 1)

    def variant(nk):
        def pair(p, carry):
            vp = v_ref[0, p, 0:nk, :]
            outs = []
            for e in range(2):
                h = 2 * p + e
                s = _dot_nt(q_ref[0, h], k_ref[0, h, 0:nk, :]) + bias_ref[:, 0:nk]
                m = s.max(axis=1, keepdims=True)
                pe = jnp.exp(s - m)
                l = pe.sum(axis=1, keepdims=True)
                outs.append(_dot(pe.astype(BF16), vp) * (1.0 / l))
            o_ref[0, p] = jnp.where(lane < V_HEAD, outs[0], outs[1]).astype(BF16)
            return carry
        lax.fori_loop(0, N_HEADS // 2, pair, 0)

    for jj in range(S // TQ):
        @pl.when(j == jj)
        def _(jj=jj):
            variant((jj + 1) * TQ)


def _dsa_attn(q, k, v, bias, *, B, S, TQ):
    nq = S // TQ
    kern = functools.partial(_dsa_attn_kernel, S=S, TQ=TQ)
    return pl.pallas_call(
        kern,
        grid=(B, nq),
        in_specs=[
            pl.BlockSpec((1, N_HEADS, TQ, LANES), lambda b, j: (b, 0, j, 0)),
            pl.BlockSpec((1, N_HEADS, S, LANES), lambda b, j: (b, 0, 0, 0)),
            pl.BlockSpec((1, N_HEADS // 2, S, LANES), lambda b, j: (b, 0, 0, 0)),
            pl.BlockSpec((TQ, S), lambda b, j: (b * nq + j, 0)),
        ],
        out_specs=pl.BlockSpec((1, N_HEADS // 2, TQ, LANES), lambda b, j: (b, 0, j, 0)),
        out_shape=jax.ShapeDtypeStruct((B, N_HEADS // 2, S, LANES), BF16),
        compiler_params=_cparams(("parallel", "arbitrary")),
        name="dsa_attn",
    )(q, k, v, bias)


def _sgu_kernel(z_ref, g_ref, ws_ref, bt_ref, y_ref, *, tm):
    row = lax.broadcasted_iota(jnp.int32, (SGU_CHUNK, SGU_CHUNK), 0)
    col = lax.broadcasted_iota(jnp.int32, (SGU_CHUNK, SGU_CHUNK), 1)
    tril = col <= row
    w = [jnp.where(tril, ws_ref[g], 0.0).astype(BF16) for g in range(SGU_GROUPS)]
    for cc in range(tm // SGU_CHUNK):
        rows = slice(cc * SGU_CHUNK, (cc + 1) * SGU_CHUNK)
        vn = _rms(z_ref[rows, SGU_WIDTH:2 * SGU_WIDTH].astype(F32), g_ref[...]).astype(BF16)
        for g in range(SGU_GROUPS):
            cols = slice(g * SGU_GROUP_DIM, (g + 1) * SGU_GROUP_DIM)
            mixed = _dot(w[g], vn[:, cols]) + bt_ref[:, g:g + 1]
            y_ref[rows, cols] = (z_ref[rows, cols].astype(F32) * mixed).astype(BF16)


def _sgu(z, g_sgu, w_spatial, b_t, *, tm):
    T = z.shape[0]
    kern = functools.partial(_sgu_kernel, tm=tm)
    return pl.pallas_call(
        kern,
        grid=(T // tm,),
        in_specs=[
            pl.BlockSpec((tm, 2 * SGU_WIDTH), lambda i: (i, 0)),
            pl.BlockSpec((1, SGU_WIDTH), lambda i: (0, 0)),
            pl.BlockSpec((SGU_GROUPS, SGU_CHUNK, SGU_CHUNK), lambda i: (0, 0, 0)),
            pl.BlockSpec((SGU_CHUNK, SGU_GROUPS), lambda i: (0, 0)),
        ],
        out_specs=pl.BlockSpec((tm, SGU_WIDTH), lambda i: (i, 0)),
        out_shape=jax.ShapeDtypeStruct((T, SGU_WIDTH), BF16),
        compiler_params=_cparams(("parallel",)),
        name="sgu",
    )(z, g_sgu, w_spatial, b_t)


def _merge_kernel(o_ref, y_ref, ga_ref, gb_ref, wa_ref, wb_ref, out_ref):
    o_a = jnp.concatenate([o_ref[0, p] for p in range(N_HEADS // 2)], axis=1)
    ya = _dot(o_a, wa_ref[...])
    yb = _dot(y_ref[...], wb_ref[...])
    out_ref[...] = (ga_ref[...].astype(F32) * ya + gb_ref[...].astype(F32) * yb).astype(BF16)


def _merge(o, y_sgu, gates, wa, wb, *, B, S, tm, tn):
    T = B * S
    D = wa.shape[1]
    nt = S // tm
    nn = D // tn
    return pl.pallas_call(
        _merge_kernel,
        grid=(T // tm, nn),
        in_specs=[
            pl.BlockSpec((1, N_HEADS // 2, tm, LANES), lambda i, j: (i // nt, 0, i % nt, 0)),
            pl.BlockSpec((tm, SGU_WIDTH), lambda i, j: (i, 0)),
            pl.BlockSpec((tm, tn), lambda i, j: (i, j)),
            pl.BlockSpec((tm, tn), lambda i, j: (i, nn + j)),
            pl.BlockSpec((wa.shape[0], tn), lambda i, j: (0, j)),
            pl.BlockSpec((wb.shape[0], tn), lambda i, j: (0, j)),
        ],
        out_specs=pl.BlockSpec((tm, tn), lambda i, j: (i, j)),
        out_shape=jax.ShapeDtypeStruct((T, D), BF16),
        compiler_params=_cparams(("parallel", "arbitrary")),
        name="merge",
    )(o, y_sgu, gates, gates, wa, wb)


def _o_proj_kernel(x_ref, m_ref, w_ref, g_ref, h_ref, n_ref):
    h = x_ref[...] + _dot(m_ref[...], w_ref[...])
    h_ref[...] = h
    n_ref[...] = _rms(h, g_ref[...]).astype(BF16)


def _o_proj(x2, merged, w_o, g_ffn, *, tm):
    T, D = x2.shape
    return pl.pallas_call(
        _o_proj_kernel,
        grid=(T // tm,),
        in_specs=[
            pl.BlockSpec((tm, D), lambda i: (i, 0)),
            pl.BlockSpec((tm, D), lambda i: (i, 0)),
            pl.BlockSpec((D, D), lambda i: (0, 0)),
            pl.BlockSpec((1, D), lambda i: (0, 0)),
        ],
        out_specs=[pl.BlockSpec((tm, D), lambda i: (i, 0)), pl.BlockSpec((tm, D), lambda i: (i, 0))],
        out_shape=[jax.ShapeDtypeStruct((T, D), F32), jax.ShapeDtypeStruct((T, D), BF16)],
        compiler_params=_cparams(("parallel",)),
        name="o_proj",
    )(x2, merged, w_o, g_ffn)


def _ffn_up_kernel(n_ref, wg_ref, wu_ref, a_ref):
    n = n_ref[...]
    g = _dot(n, wg_ref[...])
    u = _dot(n, wu_ref[...])
    a_ref[...] = (g * _sigmoid(g) * u).astype(BF16)


def _ffn_up(n2, w_gu, *, d_ff, tm, tn):
    T, D = n2.shape
    nn = d_ff // tn
    return pl.pallas_call(
        _ffn_up_kernel,
        grid=(T // tm, nn),
        in_specs=[
            pl.BlockSpec((tm, D), lambda i, j: (i, 0)),
            pl.BlockSpec((D, tn), lambda i, j: (0, j)),
            pl.BlockSpec((D, tn), lambda i, j: (0, nn + j)),
        ],
        out_specs=pl.BlockSpec((tm, tn), lambda i, j: (i, j)),
        out_shape=jax.ShapeDtypeStruct((T, d_ff), BF16),
        compiler_params=_cparams(("parallel", "arbitrary")),
        name="ffn_up",
    )(n2, w_gu, w_gu)


def _ffn_down_kernel(a_ref, w_ref, h1_ref, g_ref, h2_ref, n3_ref, acc_ref):
    k = pl.program_id(1)

    @pl.when(k == 0)
    def _():
        acc_ref[...] = h1_ref[...]

    acc_ref[...] += _dot(a_ref[...], w_ref[...])

    @pl.when(k == pl.num_programs(1) - 1)
    def _():
        h2 = acc_ref[...]
        h2_ref[...] = h2
        n3_ref[...] = _rms(h2, g_ref[...]).astype(BF16)


def _ffn_down(act, w_down, h1, g_ple, *, tm, tk):
    T, D = h1.shape
    d_ff = act.shape[1]
    return pl.pallas_call(
        _ffn_down_kernel,
        grid=(T // tm, d_ff // tk),
        in_specs=[
            pl.BlockSpec((tm, tk), lambda i, k: (i, k)),
            pl.BlockSpec((tk, D), lambda i, k: (k, 0)),
            pl.BlockSpec((tm, D), lambda i, k: (i, 0)),
            pl.BlockSpec((1, D), lambda i, k: (0, 0)),
        ],
        out_specs=[pl.BlockSpec((tm, D), lambda i, k: (i, 0)), pl.BlockSpec((tm, D), lambda i, k: (i, 0))],
        out_shape=[jax.ShapeDtypeStruct((T, D), F32), jax.ShapeDtypeStruct((T, D), BF16)],
        scratch_shapes=[pltpu.VMEM((tm, D), F32)],
        compiler_params=_cparams(("parallel", "arbitrary")),
        name="ffn_down",
    )(act, w_down, h1, g_ple)


def _ple_final_kernel(h2_ref, n3_ref, p_ref, wg_ref, wp_ref, g_ref, out_ref):
    gate = _sigmoid(_dot(n3_ref[...], wg_ref[...]))
    pp = _dot(p_ref[...].astype(BF16), wp_ref[...])
    out_ref[...] = _rms(h2_ref[...] + gate * pp, g_ref[...])


def _ple_final(h2, n3, p2, w_pg, w_pp, g_final, *, tm):
    T, D = h2.shape
    P = p2.shape[1]
    return pl.pallas_call(
        _ple_final_kernel,
        grid=(T // tm,),
        in_specs=[
            pl.BlockSpec((tm, D), lambda i: (i, 0)),
            pl.BlockSpec((tm, D), lambda i: (i, 0)),
            pl.BlockSpec((tm, P), lambda i: (i, 0)),
            pl.BlockSpec((D, D), lambda i: (0, 0)),
            pl.BlockSpec((P, D), lambda i: (0, 0)),
            pl.BlockSpec((1, D), lambda i: (0, 0)),
        ],
        out_specs=pl.BlockSpec((tm, D), lambda i: (i, 0)),
        out_shape=jax.ShapeDtypeStruct((T, D), F32),
        compiler_params=_cparams(("parallel",)),
        name="ple_final",
    )(h2, n3, p2, w_pg, w_pp, g_final)


def _rope_tables(seq, dim):
    inv = ROPE_THETA ** (-jnp.arange(0, dim, 2, dtype=F32) / dim)
    ang = jnp.arange(seq, dtype=F32)[:, None] * inv[None, :]
    return jnp.cos(ang), jnp.sin(ang)


def _lane_tables(S):
    scale = (QK_NOPE + QK_ROPE) ** -0.5
    c, s = _rope_tables(S, QK_ROPE)
    z = lambda n: jnp.zeros((S, n), F32)
    one = lambda n: jnp.ones((S, n), F32)
    tq = jnp.stack([
        jnp.concatenate([one(QK_NOPE), c, c, z(32)], 1) * scale,
        jnp.concatenate([z(QK_NOPE), -s, z(16), z(32)], 1) * scale,
        jnp.concatenate([z(QK_NOPE), z(16), s, z(32)], 1) * scale,
    ])
    ci, si = _rope_tables(S, IDX_ROPE)
    half = lambda a, b, rest: jnp.concatenate([a, b, rest], 1)
    ti = jnp.stack([
        jnp.tile(half(ci, ci, one(IDX_DIM - IDX_ROPE)), (1, 2)),
        jnp.tile(half(-si, z(16), z(IDX_DIM - IDX_ROPE)), (1, 2)),
        jnp.tile(half(z(16), si, z(IDX_DIM - IDX_ROPE)), (1, 2)),
    ])
    tk = jnp.stack([
        jnp.concatenate([c, c, z(96)], 1),
        jnp.concatenate([-s, z(16), z(96)], 1),
        jnp.concatenate([z(16), s, z(96)], 1),
    ])
    return tq, ti, tk


def kernel(x, p, g_mix, w_in, g_cq, g_ckv, w_uq, w_uk, w_uv, w_iq, w_a_proj, g_sgu, w_spatial,
           b_spatial, w_b_proj, w_o, g_ffn, w_gu, w_down, g_ple, w_ple_gate, w_ple_proj, g_final):
    B, S, D = x.shape
    T = B * S
    depth = w_in.shape[0]
    d_ff = w_down.shape[1]
    topk = min(TOPK_MAX, S // 4)
    TQ = 256
    KCH = min(512, S)
    tm = min(512, S)
    tq, ti, tk = _lane_tables(S)

    h = x.reshape(T, D)
    for i in range(depth):
        w = w_in[i]
        o0 = Q_LORA
        o1 = o0 + KV_LORA
        o2 = o1 + QK_ROPE
        o3 = o2 + IDX_DIM
        o4 = o3 + IDX_HEADS
        o5 = o4 + 2 * SGU_WIDTH
        zc = lambda n: jnp.zeros((D, n), w.dtype)
        w1 = jnp.concatenate([
            w[:, 0:o0], w[:, o0:o1],
            w[:, o2:o3], zc(LANES - IDX_DIM),
            w[:, o1:o2], zc(MISC_WIDX - QK_ROPE),
            w[:, o3:o4], zc(LANES - MISC_WIDX - IDX_HEADS),
            w[:, o4:o5], w[:, o5:],
        ], axis=1).astype(BF16)
        wq = jnp.pad(w_uq[i].reshape(Q_LORA, N_HEADS, QK_NOPE + QK_ROPE),
                     ((0, 0), (0, 0), (0, LANES - QK_NOPE - QK_ROPE))).reshape(Q_LORA, N_HEADS * LANES).astype(BF16)
        wk = jnp.pad(w_uk[i], ((0, 0), (0, 0), (0, LANES - QK_NOPE))).reshape(KV_LORA, N_HEADS * LANES).astype(BF16)
        wv = w_uv[i].reshape(KV_LORA, N_HEADS * V_HEAD).astype(BF16)
        wiq = w_iq[i].astype(BF16)

        small, z, gates = _in_proj(h, g_mix[i][None], w1, tm=tm, tn=512)
        q, k, v, qi, ki_lo, ki_hi, wt = _a_proj(small, g_cq[i][None], g_ckv[i][None], wq, wiq, wk, wv,
                                                 tq, ti, tk, B=B, S=S, tm=tm)
        bias = _dsa_index(ki_lo, ki_hi, qi, wt, B=B, S=S, TQ=TQ, KCH=KCH, topk=topk)
        o = _dsa_attn(q, k, v, bias, B=B, S=S, TQ=TQ)
        y_sgu = _sgu(z, g_sgu[i][None], w_spatial[i], jnp.transpose(b_spatial[i]), tm=tm)
        merged = _merge(o, y_sgu, gates, w_a_proj[i].astype(BF16), w_b_proj[i].astype(BF16),
                        B=B, S=S, tm=tm, tn=1024)
        h1, n2 = _o_proj(h, merged, w_o[i].astype(BF16), g_ffn[i][None], tm=min(256, S))
        act = _ffn_up(n2, w_gu[i].astype(BF16), d_ff=d_ff, tm=tm, tn=512)
        h2, n3 = _ffn_down(act, w_down[i].astype(BF16), h1, g_ple[i][None], tm=tm, tk=d_ff // 4)
        assert depth == 1
        h = _ple_final(h2, n3, p[i].reshape(T, -1), w_ple_gate[i].astype(BF16), w_ple_proj[i].astype(BF16),
                       g_final[None], tm=min(256, S))
    return h.reshape(B, S, D)
```

```python
import functools
import math

import jax
import jax.numpy as jnp
from jax import lax
from jax.experimental import pallas as pl
from jax.experimental.pallas import tpu as pltpu

F32 = jnp.float32
BF16 = jnp.bfloat16

N_HEADS = 16
QK_NOPE = 64
QK_ROPE = 32
V_HEAD = 64
Q_LORA = 512
KV_LORA = 256
IDX_HEADS = 16
IDX_DIM = 64
IDX_ROPE = 32
TOPK_MAX = 256
SGU_CHUNK = 128
SGU_GROUPS = 8
SGU_GROUP_DIM = 128
SGU_WIDTH = SGU_GROUPS * SGU_GROUP_DIM
ROPE_THETA = 10000.0
EPS = 1e-6

LANES = 128
IN_SMALL = Q_LORA + KV_LORA + QK_ROPE + IDX_DIM + IDX_HEADS
SMALL_COLS = 1024
COL_CKV = Q_LORA
COL_MISC = Q_LORA + KV_LORA
MISC_KIDX = QK_ROPE
MISC_WIDX = QK_ROPE + IDX_DIM
VMEM_LIMIT_BYTES = 56 * 1024 * 1024
NEG_BIAS = -1e30
F32_MAX = 3.4028234663852886e38


def _cparams(semantics):
    return pltpu.CompilerParams(dimension_semantics=semantics, vmem_limit_bytes=VMEM_LIMIT_BYTES)


def _rms(x, g):
    return x * lax.rsqrt(jnp.mean(x * x, axis=-1, keepdims=True) + EPS) * g


def _dot(a, b):
    return jnp.dot(a, b, preferred_element_type=F32)


def _dot_nt(a, b):
    return lax.dot_general(a, b, (((1,), (1,)), ((), ())), preferred_element_type=F32)


def _gelu_exact(x):
    return 0.5 * x * (1.0 + lax.erf(x * (1.0 / math.sqrt(2.0))))


def _sigmoid(x):
    return 1.0 / (1.0 + jnp.exp(-x))


def _in_proj_kernel(x_ref, g_ref, w_ref, small_ref, z_ref, gate_ref, n_scr, *, n_small, n_z):
    j = pl.program_id(1)

    @pl.when(j == 0)
    def _():
        n_scr[...] = _rms(x_ref[...], g_ref[...]).astype(BF16)

    acc = _dot_nt(n_scr[...], w_ref[...].astype(BF16))

    @pl.when(j < n_small)
    def _():
        small_ref[...] = acc

    @pl.when((j >= n_small) & (j < n_small + n_z))
    def _():
        z_ref[...] = _gelu_exact(acc).astype(BF16)

    @pl.when(j >= n_small + n_z)
    def _():
        gate_ref[...] = _sigmoid(acc).astype(BF16)


def _in_proj(x2, g_mix, w_t, *, tm, tn):
    T, D = x2.shape
    n_small = SMALL_COLS // tn
    n_z = (2 * SGU_WIDTH) // tn
    n_gate = (2 * D) // tn
    row_z = IN_SMALL
    row_gate = IN_SMALL + 2 * SGU_WIDTH
    assert w_t.shape == (row_gate + 2 * D, D) and row_z % 8 == 0
    grid = (T // tm, n_small + n_z + n_gate)
    kern = functools.partial(_in_proj_kernel, n_small=n_small, n_z=n_z)

    def w_row(i, j):
        tiles = jnp.where(j < n_small, j * (tn // 8),
                          jnp.where(j < n_small + n_z, row_z // 8 + (j - n_small) * (tn // 8),
                                    row_gate // 8 + (j - n_small - n_z) * (tn // 8)))
        return tiles * 8

    return pl.pallas_call(
        kern,
        grid=grid,
        in_specs=[
            pl.BlockSpec((tm, D), lambda i, j: (i, 0)),
            pl.BlockSpec((1, D), lambda i, j: (0, 0)),
            pl.BlockSpec((pl.Element(tn), pl.Element(D)), lambda i, j: (w_row(i, j), 0)),
        ],
        out_specs=[
            pl.BlockSpec((tm, tn), lambda i, j: (i, jnp.minimum(j, n_small - 1))),
            pl.BlockSpec((tm, tn), lambda i, j: (i, jnp.clip(j - n_small, 0, n_z - 1))),
            pl.BlockSpec((tm, tn), lambda i, j: (i, jnp.clip(j - n_small - n_z, 0, n_gate - 1))),
        ],
        out_shape=[
            jax.ShapeDtypeStruct((T, SMALL_COLS), F32),
            jax.ShapeDtypeStruct((T, 2 * SGU_WIDTH), BF16),
            jax.ShapeDtypeStruct((T, 2 * D), BF16),
        ],
        scratch_shapes=[pltpu.VMEM((tm, D), BF16)],
        compiler_params=_cparams(("parallel", "arbitrary")),
        name="in_proj",
    )(x2, g_mix, w_t)


def _rope_block(x, cos, sin_a, sin_b):
    return x * cos + pltpu.roll(x, LANES - 16, 1) * sin_a + pltpu.roll(x, 16, 1) * sin_b


def _a_proj_kernel(small_ref, gcq_ref, gckv_ref, wq_ref, wiq_ref, wk_ref, wv_ref,
                   tq_ref, ti_ref, tki_ref, tk_ref,
                   q_ref, k_ref, v_ref, qi_ref, kilo_ref, kihi_ref, wt_ref):
    c_q = _rms(small_ref[:, 0:Q_LORA], gcq_ref[...]).astype(BF16)
    c_kv = _rms(small_ref[:, COL_CKV:COL_CKV + KV_LORA], gckv_ref[...]).astype(BF16)

    q = _dot(c_q, wq_ref[...])
    cq, sqa, sqb = tq_ref[0], tq_ref[1], tq_ref[2]
    for h in range(N_HEADS):
        blk = q[:, h * LANES:(h + 1) * LANES]
        q_ref[0, h] = _rope_block(blk, cq, sqa, sqb).astype(BF16)

    qi = _dot(c_q, wiq_ref[...])
    ci, sia, sib = ti_ref[0], ti_ref[1], ti_ref[2]
    for hp in range(IDX_HEADS // 2):
        blk = qi[:, hp * LANES:(hp + 1) * LANES]
        qi_ref[:, hp * LANES:(hp + 1) * LANES] = _rope_block(blk, ci, sia, sib).astype(BF16)

    misc = small_ref[:, COL_MISC:COL_MISC + LANES]
    ki_lo = _rope_block(pltpu.roll(misc, LANES - MISC_KIDX, 1), tki_ref[0], tki_ref[1], tki_ref[2])
    kilo_ref[...] = ki_lo.astype(BF16)
    kihi_ref[...] = pltpu.roll(ki_lo, IDX_DIM, 1).astype(BF16)

    ck, ska, skb = tk_ref[0], tk_ref[1], tk_ref[2]
    k_rope = pltpu.roll(_rope_block(misc, ck, ska, skb), QK_NOPE, 1)
    k_nope = _dot(c_kv, wk_ref[...])
    for h in range(N_HEADS):
        k_ref[0, h] = (k_nope[:, h * LANES:(h + 1) * LANES] + k_rope).astype(BF16)

    v = _dot(c_kv, wv_ref[...])
    for p in range(N_HEADS // 2):
        v_ref[0, p] = v[:, p * LANES:(p + 1) * LANES].astype(BF16)

    w_scale = IDX_HEADS ** -0.5 * IDX_DIM ** -0.5
    wt_ref[...] = misc.T[MISC_WIDX:MISC_WIDX + IDX_HEADS, :] * w_scale


def _a_proj(small, g_cq, g_ckv, wq, wiq, wk, wv, tq, ti, tki, tk, *, B, S, tm):
    T = B * S
    nt = S // tm
    const2 = lambda i: (0, 0)
    tab = pl.BlockSpec((3, tm, LANES), lambda i: (0, i % nt, 0))
    head_spec = lambda nh: pl.BlockSpec((1, nh, tm, LANES), lambda i: (i // nt, 0, i % nt, 0))
    return pl.pallas_call(
        _a_proj_kernel,
        grid=(T // tm,),
        in_specs=[
            pl.BlockSpec((tm, SMALL_COLS), lambda i: (i, 0)),
            pl.BlockSpec((1, Q_LORA), const2),
            pl.BlockSpec((1, KV_LORA), const2),
            pl.BlockSpec(wq.shape, const2),
            pl.BlockSpec(wiq.shape, const2),
            pl.BlockSpec(wk.shape, const2),
            pl.BlockSpec(wv.shape, const2),
            tab, tab, tab, tab,
        ],
        out_specs=[
            head_spec(N_HEADS),
            head_spec(N_HEADS),
            head_spec(N_HEADS // 2),
            pl.BlockSpec((tm, IDX_HEADS * IDX_DIM), lambda i: (i, 0)),
            pl.BlockSpec((tm, LANES), lambda i: (i, 0)),
            pl.BlockSpec((tm, LANES), lambda i: (i, 0)),
            pl.BlockSpec((IDX_HEADS, tm), lambda i: (0, i)),
        ],
        out_shape=[
            jax.ShapeDtypeStruct((B, N_HEADS, S, LANES), BF16),
            jax.ShapeDtypeStruct((B, N_HEADS, S, LANES), BF16),
            jax.ShapeDtypeStruct((B, N_HEADS // 2, S, LANES), BF16),
            jax.ShapeDtypeStruct((T, IDX_HEADS * IDX_DIM), BF16),
            jax.ShapeDtypeStruct((T, LANES), BF16),
            jax.ShapeDtypeStruct((T, LANES), BF16),
            jax.ShapeDtypeStruct((IDX_HEADS, T), F32),
        ],
        compiler_params=_cparams(("parallel",)),
        name="a_proj",
    )(small, g_cq, g_ckv, wq, wiq, wk, wv, tq, ti, tki, tk)


def _dsa_index_kernel(kilo_ref, kihi_ref, qi_ref, wt_ref, bias_ref, isc_ref, mm_ref, js_ref,
                      *, S, TQ, KCH, topk):
    j = pl.program_id(1)
    q0 = j * TQ
    nkeys = q0 + TQ
    nch = j + 1
    qidx = q0 + lax.broadcasted_iota(jnp.int32, (1, TQ), 1)
    kf = float(topk)

    mm_ref[0:8, :] = jnp.full((8, TQ), jnp.inf, F32)
    mm_ref[8:16, :] = jnp.full((8, TQ), -jnp.inf, F32)
    for c in range(S // KCH):
        @pl.when(c * KCH < nkeys)
        def _(c=c):
            klo = kilo_ref[c * KCH:(c + 1) * KCH, :]
            khi = kihi_ref[c * KCH:(c + 1) * KCH, :]
            acc = jnp.zeros((KCH, TQ), F32)
            for hp in range(IDX_HEADS // 2):
                qp = qi_ref[:, hp * LANES:(hp + 1) * LANES]
                s0 = _dot_nt(klo, qp)
                s1 = _dot_nt(khi, qp)
                acc = acc + jnp.maximum(s0, 0.0) * wt_ref[2 * hp:2 * hp + 1, :]
                acc = acc + jnp.maximum(s1, 0.0) * wt_ref[2 * hp + 1:2 * hp + 2, :]
            kidx = c * KCH + lax.broadcasted_iota(jnp.int32, (KCH, TQ), 0)
            causal = kidx <= qidx
            isc_ref[c * KCH:(c + 1) * KCH, :] = jnp.where(causal, acc, -jnp.inf)
            lo_part = jnp.where(causal, acc, jnp.inf).reshape(KCH // 8, 8, TQ).min(axis=0)
            hi_part = jnp.where(causal, acc, -jnp.inf).reshape(KCH // 8, 8, TQ).max(axis=0)
            mm_ref[0:8, :] = jnp.minimum(mm_ref[0:8, :], lo_part)
            mm_ref[8:16, :] = jnp.maximum(mm_ref[8:16, :], hi_part)

    def count(pred):
        def body(c, acc):
            k0 = pl.multiple_of(c * TQ, TQ)
            blk = isc_ref[pl.ds(k0, TQ), :]
            ones = jnp.where(pred(blk, k0), 1.0, 0.0)
            return acc + ones.reshape(TQ // 8, 8, TQ).sum(axis=0)
        acc = lax.fori_loop(0, nch, body, jnp.zeros((8, TQ), F32))
        return acc.sum(axis=0, keepdims=True)

    row_min = mm_ref[0:8, :].min(axis=0, keepdims=True)
    row_max = mm_ref[8:16, :].max(axis=0, keepdims=True)
    full = (qidx + 1) <= topk
    c_max = count(lambda blk, k0: blk >= row_max)
    exact0 = c_max == kf
    tie0 = c_max > kf
    settled0 = full | exact0 | tie0
    lo0 = jnp.where(full, -F32_MAX, jnp.where(settled0, row_max, row_min))
    hi0 = jnp.where(full, -F32_MAX, jnp.where(tie0, jnp.inf, row_max))
    act0 = jnp.where(settled0, 0.0, 1.0)

    def cond(st):
        return st[3] > 0

    def body(st):
        lo, hi, act, _ = st
        mid = lo * 0.5 + hi * 0.5
        inside = (mid > lo) & (mid < hi)
        cnt = count(lambda blk, k0: blk >= mid)
        upd = (act > 0.0) & inside
        found = upd & (cnt == kf)
        lo2 = jnp.where(upd & (cnt >= kf), mid, lo)
        hi2 = jnp.where(upd & (cnt <= kf), mid, hi)
        act2 = jnp.where(upd & jnp.logical_not(found), 1.0, 0.0)
        return lo2, hi2, act2, (jnp.max(act2) > 0.0).astype(jnp.int32)

    lo, hi, _, _ = lax.while_loop(cond, body, (lo0, hi0, act0, (jnp.max(act0) > 0.0).astype(jnp.int32)))

    tie = lo < hi
    js_ref[...] = jnp.full((8, TQ), -1.0, F32)

    @pl.when(jnp.max(jnp.where(tie, 1.0, 0.0)) > 0.0)
    def _():
        need = kf - count(lambda blk, k0: blk >= hi)

        def kpos(k0):
            return (k0 + lax.broadcasted_iota(jnp.int32, (TQ, TQ), 0)).astype(F32)

        def step(_, st):
            ilo, ihi = st
            imid = jnp.floor((ilo + ihi) * 0.5)
            cnt = count(lambda blk, k0: (blk >= lo) & (blk < hi) & (kpos(k0) <= imid))
            ge = cnt >= need
            return jnp.where(ge, ilo, imid), jnp.where(ge, imid, ihi)

        nsteps = int(math.ceil(math.log2(S))) + 1
        _, ihi = lax.fori_loop(0, nsteps, step,
                               (jnp.full((1, TQ), -1.0, F32), jnp.full((1, TQ), S - 1.0, F32)))
        js_ref[0:1, :] = jnp.where(tie, ihi, -1.0)

    jstar = js_ref[0:1, :]

    for c in range(S // TQ):
        @pl.when(c < nch)
        def _(c=c):
            blk = isc_ref[c * TQ:(c + 1) * TQ, :]
            kpos = (c * TQ + lax.broadcasted_iota(jnp.int32, (TQ, TQ), 0)).astype(F32)
            sel = (blk >= hi) | ((blk >= lo) & (kpos <= jstar))
            bias_ref[:, c * TQ:(c + 1) * TQ] = jnp.where(sel, 0.0, NEG_BIAS).T

        @pl.when(c >= nch)
        def _(c=c):
            bias_ref[:, c * TQ:(c + 1) * TQ] = jnp.full((TQ, TQ), NEG_BIAS, F32)


def _dsa_index(ki_lo, ki_hi, qi, wt, *, B, S, TQ, KCH, topk):
    T = B * S
    nq = S // TQ
    kern = functools.partial(_dsa_index_kernel, S=S, TQ=TQ, KCH=KCH, topk=topk)
    return pl.pallas_call(
        kern,
        grid=(B, nq),
        in_specs=[
            pl.BlockSpec((S, LANES), lambda b, j: (b, 0)),
            pl.BlockSpec((S, LANES), lambda b, j: (b, 0)),
            pl.BlockSpec((TQ, IDX_HEADS * IDX_DIM), lambda b, j: (b * nq + j, 0)),
            pl.BlockSpec((IDX_HEADS, TQ), lambda b, j: (0, b * nq + j)),
        ],
        out_specs=pl.BlockSpec((TQ, S), lambda b, j: (b * nq + j, 0)),
        out_shape=jax.ShapeDtypeStruct((T, S), F32),
        scratch_shapes=[
            pltpu.VMEM((S, TQ), F32),
            pltpu.VMEM((16, TQ), F32),
            pltpu.VMEM((8, TQ), F32),
        ],
        compiler_params=_cparams(("parallel", "arbitrary")),
        name="dsa_index",
    )(ki_lo, ki_hi, qi, wt)


def _dsa_attn_kernel(q_ref, k_ref, v_ref, bias_ref, o_ref, *, S, TQ):
    j = pl.program_id(1)
    lane = lax.broadcasted_iota(jnp.int32, (TQ, LANES), 1)

    def variant(nk):
        def pair(p, carry):
            vp = v_ref[0, p, 0:nk, :]
            outs = []
            for e in range(2):
                h = 2 * p + e
                s = _dot_nt(q_ref[0, h], k_ref[0, h, 0:nk, :]) + bias_ref[:, 0:nk]
                m = s.max(axis=1, keepdims=True)
                pe = jnp.exp(s - m)
                l = pe.sum(axis=1, keepdims=True)
                outs.append(_dot(pe.astype(BF16), vp) * (1.0 / l))
            o_ref[0, p] = jnp.where(lane < V_HEAD, outs[0], outs[1]).astype(BF16)
            return carry
        lax.fori_loop(0, N_HEADS // 2, pair, 0)

    for jj in range(S // TQ):
        @pl.when(j == jj)
        def _(jj=jj):
            variant((jj + 1) * TQ)


def _dsa_attn(q, k, v, bias, *, B, S, TQ):
    nq = S // TQ
    kern = functools.partial(_dsa_attn_kernel, S=S, TQ=TQ)
    return pl.pallas_call(
        kern,
        grid=(B, nq),
        in_specs=[
            pl.BlockSpec((1, N_HEADS, TQ, LANES), lambda b, j: (b, 0, j, 0)),
            pl.BlockSpec((1, N_HEADS, S, LANES), lambda b, j: (b, 0, 0, 0)),
            pl.BlockSpec((1, N_HEADS // 2, S, LANES), lambda b, j: (b, 0, 0, 0)),
            pl.BlockSpec((TQ, S), lambda b, j: (b * nq + j, 0)),
        ],
        out_specs=pl.BlockSpec((1, N_HEADS // 2, TQ, LANES), lambda b, j: (b, 0, j, 0)),
        out_shape=jax.ShapeDtypeStruct((B, N_HEADS // 2, S, LANES), BF16),
        compiler_params=_cparams(("parallel", "arbitrary")),
        name="dsa_attn",
    )(q, k, v, bias)


def _sgu_kernel(z_ref, g_ref, ws_ref, bt_ref, y_ref, *, tm):
    row = lax.broadcasted_iota(jnp.int32, (SGU_CHUNK, SGU_CHUNK), 0)
    col = lax.broadcasted_iota(jnp.int32, (SGU_CHUNK, SGU_CHUNK), 1)
    tril = col <= row
    w = [jnp.where(tril, ws_ref[g], 0.0).astype(BF16) for g in range(SGU_GROUPS)]
    for cc in range(tm // SGU_CHUNK):
        rows = slice(cc * SGU_CHUNK, (cc + 1) * SGU_CHUNK)
        vn = _rms(z_ref[rows, SGU_WIDTH:2 * SGU_WIDTH].astype(F32), g_ref[...]).astype(BF16)
        for g in range(SGU_GROUPS):
            cols = slice(g * SGU_GROUP_DIM, (g + 1) * SGU_GROUP_DIM)
            mixed = _dot(w[g], vn[:, cols]) + bt_ref[:, g:g + 1]
            y_ref[rows, cols] = (z_ref[rows, cols].astype(F32) * mixed).astype(BF16)


def _sgu(z, g_sgu, w_spatial, b_t, *, tm):
    T = z.shape[0]
    kern = functools.partial(_sgu_kernel, tm=tm)
    return pl.pallas_call(
        kern,
        grid=(T // tm,),
        in_specs=[
            pl.BlockSpec((tm, 2 * SGU_WIDTH), lambda i: (i, 0)),
            pl.BlockSpec((1, SGU_WIDTH), lambda i: (0, 0)),
            pl.BlockSpec((SGU_GROUPS, SGU_CHUNK, SGU_CHUNK), lambda i: (0, 0, 0)),
            pl.BlockSpec((SGU_CHUNK, SGU_GROUPS), lambda i: (0, 0)),
        ],
        out_specs=pl.BlockSpec((tm, SGU_WIDTH), lambda i: (i, 0)),
        out_shape=jax.ShapeDtypeStruct((T, SGU_WIDTH), BF16),
        compiler_params=_cparams(("parallel",)),
        name="sgu",
    )(z, g_sgu, w_spatial, b_t)


def _merge_kernel(o_ref, y_ref, ga_ref, gb_ref, wa_ref, wb_ref, out_ref, wa_bf, wb_bf):
    @pl.when(pl.program_id(1) == 0)
    def _():
        wa_bf[...] = wa_ref[...].astype(BF16)
        wb_bf[...] = wb_ref[...].astype(BF16)

    o_a = jnp.concatenate([o_ref[0, p] for p in range(N_HEADS // 2)], axis=1)
    ya = _dot(o_a, wa_bf[...])
    yb = _dot(y_ref[...], wb_bf[...])
    out_ref[...] = (ga_ref[...].astype(F32) * ya + gb_ref[...].astype(F32) * yb).astype(BF16)


def _merge(o, y_sgu, gates, wa, wb, *, B, S, tm, tn):
    T = B * S
    D = wa.shape[1]
    nt = S // tm
    nn = D // tn
    return pl.pallas_call(
        _merge_kernel,
        grid=(nn, T // tm),
        in_specs=[
            pl.BlockSpec((1, N_HEADS // 2, tm, LANES), lambda j, i: (i // nt, 0, i % nt, 0)),
            pl.BlockSpec((tm, SGU_WIDTH), lambda j, i: (i, 0)),
            pl.BlockSpec((tm, tn), lambda j, i: (i, j)),
            pl.BlockSpec((tm, tn), lambda j, i: (i, nn + j)),
            pl.BlockSpec((wa.shape[0], tn), lambda j, i: (0, j)),
            pl.BlockSpec((wb.shape[0], tn), lambda j, i: (0, j)),
        ],
        out_specs=pl.BlockSpec((tm, tn), lambda j, i: (i, j)),
        out_shape=jax.ShapeDtypeStruct((T, D), BF16),
        scratch_shapes=[pltpu.VMEM((wa.shape[0], tn), BF16), pltpu.VMEM((wb.shape[0], tn), BF16)],
        compiler_params=_cparams(("arbitrary", "arbitrary")),
        name="merge",
    )(o, y_sgu, gates, gates, wa, wb)


def _o_proj_kernel(x_ref, m_ref, w_ref, g_ref, h_ref, n_ref, w_bf):
    @pl.when(pl.program_id(0) == 0)
    def _():
        w_bf[...] = w_ref[...].astype(BF16)

    h = x_ref[...] + _dot(m_ref[...], w_bf[...])
    h_ref[...] = h
    n_ref[...] = _rms(h, g_ref[...]).astype(BF16)


def _o_proj(x2, merged, w_o, g_ffn, *, tm):
    T, D = x2.shape
    return pl.pallas_call(
        _o_proj_kernel,
        grid=(T // tm,),
        in_specs=[
            pl.BlockSpec((tm, D), lambda i: (i, 0)),
            pl.BlockSpec((tm, D), lambda i: (i, 0)),
            pl.BlockSpec((D, D), lambda i: (0, 0), pipeline_mode=pl.Buffered(1)),
            pl.BlockSpec((1, D), lambda i: (0, 0)),
        ],
        out_specs=[pl.BlockSpec((tm, D), lambda i: (i, 0)), pl.BlockSpec((tm, D), lambda i: (i, 0))],
        out_shape=[jax.ShapeDtypeStruct((T, D), F32), jax.ShapeDtypeStruct((T, D), BF16)],
        scratch_shapes=[pltpu.VMEM((D, D), BF16)],
        compiler_params=_cparams(("arbitrary",)),
        name="o_proj",
    )(x2, merged, w_o, g_ffn)


def _ffn_up_kernel(n_ref, wg_ref, wu_ref, a_ref, wg_bf, wu_bf):
    @pl.when(pl.program_id(1) == 0)
    def _():
        wg_bf[...] = wg_ref[...].astype(BF16)
        wu_bf[...] = wu_ref[...].astype(BF16)

    n = n_ref[...]
    g = _dot(n, wg_bf[...])
    u = _dot(n, wu_bf[...])
    a_ref[...] = (g * _sigmoid(g) * u).astype(BF16)


def _ffn_up(n2, w_gu, *, d_ff, tm, tn):
    T, D = n2.shape
    nn = d_ff // tn
    return pl.pallas_call(
        _ffn_up_kernel,
        grid=(nn, T // tm),
        in_specs=[
            pl.BlockSpec((tm, D), lambda j, i: (i, 0)),
            pl.BlockSpec((D, tn), lambda j, i: (0, j)),
            pl.BlockSpec((D, tn), lambda j, i: (0, nn + j)),
        ],
        out_specs=pl.BlockSpec((tm, tn), lambda j, i: (i, j)),
        out_shape=jax.ShapeDtypeStruct((T, d_ff), BF16),
        scratch_shapes=[pltpu.VMEM((D, tn), BF16), pltpu.VMEM((D, tn), BF16)],
        compiler_params=_cparams(("arbitrary", "arbitrary")),
        name="ffn_up",
    )(n2, w_gu, w_gu)


def _ffn_down_kernel(a_ref, w_ref, h1_ref, h2_ref, w_bf):
    @pl.when(pl.program_id(1) == 0)
    def _():
        w_bf[...] = w_ref[...].astype(BF16)

    h2_ref[...] = h1_ref[...] + _dot(a_ref[...], w_bf[...])


def _ffn_down(act, w_down, h1, *, tm, tn):
    T, D = h1.shape
    d_ff = act.shape[1]
    return pl.pallas_call(
        _ffn_down_kernel,
        grid=(D // tn, T // tm),
        in_specs=[
            pl.BlockSpec((tm, d_ff), lambda j, i: (i, 0)),
            pl.BlockSpec((d_ff, tn), lambda j, i: (0, j)),
            pl.BlockSpec((tm, tn), lambda j, i: (i, j)),
        ],
        out_specs=pl.BlockSpec((tm, tn), lambda j, i: (i, j)),
        out_shape=jax.ShapeDtypeStruct((T, D), F32),
        scratch_shapes=[pltpu.VMEM((d_ff, tn), BF16)],
        compiler_params=_cparams(("arbitrary", "arbitrary")),
        name="ffn_down",
    )(act, w_down, h1)


def _ple_final_kernel(h2_ref, p_ref, wg_ref, wp_ref, gp_ref, gf_ref, out_ref, wg_bf, wp_bf):
    @pl.when(pl.program_id(0) == 0)
    def _():
        wg_bf[...] = wg_ref[...].astype(BF16)
        wp_bf[...] = wp_ref[...].astype(BF16)

    h2 = h2_ref[...]
    gate = _sigmoid(_dot(_rms(h2, gp_ref[...]).astype(BF16), wg_bf[...]))
    pp = _dot(p_ref[...].astype(BF16), wp_bf[...])
    out_ref[...] = _rms(h2 + gate * pp, gf_ref[...])


def _ple_final(h2, p2, w_pg, w_pp, g_ple, g_final, *, tm):
    T, D = h2.shape
    P = p2.shape[1]
    once = pl.Buffered(1)
    return pl.pallas_call(
        _ple_final_kernel,
        grid=(T // tm,),
        in_specs=[
            pl.BlockSpec((tm, D), lambda i: (i, 0)),
            pl.BlockSpec((tm, P), lambda i: (i, 0)),
            pl.BlockSpec((D, D), lambda i: (0, 0), pipeline_mode=once),
            pl.BlockSpec((P, D), lambda i: (0, 0), pipeline_mode=once),
            pl.BlockSpec((1, D), lambda i: (0, 0)),
            pl.BlockSpec((1, D), lambda i: (0, 0)),
        ],
        out_specs=pl.BlockSpec((tm, D), lambda i: (i, 0)),
        out_shape=jax.ShapeDtypeStruct((T, D), F32),
        scratch_shapes=[pltpu.VMEM((D, D), BF16), pltpu.VMEM((P, D), BF16)],
        compiler_params=_cparams(("arbitrary",)),
        name="ple_final",
    )(h2, p2, w_pg, w_pp, g_ple, g_final)


def _rope_tables(seq, dim):
    inv = ROPE_THETA ** (-jnp.arange(0, dim, 2, dtype=F32) / dim)
    ang = jnp.arange(seq, dtype=F32)[:, None] * inv[None, :]
    return jnp.cos(ang), jnp.sin(ang)


def _lane_tables(S):
    scale = (QK_NOPE + QK_ROPE) ** -0.5
    c, s = _rope_tables(S, QK_ROPE)
    z = lambda n: jnp.zeros((S, n), F32)
    one = lambda n: jnp.ones((S, n), F32)
    tq = jnp.stack([
        jnp.concatenate([one(QK_NOPE), c, c, z(32)], 1) * scale,
        jnp.concatenate([z(QK_NOPE), -s, z(16), z(32)], 1) * scale,
        jnp.concatenate([z(QK_NOPE), z(16), s, z(32)], 1) * scale,
    ])
    ci, si = _rope_tables(S, IDX_ROPE)
    half = lambda a, b, rest: jnp.concatenate([a, b, rest], 1)
    ti = jnp.stack([
        jnp.tile(half(ci, ci, one(IDX_DIM - IDX_ROPE)), (1, 2)),
        jnp.tile(half(-si, z(16), z(IDX_DIM - IDX_ROPE)), (1, 2)),
        jnp.tile(half(z(16), si, z(IDX_DIM - IDX_ROPE)), (1, 2)),
    ])
    tki = ti * (jnp.arange(LANES) < IDX_DIM).astype(F32)
    tk = jnp.stack([
        jnp.concatenate([c, c, z(96)], 1),
        jnp.concatenate([-s, z(16), z(96)], 1),
        jnp.concatenate([z(16), s, z(96)], 1),
    ])
    return tq, ti, tki, tk


def kernel(x, p, g_mix, w_in, g_cq, g_ckv, w_uq, w_uk, w_uv, w_iq, w_a_proj, g_sgu, w_spatial,
           b_spatial, w_b_proj, w_o, g_ffn, w_gu, w_down, g_ple, w_ple_gate, w_ple_proj, g_final):
    B, S, D = x.shape
    T = B * S
    depth = w_in.shape[0]
    d_ff = w_down.shape[1]
    topk = min(TOPK_MAX, S // 4)
    TQ = 256
    KCH = min(512, S)
    tm = min(512, S)
    tq, ti, tki, tk = _lane_tables(S)

    h = x.reshape(T, D)
    for i in range(depth):
        wq = jnp.pad(w_uq[i].reshape(Q_LORA, N_HEADS, QK_NOPE + QK_ROPE),
                     ((0, 0), (0, 0), (0, LANES - QK_NOPE - QK_ROPE))).reshape(Q_LORA, N_HEADS * LANES).astype(BF16)
        wk = jnp.pad(w_uk[i], ((0, 0), (0, 0), (0, LANES - QK_NOPE))).reshape(KV_LORA, N_HEADS * LANES).astype(BF16)
        wv = w_uv[i].reshape(KV_LORA, N_HEADS * V_HEAD).astype(BF16)
        wiq = w_iq[i].astype(BF16)

        small, z, gates = _in_proj(h, g_mix[i][None], jnp.transpose(w_in[i]), tm=min(1024, S), tn=512)
        q, k, v, qi, ki_lo, ki_hi, wt = _a_proj(small, g_cq[i][None], g_ckv[i][None], wq, wiq, wk, wv,
                                                 tq, ti, tki, tk, B=B, S=S, tm=tm)
        bias = _dsa_index(ki_lo, ki_hi, qi, wt, B=B, S=S, TQ=TQ, KCH=KCH, topk=topk)
        o = _dsa_attn(q, k, v, bias, B=B, S=S, TQ=TQ)
        y_sgu = _sgu(z, g_sgu[i][None], w_spatial[i], jnp.transpose(b_spatial[i]), tm=tm)
        merged = _merge(o, y_sgu, gates, w_a_proj[i], w_b_proj[i], B=B, S=S, tm=tm, tn=1024)
        h1, n2 = _o_proj(h, merged, w_o[i], g_ffn[i][None], tm=min(256, S))
        act = _ffn_up(n2, w_gu[i], d_ff=d_ff, tm=min(1024, S), tn=512)
        h2 = _ffn_down(act, w_down[i], h1, tm=tm, tn=512)
        assert depth == 1
        h = _ple_final(h2, p[i].reshape(T, -1), w_ple_gate[i], w_ple_proj[i], g_ple[i][None],
                       g_final[None], tm=min(256, S))
    return h.reshape(B, S, D)
```

```python
import functools
import math

import jax
import jax.numpy as jnp
from jax import lax
from jax.experimental import pallas as pl
from jax.experimental.pallas import tpu as pltpu

F32 = jnp.float32
BF16 = jnp.bfloat16

N_HEADS = 16
QK_NOPE = 64
QK_ROPE = 32
V_HEAD = 64
Q_LORA = 512
KV_LORA = 256
IDX_HEADS = 16
IDX_DIM = 64
IDX_ROPE = 32
TOPK_MAX = 256
SGU_CHUNK = 128
SGU_GROUPS = 8
SGU_GROUP_DIM = 128
SGU_WIDTH = SGU_GROUPS * SGU_GROUP_DIM
ROPE_THETA = 10000.0
EPS = 1e-6

LANES = 128
IN_SMALL = Q_LORA + KV_LORA + QK_ROPE + IDX_DIM + IDX_HEADS
SMALL_COLS = 1024
COL_CKV = Q_LORA
COL_MISC = Q_LORA + KV_LORA
MISC_KIDX = QK_ROPE
MISC_WIDX = QK_ROPE + IDX_DIM
VMEM_LIMIT_BYTES = 56 * 1024 * 1024
NEG_BIAS = -1e30
F32_MAX = 3.4028234663852886e38
STEPS_PER_CHECK = 4
COUNT_CHAINS = 4


def _cparams(semantics):
    return pltpu.CompilerParams(dimension_semantics=semantics, vmem_limit_bytes=VMEM_LIMIT_BYTES)


def _rms(x, g):
    return x * lax.rsqrt(jnp.mean(x * x, axis=-1, keepdims=True) + EPS) * g


def _dot(a, b):
    return jnp.dot(a, b, preferred_element_type=F32)


def _dot_nt(a, b):
    return lax.dot_general(a, b, (((1,), (1,)), ((), ())), preferred_element_type=F32)


def _gelu_exact(x):
    return 0.5 * x * (1.0 + lax.erf(x * (1.0 / math.sqrt(2.0))))


def _sigmoid(x):
    return 1.0 / (1.0 + jnp.exp(-x))


def _in_proj_kernel(x_ref, g_ref, w_ref, small_ref, z_ref, gate_ref, n_scr, *, n_small, n_z):
    j = pl.program_id(1)

    @pl.when(j == 0)
    def _():
        n_scr[...] = _rms(x_ref[...], g_ref[...]).astype(BF16)

    acc = _dot_nt(n_scr[...], w_ref[...].astype(BF16))

    @pl.when(j < n_small)
    def _():
        small_ref[...] = acc

    @pl.when((j >= n_small) & (j < n_small + n_z))
    def _():
        z_ref[...] = _gelu_exact(acc).astype(BF16)

    @pl.when(j >= n_small + n_z)
    def _():
        gate_ref[...] = _sigmoid(acc).astype(BF16)


def _in_proj(x2, g_mix, w_t, *, tm, tn):
    T, D = x2.shape
    n_small = SMALL_COLS // tn
    n_z = (2 * SGU_WIDTH) // tn
    n_gate = (2 * D) // tn
    row_z = IN_SMALL
    row_gate = IN_SMALL + 2 * SGU_WIDTH
    assert w_t.shape == (row_gate + 2 * D, D) and row_z % 8 == 0
    grid = (T // tm, n_small + n_z + n_gate)
    kern = functools.partial(_in_proj_kernel, n_small=n_small, n_z=n_z)

    def w_row(i, j):
        tiles = jnp.where(j < n_small, j * (tn // 8),
                          jnp.where(j < n_small + n_z, row_z // 8 + (j - n_small) * (tn // 8),
                                    row_gate // 8 + (j - n_small - n_z) * (tn // 8)))
        return tiles * 8

    return pl.pallas_call(
        kern,
        grid=grid,
        in_specs=[
            pl.BlockSpec((tm, D), lambda i, j: (i, 0)),
            pl.BlockSpec((1, D), lambda i, j: (0, 0)),
            pl.BlockSpec((pl.Element(tn), pl.Element(D)), lambda i, j: (w_row(i, j), 0)),
        ],
        out_specs=[
            pl.BlockSpec((tm, tn), lambda i, j: (i, jnp.minimum(j, n_small - 1))),
            pl.BlockSpec((tm, tn), lambda i, j: (i, jnp.clip(j - n_small, 0, n_z - 1))),
            pl.BlockSpec((tm, tn), lambda i, j: (i, jnp.clip(j - n_small - n_z, 0, n_gate - 1))),
        ],
        out_shape=[
            jax.ShapeDtypeStruct((T, SMALL_COLS), F32),
            jax.ShapeDtypeStruct((T, 2 * SGU_WIDTH), BF16),
            jax.ShapeDtypeStruct((T, 2 * D), BF16),
        ],
        scratch_shapes=[pltpu.VMEM((tm, D), BF16)],
        compiler_params=_cparams(("parallel", "arbitrary")),
        name="in_proj",
    )(x2, g_mix, w_t)


def _rope_block(x, cos, sin_a, sin_b):
    return x * cos + pltpu.roll(x, LANES - 16, 1) * sin_a + pltpu.roll(x, 16, 1) * sin_b


def _a_proj_kernel(small_ref, gcq_ref, gckv_ref, wq_ref, wiq_ref, wk_ref, wv_ref,
                   tq_ref, ti_ref, tki_ref, tk_ref,
                   q_ref, k_ref, v_ref, qi_ref, kilo_ref, kihi_ref, wt_ref):
    c_q = _rms(small_ref[:, 0:Q_LORA], gcq_ref[...]).astype(BF16)
    c_kv = _rms(small_ref[:, COL_CKV:COL_CKV + KV_LORA], gckv_ref[...]).astype(BF16)

    q = _dot(c_q, wq_ref[...])
    cq, sqa, sqb = tq_ref[0], tq_ref[1], tq_ref[2]
    for h in range(N_HEADS):
        blk = q[:, h * LANES:(h + 1) * LANES]
        q_ref[0, h] = _rope_block(blk, cq, sqa, sqb).astype(BF16)

    qi = _dot(c_q, wiq_ref[...])
    ci, sia, sib = ti_ref[0], ti_ref[1], ti_ref[2]
    for hp in range(IDX_HEADS // 2):
        blk = qi[:, hp * LANES:(hp + 1) * LANES]
        qi_ref[:, hp * LANES:(hp + 1) * LANES] = _rope_block(blk, ci, sia, sib).astype(BF16)

    misc = small_ref[:, COL_MISC:COL_MISC + LANES]
    ki_lo = _rope_block(pltpu.roll(misc, LANES - MISC_KIDX, 1), tki_ref[0], tki_ref[1], tki_ref[2])
    kilo_ref[...] = ki_lo.astype(BF16)
    kihi_ref[...] = pltpu.roll(ki_lo, IDX_DIM, 1).astype(BF16)

    ck, ska, skb = tk_ref[0], tk_ref[1], tk_ref[2]
    k_rope = pltpu.roll(_rope_block(misc, ck, ska, skb), QK_NOPE, 1)
    k_nope = _dot(c_kv, wk_ref[...])
    for h in range(N_HEADS):
        k_ref[0, h] = (k_nope[:, h * LANES:(h + 1) * LANES] + k_rope).astype(BF16)

    v = _dot(c_kv, wv_ref[...])
    for p in range(N_HEADS // 2):
        v_ref[0, p] = v[:, p * LANES:(p + 1) * LANES].astype(BF16)

    w_scale = IDX_HEADS ** -0.5 * IDX_DIM ** -0.5
    wt_ref[...] = misc.T[MISC_WIDX:MISC_WIDX + IDX_HEADS, :] * w_scale


def _a_proj(small, g_cq, g_ckv, wq, wiq, wk, wv, tq, ti, tki, tk, *, B, S, tm):
    T = B * S
    nt = S // tm
    const2 = lambda i: (0, 0)
    tab = pl.BlockSpec((3, tm, LANES), lambda i: (0, i % nt, 0))
    head_spec = lambda nh: pl.BlockSpec((1, nh, tm, LANES), lambda i: (i // nt, 0, i % nt, 0))
    return pl.pallas_call(
        _a_proj_kernel,
        grid=(T // tm,),
        in_specs=[
            pl.BlockSpec((tm, SMALL_COLS), lambda i: (i, 0)),
            pl.BlockSpec((1, Q_LORA), const2),
            pl.BlockSpec((1, KV_LORA), const2),
            pl.BlockSpec(wq.shape, const2),
            pl.BlockSpec(wiq.shape, const2),
            pl.BlockSpec(wk.shape, const2),
            pl.BlockSpec(wv.shape, const2),
            tab, tab, tab, tab,
        ],
        out_specs=[
            head_spec(N_HEADS),
            head_spec(N_HEADS),
            head_spec(N_HEADS // 2),
            pl.BlockSpec((tm, IDX_HEADS * IDX_DIM), lambda i: (i, 0)),
            pl.BlockSpec((tm, LANES), lambda i: (i, 0)),
            pl.BlockSpec((tm, LANES), lambda i: (i, 0)),
            pl.BlockSpec((IDX_HEADS, tm), lambda i: (0, i)),
        ],
        out_shape=[
            jax.ShapeDtypeStruct((B, N_HEADS, S, LANES), BF16),
            jax.ShapeDtypeStruct((B, N_HEADS, S, LANES), BF16),
            jax.ShapeDtypeStruct((B, N_HEADS // 2, S, LANES), BF16),
            jax.ShapeDtypeStruct((T, IDX_HEADS * IDX_DIM), BF16),
            jax.ShapeDtypeStruct((T, LANES), BF16),
            jax.ShapeDtypeStruct((T, LANES), BF16),
            jax.ShapeDtypeStruct((IDX_HEADS, T), F32),
        ],
        compiler_params=_cparams(("parallel",)),
        name="a_proj",
    )(small, g_cq, g_ckv, wq, wiq, wk, wv, tq, ti, tki, tk)


def _dsa_index_kernel(kilo_ref, kihi_ref, qi_ref, wt_ref, bias_ref, isc_ref, mm_ref, js_ref,
                      *, S, TQ, KCH, topk):
    j = pl.program_id(1)
    q0 = j * TQ
    nkeys = q0 + TQ
    qidx = q0 + lax.broadcasted_iota(jnp.int32, (1, TQ), 1)
    kf = float(topk)

    mm_ref[0:8, :] = jnp.full((8, TQ), jnp.inf, F32)
    mm_ref[8:16, :] = jnp.full((8, TQ), -jnp.inf, F32)
    for c in range(S // KCH):
        @pl.when(c * KCH < nkeys)
        def _(c=c):
            klo = kilo_ref[c * KCH:(c + 1) * KCH, :]
            khi = kihi_ref[c * KCH:(c + 1) * KCH, :]
            acc = jnp.zeros((KCH, TQ), F32)
            for hp in range(IDX_HEADS // 2):
                qp = qi_ref[:, hp * LANES:(hp + 1) * LANES]
                s0 = _dot_nt(klo, qp)
                s1 = _dot_nt(khi, qp)
                acc = acc + jnp.maximum(s0, 0.0) * wt_ref[2 * hp:2 * hp + 1, :]
                acc = acc + jnp.maximum(s1, 0.0) * wt_ref[2 * hp + 1:2 * hp + 2, :]
            kidx = c * KCH + lax.broadcasted_iota(jnp.int32, (KCH, TQ), 0)
            causal = kidx <= qidx
            isc_ref[c * KCH:(c + 1) * KCH, :] = jnp.where(causal, acc, -jnp.inf)
            lo_part = jnp.where(causal, acc, jnp.inf).reshape(KCH // 8, 8, TQ).min(axis=0)
            hi_part = jnp.where(causal, acc, -jnp.inf).reshape(KCH // 8, 8, TQ).max(axis=0)
            mm_ref[0:8, :] = jnp.minimum(mm_ref[0:8, :], lo_part)
            mm_ref[8:16, :] = jnp.maximum(mm_ref[8:16, :], hi_part)

    def select(nch):
        def count(pred):
            acc = jnp.zeros((COUNT_CHAINS, 8, TQ), F32)
            for c in range(nch):
                ones = jnp.where(pred(isc_ref[c * TQ:(c + 1) * TQ, :], c * TQ), 1.0, 0.0)
                acc = acc + ones.reshape(COUNT_CHAINS, TQ // (8 * COUNT_CHAINS), 8, TQ).sum(axis=1)
            return acc.sum(axis=0).sum(axis=0, keepdims=True)

        row_min = mm_ref[0:8, :].min(axis=0, keepdims=True)
        row_max = mm_ref[8:16, :].max(axis=0, keepdims=True)
        full = (qidx + 1) <= topk
        c_max = count(lambda blk, k0: blk >= row_max)
        exact0 = c_max == kf
        tie0 = c_max > kf
        settled0 = full | exact0 | tie0
        lo0 = jnp.where(full, -F32_MAX, jnp.where(settled0, row_max, row_min))
        hi0 = jnp.where(full, -F32_MAX, jnp.where(tie0, jnp.inf, row_max))
        act0 = jnp.where(settled0, 0.0, 1.0)

        def step(lo, hi, act):
            mid = lo * 0.5 + hi * 0.5
            inside = (mid > lo) & (mid < hi)
            cnt = count(lambda blk, k0: blk >= mid)
            upd = (act > 0.0) & inside
            found = upd & (cnt == kf)
            lo2 = jnp.where(upd & (cnt >= kf), mid, lo)
            hi2 = jnp.where(upd & (cnt <= kf), mid, hi)
            return lo2, hi2, jnp.where(upd & jnp.logical_not(found), 1.0, 0.0)

        def any_active(act):
            return (jnp.max(act) > 0.0).astype(jnp.int32)

        def body(st):
            lo, hi, act, _ = st
            for _ in range(STEPS_PER_CHECK):
                lo, hi, act = step(lo, hi, act)
            return lo, hi, act, any_active(act)

        lo, hi, _, _ = lax.while_loop(lambda st: st[3] > 0, body, (lo0, hi0, act0, any_active(act0)))

        tie = lo < hi
        js_ref[...] = jnp.full((8, TQ), -1.0, F32)

        @pl.when(jnp.max(jnp.where(tie, 1.0, 0.0)) > 0.0)
        def _():
            need = kf - count(lambda blk, k0: blk >= hi)

            def kpos(k0):
                return (k0 + lax.broadcasted_iota(jnp.int32, (TQ, TQ), 0)).astype(F32)

            def tie_step(_, st):
                ilo, ihi = st
                imid = jnp.floor((ilo + ihi) * 0.5)
                cnt = count(lambda blk, k0: (blk >= lo) & (blk < hi) & (kpos(k0) <= imid))
                ge = cnt >= need
                return jnp.where(ge, ilo, imid), jnp.where(ge, imid, ihi)

            nsteps = int(math.ceil(math.log2(S))) + 1
            _, ihi = lax.fori_loop(0, nsteps, tie_step,
                                   (jnp.full((1, TQ), -1.0, F32), jnp.full((1, TQ), S - 1.0, F32)))
            js_ref[0:1, :] = jnp.where(tie, ihi, -1.0)

        jstar = js_ref[0:1, :]

        for c in range(S // TQ):
            if c < nch:
                blk = isc_ref[c * TQ:(c + 1) * TQ, :]
                kpos_c = (c * TQ + lax.broadcasted_iota(jnp.int32, (TQ, TQ), 0)).astype(F32)
                sel = (blk >= hi) | ((blk >= lo) & (kpos_c <= jstar))
                bias_ref[:, c * TQ:(c + 1) * TQ] = jnp.where(sel, 0.0, NEG_BIAS).T
            else:
                bias_ref[:, c * TQ:(c + 1) * TQ] = jnp.full((TQ, TQ), NEG_BIAS, F32)

    for jj in range(S // TQ):
        @pl.when(j == jj)
        def _(jj=jj):
            select(jj + 1)


def _dsa_index(ki_lo, ki_hi, qi, wt, *, B, S, TQ, KCH, topk):
    T = B * S
    nq = S // TQ
    kern = functools.partial(_dsa_index_kernel, S=S, TQ=TQ, KCH=KCH, topk=topk)
    return pl.pallas_call(
        kern,
        grid=(B, nq),
        in_specs=[
            pl.BlockSpec((S, LANES), lambda b, j: (b, 0)),
            pl.BlockSpec((S, LANES), lambda b, j: (b, 0)),
            pl.BlockSpec((TQ, IDX_HEADS * IDX_DIM), lambda b, j: (b * nq + j, 0)),
            pl.BlockSpec((IDX_HEADS, TQ), lambda b, j: (0, b * nq + j)),
        ],
        out_specs=pl.BlockSpec((TQ, S), lambda b, j: (b * nq + j, 0)),
        out_shape=jax.ShapeDtypeStruct((T, S), F32),
        scratch_shapes=[
            pltpu.VMEM((S, TQ), F32),
            pltpu.VMEM((16, TQ), F32),
            pltpu.VMEM((8, TQ), F32),
        ],
        compiler_params=_cparams(("parallel", "arbitrary")),
        name="dsa_index",
    )(ki_lo, ki_hi, qi, wt)


def _dsa_attn_kernel(q_ref, k_ref, v_ref, bias_ref, o_ref, *, S, TQ):
    j = pl.program_id(1)
    lane = lax.broadcasted_iota(jnp.int32, (TQ, LANES), 1)

    def variant(nk):
        ones = jnp.ones((nk, LANES), BF16)

        def pair(p, carry):
            vp = jnp.concatenate([v_ref[0, p, 0:nk, :], ones], axis=1)
            outs = []
            for e in range(2):
                h = 2 * p + e
                s = _dot_nt(q_ref[0, h], k_ref[0, h, 0:nk, :]) + bias_ref[:, 0:nk]
                m = s.max(axis=1, keepdims=True)
                pv = _dot(jnp.exp2(s - m).astype(BF16), vp)
                outs.append(pv[:, 0:LANES] * (1.0 / pv[:, LANES:2 * LANES]))
            o_ref[0, p] = jnp.where(lane < V_HEAD, outs[0], outs[1]).astype(BF16)
            return carry
        lax.fori_loop(0, N_HEADS // 2, pair, 0, unroll=4)

    for jj in range(S // TQ):
        @pl.when(j == jj)
        def _(jj=jj):
            variant((jj + 1) * TQ)


def _dsa_attn(q, k, v, bias, *, B, S, TQ):
    nq = S // TQ
    kern = functools.partial(_dsa_attn_kernel, S=S, TQ=TQ)
    return pl.pallas_call(
        kern,
        grid=(B, nq),
        in_specs=[
            pl.BlockSpec((1, N_HEADS, TQ, LANES), lambda b, j: (b, 0, j, 0)),
            pl.BlockSpec((1, N_HEADS, S, LANES), lambda b, j: (b, 0, 0, 0)),
            pl.BlockSpec((1, N_HEADS // 2, S, LANES), lambda b, j: (b, 0, 0, 0)),
            pl.BlockSpec((TQ, S), lambda b, j: (b * nq + j, 0)),
        ],
        out_specs=pl.BlockSpec((1, N_HEADS // 2, TQ, LANES), lambda b, j: (b, 0, j, 0)),
        out_shape=jax.ShapeDtypeStruct((B, N_HEADS // 2, S, LANES), BF16),
        compiler_params=_cparams(("parallel", "arbitrary")),
        name="dsa_attn",
    )(q, k, v, bias)


def _sgu_kernel(z_ref, g_ref, ws_ref, bt_ref, y_ref, *, tm):
    row = lax.broadcasted_iota(jnp.int32, (SGU_CHUNK, SGU_CHUNK), 0)
    col = lax.broadcasted_iota(jnp.int32, (SGU_CHUNK, SGU_CHUNK), 1)
    tril = col <= row
    w = [jnp.where(tril, ws_ref[g], 0.0).astype(BF16) for g in range(SGU_GROUPS)]
    for cc in range(tm // SGU_CHUNK):
        rows = slice(cc * SGU_CHUNK, (cc + 1) * SGU_CHUNK)
        vn = _rms(z_ref[rows, SGU_WIDTH:2 * SGU_WIDTH].astype(F32), g_ref[...]).astype(BF16)
        for g in range(SGU_GROUPS):
            cols = slice(g * SGU_GROUP_DIM, (g + 1) * SGU_GROUP_DIM)
            mixed = _dot(w[g], vn[:, cols]) + bt_ref[:, g:g + 1]
            y_ref[rows, cols] = (z_ref[rows, cols].astype(F32) * mixed).astype(BF16)


def _sgu(z, g_sgu, w_spatial, b_t, *, tm):
    T = z.shape[0]
    kern = functools.partial(_sgu_kernel, tm=tm)
    return pl.pallas_call(
        kern,
        grid=(T // tm,),
        in_specs=[
            pl.BlockSpec((tm, 2 * SGU_WIDTH), lambda i: (i, 0)),
            pl.BlockSpec((1, SGU_WIDTH), lambda i: (0, 0)),
            pl.BlockSpec((SGU_GROUPS, SGU_CHUNK, SGU_CHUNK), lambda i: (0, 0, 0)),
            pl.BlockSpec((SGU_CHUNK, SGU_GROUPS), lambda i: (0, 0)),
        ],
        out_specs=pl.BlockSpec((tm, SGU_WIDTH), lambda i: (i, 0)),
        out_shape=jax.ShapeDtypeStruct((T, SGU_WIDTH), BF16),
        compiler_params=_cparams(("parallel",)),
        name="sgu",
    )(z, g_sgu, w_spatial, b_t)


def _merge_kernel(o_ref, y_ref, ga_ref, gb_ref, wa_ref, wb_ref, out_ref, wa_bf, wb_bf):
    @pl.when(pl.program_id(1) == 0)
    def _():
        wa_bf[...] = wa_ref[...].astype(BF16)
        wb_bf[...] = wb_ref[...].astype(BF16)

    o_a = jnp.concatenate([o_ref[0, p] for p in range(N_HEADS // 2)], axis=1)
    ya = _dot(o_a, wa_bf[...])
    yb = _dot(y_ref[...], wb_bf[...])
    out_ref[...] = (ga_ref[...].astype(F32) * ya + gb_ref[...].astype(F32) * yb).astype(BF16)


def _merge(o, y_sgu, gates, wa, wb, *, B, S, tm, tn):
    T = B * S
    D = wa.shape[1]
    nt = S // tm
    nn = D // tn
    return pl.pallas_call(
        _merge_kernel,
        grid=(nn, T // tm),
        in_specs=[
            pl.BlockSpec((1, N_HEADS // 2, tm, LANES), lambda j, i: (i // nt, 0, i % nt, 0)),
            pl.BlockSpec((tm, SGU_WIDTH), lambda j, i: (i, 0)),
            pl.BlockSpec((tm, tn), lambda j, i: (i, j)),
            pl.BlockSpec((tm, tn), lambda j, i: (i, nn + j)),
            pl.BlockSpec((wa.shape[0], tn), lambda j, i: (0, j)),
            pl.BlockSpec((wb.shape[0], tn), lambda j, i: (0, j)),
        ],
        out_specs=pl.BlockSpec((tm, tn), lambda j, i: (i, j)),
        out_shape=jax.ShapeDtypeStruct((T, D), BF16),
        scratch_shapes=[pltpu.VMEM((wa.shape[0], tn), BF16), pltpu.VMEM((wb.shape[0], tn), BF16)],
        compiler_params=_cparams(("arbitrary", "arbitrary")),
        name="merge",
    )(o, y_sgu, gates, gates, wa, wb)


def _o_proj_kernel(x_ref, m_ref, w_ref, g_ref, h_ref, n_ref, w_bf):
    @pl.when(pl.program_id(0) == 0)
    def _():
        w_bf[...] = w_ref[...].astype(BF16)

    h = x_ref[...] + _dot(m_ref[...], w_bf[...])
    h_ref[...] = h
    n_ref[...] = _rms(h, g_ref[...]).astype(BF16)


def _o_proj(x2, merged, w_o, g_ffn, *, tm):
    T, D = x2.shape
    return pl.pallas_call(
        _o_proj_kernel,
        grid=(T // tm,),
        in_specs=[
            pl.BlockSpec((tm, D), lambda i: (i, 0)),
            pl.BlockSpec((tm, D), lambda i: (i, 0)),
            pl.BlockSpec((D, D), lambda i: (0, 0), pipeline_mode=pl.Buffered(1)),
            pl.BlockSpec((1, D), lambda i: (0, 0)),
        ],
        out_specs=[pl.BlockSpec((tm, D), lambda i: (i, 0)), pl.BlockSpec((tm, D), lambda i: (i, 0))],
        out_shape=[jax.ShapeDtypeStruct((T, D), F32), jax.ShapeDtypeStruct((T, D), BF16)],
        scratch_shapes=[pltpu.VMEM((D, D), BF16)],
        compiler_params=_cparams(("arbitrary",)),
        name="o_proj",
    )(x2, merged, w_o, g_ffn)


def _ffn_up_kernel(n_ref, wg_ref, wu_ref, a_ref, wg_bf, wu_bf):
    @pl.when(pl.program_id(1) == 0)
    def _():
        wg_bf[...] = wg_ref[...].astype(BF16)
        wu_bf[...] = wu_ref[...].astype(BF16)

    n = n_ref[...]
    g = _dot(n, wg_bf[...])
    u = _dot(n, wu_bf[...])
    a_ref[...] = (g * _sigmoid(g) * u).astype(BF16)


def _ffn_up(n2, w_gu, *, d_ff, tm, tn):
    T, D = n2.shape
    nn = d_ff // tn
    return pl.pallas_call(
        _ffn_up_kernel,
        grid=(nn, T // tm),
        in_specs=[
            pl.BlockSpec((tm, D), lambda j, i: (i, 0)),
            pl.BlockSpec((D, tn), lambda j, i: (0, j)),
            pl.BlockSpec((D, tn), lambda j, i: (0, nn + j)),
        ],
        out_specs=pl.BlockSpec((tm, tn), lambda j, i: (i, j)),
        out_shape=jax.ShapeDtypeStruct((T, d_ff), BF16),
        scratch_shapes=[pltpu.VMEM((D, tn), BF16), pltpu.VMEM((D, tn), BF16)],
        compiler_params=_cparams(("arbitrary", "arbitrary")),
        name="ffn_up",
    )(n2, w_gu, w_gu)


def _ffn_down_kernel(a_ref, w_ref, h1_ref, h2_ref, w_bf):
    @pl.when(pl.program_id(1) == 0)
    def _():
        w_bf[...] = w_ref[...].astype(BF16)

    h2_ref[...] = h1_ref[...] + _dot(a_ref[...], w_bf[...])


def _ffn_down(act, w_down, h1, *, tm, tn):
    T, D = h1.shape
    d_ff = act.shape[1]
    return pl.pallas_call(
        _ffn_down_kernel,
        grid=(D // tn, T // tm),
        in_specs=[
            pl.BlockSpec((tm, d_ff), lambda j, i: (i, 0)),
            pl.BlockSpec((d_ff, tn), lambda j, i: (0, j)),
            pl.BlockSpec((tm, tn), lambda j, i: (i, j)),
        ],
        out_specs=pl.BlockSpec((tm, tn), lambda j, i: (i, j)),
        out_shape=jax.ShapeDtypeStruct((T, D), F32),
        scratch_shapes=[pltpu.VMEM((d_ff, tn), BF16)],
        compiler_params=_cparams(("arbitrary", "arbitrary")),
        name="ffn_down",
    )(act, w_down, h1)


def _ple_final_kernel(h2_ref, p_ref, wg_ref, wp_ref, gp_ref, gf_ref, out_ref, wg_bf, wp_bf):
    @pl.when(pl.program_id(0) == 0)
    def _():
        wg_bf[...] = wg_ref[...].astype(BF16)
        wp_bf[...] = wp_ref[...].astype(BF16)

    h2 = h2_ref[...]
    gate = _sigmoid(_dot(_rms(h2, gp_ref[...]).astype(BF16), wg_bf[...]))
    pp = _dot(p_ref[...].astype(BF16), wp_bf[...])
    out_ref[...] = _rms(h2 + gate * pp, gf_ref[...])


def _ple_final(h2, p2, w_pg, w_pp, g_ple, g_final, *, tm):
    T, D = h2.shape
    P = p2.shape[1]
    once = pl.Buffered(1)
    return pl.pallas_call(
        _ple_final_kernel,
        grid=(T // tm,),
        in_specs=[
            pl.BlockSpec((tm, D), lambda i: (i, 0)),
            pl.BlockSpec((tm, P), lambda i: (i, 0)),
            pl.BlockSpec((D, D), lambda i: (0, 0), pipeline_mode=once),
            pl.BlockSpec((P, D), lambda i: (0, 0), pipeline_mode=once),
            pl.BlockSpec((1, D), lambda i: (0, 0)),
            pl.BlockSpec((1, D), lambda i: (0, 0)),
        ],
        out_specs=pl.BlockSpec((tm, D), lambda i: (i, 0)),
        out_shape=jax.ShapeDtypeStruct((T, D), F32),
        scratch_shapes=[pltpu.VMEM((D, D), BF16), pltpu.VMEM((P, D), BF16)],
        compiler_params=_cparams(("arbitrary",)),
        name="ple_final",
    )(h2, p2, w_pg, w_pp, g_ple, g_final)


def _rope_tables(seq, dim):
    inv = ROPE_THETA ** (-jnp.arange(0, dim, 2, dtype=F32) / dim)
    ang = jnp.arange(seq, dtype=F32)[:, None] * inv[None, :]
    return jnp.cos(ang), jnp.sin(ang)


def _lane_tables(S):
    scale = (QK_NOPE + QK_ROPE) ** -0.5 * math.log2(math.e)
    c, s = _rope_tables(S, QK_ROPE)
    z = lambda n: jnp.zeros((S, n), F32)
    one = lambda n: jnp.ones((S, n), F32)
    tq = jnp.stack([
        jnp.concatenate([one(QK_NOPE), c, c, z(32)], 1) * scale,
        jnp.concatenate([z(QK_NOPE), -s, z(16), z(32)], 1) * scale,
        jnp.concatenate([z(QK_NOPE), z(16), s, z(32)], 1) * scale,
    ])
    ci, si = _rope_tables(S, IDX_ROPE)
    half = lambda a, b, rest: jnp.concatenate([a, b, rest], 1)
    ti = jnp.stack([
        jnp.tile(half(ci, ci, one(IDX_DIM - IDX_ROPE)), (1, 2)),
        jnp.tile(half(-si, z(16), z(IDX_DIM - IDX_ROPE)), (1, 2)),
        jnp.tile(half(z(16), si, z(IDX_DIM - IDX_ROPE)), (1, 2)),
    ])
    tki = ti * (jnp.arange(LANES) < IDX_DIM).astype(F32)
    tk = jnp.stack([
        jnp.concatenate([c, c, z(96)], 1),
        jnp.concatenate([-s, z(16), z(96)], 1),
        jnp.concatenate([z(16), s, z(96)], 1),
    ])
    return tq, ti, tki, tk


def kernel(x, p, g_mix, w_in, g_cq, g_ckv, w_uq, w_uk, w_uv, w_iq, w_a_proj, g_sgu, w_spatial,
           b_spatial, w_b_proj, w_o, g_ffn, w_gu, w_down, g_ple, w_ple_gate, w_ple_proj, g_final):
    B, S, D = x.shape
    T = B * S
    depth = w_in.shape[0]
    d_ff = w_down.shape[1]
    topk = min(TOPK_MAX, S // 4)
    TQ = 256
    KCH = min(512, S)
    tm = min(512, S)
    tq, ti, tki, tk = _lane_tables(S)

    h = x.reshape(T, D)
    for i in range(depth):
        wq = jnp.pad(w_uq[i].reshape(Q_LORA, N_HEADS, QK_NOPE + QK_ROPE),
                     ((0, 0), (0, 0), (0, LANES - QK_NOPE - QK_ROPE))).reshape(Q_LORA, N_HEADS * LANES).astype(BF16)
        wk = jnp.pad(w_uk[i], ((0, 0), (0, 0), (0, LANES - QK_NOPE))).reshape(KV_LORA, N_HEADS * LANES).astype(BF16)
        wv = w_uv[i].reshape(KV_LORA, N_HEADS * V_HEAD).astype(BF16)
        wiq = w_iq[i].astype(BF16)

        small, z, gates = _in_proj(h, g_mix[i][None], jnp.transpose(w_in[i]), tm=min(1024, S), tn=512)
        q, k, v, qi, ki_lo, ki_hi, wt = _a_proj(small, g_cq[i][None], g_ckv[i][None], wq, wiq, wk, wv,
                                                 tq, ti, tki, tk, B=B, S=S, tm=tm)
        bias = _dsa_index(ki_lo, ki_hi, qi, wt, B=B, S=S, TQ=TQ, KCH=KCH, topk=topk)
        o = _dsa_attn(q, k, v, bias, B=B, S=S, TQ=TQ)
        y_sgu = _sgu(z, g_sgu[i][None], w_spatial[i], jnp.transpose(b_spatial[i]), tm=tm)
        merged = _merge(o, y_sgu, gates, w_a_proj[i], w_b_proj[i], B=B, S=S, tm=tm, tn=1024)
        h1, n2 = _o_proj(h, merged, w_o[i], g_ffn[i][None], tm=min(256, S))
        act = _ffn_up(n2, w_gu[i], d_ff=d_ff, tm=min(1024, S), tn=512)
        h2 = _ffn_down(act, w_down[i], h1, tm=tm, tn=512)
        assert depth == 1
        h = _ple_final(h2, p[i].reshape(T, -1), w_ple_gate[i], w_ple_proj[i], g_ple[i][None],
                       g_final[None], tm=min(256, S))
    return h.reshape(B, S, D)
```

```python
import functools
import math

import jax
import jax.numpy as jnp
from jax import lax
from jax.experimental import pallas as pl
from jax.experimental.pallas import tpu as pltpu

F32 = jnp.float32
BF16 = jnp.bfloat16

N_HEADS = 16
QK_NOPE = 64
QK_ROPE = 32
V_HEAD = 64
Q_LORA = 512
KV_LORA = 256
IDX_HEADS = 16
IDX_DIM = 64
IDX_ROPE = 32
TOPK_MAX = 256
SGU_CHUNK = 128
SGU_GROUPS = 8
SGU_GROUP_DIM = 128
SGU_WIDTH = SGU_GROUPS * SGU_GROUP_DIM
ROPE_THETA = 10000.0
EPS = 1e-6

LANES = 128
IN_SMALL = Q_LORA + KV_LORA + QK_ROPE + IDX_DIM + IDX_HEADS
SMALL_COLS = 1024
COL_CKV = Q_LORA
COL_MISC = Q_LORA + KV_LORA
MISC_KIDX = QK_ROPE
MISC_WIDX = QK_ROPE + IDX_DIM
VMEM_LIMIT_BYTES = 56 * 1024 * 1024
NEG_BIAS = -1e30
F32_MAX = 3.4028234663852886e38
STEPS_PER_CHECK = 4
COUNT_CHAINS = 4


def _cparams(semantics):
    return pltpu.CompilerParams(dimension_semantics=semantics, vmem_limit_bytes=VMEM_LIMIT_BYTES)


def _rms(x, g):
    return x * lax.rsqrt(jnp.mean(x * x, axis=-1, keepdims=True) + EPS) * g


def _dot(a, b):
    return jnp.dot(a, b, preferred_element_type=F32)


def _dot_nt(a, b):
    return lax.dot_general(a, b, (((1,), (1,)), ((), ())), preferred_element_type=F32)


def _gelu_exact(x):
    return 0.5 * x * (1.0 + lax.erf(x * (1.0 / math.sqrt(2.0))))


def _sigmoid(x):
    return 1.0 / (1.0 + jnp.exp(-x))


def _in_proj_kernel(x_ref, g_ref, w_ref, small_ref, z_ref, gate_ref, n_scr, *, n_small, n_z):
    j = pl.program_id(1)

    @pl.when(j == 0)
    def _():
        n_scr[...] = _rms(x_ref[...], g_ref[...]).astype(BF16)

    def proj():
        return _dot_nt(n_scr[...], w_ref[...].astype(BF16))

    @pl.when(j < n_small)
    def _():
        small_ref[...] = proj()

    @pl.when((j >= n_small) & (j < n_small + n_z))
    def _():
        z_ref[...] = _gelu_exact(proj()).astype(BF16)

    @pl.when(j >= n_small + n_z)
    def _():
        gate_ref[...] = _sigmoid(proj()).astype(BF16)


def _in_proj(x2, g_mix, w_t, *, tm, tn):
    T, D = x2.shape
    n_small = SMALL_COLS // tn
    n_z = (2 * SGU_WIDTH) // tn
    n_gate = (2 * D) // tn
    row_z = IN_SMALL
    row_gate = IN_SMALL + 2 * SGU_WIDTH
    assert w_t.shape == (row_gate + 2 * D, D) and row_z % 8 == 0
    grid = (T // tm, n_small + n_z + n_gate)
    kern = functools.partial(_in_proj_kernel, n_small=n_small, n_z=n_z)

    def w_row(i, j):
        tiles = jnp.where(j < n_small, j * (tn // 8),
                          jnp.where(j < n_small + n_z, row_z // 8 + (j - n_small) * (tn // 8),
                                    row_gate // 8 + (j - n_small - n_z) * (tn // 8)))
        return tiles * 8

    return pl.pallas_call(
        kern,
        grid=grid,
        in_specs=[
            pl.BlockSpec((tm, D), lambda i, j: (i, 0)),
            pl.BlockSpec((1, D), lambda i, j: (0, 0)),
            pl.BlockSpec((pl.Element(tn), pl.Element(D)), lambda i, j: (w_row(i, j), 0)),
        ],
        out_specs=[
            pl.BlockSpec((tm, tn), lambda i, j: (i, jnp.minimum(j, n_small - 1))),
            pl.BlockSpec((tm, tn), lambda i, j: (i, jnp.clip(j - n_small, 0, n_z - 1))),
            pl.BlockSpec((tm, tn), lambda i, j: (i, jnp.clip(j - n_small - n_z, 0, n_gate - 1))),
        ],
        out_shape=[
            jax.ShapeDtypeStruct((T, SMALL_COLS), F32),
            jax.ShapeDtypeStruct((T, 2 * SGU_WIDTH), BF16),
            jax.ShapeDtypeStruct((T, 2 * D), BF16),
        ],
        scratch_shapes=[pltpu.VMEM((tm, D), BF16)],
        compiler_params=_cparams(("parallel", "arbitrary")),
        name="in_proj",
    )(x2, g_mix, w_t)


def _rope_block(x, cos, sin_a, sin_b):
    return x * cos + pltpu.roll(x, LANES - 16, 1) * sin_a + pltpu.roll(x, 16, 1) * sin_b


def _a_proj_kernel(small_ref, gcq_ref, gckv_ref, wq_ref, wiq_ref, wiqr_ref, wk_ref, wv_ref,
                   tq_ref, ti_ref, tki_ref, tk_ref,
                   q_ref, k_ref, v_ref, qi_ref, kilo_ref, kihi_ref, wt_ref):
    c_q = _rms(small_ref[:, 0:Q_LORA], gcq_ref[...]).astype(BF16)
    c_kv = _rms(small_ref[:, COL_CKV:COL_CKV + KV_LORA], gckv_ref[...]).astype(BF16)

    q = _dot(c_q, wq_ref[...])
    cq, sq = tq_ref[0], tq_ref[1]
    for h in range(N_HEADS):
        blk = q[:, h * LANES:(h + 1) * LANES]
        q_ref[0, h] = (blk * cq + pltpu.roll(blk, LANES - QK_ROPE, 1) * sq).astype(BF16)

    ci, si = ti_ref[0], ti_ref[1]
    qi = _dot(c_q, wiq_ref[...])
    qir = _dot(c_q, wiqr_ref[...])
    for hp in range(IDX_HEADS // 2):
        cols = slice(hp * LANES, (hp + 1) * LANES)
        qi_ref[:, cols] = (qi[:, cols] * ci + qir[:, cols] * si).astype(BF16)

    misc = small_ref[:, COL_MISC:COL_MISC + LANES]
    ki_lo = _rope_block(pltpu.roll(misc, LANES - MISC_KIDX, 1), tki_ref[0], tki_ref[1], tki_ref[2])
    kilo_ref[...] = ki_lo.astype(BF16)
    kihi_ref[...] = pltpu.roll(ki_lo, IDX_DIM, 1).astype(BF16)

    ck, ska, skb = tk_ref[0], tk_ref[1], tk_ref[2]
    k_rope = pltpu.roll(_rope_block(misc, ck, ska, skb), QK_NOPE, 1)
    k_nope = _dot(c_kv, wk_ref[...])
    for h in range(N_HEADS):
        k_ref[0, h] = (k_nope[:, h * LANES:(h + 1) * LANES] + k_rope).astype(BF16)

    v = _dot(c_kv, wv_ref[...])
    for p in range(N_HEADS // 2):
        v_ref[0, p] = v[:, p * LANES:(p + 1) * LANES].astype(BF16)

    w_scale = IDX_HEADS ** -0.5 * IDX_DIM ** -0.5
    wt_ref[...] = misc.T[MISC_WIDX:MISC_WIDX + IDX_HEADS, :] * w_scale


def _a_proj(small, g_cq, g_ckv, wq, wiq, wiqr, wk, wv, tq, ti, tki, tk, *, B, S, tm):
    T = B * S
    nt = S // tm
    const2 = lambda i: (0, 0)
    tab = lambda n: pl.BlockSpec((n, tm, LANES), lambda i: (0, i % nt, 0))
    head_spec = lambda nh: pl.BlockSpec((1, nh, tm, LANES), lambda i: (i // nt, 0, i % nt, 0))
    return pl.pallas_call(
        _a_proj_kernel,
        grid=(T // tm,),
        in_specs=[
            pl.BlockSpec((tm, SMALL_COLS), lambda i: (i, 0)),
            pl.BlockSpec((1, Q_LORA), const2),
            pl.BlockSpec((1, KV_LORA), const2),
            pl.BlockSpec(wq.shape, const2),
            pl.BlockSpec(wiq.shape, const2),
            pl.BlockSpec(wiqr.shape, const2),
            pl.BlockSpec(wk.shape, const2),
            pl.BlockSpec(wv.shape, const2),
            tab(2), tab(2), tab(3), tab(3),
        ],
        out_specs=[
            head_spec(N_HEADS),
            head_spec(N_HEADS),
            head_spec(N_HEADS // 2),
            pl.BlockSpec((tm, IDX_HEADS * IDX_DIM), lambda i: (i, 0)),
            pl.BlockSpec((tm, LANES), lambda i: (i, 0)),
            pl.BlockSpec((tm, LANES), lambda i: (i, 0)),
            pl.BlockSpec((IDX_HEADS, tm), lambda i: (0, i)),
        ],
        out_shape=[
            jax.ShapeDtypeStruct((B, N_HEADS, S, LANES), BF16),
            jax.ShapeDtypeStruct((B, N_HEADS, S, LANES), BF16),
            jax.ShapeDtypeStruct((B, N_HEADS // 2, S, LANES), BF16),
            jax.ShapeDtypeStruct((T, IDX_HEADS * IDX_DIM), BF16),
            jax.ShapeDtypeStruct((T, LANES), BF16),
            jax.ShapeDtypeStruct((T, LANES), BF16),
            jax.ShapeDtypeStruct((IDX_HEADS, T), F32),
        ],
        compiler_params=_cparams(("parallel",)),
        name="a_proj",
    )(small, g_cq, g_ckv, wq, wiq, wiqr, wk, wv, tq, ti, tki, tk)


def _dsa_index_kernel(kilo_ref, kihi_ref, qi_ref, wt_ref, bias_ref, isc_ref, mm_ref, js_ref,
                      *, S, TQ, KCH, topk):
    j = pl.program_id(1)
    q0 = j * TQ
    nkeys = q0 + TQ
    qidx = q0 + lax.broadcasted_iota(jnp.int32, (1, TQ), 1)
    kf = float(topk)

    mm_ref[0:8, :] = jnp.full((8, TQ), jnp.inf, F32)
    mm_ref[8:16, :] = jnp.full((8, TQ), -jnp.inf, F32)
    for c in range(S // KCH):
        @pl.when(c * KCH < nkeys)
        def _(c=c):
            klo = kilo_ref[c * KCH:(c + 1) * KCH, :]
            khi = kihi_ref[c * KCH:(c + 1) * KCH, :]
            acc = jnp.zeros((KCH, TQ), F32)
            for hp in range(IDX_HEADS // 2):
                qp = qi_ref[:, hp * LANES:(hp + 1) * LANES]
                s0 = _dot_nt(klo, qp)
                s1 = _dot_nt(khi, qp)
                acc = acc + jnp.maximum(s0, 0.0) * wt_ref[2 * hp:2 * hp + 1, :]
                acc = acc + jnp.maximum(s1, 0.0) * wt_ref[2 * hp + 1:2 * hp + 2, :]
            kidx = c * KCH + lax.broadcasted_iota(jnp.int32, (KCH, TQ), 0)
            causal = kidx <= qidx
            isc_ref[c * KCH:(c + 1) * KCH, :] = jnp.where(causal, acc, -jnp.inf)
            lo_part = jnp.where(causal, acc, jnp.inf).reshape(KCH // 8, 8, TQ).min(axis=0)
            hi_part = jnp.where(causal, acc, -jnp.inf).reshape(KCH // 8, 8, TQ).max(axis=0)
            mm_ref[0:8, :] = jnp.minimum(mm_ref[0:8, :], lo_part)
            mm_ref[8:16, :] = jnp.maximum(mm_ref[8:16, :], hi_part)

    def select(nch):
        def count(pred):
            acc = jnp.zeros((COUNT_CHAINS, 8, TQ), F32)
            for c in range(nch):
                ones = jnp.where(pred(isc_ref[c * TQ:(c + 1) * TQ, :], c * TQ), 1.0, 0.0)
                acc = acc + ones.reshape(COUNT_CHAINS, TQ // (8 * COUNT_CHAINS), 8, TQ).sum(axis=1)
            return acc.sum(axis=0).sum(axis=0, keepdims=True)

        row_min = mm_ref[0:8, :].min(axis=0, keepdims=True)
        row_max = mm_ref[8:16, :].max(axis=0, keepdims=True)
        full = (qidx + 1) <= topk
        c_max = count(lambda blk, k0: blk >= row_max)
        exact0 = c_max == kf
        tie0 = c_max > kf
        settled0 = full | exact0 | tie0
        lo0 = jnp.where(full, -F32_MAX, jnp.where(settled0, row_max, row_min))
        hi0 = jnp.where(full, -F32_MAX, jnp.where(tie0, jnp.inf, row_max))
        act0 = jnp.where(settled0, 0.0, 1.0)

        def step(lo, hi, act):
            mid = lo * 0.5 + hi * 0.5
            inside = (mid > lo) & (mid < hi)
            cnt = count(lambda blk, k0: blk >= mid)
            upd = (act > 0.0) & inside
            found = upd & (cnt == kf)
            lo2 = jnp.where(upd & (cnt >= kf), mid, lo)
            hi2 = jnp.where(upd & (cnt <= kf), mid, hi)
            return lo2, hi2, jnp.where(upd & jnp.logical_not(found), 1.0, 0.0)

        def any_active(act):
            return (jnp.max(act) > 0.0).astype(jnp.int32)

        def body(st):
            lo, hi, act, _ = st
            for _ in range(STEPS_PER_CHECK):
                lo, hi, act = step(lo, hi, act)
            return lo, hi, act, any_active(act)

        lo, hi, _, _ = lax.while_loop(lambda st: st[3] > 0, body, (lo0, hi0, act0, any_active(act0)))

        tie = lo < hi
        js_ref[...] = jnp.full((8, TQ), -1.0, F32)

        @pl.when(jnp.max(jnp.where(tie, 1.0, 0.0)) > 0.0)
        def _():
            need = kf - count(lambda blk, k0: blk >= hi)

            def kpos(k0):
                return (k0 + lax.broadcasted_iota(jnp.int32, (TQ, TQ), 0)).astype(F32)

            def tie_step(_, st):
                ilo, ihi = st
                imid = jnp.floor((ilo + ihi) * 0.5)
                cnt = count(lambda blk, k0: (blk >= lo) & (blk < hi) & (kpos(k0) <= imid))
                ge = cnt >= need
                return jnp.where(ge, ilo, imid), jnp.where(ge, imid, ihi)

            nsteps = int(math.ceil(math.log2(S))) + 1
            _, ihi = lax.fori_loop(0, nsteps, tie_step,
                                   (jnp.full((1, TQ), -1.0, F32), jnp.full((1, TQ), S - 1.0, F32)))
            js_ref[0:1, :] = jnp.where(tie, ihi, -1.0)

        jstar = js_ref[0:1, :]

        for c in range(S // TQ):
            if c < nch:
                blk = isc_ref[c * TQ:(c + 1) * TQ, :]
                kpos_c = (c * TQ + lax.broadcasted_iota(jnp.int32, (TQ, TQ), 0)).astype(F32)
                sel = (blk >= hi) | ((blk >= lo) & (kpos_c <= jstar))
                bias_ref[:, c * TQ:(c + 1) * TQ] = jnp.where(sel, 0.0, NEG_BIAS).T
            else:
                bias_ref[:, c * TQ:(c + 1) * TQ] = jnp.full((TQ, TQ), NEG_BIAS, F32)

    for jj in range(S // TQ):
        @pl.when(j == jj)
        def _(jj=jj):
            select(jj + 1)


def _dsa_index(ki_lo, ki_hi, qi, wt, *, B, S, TQ, KCH, topk):
    T = B * S
    nq = S // TQ
    kern = functools.partial(_dsa_index_kernel, S=S, TQ=TQ, KCH=KCH, topk=topk)
    return pl.pallas_call(
        kern,
        grid=(B, nq),
        in_specs=[
            pl.BlockSpec((S, LANES), lambda b, j: (b, 0)),
            pl.BlockSpec((S, LANES), lambda b, j: (b, 0)),
            pl.BlockSpec((TQ, IDX_HEADS * IDX_DIM), lambda b, j: (b * nq + j, 0)),
            pl.BlockSpec((IDX_HEADS, TQ), lambda b, j: (0, b * nq + j)),
        ],
        out_specs=pl.BlockSpec((TQ, S), lambda b, j: (b * nq + j, 0)),
        out_shape=jax.ShapeDtypeStruct((T, S), F32),
        scratch_shapes=[
            pltpu.VMEM((S, TQ), F32),
            pltpu.VMEM((16, TQ), F32),
            pltpu.VMEM((8, TQ), F32),
        ],
        compiler_params=_cparams(("parallel", "arbitrary")),
        name="dsa_index",
    )(ki_lo, ki_hi, qi, wt)


def _dsa_attn_kernel(q_ref, k_ref, v_ref, bias_ref, o_ref, *, S, TQ):
    j = pl.program_id(1)
    lane = lax.broadcasted_iota(jnp.int32, (TQ, LANES), 1)

    def variant(nk):
        ones = jnp.ones((nk, LANES), BF16)

        def pair(p, carry):
            vp = jnp.concatenate([v_ref[0, p, 0:nk, :], ones], axis=1)
            outs = []
            for e in range(2):
                h = 2 * p + e
                s = _dot_nt(q_ref[0, h], k_ref[0, h, 0:nk, :]) + bias_ref[:, 0:nk]
                m = s.max(axis=1, keepdims=True)
                pv = _dot(jnp.exp2(s - m).astype(BF16), vp)
                outs.append(pv[:, 0:LANES] * (1.0 / pv[:, LANES:2 * LANES]))
            o_ref[0, p] = jnp.where(lane < V_HEAD, outs[0], outs[1]).astype(BF16)
            return carry
        lax.fori_loop(0, N_HEADS // 2, pair, 0, unroll=4)

    for jj in range(S // TQ):
        @pl.when(j == jj)
        def _(jj=jj):
            variant((jj + 1) * TQ)


def _dsa_attn(q, k, v, bias, *, B, S, TQ):
    nq = S // TQ
    kern = functools.partial(_dsa_attn_kernel, S=S, TQ=TQ)
    return pl.pallas_call(
        kern,
        grid=(B, nq),
        in_specs=[
            pl.BlockSpec((1, N_HEADS, TQ, LANES), lambda b, j: (b, 0, j, 0)),
            pl.BlockSpec((1, N_HEADS, S, LANES), lambda b, j: (b, 0, 0, 0)),
            pl.BlockSpec((1, N_HEADS // 2, S, LANES), lambda b, j: (b, 0, 0, 0)),
            pl.BlockSpec((TQ, S), lambda b, j: (b * nq + j, 0)),
        ],
        out_specs=pl.BlockSpec((1, N_HEADS // 2, TQ, LANES), lambda b, j: (b, 0, j, 0)),
        out_shape=jax.ShapeDtypeStruct((B, N_HEADS // 2, S, LANES), BF16),
        compiler_params=_cparams(("parallel", "arbitrary")),
        name="dsa_attn",
    )(q, k, v, bias)


def _sgu_kernel(z_ref, g_ref, ws_ref, bt_ref, y_ref, *, tm):
    row = lax.broadcasted_iota(jnp.int32, (SGU_CHUNK, SGU_CHUNK), 0)
    col = lax.broadcasted_iota(jnp.int32, (SGU_CHUNK, SGU_CHUNK), 1)
    tril = col <= row
    w = [jnp.where(tril, ws_ref[g], 0.0).astype(BF16) for g in range(SGU_GROUPS)]
    for cc in range(tm // SGU_CHUNK):
        rows = slice(cc * SGU_CHUNK, (cc + 1) * SGU_CHUNK)
        vn = _rms(z_ref[rows, SGU_WIDTH:2 * SGU_WIDTH].astype(F32), g_ref[...]).astype(BF16)
        for g in range(SGU_GROUPS):
            cols = slice(g * SGU_GROUP_DIM, (g + 1) * SGU_GROUP_DIM)
            mixed = _dot(w[g], vn[:, cols]) + bt_ref[:, g:g + 1]
            y_ref[rows, cols] = (z_ref[rows, cols].astype(F32) * mixed).astype(BF16)


def _sgu(z, g_sgu, w_spatial, b_t, *, tm):
    T = z.shape[0]
    kern = functools.partial(_sgu_kernel, tm=tm)
    return pl.pallas_call(
        kern,
        grid=(T // tm,),
        in_specs=[
            pl.BlockSpec((tm, 2 * SGU_WIDTH), lambda i: (i, 0)),
            pl.BlockSpec((1, SGU_WIDTH), lambda i: (0, 0)),
            pl.BlockSpec((SGU_GROUPS, SGU_CHUNK, SGU_CHUNK), lambda i: (0, 0, 0)),
            pl.BlockSpec((SGU_CHUNK, SGU_GROUPS), lambda i: (0, 0)),
        ],
        out_specs=pl.BlockSpec((tm, SGU_WIDTH), lambda i: (i, 0)),
        out_shape=jax.ShapeDtypeStruct((T, SGU_WIDTH), BF16),
        compiler_params=_cparams(("parallel",)),
        name="sgu",
    )(z, g_sgu, w_spatial, b_t)


def _merge_kernel(o_ref, y_ref, ga_ref, gb_ref, wa_ref, wb_ref, out_ref, wa_bf, wb_bf):
    @pl.when(pl.program_id(1) == 0)
    def _():
        wa_bf[...] = wa_ref[...].astype(BF16)
        wb_bf[...] = wb_ref[...].astype(BF16)

    o_a = jnp.concatenate([o_ref[0, p] for p in range(N_HEADS // 2)], axis=1)
    ya = _dot(o_a, wa_bf[...])
    yb = _dot(y_ref[...], wb_bf[...])
    out_ref[...] = (ga_ref[...].astype(F32) * ya + gb_ref[...].astype(F32) * yb).astype(BF16)


def _merge(o, y_sgu, gates, wa, wb, *, B, S, tm, tn):
    T = B * S
    D = wa.shape[1]
    nt = S // tm
    nn = D // tn
    return pl.pallas_call(
        _merge_kernel,
        grid=(nn, T // tm),
        in_specs=[
            pl.BlockSpec((1, N_HEADS // 2, tm, LANES), lambda j, i: (i // nt, 0, i % nt, 0)),
            pl.BlockSpec((tm, SGU_WIDTH), lambda j, i: (i, 0)),
            pl.BlockSpec((tm, tn), lambda j, i: (i, j)),
            pl.BlockSpec((tm, tn), lambda j, i: (i, nn + j)),
            pl.BlockSpec((wa.shape[0], tn), lambda j, i: (0, j)),
            pl.BlockSpec((wb.shape[0], tn), lambda j, i: (0, j)),
        ],
        out_specs=pl.BlockSpec((tm, tn), lambda j, i: (i, j)),
        out_shape=jax.ShapeDtypeStruct((T, D), BF16),
        scratch_shapes=[pltpu.VMEM((wa.shape[0], tn), BF16), pltpu.VMEM((wb.shape[0], tn), BF16)],
        compiler_params=_cparams(("arbitrary", "arbitrary")),
        name="merge",
    )(o, y_sgu, gates, gates, wa, wb)


def _o_proj_kernel(x_ref, m_ref, w_ref, g_ref, h_ref, n_ref, w_bf):
    @pl.when(pl.program_id(0) == 0)
    def _():
        w_bf[...] = w_ref[...].astype(BF16)

    h = x_ref[...] + _dot(m_ref[...], w_bf[...])
    h_ref[...] = h
    n_ref[...] = _rms(h, g_ref[...]).astype(BF16)


def _o_proj(x2, merged, w_o, g_ffn, *, tm):
    T, D = x2.shape
    return pl.pallas_call(
        _o_proj_kernel,
        grid=(T // tm,),
        in_specs=[
            pl.BlockSpec((tm, D), lambda i: (i, 0)),
            pl.BlockSpec((tm, D), lambda i: (i, 0)),
            pl.BlockSpec((D, D), lambda i: (0, 0), pipeline_mode=pl.Buffered(1)),
            pl.BlockSpec((1, D), lambda i: (0, 0)),
        ],
        out_specs=[pl.BlockSpec((tm, D), lambda i: (i, 0)), pl.BlockSpec((tm, D), lambda i: (i, 0))],
        out_shape=[jax.ShapeDtypeStruct((T, D), F32), jax.ShapeDtypeStruct((T, D), BF16)],
        scratch_shapes=[pltpu.VMEM((D, D), BF16)],
        compiler_params=_cparams(("arbitrary",)),
        name="o_proj",
    )(x2, merged, w_o, g_ffn)


def _ffn_up_kernel(n_ref, wg_ref, wu_ref, a_ref, wg_bf, wu_bf):
    @pl.when(pl.program_id(1) == 0)
    def _():
        wg_bf[...] = wg_ref[...].astype(BF16)
        wu_bf[...] = wu_ref[...].astype(BF16)

    n = n_ref[...]
    g = _dot(n, wg_bf[...])
    u = _dot(n, wu_bf[...])
    a_ref[...] = (g * _sigmoid(g) * u).astype(BF16)


def _ffn_up(n2, w_gu, *, d_ff, tm, tn):
    T, D = n2.shape
    nn = d_ff // tn
    return pl.pallas_call(
        _ffn_up_kernel,
        grid=(nn, T // tm),
        in_specs=[
            pl.BlockSpec((tm, D), lambda j, i: (i, 0)),
            pl.BlockSpec((D, tn), lambda j, i: (0, j)),
            pl.BlockSpec((D, tn), lambda j, i: (0, nn + j)),
        ],
        out_specs=pl.BlockSpec((tm, tn), lambda j, i: (i, j)),
        out_shape=jax.ShapeDtypeStruct((T, d_ff), BF16),
        scratch_shapes=[pltpu.VMEM((D, tn), BF16), pltpu.VMEM((D, tn), BF16)],
        compiler_params=_cparams(("arbitrary", "arbitrary")),
        name="ffn_up",
    )(n2, w_gu, w_gu)


def _ffn_down_kernel(a_ref, w_ref, h1_ref, h2_ref, w_bf):
    @pl.when(pl.program_id(1) == 0)
    def _():
        w_bf[...] = w_ref[...].astype(BF16)

    h2_ref[...] = h1_ref[...] + _dot(a_ref[...], w_bf[...])


def _ffn_down(act, w_down, h1, *, tm, tn):
    T, D = h1.shape
    d_ff = act.shape[1]
    return pl.pallas_call(
        _ffn_down_kernel,
        grid=(D // tn, T // tm),
        in_specs=[
            pl.BlockSpec((tm, d_ff), lambda j, i: (i, 0)),
            pl.BlockSpec((d_ff, tn), lambda j, i: (0, j)),
            pl.BlockSpec((tm, tn), lambda j, i: (i, j)),
        ],
        out_specs=pl.BlockSpec((tm, tn), lambda j, i: (i, j)),
        out_shape=jax.ShapeDtypeStruct((T, D), F32),
        scratch_shapes=[pltpu.VMEM((d_ff, tn), BF16)],
        compiler_params=_cparams(("arbitrary", "arbitrary")),
        name="ffn_down",
    )(act, w_down, h1)


def _ple_final_kernel(h2_ref, p_ref, wg_ref, wp_ref, gp_ref, gf_ref, out_ref, wg_bf, wp_bf):
    @pl.when(pl.program_id(0) == 0)
    def _():
        wg_bf[...] = wg_ref[...].astype(BF16)
        wp_bf[...] = wp_ref[...].astype(BF16)

    h2 = h2_ref[...]
    gate = _sigmoid(_dot(_rms(h2, gp_ref[...]).astype(BF16), wg_bf[...]))
    pp = _dot(p_ref[...].astype(BF16), wp_bf[...])
    out_ref[...] = _rms(h2 + gate * pp, gf_ref[...])


def _ple_final(h2, p2, w_pg, w_pp, g_ple, g_final, *, tm):
    T, D = h2.shape
    P = p2.shape[1]
    once = pl.Buffered(1)
    return pl.pallas_call(
        _ple_final_kernel,
        grid=(T // tm,),
        in_specs=[
            pl.BlockSpec((tm, D), lambda i: (i, 0)),
            pl.BlockSpec((tm, P), lambda i: (i, 0)),
            pl.BlockSpec((D, D), lambda i: (0, 0), pipeline_mode=once),
            pl.BlockSpec((P, D), lambda i: (0, 0), pipeline_mode=once),
            pl.BlockSpec((1, D), lambda i: (0, 0)),
            pl.BlockSpec((1, D), lambda i: (0, 0)),
        ],
        out_specs=pl.BlockSpec((tm, D), lambda i: (i, 0)),
        out_shape=jax.ShapeDtypeStruct((T, D), F32),
        scratch_shapes=[pltpu.VMEM((D, D), BF16), pltpu.VMEM((P, D), BF16)],
        compiler_params=_cparams(("arbitrary",)),
        name="ple_final",
    )(h2, p2, w_pg, w_pp, g_ple, g_final)


def _rope_tables(seq, dim):
    inv = ROPE_THETA ** (-jnp.arange(0, dim, 2, dtype=F32) / dim)
    ang = jnp.arange(seq, dtype=F32)[:, None] * inv[None, :]
    return jnp.cos(ang), jnp.sin(ang)


def _lane_tables(S):
    scale = (QK_NOPE + QK_ROPE) ** -0.5 * math.log2(math.e)
    c, s = _rope_tables(S, QK_ROPE)
    z = lambda n: jnp.zeros((S, n), F32)
    one = lambda n: jnp.ones((S, n), F32)
    tq = jnp.stack([
        jnp.concatenate([one(QK_NOPE), c, c, z(32)], 1) * scale,
        jnp.concatenate([z(QK_NOPE), -s, s, z(32)], 1) * scale,
    ])
    ci, si = _rope_tables(S, IDX_ROPE)
    half = lambda a, b, rest: jnp.concatenate([a, b, rest], 1)
    ti = jnp.stack([
        jnp.tile(half(ci, ci, one(IDX_DIM - IDX_ROPE)), (1, 2)),
        jnp.tile(half(si, si, z(IDX_DIM - IDX_ROPE)), (1, 2)),
    ])
    tki = jnp.stack([
        jnp.concatenate([half(ci, ci, one(IDX_DIM - IDX_ROPE)), z(LANES - IDX_DIM)], 1),
        jnp.concatenate([half(-si, z(16), z(IDX_DIM - IDX_ROPE)), z(LANES - IDX_DIM)], 1),
        jnp.concatenate([half(z(16), si, z(IDX_DIM - IDX_ROPE)), z(LANES - IDX_DIM)], 1),
    ])
    tk = jnp.stack([
        jnp.concatenate([c, c, z(96)], 1),
        jnp.concatenate([-s, z(16), z(96)], 1),
        jnp.concatenate([z(16), s, z(96)], 1),
    ])
    return tq, ti, tki, tk


def kernel(x, p, g_mix, w_in, g_cq, g_ckv, w_uq, w_uk, w_uv, w_iq, w_a_proj, g_sgu, w_spatial,
           b_spatial, w_b_proj, w_o, g_ffn, w_gu, w_down, g_ple, w_ple_gate, w_ple_proj, g_final):
    B, S, D = x.shape
    T = B * S
    depth = w_in.shape[0]
    d_ff = w_down.shape[1]
    topk = min(TOPK_MAX, S // 4)
    TQ = 256
    KCH = min(512, S)
    tm = min(512, S)
    tq, ti, tki, tk = _lane_tables(S)

    h = x.reshape(T, D)
    for i in range(depth):
        wq3 = w_uq[i].reshape(Q_LORA, N_HEADS, QK_NOPE + QK_ROPE)
        x1, x2 = wq3[..., QK_NOPE:QK_NOPE + QK_ROPE // 2], wq3[..., QK_NOPE + QK_ROPE // 2:]
        wq = jnp.concatenate([wq3, x2, x1], axis=-1).reshape(Q_LORA, N_HEADS * LANES).astype(BF16)
        wk = jnp.pad(w_uk[i], ((0, 0), (0, 0), (0, LANES - QK_NOPE))).reshape(KV_LORA, N_HEADS * LANES).astype(BF16)
        wv = w_uv[i].reshape(KV_LORA, N_HEADS * V_HEAD).astype(BF16)
        wi3 = w_iq[i].reshape(Q_LORA, IDX_HEADS, IDX_DIM)
        wiq = w_iq[i].astype(BF16)
        wiqr = jnp.concatenate([-wi3[..., IDX_ROPE // 2:IDX_ROPE], wi3[..., :IDX_ROPE // 2],
                                jnp.zeros_like(wi3[..., IDX_ROPE:])], axis=-1).reshape(w_iq[i].shape).astype(BF16)

        small, z, gates = _in_proj(h, g_mix[i][None], jnp.transpose(w_in[i]), tm=min(1024, S), tn=512)
        q, k, v, qi, ki_lo, ki_hi, wt = _a_proj(small, g_cq[i][None], g_ckv[i][None], wq, wiq, wiqr, wk, wv,
                                                 tq, ti, tki, tk, B=B, S=S, tm=tm)
        bias = _dsa_index(ki_lo, ki_hi, qi, wt, B=B, S=S, TQ=TQ, KCH=KCH, topk=topk)
        o = _dsa_attn(q, k, v, bias, B=B, S=S, TQ=TQ)
        y_sgu = _sgu(z, g_sgu[i][None], w_spatial[i], jnp.transpose(b_spatial[i]), tm=tm)
        merged = _merge(o, y_sgu, gates, w_a_proj[i], w_b_proj[i], B=B, S=S, tm=tm, tn=1024)
        h1, n2 = _o_proj(h, merged, w_o[i], g_ffn[i][None], tm=min(256, S))
        act = _ffn_up(n2, w_gu[i], d_ff=d_ff, tm=min(1024, S), tn=512)
        h2 = _ffn_down(act, w_down[i], h1, tm=tm, tn=512)
        assert depth == 1
        h = _ple_final(h2, p[i].reshape(T, -1), w_ple_gate[i], w_ple_proj[i], g_ple[i][None],
                       g_final[None], tm=min(256, S))
    return h.reshape(B, S, D)
```

```python
import functools
import math

import jax
import jax.numpy as jnp
from jax import lax
from jax.experimental import pallas as pl
from jax.experimental.pallas import tpu as pltpu

F32 = jnp.float32
BF16 = jnp.bfloat16

N_HEADS = 16
QK_NOPE = 64
QK_ROPE = 32
V_HEAD = 64
Q_LORA = 512
KV_LORA = 256
IDX_HEADS = 16
IDX_DIM = 64
IDX_ROPE = 32
TOPK_MAX = 256
SGU_CHUNK = 128
SGU_GROUPS = 8
SGU_GROUP_DIM = 128
SGU_WIDTH = SGU_GROUPS * SGU_GROUP_DIM
ROPE_THETA = 10000.0
EPS = 1e-6

LANES = 128
IN_SMALL = Q_LORA + KV_LORA + QK_ROPE + IDX_DIM + IDX_HEADS
SMALL_COLS = 1024
COL_CKV = Q_LORA
COL_MISC = Q_LORA + KV_LORA
MISC_KIDX = QK_ROPE
MISC_WIDX = QK_ROPE + IDX_DIM
VMEM_LIMIT_BYTES = 56 * 1024 * 1024
NEG_BIAS = -1e30
F32_MAX = 3.4028234663852886e38
STEPS_PER_CHECK = 4
COUNT_CHAINS = 4


def _cparams(semantics):
    return pltpu.CompilerParams(dimension_semantics=semantics, vmem_limit_bytes=VMEM_LIMIT_BYTES)


def _rms(x, g):
    return x * lax.rsqrt(jnp.mean(x * x, axis=-1, keepdims=True) + EPS) * g


def _dot(a, b):
    return jnp.dot(a, b, preferred_element_type=F32)


def _dot_nt(a, b):
    return lax.dot_general(a, b, (((1,), (1,)), ((), ())), preferred_element_type=F32)


def _gelu_exact(x):
    return 0.5 * x * (1.0 + lax.erf(x * (1.0 / math.sqrt(2.0))))


def _sigmoid(x):
    return 1.0 / (1.0 + jnp.exp(-x))


def _in_proj_kernel(x_ref, g_ref, w_ref, small_ref, z_ref, gate_ref, n_scr, *, n_small, n_z):
    j = pl.program_id(1)

    @pl.when(j == 0)
    def _():
        n_scr[...] = _rms(x_ref[...], g_ref[...]).astype(BF16)

    def proj():
        return _dot_nt(n_scr[...], w_ref[...].astype(BF16))

    @pl.when(j < n_small)
    def _():
        small_ref[...] = proj()

    @pl.when((j >= n_small) & (j < n_small + n_z))
    def _():
        z_ref[...] = _gelu_exact(proj()).astype(BF16)

    @pl.when(j >= n_small + n_z)
    def _():
        gate_ref[...] = _sigmoid(proj()).astype(BF16)


def _in_proj(x2, g_mix, w_t, *, tm, tn):
    T, D = x2.shape
    n_small = SMALL_COLS // tn
    n_z = (2 * SGU_WIDTH) // tn
    n_gate = (2 * D) // tn
    row_z = IN_SMALL
    row_gate = IN_SMALL + 2 * SGU_WIDTH
    assert w_t.shape == (row_gate + 2 * D, D) and row_z % 8 == 0
    grid = (T // tm, n_small + n_z + n_gate)
    kern = functools.partial(_in_proj_kernel, n_small=n_small, n_z=n_z)

    def w_row(i, j):
        tiles = jnp.where(j < n_small, j * (tn // 8),
                          jnp.where(j < n_small + n_z, row_z // 8 + (j - n_small) * (tn // 8),
                                    row_gate // 8 + (j - n_small - n_z) * (tn // 8)))
        return tiles * 8

    return pl.pallas_call(
        kern,
        grid=grid,
        in_specs=[
            pl.BlockSpec((tm, D), lambda i, j: (i, 0)),
            pl.BlockSpec((1, D), lambda i, j: (0, 0)),
            pl.BlockSpec((pl.Element(tn), pl.Element(D)), lambda i, j: (w_row(i, j), 0)),
        ],
        out_specs=[
            pl.BlockSpec((tm, tn), lambda i, j: (i, jnp.minimum(j, n_small - 1))),
            pl.BlockSpec((tm, tn), lambda i, j: (i, jnp.clip(j - n_small, 0, n_z - 1))),
            pl.BlockSpec((tm, tn), lambda i, j: (i, jnp.clip(j - n_small - n_z, 0, n_gate - 1))),
        ],
        out_shape=[
            jax.ShapeDtypeStruct((T, SMALL_COLS), F32),
            jax.ShapeDtypeStruct((T, 2 * SGU_WIDTH), BF16),
            jax.ShapeDtypeStruct((T, 2 * D), BF16),
        ],
        scratch_shapes=[pltpu.VMEM((tm, D), BF16)],
        compiler_params=_cparams(("parallel", "arbitrary")),
        name="in_proj",
    )(x2, g_mix, w_t)


def _rope_block(x, cos, sin_a, sin_b):
    return x * cos + pltpu.roll(x, LANES - 16, 1) * sin_a + pltpu.roll(x, 16, 1) * sin_b


def _a_proj_kernel(small_ref, gcq_ref, gckv_ref, wq_ref, wiq_ref, wiqr_ref, wk_ref, wv_ref,
                   tq_ref, ti_ref, tki_ref, tk_ref,
                   q_ref, k_ref, v_ref, qi_ref, kilo_ref, kihi_ref, wt_ref):
    c_q = _rms(small_ref[:, 0:Q_LORA], gcq_ref[...]).astype(BF16)
    c_kv = _rms(small_ref[:, COL_CKV:COL_CKV + KV_LORA], gckv_ref[...]).astype(BF16)

    q = _dot(c_q, wq_ref[...])
    cq, sq = tq_ref[0], tq_ref[1]
    for h in range(N_HEADS):
        blk = q[:, h * LANES:(h + 1) * LANES]
        q_ref[0, h] = (blk * cq + pltpu.roll(blk, LANES - QK_ROPE, 1) * sq).astype(BF16)

    ci, si = ti_ref[0], ti_ref[1]
    qi = _dot(c_q, wiq_ref[...])
    qir = _dot(c_q, wiqr_ref[...])
    for hp in range(IDX_HEADS // 2):
        cols = slice(hp * LANES, (hp + 1) * LANES)
        qi_ref[:, cols] = (qi[:, cols] * ci + qir[:, cols] * si).astype(BF16)

    misc = small_ref[:, COL_MISC:COL_MISC + LANES]
    ki_lo = _rope_block(pltpu.roll(misc, LANES - MISC_KIDX, 1), tki_ref[0], tki_ref[1], tki_ref[2])
    kilo_ref[...] = ki_lo.astype(BF16)
    kihi_ref[...] = pltpu.roll(ki_lo, IDX_DIM, 1).astype(BF16)

    ck, ska, skb = tk_ref[0], tk_ref[1], tk_ref[2]
    k_rope = pltpu.roll(_rope_block(misc, ck, ska, skb), QK_NOPE, 1)
    k_nope = _dot(c_kv, wk_ref[...])
    for h in range(N_HEADS):
        k_ref[0, h] = (k_nope[:, h * LANES:(h + 1) * LANES] + k_rope).astype(BF16)

    v = _dot(c_kv, wv_ref[...])
    for p in range(N_HEADS // 2):
        v_ref[0, p] = v[:, p * LANES:(p + 1) * LANES].astype(BF16)

    w_scale = IDX_HEADS ** -0.5 * IDX_DIM ** -0.5
    wt_ref[...] = misc.T[MISC_WIDX:MISC_WIDX + IDX_HEADS, :] * w_scale


def _a_proj(small, g_cq, g_ckv, wq, wiq, wiqr, wk, wv, tq, ti, tki, tk, *, B, S, tm):
    T = B * S
    nt = S // tm
    const2 = lambda i: (0, 0)
    tab = lambda n: pl.BlockSpec((n, tm, LANES), lambda i: (0, i % nt, 0))
    head_spec = lambda nh: pl.BlockSpec((1, nh, tm, LANES), lambda i: (i // nt, 0, i % nt, 0))
    return pl.pallas_call(
        _a_proj_kernel,
        grid=(T // tm,),
        in_specs=[
            pl.BlockSpec((tm, SMALL_COLS), lambda i: (i, 0)),
            pl.BlockSpec((1, Q_LORA), const2),
            pl.BlockSpec((1, KV_LORA), const2),
            pl.BlockSpec(wq.shape, const2),
            pl.BlockSpec(wiq.shape, const2),
            pl.BlockSpec(wiqr.shape, const2),
            pl.BlockSpec(wk.shape, const2),
            pl.BlockSpec(wv.shape, const2),
            tab(2), tab(2), tab(3), tab(3),
        ],
        out_specs=[
            head_spec(N_HEADS),
            head_spec(N_HEADS),
            head_spec(N_HEADS // 2),
            pl.BlockSpec((tm, IDX_HEADS * IDX_DIM), lambda i: (i, 0)),
            pl.BlockSpec((tm, LANES), lambda i: (i, 0)),
            pl.BlockSpec((tm, LANES), lambda i: (i, 0)),
            pl.BlockSpec((IDX_HEADS, tm), lambda i: (0, i)),
        ],
        out_shape=[
            jax.ShapeDtypeStruct((B, N_HEADS, S, LANES), BF16),
            jax.ShapeDtypeStruct((B, N_HEADS, S, LANES), BF16),
            jax.ShapeDtypeStruct((B, N_HEADS // 2, S, LANES), BF16),
            jax.ShapeDtypeStruct((T, IDX_HEADS * IDX_DIM), BF16),
            jax.ShapeDtypeStruct((T, LANES), BF16),
            jax.ShapeDtypeStruct((T, LANES), BF16),
            jax.ShapeDtypeStruct((IDX_HEADS, T), F32),
        ],
        compiler_params=_cparams(("parallel",)),
        name="a_proj",
    )(small, g_cq, g_ckv, wq, wiq, wiqr, wk, wv, tq, ti, tki, tk)


def _dsa_index_kernel(kilo_ref, kihi_ref, qi_ref, wt_ref, bias_ref, isc_ref, mm_ref, js_ref,
                      *, S, TQ, KCH, topk):
    j = pl.program_id(1)
    q0 = j * TQ
    nkeys = q0 + TQ
    qidx = q0 + lax.broadcasted_iota(jnp.int32, (1, TQ), 1)
    kf = float(topk)

    mm_ref[0:8, :] = jnp.full((8, TQ), jnp.inf, F32)
    mm_ref[8:16, :] = jnp.full((8, TQ), -jnp.inf, F32)
    for c in range(S // KCH):
        @pl.when(c * KCH < nkeys)
        def _(c=c):
            klo = kilo_ref[c * KCH:(c + 1) * KCH, :]
            khi = kihi_ref[c * KCH:(c + 1) * KCH, :]
            acc = jnp.zeros((KCH, TQ), F32)
            for hp in range(IDX_HEADS // 2):
                qp = qi_ref[:, hp * LANES:(hp + 1) * LANES]
                s0 = _dot_nt(klo, qp)
                s1 = _dot_nt(khi, qp)
                acc = acc + jnp.maximum(s0, 0.0) * wt_ref[2 * hp:2 * hp + 1, :]
                acc = acc + jnp.maximum(s1, 0.0) * wt_ref[2 * hp + 1:2 * hp + 2, :]
            kidx = c * KCH + lax.broadcasted_iota(jnp.int32, (KCH, TQ), 0)
            causal = kidx <= qidx
            isc_ref[c * KCH:(c + 1) * KCH, :] = jnp.where(causal, acc, -jnp.inf)
            lo_part = jnp.where(causal, acc, jnp.inf).reshape(KCH // 8, 8, TQ).min(axis=0)
            hi_part = jnp.where(causal, acc, -jnp.inf).reshape(KCH // 8, 8, TQ).max(axis=0)
            mm_ref[0:8, :] = jnp.minimum(mm_ref[0:8, :], lo_part)
            mm_ref[8:16, :] = jnp.maximum(mm_ref[8:16, :], hi_part)

    def select(nch):
        def count(pred):
            acc = jnp.zeros((COUNT_CHAINS, 8, TQ), F32)
            for c in range(nch):
                ones = jnp.where(pred(isc_ref[c * TQ:(c + 1) * TQ, :], c * TQ), 1.0, 0.0)
                acc = acc + ones.reshape(COUNT_CHAINS, TQ // (8 * COUNT_CHAINS), 8, TQ).sum(axis=1)
            return acc.sum(axis=0).sum(axis=0, keepdims=True)

        row_min = mm_ref[0:8, :].min(axis=0, keepdims=True)
        row_max = mm_ref[8:16, :].max(axis=0, keepdims=True)
        full = (qidx + 1) <= topk
        c_max = count(lambda blk, k0: blk >= row_max)
        exact0 = c_max == kf
        tie0 = c_max > kf
        settled0 = full | exact0 | tie0
        lo0 = jnp.where(full, -F32_MAX, jnp.where(settled0, row_max, row_min))
        hi0 = jnp.where(full, -F32_MAX, jnp.where(tie0, jnp.inf, row_max))
        act0 = jnp.where(settled0, 0.0, 1.0)

        def step(lo, hi, act):
            mid = lo * 0.5 + hi * 0.5
            inside = (mid > lo) & (mid < hi)
            cnt = count(lambda blk, k0: blk >= mid)
            upd = (act > 0.0) & inside
            found = upd & (cnt == kf)
            lo2 = jnp.where(upd & (cnt >= kf), mid, lo)
            hi2 = jnp.where(upd & (cnt <= kf), mid, hi)
            return lo2, hi2, jnp.where(upd & jnp.logical_not(found), 1.0, 0.0)

        def any_active(act):
            return (jnp.max(act) > 0.0).astype(jnp.int32)

        def body(st):
            lo, hi, act, _ = st
            for _ in range(STEPS_PER_CHECK):
                lo, hi, act = step(lo, hi, act)
            return lo, hi, act, any_active(act)

        lo, hi, _, _ = lax.while_loop(lambda st: st[3] > 0, body, (lo0, hi0, act0, any_active(act0)))

        tie = lo < hi
        js_ref[...] = jnp.full((8, TQ), -1.0, F32)

        @pl.when(jnp.max(jnp.where(tie, 1.0, 0.0)) > 0.0)
        def _():
            need = kf - count(lambda blk, k0: blk >= hi)

            def kpos(k0):
                return (k0 + lax.broadcasted_iota(jnp.int32, (TQ, TQ), 0)).astype(F32)

            def tie_step(_, st):
                ilo, ihi = st
                imid = jnp.floor((ilo + ihi) * 0.5)
                cnt = count(lambda blk, k0: (blk >= lo) & (blk < hi) & (kpos(k0) <= imid))
                ge = cnt >= need
                return jnp.where(ge, ilo, imid), jnp.where(ge, imid, ihi)

            nsteps = int(math.ceil(math.log2(S))) + 1
            _, ihi = lax.fori_loop(0, nsteps, tie_step,
                                   (jnp.full((1, TQ), -1.0, F32), jnp.full((1, TQ), S - 1.0, F32)))
            js_ref[0:1, :] = jnp.where(tie, ihi, -1.0)

        jstar = js_ref[0:1, :]

        for c in range(S // TQ):
            if c < nch:
                blk = isc_ref[c * TQ:(c + 1) * TQ, :]
                kpos_c = (c * TQ + lax.broadcasted_iota(jnp.int32, (TQ, TQ), 0)).astype(F32)
                sel = (blk >= hi) | ((blk >= lo) & (kpos_c <= jstar))
                bias_ref[:, c * TQ:(c + 1) * TQ] = jnp.where(sel, 0.0, NEG_BIAS).T
            else:
                bias_ref[:, c * TQ:(c + 1) * TQ] = jnp.full((TQ, TQ), NEG_BIAS, F32)

    for jj in range(S // TQ):
        @pl.when(j == jj)
        def _(jj=jj):
            select(jj + 1)


def _dsa_index(ki_lo, ki_hi, qi, wt, *, B, S, TQ, KCH, topk):
    T = B * S
    nq = S // TQ
    kern = functools.partial(_dsa_index_kernel, S=S, TQ=TQ, KCH=KCH, topk=topk)
    return pl.pallas_call(
        kern,
        grid=(B, nq),
        in_specs=[
            pl.BlockSpec((S, LANES), lambda b, j: (b, 0)),
            pl.BlockSpec((S, LANES), lambda b, j: (b, 0)),
            pl.BlockSpec((TQ, IDX_HEADS * IDX_DIM), lambda b, j: (b * nq + j, 0)),
            pl.BlockSpec((IDX_HEADS, TQ), lambda b, j: (0, b * nq + j)),
        ],
        out_specs=pl.BlockSpec((TQ, S), lambda b, j: (b * nq + j, 0)),
        out_shape=jax.ShapeDtypeStruct((T, S), F32),
        scratch_shapes=[
            pltpu.VMEM((S, TQ), F32),
            pltpu.VMEM((16, TQ), F32),
            pltpu.VMEM((8, TQ), F32),
        ],
        compiler_params=_cparams(("parallel", "arbitrary")),
        name="dsa_index",
    )(ki_lo, ki_hi, qi, wt)


def _dsa_attn_kernel(q_ref, k_ref, v_ref, bias_ref, o_ref, *, S, TQ):
    j = pl.program_id(1)
    lane = lax.broadcasted_iota(jnp.int32, (TQ, LANES), 1)

    def variant(nk):
        ones = jnp.ones((nk, LANES), BF16)

        def pair(p, carry):
            vp = jnp.concatenate([v_ref[0, p, 0:nk, :], ones], axis=1)
            outs = []
            for e in range(2):
                h = 2 * p + e
                s = _dot_nt(q_ref[0, h], k_ref[0, h, 0:nk, :]) + bias_ref[:, 0:nk]
                m = s.max(axis=1, keepdims=True)
                pv = _dot(jnp.exp2(s - m).astype(BF16), vp)
                outs.append(pv[:, 0:LANES] * (1.0 / pv[:, LANES:2 * LANES]))
            o_ref[0, p] = jnp.where(lane < V_HEAD, outs[0], outs[1]).astype(BF16)
            return carry
        lax.fori_loop(0, N_HEADS // 2, pair, 0, unroll=4)

    for jj in range(S // TQ):
        @pl.when(j == jj)
        def _(jj=jj):
            variant((jj + 1) * TQ)


def _dsa_attn(q, k, v, bias, *, B, S, TQ):
    nq = S // TQ
    kern = functools.partial(_dsa_attn_kernel, S=S, TQ=TQ)
    return pl.pallas_call(
        kern,
        grid=(B, nq),
        in_specs=[
            pl.BlockSpec((1, N_HEADS, TQ, LANES), lambda b, j: (b, 0, j, 0)),
            pl.BlockSpec((1, N_HEADS, S, LANES), lambda b, j: (b, 0, 0, 0)),
            pl.BlockSpec((1, N_HEADS // 2, S, LANES), lambda b, j: (b, 0, 0, 0)),
            pl.BlockSpec((TQ, S), lambda b, j: (b * nq + j, 0)),
        ],
        out_specs=pl.BlockSpec((1, N_HEADS // 2, TQ, LANES), lambda b, j: (b, 0, j, 0)),
        out_shape=jax.ShapeDtypeStruct((B, N_HEADS // 2, S, LANES), BF16),
        compiler_params=_cparams(("parallel", "arbitrary")),
        name="dsa_attn",
    )(q, k, v, bias)


def _sgu_kernel(z_ref, g_ref, ws_ref, bt_ref, y_ref, *, tm):
    row = lax.broadcasted_iota(jnp.int32, (SGU_CHUNK, SGU_CHUNK), 0)
    col = lax.broadcasted_iota(jnp.int32, (SGU_CHUNK, SGU_CHUNK), 1)
    tril = col <= row
    w = [jnp.where(tril, ws_ref[g], 0.0).astype(BF16) for g in range(SGU_GROUPS)]
    for cc in range(tm // SGU_CHUNK):
        rows = slice(cc * SGU_CHUNK, (cc + 1) * SGU_CHUNK)
        vn = _rms(z_ref[rows, SGU_WIDTH:2 * SGU_WIDTH].astype(F32), g_ref[...]).astype(BF16)
        for g in range(SGU_GROUPS):
            cols = slice(g * SGU_GROUP_DIM, (g + 1) * SGU_GROUP_DIM)
            mixed = _dot(w[g], vn[:, cols]) + bt_ref[:, g:g + 1]
            y_ref[rows, cols] = (z_ref[rows, cols].astype(F32) * mixed).astype(BF16)


def _sgu(z, g_sgu, w_spatial, b_t, *, tm):
    T = z.shape[0]
    kern = functools.partial(_sgu_kernel, tm=tm)
    return pl.pallas_call(
        kern,
        grid=(T // tm,),
        in_specs=[
            pl.BlockSpec((tm, 2 * SGU_WIDTH), lambda i: (i, 0)),
            pl.BlockSpec((1, SGU_WIDTH), lambda i: (0, 0)),
            pl.BlockSpec((SGU_GROUPS, SGU_CHUNK, SGU_CHUNK), lambda i: (0, 0, 0)),
            pl.BlockSpec((SGU_CHUNK, SGU_GROUPS), lambda i: (0, 0)),
        ],
        out_specs=pl.BlockSpec((tm, SGU_WIDTH), lambda i: (i, 0)),
        out_shape=jax.ShapeDtypeStruct((T, SGU_WIDTH), BF16),
        compiler_params=_cparams(("parallel",)),
        name="sgu",
    )(z, g_sgu, w_spatial, b_t)


def _merge_kernel(o_ref, y_ref, ga_ref, gb_ref, wa_ref, wb_ref, out_ref, wa_bf, wb_bf):
    @pl.when(pl.program_id(1) == 0)
    def _():
        wa_bf[...] = wa_ref[...].astype(BF16)
        wb_bf[...] = wb_ref[...].astype(BF16)

    o_a = jnp.concatenate([o_ref[0, p] for p in range(N_HEADS // 2)], axis=1)
    ya = _dot(o_a, wa_bf[...])
    yb = _dot(y_ref[...], wb_bf[...])
    out_ref[...] = (ga_ref[...].astype(F32) * ya + gb_ref[...].astype(F32) * yb).astype(BF16)


def _merge(o, y_sgu, gates, wa, wb, *, B, S, tm, tn):
    T = B * S
    D = wa.shape[1]
    nt = S // tm
    nn = D // tn
    return pl.pallas_call(
        _merge_kernel,
        grid=(nn, T // tm),
        in_specs=[
            pl.BlockSpec((1, N_HEADS // 2, tm, LANES), lambda j, i: (i // nt, 0, i % nt, 0)),
            pl.BlockSpec((tm, SGU_WIDTH), lambda j, i: (i, 0)),
            pl.BlockSpec((tm, tn), lambda j, i: (i, j)),
            pl.BlockSpec((tm, tn), lambda j, i: (i, nn + j)),
            pl.BlockSpec((wa.shape[0], tn), lambda j, i: (0, j)),
            pl.BlockSpec((wb.shape[0], tn), lambda j, i: (0, j)),
        ],
        out_specs=pl.BlockSpec((tm, tn), lambda j, i: (i, j)),
        out_shape=jax.ShapeDtypeStruct((T, D), BF16),
        scratch_shapes=[pltpu.VMEM((wa.shape[0], tn), BF16), pltpu.VMEM((wb.shape[0], tn), BF16)],
        compiler_params=_cparams(("arbitrary", "arbitrary")),
        name="merge",
    )(o, y_sgu, gates, gates, wa, wb)


def _o_proj_kernel(x_ref, m_ref, w_ref, g_ref, h_ref, n_ref, w_bf):
    @pl.when(pl.program_id(0) == 0)
    def _():
        w_bf[...] = w_ref[...].astype(BF16)

    h = x_ref[...] + _dot(m_ref[...], w_bf[...])
    h_ref[...] = h
    n_ref[...] = _rms(h, g_ref[...]).astype(BF16)


def _o_proj(x2, merged, w_o, g_ffn, *, tm):
    T, D = x2.shape
    return pl.pallas_call(
        _o_proj_kernel,
        grid=(T // tm,),
        in_specs=[
            pl.BlockSpec((tm, D), lambda i: (i, 0)),
            pl.BlockSpec((tm, D), lambda i: (i, 0)),
            pl.BlockSpec((D, D), lambda i: (0, 0), pipeline_mode=pl.Buffered(1)),
            pl.BlockSpec((1, D), lambda i: (0, 0)),
        ],
        out_specs=[pl.BlockSpec((tm, D), lambda i: (i, 0)), pl.BlockSpec((tm, D), lambda i: (i, 0))],
        out_shape=[jax.ShapeDtypeStruct((T, D), F32), jax.ShapeDtypeStruct((T, D), BF16)],
        scratch_shapes=[pltpu.VMEM((D, D), BF16)],
        compiler_params=_cparams(("arbitrary",)),
        name="o_proj",
    )(x2, merged, w_o, g_ffn)


def _ffn_up_kernel(n_ref, wg_ref, wu_ref, a_ref, wg_bf, wu_bf):
    @pl.when(pl.program_id(1) == 0)
    def _():
        wg_bf[...] = wg_ref[...].astype(BF16)
        wu_bf[...] = wu_ref[...].astype(BF16)

    n = n_ref[...]
    g = _dot(n, wg_bf[...])
    u = _dot(n, wu_bf[...])
    a_ref[...] = (g * _sigmoid(g) * u).astype(BF16)


def _ffn_up(n2, w_gu, *, d_ff, tm, tn):
    T, D = n2.shape
    nn = d_ff // tn
    return pl.pallas_call(
        _ffn_up_kernel,
        grid=(nn, T // tm),
        in_specs=[
            pl.BlockSpec((tm, D), lambda j, i: (i, 0)),
            pl.BlockSpec((D, tn), lambda j, i: (0, j)),
            pl.BlockSpec((D, tn), lambda j, i: (0, nn + j)),
        ],
        out_specs=pl.BlockSpec((tm, tn), lambda j, i: (i, j)),
        out_shape=jax.ShapeDtypeStruct((T, d_ff), BF16),
        scratch_shapes=[pltpu.VMEM((D, tn), BF16), pltpu.VMEM((D, tn), BF16)],
        compiler_params=_cparams(("arbitrary", "arbitrary")),
        name="ffn_up",
    )(n2, w_gu, w_gu)


def _ffn_down_kernel(a_ref, w_ref, h1_ref, h2_ref, w_bf):
    @pl.when(pl.program_id(1) == 0)
    def _():
        w_bf[...] = w_ref[...].astype(BF16)

    h2_ref[...] = h1_ref[...] + _dot(a_ref[...], w_bf[...])


def _ffn_down(act, w_down, h1, *, tm, tn):
    T, D = h1.shape
    d_ff = act.shape[1]
    return pl.pallas_call(
        _ffn_down_kernel,
        grid=(D // tn, T // tm),
        in_specs=[
            pl.BlockSpec((tm, d_ff), lambda j, i: (i, 0)),
            pl.BlockSpec((d_ff, tn), lambda j, i: (0, j)),
            pl.BlockSpec((tm, tn), lambda j, i: (i, j)),
        ],
        out_specs=pl.BlockSpec((tm, tn), lambda j, i: (i, j)),
        out_shape=jax.ShapeDtypeStruct((T, D), F32),
        scratch_shapes=[pltpu.VMEM((d_ff, tn), BF16)],
        compiler_params=_cparams(("arbitrary", "arbitrary")),
        name="ffn_down",
    )(act, w_down, h1)


def _ple_final_kernel(h2_ref, p_ref, wg_ref, wp_ref, gp_ref, gf_ref, out_ref, wg_bf, wp_bf):
    @pl.when(pl.program_id(0) == 0)
    def _():
        wg_bf[...] = wg_ref[...].astype(BF16)
        wp_bf[...] = wp_ref[...].astype(BF16)

    h2 = h2_ref[...]
    gate = _sigmoid(_dot(_rms(h2, gp_ref[...]).astype(BF16), wg_bf[...]))
    pp = _dot(p_ref[...].astype(BF16), wp_bf[...])
    out_ref[...] = _rms(h2 + gate * pp, gf_ref[...])


def _ple_final(h2, p2, w_pg, w_pp, g_ple, g_final, *, tm):
    T, D = h2.shape
    P = p2.shape[1]
    once = pl.Buffered(1)
    return pl.pallas_call(
        _ple_final_kernel,
        grid=(T // tm,),
        in_specs=[
            pl.BlockSpec((tm, D), lambda i: (i, 0)),
            pl.BlockSpec((tm, P), lambda i: (i, 0)),
            pl.BlockSpec((D, D), lambda i: (0, 0), pipeline_mode=once),
            pl.BlockSpec((P, D), lambda i: (0, 0), pipeline_mode=once),
            pl.BlockSpec((1, D), lambda i: (0, 0)),
            pl.BlockSpec((1, D), lambda i: (0, 0)),
        ],
        out_specs=pl.BlockSpec((tm, D), lambda i: (i, 0)),
        out_shape=jax.ShapeDtypeStruct((T, D), F32),
        scratch_shapes=[pltpu.VMEM((D, D), BF16), pltpu.VMEM((P, D), BF16)],
        compiler_params=_cparams(("arbitrary",)),
        name="ple_final",
    )(h2, p2, w_pg, w_pp, g_ple, g_final)


def _rope_tables(seq, dim):
    inv = ROPE_THETA ** (-jnp.arange(0, dim, 2, dtype=F32) / dim)
    ang = jnp.arange(seq, dtype=F32)[:, None] * inv[None, :]
    return jnp.cos(ang), jnp.sin(ang)


def _lane_tables(S):
    scale = (QK_NOPE + QK_ROPE) ** -0.5 * math.log2(math.e)
    c, s = _rope_tables(S, QK_ROPE)
    z = lambda n: jnp.zeros((S, n), F32)
    one = lambda n: jnp.ones((S, n), F32)
    tq = jnp.stack([
        jnp.concatenate([one(QK_NOPE), c, c, z(32)], 1) * scale,
        jnp.concatenate([z(QK_NOPE), -s, s, z(32)], 1) * scale,
    ])
    ci, si = _rope_tables(S, IDX_ROPE)
    half = lambda a, b, rest: jnp.concatenate([a, b, rest], 1)
    ti = jnp.stack([
        jnp.tile(half(ci, ci, one(IDX_DIM - IDX_ROPE)), (1, 2)),
        jnp.tile(half(si, si, z(IDX_DIM - IDX_ROPE)), (1, 2)),
    ])
    tki = jnp.stack([
        jnp.concatenate([half(ci, ci, one(IDX_DIM - IDX_ROPE)), z(LANES - IDX_DIM)], 1),
        jnp.concatenate([half(-si, z(16), z(IDX_DIM - IDX_ROPE)), z(LANES - IDX_DIM)], 1),
        jnp.concatenate([half(z(16), si, z(IDX_DIM - IDX_ROPE)), z(LANES - IDX_DIM)], 1),
    ])
    tk = jnp.stack([
        jnp.concatenate([c, c, z(96)], 1),
        jnp.concatenate([-s, z(16), z(96)], 1),
        jnp.concatenate([z(16), s, z(96)], 1),
    ])
    return tq, ti, tki, tk


def kernel(x, p, g_mix, w_in, g_cq, g_ckv, w_uq, w_uk, w_uv, w_iq, w_a_proj, g_sgu, w_spatial,
           b_spatial, w_b_proj, w_o, g_ffn, w_gu, w_down, g_ple, w_ple_gate, w_ple_proj, g_final):
    B, S, D = x.shape
    T = B * S
    depth = w_in.shape[0]
    d_ff = w_down.shape[1]
    topk = min(TOPK_MAX, S // 4)
    TQ = 256
    KCH = min(512, S)
    tm = min(512, S)
    tq, ti, tki, tk = _lane_tables(S)

    h = x.reshape(T, D)
    for i in range(depth):
        wq3 = w_uq[i].reshape(Q_LORA, N_HEADS, QK_NOPE + QK_ROPE)
        x1, x2 = wq3[..., QK_NOPE:QK_NOPE + QK_ROPE // 2], wq3[..., QK_NOPE + QK_ROPE // 2:]
        wq = jnp.concatenate([wq3, x2, x1], axis=-1).reshape(Q_LORA, N_HEADS * LANES).astype(BF16)
        wk = jnp.pad(w_uk[i], ((0, 0), (0, 0), (0, LANES - QK_NOPE))).reshape(KV_LORA, N_HEADS * LANES).astype(BF16)
        wv = w_uv[i].reshape(KV_LORA, N_HEADS * V_HEAD).astype(BF16)
        wi3 = w_iq[i].reshape(Q_LORA, IDX_HEADS, IDX_DIM)
        wiq = w_iq[i].astype(BF16)
        wiqr = jnp.concatenate([-wi3[..., IDX_ROPE // 2:IDX_ROPE], wi3[..., :IDX_ROPE // 2],
                                jnp.zeros_like(wi3[..., IDX_ROPE:])], axis=-1).reshape(w_iq[i].shape).astype(BF16)

        small, z, gates = _in_proj(h, g_mix[i][None], jnp.transpose(w_in[i]), tm=min(1024, S), tn=512)
        q, k, v, qi, ki_lo, ki_hi, wt = _a_proj(small, g_cq[i][None], g_ckv[i][None], wq, wiq, wiqr, wk, wv,
                                                 tq, ti, tki, tk, B=B, S=S, tm=tm)
        bias = _dsa_index(ki_lo, ki_hi, qi, wt, B=B, S=S, TQ=TQ, KCH=KCH, topk=topk)
        o = _dsa_attn(q, k, v, bias, B=B, S=S, TQ=TQ)
        y_sgu = _sgu(z, g_sgu[i][None], w_spatial[i], jnp.transpose(b_spatial[i]), tm=tm)
        merged = _merge(o, y_sgu, gates, w_a_proj[i], w_b_proj[i], B=B, S=S, tm=tm, tn=1024)
        h1, n2 = _o_proj(h, merged, w_o[i], g_ffn[i][None], tm=min(256, S))
        act = _ffn_up(n2, w_gu[i], d_ff=d_ff, tm=min(2048, S), tn=512)
        h2 = _ffn_down(act, w_down[i], h1, tm=tm, tn=512)
        assert depth == 1
        h = _ple_final(h2, p[i].reshape(T, -1), w_ple_gate[i], w_ple_proj[i], g_ple[i][None],
                       g_final[None], tm=min(256, S))
    return h.reshape(B, S, D)
```

```python
import functools
import math

import numpy as np
import jax
import jax.numpy as jnp
from jax import lax
from jax.experimental import pallas as pl
from jax.experimental.pallas import tpu as pltpu

F32 = jnp.float32
BF16 = jnp.bfloat16

N_HEADS = 16
QK_NOPE = 64
QK_ROPE = 32
V_HEAD = 64
Q_LORA = 512
KV_LORA = 256
IDX_HEADS = 16
IDX_DIM = 64
IDX_ROPE = 32
TOPK_MAX = 256
SGU_CHUNK = 128
SGU_GROUPS = 8
SGU_GROUP_DIM = 128
SGU_WIDTH = SGU_GROUPS * SGU_GROUP_DIM
ROPE_THETA = 10000.0
EPS = 1e-6

LANES = 128
IN_SMALL = Q_LORA + KV_LORA + QK_ROPE + IDX_DIM + IDX_HEADS
SMALL_COLS = 1024
COL_CKV = Q_LORA
COL_MISC = Q_LORA + KV_LORA
MISC_KIDX = QK_ROPE
MISC_WIDX = QK_ROPE + IDX_DIM
VMEM_LIMIT_BYTES = 56 * 1024 * 1024
NEG_BIAS = -1e30
F32_MAX = 3.4028234663852886e38
STEPS_PER_CHECK = 4
COUNT_CHAINS = 4


def _cparams(semantics):
    return pltpu.CompilerParams(dimension_semantics=semantics, vmem_limit_bytes=VMEM_LIMIT_BYTES)


def _rms(x, g):
    return x * lax.rsqrt(jnp.mean(x * x, axis=-1, keepdims=True) + EPS) * g


def _dot(a, b):
    return jnp.dot(a, b, preferred_element_type=F32)


def _dot_nt(a, b):
    return lax.dot_general(a, b, (((1,), (1,)), ((), ())), preferred_element_type=F32)


def _gelu_exact(x):
    return 0.5 * x * (1.0 + lax.erf(x * (1.0 / math.sqrt(2.0))))


def _sigmoid(x):
    return 0.5 * jnp.tanh(0.5 * x) + 0.5


def _in_proj_kernel(x_ref, g_ref, w_ref, small_ref, z_ref, gate_ref, n_scr, *, n_small, n_z):
    j = pl.program_id(1)

    @pl.when(j == 0)
    def _():
        n_scr[...] = _rms(x_ref[...], g_ref[...]).astype(BF16)

    def proj():
        return _dot_nt(n_scr[...], w_ref[...].astype(BF16))

    @pl.when(j < n_small)
    def _():
        small_ref[...] = proj()

    @pl.when((j >= n_small) & (j < n_small + n_z))
    def _():
        z_ref[...] = _gelu_exact(proj()).astype(BF16)

    @pl.when(j >= n_small + n_z)
    def _():
        gate_ref[...] = _sigmoid(proj()).astype(BF16)


def _in_proj(x2, g_mix, w_t, *, tm, tn):
    T, D = x2.shape
    n_small = SMALL_COLS // tn
    n_z = (2 * SGU_WIDTH) // tn
    n_gate = (2 * D) // tn
    row_z = IN_SMALL
    row_gate = IN_SMALL + 2 * SGU_WIDTH
    assert w_t.shape == (row_gate + 2 * D, D) and row_z % 8 == 0
    grid = (T // tm, n_small + n_z + n_gate)
    kern = functools.partial(_in_proj_kernel, n_small=n_small, n_z=n_z)

    def w_row(i, j):
        tiles = jnp.where(j < n_small, j * (tn // 8),
                          jnp.where(j < n_small + n_z, row_z // 8 + (j - n_small) * (tn // 8),
                                    row_gate // 8 + (j - n_small - n_z) * (tn // 8)))
        return tiles * 8

    return pl.pallas_call(
        kern,
        grid=grid,
        in_specs=[
            pl.BlockSpec((tm, D), lambda i, j: (i, 0)),
            pl.BlockSpec((1, D), lambda i, j: (0, 0)),
            pl.BlockSpec((pl.Element(tn), pl.Element(D)), lambda i, j: (w_row(i, j), 0)),
        ],
        out_specs=[
            pl.BlockSpec((tm, tn), lambda i, j: (i, jnp.minimum(j, n_small - 1))),
            pl.BlockSpec((tm, tn), lambda i, j: (i, jnp.clip(j - n_small, 0, n_z - 1))),
            pl.BlockSpec((tm, tn), lambda i, j: (i, jnp.clip(j - n_small - n_z, 0, n_gate - 1))),
        ],
        out_shape=[
            jax.ShapeDtypeStruct((T, SMALL_COLS), F32),
            jax.ShapeDtypeStruct((T, 2 * SGU_WIDTH), BF16),
            jax.ShapeDtypeStruct((T, 2 * D), BF16),
        ],
        scratch_shapes=[pltpu.VMEM((tm, D), BF16)],
        compiler_params=_cparams(("parallel", "arbitrary")),
        name="in_proj",
    )(x2, g_mix, w_t)


def _rope_block(x, cos, sin_a, sin_b):
    return x * cos + pltpu.roll(x, LANES - 16, 1) * sin_a + pltpu.roll(x, 16, 1) * sin_b


def _a_proj_kernel(small_ref, gcq_ref, gckv_ref, wq_ref, wiq_ref, wiqr_ref, wk_ref, wv_ref, tab_ref,
                   q_ref, k_ref, v_ref, qi_ref, kilo_ref, kihi_ref, wt_ref):
    c_q = _rms(small_ref[:, 0:Q_LORA], gcq_ref[...]).astype(BF16)
    c_kv = _rms(small_ref[:, COL_CKV:COL_CKV + KV_LORA], gckv_ref[...]).astype(BF16)

    q = _dot(c_q, wq_ref[...])
    cq, sq = tab_ref[TAB_Q], tab_ref[TAB_Q + 1]
    for h in range(N_HEADS):
        blk = q[:, h * LANES:(h + 1) * LANES]
        q_ref[0, h] = (blk * cq + pltpu.roll(blk, LANES - QK_ROPE, 1) * sq).astype(BF16)

    ci, si = tab_ref[TAB_QI], tab_ref[TAB_QI + 1]
    qi = _dot(c_q, wiq_ref[...])
    qir = _dot(c_q, wiqr_ref[...])
    for hp in range(IDX_HEADS // 2):
        cols = slice(hp * LANES, (hp + 1) * LANES)
        qi_ref[:, cols] = (qi[:, cols] * ci + qir[:, cols] * si).astype(BF16)

    misc = small_ref[:, COL_MISC:COL_MISC + LANES]
    ki_lo = _rope_block(pltpu.roll(misc, LANES - MISC_KIDX, 1),
                        tab_ref[TAB_KI], tab_ref[TAB_KI + 1], tab_ref[TAB_KI + 2])
    kilo_ref[...] = ki_lo.astype(BF16)
    kihi_ref[...] = pltpu.roll(ki_lo, IDX_DIM, 1).astype(BF16)

    k_rope = pltpu.roll(_rope_block(misc, tab_ref[TAB_KR], tab_ref[TAB_KR + 1], tab_ref[TAB_KR + 2]),
                        QK_NOPE, 1)
    k_nope = _dot(c_kv, wk_ref[...])
    for h in range(N_HEADS):
        k_ref[0, h] = (k_nope[:, h * LANES:(h + 1) * LANES] + k_rope).astype(BF16)

    v = _dot(c_kv, wv_ref[...])
    for p in range(N_HEADS // 2):
        v_ref[0, p] = v[:, p * LANES:(p + 1) * LANES].astype(BF16)

    w_scale = IDX_HEADS ** -0.5 * IDX_DIM ** -0.5
    wt_ref[...] = misc.T[MISC_WIDX:MISC_WIDX + IDX_HEADS, :] * w_scale


def _a_proj(small, g_cq, g_ckv, wq, wiq, wiqr, wk, wv, tables, *, B, S, tm):
    T = B * S
    nt = S // tm
    const2 = lambda i: (0, 0)
    head_spec = lambda nh: pl.BlockSpec((1, nh, tm, LANES), lambda i: (i // nt, 0, i % nt, 0))
    return pl.pallas_call(
        _a_proj_kernel,
        grid=(T // tm,),
        in_specs=[
            pl.BlockSpec((tm, SMALL_COLS), lambda i: (i, 0)),
            pl.BlockSpec((1, Q_LORA), const2),
            pl.BlockSpec((1, KV_LORA), const2),
            pl.BlockSpec(wq.shape, const2),
            pl.BlockSpec(wiq.shape, const2),
            pl.BlockSpec(wiqr.shape, const2),
            pl.BlockSpec(wk.shape, const2),
            pl.BlockSpec(wv.shape, const2),
            pl.BlockSpec((N_TABLES, tm, LANES), lambda i: (0, i % nt, 0)),
        ],
        out_specs=[
            head_spec(N_HEADS),
            head_spec(N_HEADS),
            head_spec(N_HEADS // 2),
            pl.BlockSpec((tm, IDX_HEADS * IDX_DIM), lambda i: (i, 0)),
            pl.BlockSpec((tm, LANES), lambda i: (i, 0)),
            pl.BlockSpec((tm, LANES), lambda i: (i, 0)),
            pl.BlockSpec((IDX_HEADS, tm), lambda i: (0, i)),
        ],
        out_shape=[
            jax.ShapeDtypeStruct((B, N_HEADS, S, LANES), BF16),
            jax.ShapeDtypeStruct((B, N_HEADS, S, LANES), BF16),
            jax.ShapeDtypeStruct((B, N_HEADS // 2, S, LANES), BF16),
            jax.ShapeDtypeStruct((T, IDX_HEADS * IDX_DIM), BF16),
            jax.ShapeDtypeStruct((T, LANES), BF16),
            jax.ShapeDtypeStruct((T, LANES), BF16),
            jax.ShapeDtypeStruct((IDX_HEADS, T), F32),
        ],
        compiler_params=_cparams(("parallel",)),
        name="a_proj",
    )(small, g_cq, g_ckv, wq, wiq, wiqr, wk, wv, tables)


def _dsa_index_kernel(kilo_ref, kihi_ref, qi_ref, wt_ref, bias_ref, isc_ref, mm_ref, js_ref,
                      *, S, TQ, KCH, topk):
    j = pl.program_id(1)
    q0 = j * TQ
    nkeys = q0 + TQ
    qidx = q0 + lax.broadcasted_iota(jnp.int32, (1, TQ), 1)
    kf = float(topk)

    mm_ref[0:8, :] = jnp.full((8, TQ), jnp.inf, F32)
    mm_ref[8:16, :] = jnp.full((8, TQ), -jnp.inf, F32)
    for c in range(S // KCH):
        @pl.when(c * KCH < nkeys)
        def _(c=c):
            klo = kilo_ref[c * KCH:(c + 1) * KCH, :]
            khi = kihi_ref[c * KCH:(c + 1) * KCH, :]
            acc = jnp.zeros((KCH, TQ), F32)
            for hp in range(IDX_HEADS // 2):
                qp = qi_ref[:, hp * LANES:(hp + 1) * LANES]
                s0 = _dot_nt(klo, qp)
                s1 = _dot_nt(khi, qp)
                acc = acc + jnp.maximum(s0, 0.0) * wt_ref[2 * hp:2 * hp + 1, :]
                acc = acc + jnp.maximum(s1, 0.0) * wt_ref[2 * hp + 1:2 * hp + 2, :]
            kidx = c * KCH + lax.broadcasted_iota(jnp.int32, (KCH, TQ), 0)
            causal = kidx <= qidx
            isc_ref[c * KCH:(c + 1) * KCH, :] = jnp.where(causal, acc, -jnp.inf)
            lo_part = jnp.where(causal, acc, jnp.inf).reshape(KCH // 8, 8, TQ).min(axis=0)
            hi_part = jnp.where(causal, acc, -jnp.inf).reshape(KCH // 8, 8, TQ).max(axis=0)
            mm_ref[0:8, :] = jnp.minimum(mm_ref[0:8, :], lo_part)
            mm_ref[8:16, :] = jnp.maximum(mm_ref[8:16, :], hi_part)

    def select(nch):
        def count(pred):
            acc = jnp.zeros((COUNT_CHAINS, 8, TQ), F32)
            for c in range(nch):
                ones = jnp.where(pred(isc_ref[c * TQ:(c + 1) * TQ, :], c * TQ), 1.0, 0.0)
                acc = acc + ones.reshape(COUNT_CHAINS, TQ // (8 * COUNT_CHAINS), 8, TQ).sum(axis=1)
            return acc.sum(axis=0).sum(axis=0, keepdims=True)

        row_min = mm_ref[0:8, :].min(axis=0, keepdims=True)
        row_max = mm_ref[8:16, :].max(axis=0, keepdims=True)
        full = (qidx + 1) <= topk
        c_max = count(lambda blk, k0: blk >= row_max)
        exact0 = c_max == kf
        tie0 = c_max > kf
        settled0 = full | exact0 | tie0
        lo0 = jnp.where(full, -F32_MAX, jnp.where(settled0, row_max, row_min))
        hi0 = jnp.where(full, -F32_MAX, jnp.where(tie0, jnp.inf, row_max))
        act0 = jnp.where(settled0, 0.0, 1.0)

        def step(lo, hi, act):
            mid = lo * 0.5 + hi * 0.5
            inside = (mid > lo) & (mid < hi)
            cnt = count(lambda blk, k0: blk >= mid)
            upd = (act > 0.0) & inside
            found = upd & (cnt == kf)
            lo2 = jnp.where(upd & (cnt >= kf), mid, lo)
            hi2 = jnp.where(upd & (cnt <= kf), mid, hi)
            return lo2, hi2, jnp.where(upd & jnp.logical_not(found), 1.0, 0.0)

        def any_active(act):
            return (jnp.max(act) > 0.0).astype(jnp.int32)

        def body(st):
            lo, hi, act, _ = st
            for _ in range(STEPS_PER_CHECK):
                lo, hi, act = step(lo, hi, act)
            return lo, hi, act, any_active(act)

        lo, hi, _, _ = lax.while_loop(lambda st: st[3] > 0, body, (lo0, hi0, act0, any_active(act0)))

        tie = lo < hi
        js_ref[...] = jnp.full((8, TQ), -1.0, F32)

        @pl.when(jnp.max(jnp.where(tie, 1.0, 0.0)) > 0.0)
        def _():
            need = kf - count(lambda blk, k0: blk >= hi)

            def kpos(k0):
                return (k0 + lax.broadcasted_iota(jnp.int32, (TQ, TQ), 0)).astype(F32)

            def tie_step(_, st):
                ilo, ihi = st
                imid = jnp.floor((ilo + ihi) * 0.5)
                cnt = count(lambda blk, k0: (blk >= lo) & (blk < hi) & (kpos(k0) <= imid))
                ge = cnt >= need
                return jnp.where(ge, ilo, imid), jnp.where(ge, imid, ihi)

            nsteps = int(math.ceil(math.log2(S))) + 1
            _, ihi = lax.fori_loop(0, nsteps, tie_step,
                                   (jnp.full((1, TQ), -1.0, F32), jnp.full((1, TQ), S - 1.0, F32)))
            js_ref[0:1, :] = jnp.where(tie, ihi, -1.0)

        jstar = js_ref[0:1, :]

        for c in range(S // TQ):
            if c < nch:
                blk = isc_ref[c * TQ:(c + 1) * TQ, :]
                kpos_c = (c * TQ + lax.broadcasted_iota(jnp.int32, (TQ, TQ), 0)).astype(F32)
                sel = (blk >= hi) | ((blk >= lo) & (kpos_c <= jstar))
                bias_ref[:, c * TQ:(c + 1) * TQ] = jnp.where(sel, 0.0, NEG_BIAS).T
            else:
                bias_ref[:, c * TQ:(c + 1) * TQ] = jnp.full((TQ, TQ), NEG_BIAS, F32)

    for jj in range(S // TQ):
        @pl.when(j == jj)
        def _(jj=jj):
            select(jj + 1)


def _dsa_index(ki_lo, ki_hi, qi, wt, *, B, S, TQ, KCH, topk):
    T = B * S
    nq = S // TQ
    kern = functools.partial(_dsa_index_kernel, S=S, TQ=TQ, KCH=KCH, topk=topk)
    return pl.pallas_call(
        kern,
        grid=(B, nq),
        in_specs=[
            pl.BlockSpec((S, LANES), lambda b, j: (b, 0)),
            pl.BlockSpec((S, LANES), lambda b, j: (b, 0)),
            pl.BlockSpec((TQ, IDX_HEADS * IDX_DIM), lambda b, j: (b * nq + j, 0)),
            pl.BlockSpec((IDX_HEADS, TQ), lambda b, j: (0, b * nq + j)),
        ],
        out_specs=pl.BlockSpec((TQ, S), lambda b, j: (b * nq + j, 0)),
        out_shape=jax.ShapeDtypeStruct((T, S), F32),
        scratch_shapes=[
            pltpu.VMEM((S, TQ), F32),
            pltpu.VMEM((16, TQ), F32),
            pltpu.VMEM((8, TQ), F32),
        ],
        compiler_params=_cparams(("parallel", "arbitrary")),
        name="dsa_index",
    )(ki_lo, ki_hi, qi, wt)


def _dsa_attn_kernel(q_ref, k_ref, v_ref, bias_ref, o_ref, *, S, TQ):
    j = pl.program_id(1)
    lane = lax.broadcasted_iota(jnp.int32, (TQ, LANES), 1)

    def variant(nk):
        ones = jnp.ones((nk, LANES), BF16)

        def pair(p, carry):
            vp = jnp.concatenate([v_ref[0, p, 0:nk, :], ones], axis=1)
            outs = []
            for e in range(2):
                h = 2 * p + e
                s = _dot_nt(q_ref[0, h], k_ref[0, h, 0:nk, :]) + bias_ref[:, 0:nk]
                m = s.max(axis=1, keepdims=True)
                pv = _dot(jnp.exp2(s - m).astype(BF16), vp)
                outs.append(pv[:, 0:LANES] * (1.0 / pv[:, LANES:2 * LANES]))
            o_ref[0, p] = jnp.where(lane < V_HEAD, outs[0], outs[1]).astype(BF16)
            return carry
        lax.fori_loop(0, N_HEADS // 2, pair, 0, unroll=True)

    for jj in range(S // TQ):
        @pl.when(j == jj)
        def _(jj=jj):
            variant((jj + 1) * TQ)


def _dsa_attn(q, k, v, bias, *, B, S, TQ):
    nq = S // TQ
    kern = functools.partial(_dsa_attn_kernel, S=S, TQ=TQ)
    return pl.pallas_call(
        kern,
        grid=(B, nq),
        in_specs=[
            pl.BlockSpec((1, N_HEADS, TQ, LANES), lambda b, j: (b, 0, j, 0)),
            pl.BlockSpec((1, N_HEADS, S, LANES), lambda b, j: (b, 0, 0, 0)),
            pl.BlockSpec((1, N_HEADS // 2, S, LANES), lambda b, j: (b, 0, 0, 0)),
            pl.BlockSpec((TQ, S), lambda b, j: (b * nq + j, 0)),
        ],
        out_specs=pl.BlockSpec((1, N_HEADS // 2, TQ, LANES), lambda b, j: (b, 0, j, 0)),
        out_shape=jax.ShapeDtypeStruct((B, N_HEADS // 2, S, LANES), BF16),
        compiler_params=_cparams(("parallel", "arbitrary")),
        name="dsa_attn",
    )(q, k, v, bias)


def _sgu_kernel(z_ref, g_ref, ws_ref, bt_ref, y_ref, *, tm):
    row = lax.broadcasted_iota(jnp.int32, (SGU_CHUNK, SGU_CHUNK), 0)
    col = lax.broadcasted_iota(jnp.int32, (SGU_CHUNK, SGU_CHUNK), 1)
    tril = col <= row
    w = [jnp.where(tril, ws_ref[g], 0.0).astype(BF16) for g in range(SGU_GROUPS)]
    for cc in range(tm // SGU_CHUNK):
        rows = slice(cc * SGU_CHUNK, (cc + 1) * SGU_CHUNK)
        vn = _rms(z_ref[rows, SGU_WIDTH:2 * SGU_WIDTH].astype(F32), g_ref[...]).astype(BF16)
        for g in range(SGU_GROUPS):
            cols = slice(g * SGU_GROUP_DIM, (g + 1) * SGU_GROUP_DIM)
            mixed = _dot(w[g], vn[:, cols]) + bt_ref[:, g:g + 1]
            y_ref[rows, cols] = (z_ref[rows, cols].astype(F32) * mixed).astype(BF16)


def _sgu(z, g_sgu, w_spatial, b_t, *, tm):
    T = z.shape[0]
    kern = functools.partial(_sgu_kernel, tm=tm)
    return pl.pallas_call(
        kern,
        grid=(T // tm,),
        in_specs=[
            pl.BlockSpec((tm, 2 * SGU_WIDTH), lambda i: (i, 0)),
            pl.BlockSpec((1, SGU_WIDTH), lambda i: (0, 0)),
            pl.BlockSpec((SGU_GROUPS, SGU_CHUNK, SGU_CHUNK), lambda i: (0, 0, 0)),
            pl.BlockSpec((SGU_CHUNK, SGU_GROUPS), lambda i: (0, 0)),
        ],
        out_specs=pl.BlockSpec((tm, SGU_WIDTH), lambda i: (i, 0)),
        out_shape=jax.ShapeDtypeStruct((T, SGU_WIDTH), BF16),
        compiler_params=_cparams(("parallel",)),
        name="sgu",
    )(z, g_sgu, w_spatial, b_t)


def _merge_kernel(o_ref, y_ref, ga_ref, gb_ref, wa_ref, wb_ref, out_ref, wa_bf, wb_bf):
    @pl.when(pl.program_id(1) == 0)
    def _():
        wa_bf[...] = wa_ref[...].astype(BF16)
        wb_bf[...] = wb_ref[...].astype(BF16)

    o_a = jnp.concatenate([o_ref[0, p] for p in range(N_HEADS // 2)], axis=1)
    ya = _dot(o_a, wa_bf[...])
    yb = _dot(y_ref[...], wb_bf[...])
    out_ref[...] = (ga_ref[...].astype(F32) * ya + gb_ref[...].astype(F32) * yb).astype(BF16)


def _merge(o, y_sgu, gates, wa, wb, *, B, S, tm, tn):
    T = B * S
    D = wa.shape[1]
    nt = S // tm
    nn = D // tn
    return pl.pallas_call(
        _merge_kernel,
        grid=(nn, T // tm),
        in_specs=[
            pl.BlockSpec((1, N_HEADS // 2, tm, LANES), lambda j, i: (i // nt, 0, i % nt, 0)),
            pl.BlockSpec((tm, SGU_WIDTH), lambda j, i: (i, 0)),
            pl.BlockSpec((tm, tn), lambda j, i: (i, j)),
            pl.BlockSpec((tm, tn), lambda j, i: (i, nn + j)),
            pl.BlockSpec((wa.shape[0], tn), lambda j, i: (0, j)),
            pl.BlockSpec((wb.shape[0], tn), lambda j, i: (0, j)),
        ],
        out_specs=pl.BlockSpec((tm, tn), lambda j, i: (i, j)),
        out_shape=jax.ShapeDtypeStruct((T, D), BF16),
        scratch_shapes=[pltpu.VMEM((wa.shape[0], tn), BF16), pltpu.VMEM((wb.shape[0], tn), BF16)],
        compiler_params=_cparams(("arbitrary", "arbitrary")),
        name="merge",
    )(o, y_sgu, gates, gates, wa, wb)


def _o_proj_kernel(x_ref, m_ref, w_ref, g_ref, h_ref, n_ref, w_bf):
    @pl.when(pl.program_id(0) == 0)
    def _():
        w_bf[...] = w_ref[...].astype(BF16)

    h = x_ref[...] + _dot(m_ref[...], w_bf[...])
    h_ref[...] = h
    n_ref[...] = _rms(h, g_ref[...]).astype(BF16)


def _o_proj(x2, merged, w_o, g_ffn, *, tm):
    T, D = x2.shape
    return pl.pallas_call(
        _o_proj_kernel,
        grid=(T // tm,),
        in_specs=[
            pl.BlockSpec((tm, D), lambda i: (i, 0)),
            pl.BlockSpec((tm, D), lambda i: (i, 0)),
            pl.BlockSpec((D, D), lambda i: (0, 0), pipeline_mode=pl.Buffered(1)),
            pl.BlockSpec((1, D), lambda i: (0, 0)),
        ],
        out_specs=[pl.BlockSpec((tm, D), lambda i: (i, 0)), pl.BlockSpec((tm, D), lambda i: (i, 0))],
        out_shape=[jax.ShapeDtypeStruct((T, D), F32), jax.ShapeDtypeStruct((T, D), BF16)],
        scratch_shapes=[pltpu.VMEM((D, D), BF16)],
        compiler_params=_cparams(("arbitrary",)),
        name="o_proj",
    )(x2, merged, w_o, g_ffn)


def _ffn_up_kernel(n_ref, wg_ref, wu_ref, a_ref, wg_bf, wu_bf):
    @pl.when(pl.program_id(1) == 0)
    def _():
        wg_bf[...] = wg_ref[...].astype(BF16)
        wu_bf[...] = wu_ref[...].astype(BF16)

    n = n_ref[...]
    g = _dot(n, wg_bf[...])
    u = _dot(n, wu_bf[...])
    a_ref[...] = (g * _sigmoid(g) * u).astype(BF16)


def _ffn_up(n2, w_gu, *, d_ff, tm, tn):
    T, D = n2.shape
    nn = d_ff // tn
    return pl.pallas_call(
        _ffn_up_kernel,
        grid=(nn, T // tm),
        in_specs=[
            pl.BlockSpec((tm, D), lambda j, i: (i, 0)),
            pl.BlockSpec((D, tn), lambda j, i: (0, j)),
            pl.BlockSpec((D, tn), lambda j, i: (0, nn + j)),
        ],
        out_specs=pl.BlockSpec((tm, tn), lambda j, i: (i, j)),
        out_shape=jax.ShapeDtypeStruct((T, d_ff), BF16),
        scratch_shapes=[pltpu.VMEM((D, tn), BF16), pltpu.VMEM((D, tn), BF16)],
        compiler_params=_cparams(("arbitrary", "arbitrary")),
        name="ffn_up",
    )(n2, w_gu, w_gu)


def _ffn_down_kernel(a_ref, w_ref, h1_ref, h2_ref, w_bf):
    @pl.when(pl.program_id(1) == 0)
    def _():
        w_bf[...] = w_ref[...].astype(BF16)

    h2_ref[...] = h1_ref[...] + _dot(a_ref[...], w_bf[...])


def _ffn_down(act, w_down, h1, *, tm, tn):
    T, D = h1.shape
    d_ff = act.shape[1]
    return pl.pallas_call(
        _ffn_down_kernel,
        grid=(D // tn, T // tm),
        in_specs=[
            pl.BlockSpec((tm, d_ff), lambda j, i: (i, 0)),
            pl.BlockSpec((d_ff, tn), lambda j, i: (0, j)),
            pl.BlockSpec((tm, tn), lambda j, i: (i, j)),
        ],
        out_specs=pl.BlockSpec((tm, tn), lambda j, i: (i, j)),
        out_shape=jax.ShapeDtypeStruct((T, D), F32),
        scratch_shapes=[pltpu.VMEM((d_ff, tn), BF16)],
        compiler_params=_cparams(("arbitrary", "arbitrary")),
        name="ffn_down",
    )(act, w_down, h1)


def _ple_final_kernel(h2_ref, p_ref, wg_ref, wp_ref, gp_ref, gf_ref, out_ref, wg_bf, wp_bf):
    @pl.when(pl.program_id(0) == 0)
    def _():
        wg_bf[...] = wg_ref[...].astype(BF16)
        wp_bf[...] = wp_ref[...].astype(BF16)

    h2 = h2_ref[...]
    gate = _sigmoid(_dot(_rms(h2, gp_ref[...]).astype(BF16), wg_bf[...]))
    pp = _dot(p_ref[...].astype(BF16), wp_bf[...])
    out_ref[...] = _rms(h2 + gate * pp, gf_ref[...])


def _ple_final(h2, p2, w_pg, w_pp, g_ple, g_final, *, tm):
    T, D = h2.shape
    P = p2.shape[1]
    once = pl.Buffered(1)
    return pl.pallas_call(
        _ple_final_kernel,
        grid=(T // tm,),
        in_specs=[
            pl.BlockSpec((tm, D), lambda i: (i, 0)),
            pl.BlockSpec((tm, P), lambda i: (i, 0)),
            pl.BlockSpec((D, D), lambda i: (0, 0), pipeline_mode=once),
            pl.BlockSpec((P, D), lambda i: (0, 0), pipeline_mode=once),
            pl.BlockSpec((1, D), lambda i: (0, 0)),
            pl.BlockSpec((1, D), lambda i: (0, 0)),
        ],
        out_specs=pl.BlockSpec((tm, D), lambda i: (i, 0)),
        out_shape=jax.ShapeDtypeStruct((T, D), F32),
        scratch_shapes=[pltpu.VMEM((D, D), BF16), pltpu.VMEM((P, D), BF16)],
        compiler_params=_cparams(("arbitrary",)),
        name="ple_final",
    )(h2, p2, w_pg, w_pp, g_ple, g_final)


TAB_Q = 0
TAB_QI = 2
TAB_KI = 4
TAB_KR = 7
N_TABLES = 10


def _lane_tables(S):
    assert QK_ROPE == IDX_ROPE == 32
    scale = (QK_NOPE + QK_ROPE) ** -0.5 * math.log2(math.e)
    a = np.zeros((N_TABLES, LANES), np.float32)
    bc = np.zeros_like(a)
    bs = np.zeros_like(a)
    a[TAB_Q, 0:QK_NOPE] = scale
    bc[TAB_Q, 64:96] = scale
    bs[TAB_Q + 1, 64:80] = -scale
    bs[TAB_Q + 1, 80:96] = scale
    for o in (0, IDX_DIM):
        bc[TAB_QI, o:o + 32] = 1.0
        a[TAB_QI, o + 32:o + 64] = 1.0
        bs[TAB_QI + 1, o:o + 32] = 1.0
    bc[TAB_KI, 0:32] = 1.0
    a[TAB_KI, 32:64] = 1.0
    bs[TAB_KI + 1, 0:16] = -1.0
    bs[TAB_KI + 2, 16:32] = 1.0
    bc[TAB_KR, 0:32] = 1.0
    bs[TAB_KR + 1, 0:16] = -1.0
    bs[TAB_KR + 2, 16:32] = 1.0
    inv = ROPE_THETA ** (-jnp.arange(0, QK_ROPE, 2, dtype=F32) / QK_ROPE)
    ang = jnp.arange(S, dtype=F32)[:, None] * jnp.tile(inv, LANES // inv.shape[0])[None, :]
    return a[:, None, :] + bc[:, None, :] * jnp.cos(ang)[None] + bs[:, None, :] * jnp.sin(ang)[None]


def kernel(x, p, g_mix, w_in, g_cq, g_ckv, w_uq, w_uk, w_uv, w_iq, w_a_proj, g_sgu, w_spatial,
           b_spatial, w_b_proj, w_o, g_ffn, w_gu, w_down, g_ple, w_ple_gate, w_ple_proj, g_final):
    B, S, D = x.shape
    T = B * S
    depth = w_in.shape[0]
    d_ff = w_down.shape[1]
    topk = min(TOPK_MAX, S // 4)
    TQ = 256
    KCH = min(512, S)
    tm = min(512, S)
    tables = _lane_tables(S)

    h = x.reshape(T, D)
    for i in range(depth):
        wq3 = w_uq[i].reshape(Q_LORA, N_HEADS, QK_NOPE + QK_ROPE)
        x1, x2 = wq3[..., QK_NOPE:QK_NOPE + QK_ROPE // 2], wq3[..., QK_NOPE + QK_ROPE // 2:]
        wq = jnp.concatenate([wq3, x2, x1], axis=-1).reshape(Q_LORA, N_HEADS * LANES).astype(BF16)
        wk = jnp.pad(w_uk[i], ((0, 0), (0, 0), (0, LANES - QK_NOPE))).reshape(KV_LORA, N_HEADS * LANES).astype(BF16)
        wv = w_uv[i].reshape(KV_LORA, N_HEADS * V_HEAD).astype(BF16)
        wi3 = w_iq[i].reshape(Q_LORA, IDX_HEADS, IDX_DIM)
        wiq = w_iq[i].astype(BF16)
        wiqr = jnp.concatenate([-wi3[..., IDX_ROPE // 2:IDX_ROPE], wi3[..., :IDX_ROPE // 2],
                                jnp.zeros_like(wi3[..., IDX_ROPE:])], axis=-1).reshape(w_iq[i].shape).astype(BF16)

        small, z, gates = _in_proj(h, g_mix[i][None], jnp.transpose(w_in[i]), tm=min(1024, S), tn=512)
        q, k, v, qi, ki_lo, ki_hi, wt = _a_proj(small, g_cq[i][None], g_ckv[i][None], wq, wiq, wiqr, wk, wv,
                                                 tables, B=B, S=S, tm=tm)
        bias = _dsa_index(ki_lo, ki_hi, qi, wt, B=B, S=S, TQ=TQ, KCH=KCH, topk=topk)
        o = _dsa_attn(q, k, v, bias, B=B, S=S, TQ=TQ)
        y_sgu = _sgu(z, g_sgu[i][None], w_spatial[i], jnp.transpose(b_spatial[i]), tm=tm)
        merged = _merge(o, y_sgu, gates, w_a_proj[i], w_b_proj[i], B=B, S=S, tm=tm, tn=1024)
        h1, n2 = _o_proj(h, merged, w_o[i], g_ffn[i][None], tm=min(256, S))
        act = _ffn_up(n2, w_gu[i], d_ff=d_ff, tm=min(1024, S), tn=512)
        h2 = _ffn_down(act, w_down[i], h1, tm=tm, tn=512)
        assert depth == 1
        h = _ple_final(h2, p[i].reshape(T, -1), w_ple_gate[i], w_ple_proj[i], g_ple[i][None],
                       g_final[None], tm=min(256, S))
    return h.reshape(B, S, D)
```

```python
import functools
import math

import numpy as np
import jax
import jax.numpy as jnp
from jax import lax
from jax.experimental import pallas as pl
from jax.experimental.pallas import tpu as pltpu

F32 = jnp.float32
BF16 = jnp.bfloat16

N_HEADS = 16
QK_NOPE = 64
QK_ROPE = 32
V_HEAD = 64
Q_LORA = 512
KV_LORA = 256
IDX_HEADS = 16
IDX_DIM = 64
IDX_ROPE = 32
TOPK_MAX = 256
SGU_CHUNK = 128
SGU_GROUPS = 8
SGU_GROUP_DIM = 128
SGU_WIDTH = SGU_GROUPS * SGU_GROUP_DIM
ROPE_THETA = 10000.0
EPS = 1e-6

LANES = 128
IN_SMALL = Q_LORA + KV_LORA + QK_ROPE + IDX_DIM + IDX_HEADS
SMALL_COLS = 1024
COL_CKV = Q_LORA
COL_MISC = Q_LORA + KV_LORA
MISC_KIDX = QK_ROPE
MISC_WIDX = QK_ROPE + IDX_DIM
VMEM_LIMIT_BYTES = 56 * 1024 * 1024
NEG_BIAS = -1e30
F32_MAX = 3.4028234663852886e38
STEPS_PER_CHECK = 4
COUNT_CHAINS = 4


def _cparams(semantics):
    return pltpu.CompilerParams(dimension_semantics=semantics, vmem_limit_bytes=VMEM_LIMIT_BYTES)


def _rms(x, g):
    return x * lax.rsqrt(jnp.mean(x * x, axis=-1, keepdims=True) + EPS) * g


def _dot(a, b):
    return jnp.dot(a, b, preferred_element_type=F32)


def _dot_nt(a, b):
    return lax.dot_general(a, b, (((1,), (1,)), ((), ())), preferred_element_type=F32)


def _gelu_exact(x):
    return 0.5 * x * (1.0 + lax.erf(x * (1.0 / math.sqrt(2.0))))


def _sigmoid(x):
    return 0.5 * jnp.tanh(0.5 * x) + 0.5


def _in_proj_kernel(x_ref, g_ref, w_ref, small_ref, z_ref, gate_ref, n_scr, *, n_small, n_z):
    j = pl.program_id(1)

    @pl.when(j == 0)
    def _():
        n_scr[...] = _rms(x_ref[...], g_ref[...]).astype(BF16)

    def proj():
        return _dot_nt(n_scr[...], w_ref[...].astype(BF16))

    @pl.when(j < n_small)
    def _():
        small_ref[...] = proj()

    @pl.when((j >= n_small) & (j < n_small + n_z))
    def _():
        z_ref[...] = _gelu_exact(proj()).astype(BF16)

    @pl.when(j >= n_small + n_z)
    def _():
        gate_ref[...] = _sigmoid(proj()).astype(BF16)


def _in_proj(x2, g_mix, w_t, *, tm, tn):
    T, D = x2.shape
    n_small = SMALL_COLS // tn
    n_z = (2 * SGU_WIDTH) // tn
    n_gate = (2 * D) // tn
    row_z = IN_SMALL
    row_gate = IN_SMALL + 2 * SGU_WIDTH
    assert w_t.shape == (row_gate + 2 * D, D) and row_z % 8 == 0
    grid = (T // tm, n_small + n_z + n_gate)
    kern = functools.partial(_in_proj_kernel, n_small=n_small, n_z=n_z)

    def w_row(i, j):
        tiles = jnp.where(j < n_small, j * (tn // 8),
                          jnp.where(j < n_small + n_z, row_z // 8 + (j - n_small) * (tn // 8),
                                    row_gate // 8 + (j - n_small - n_z) * (tn // 8)))
        return tiles * 8

    return pl.pallas_call(
        kern,
        grid=grid,
        in_specs=[
            pl.BlockSpec((tm, D), lambda i, j: (i, 0)),
            pl.BlockSpec((1, D), lambda i, j: (0, 0)),
            pl.BlockSpec((pl.Element(tn), pl.Element(D)), lambda i, j: (w_row(i, j), 0)),
        ],
        out_specs=[
            pl.BlockSpec((tm, tn), lambda i, j: (i, jnp.minimum(j, n_small - 1))),
            pl.BlockSpec((tm, tn), lambda i, j: (i, jnp.clip(j - n_small, 0, n_z - 1))),
            pl.BlockSpec((tm, tn), lambda i, j: (i, jnp.clip(j - n_small - n_z, 0, n_gate - 1))),
        ],
        out_shape=[
            jax.ShapeDtypeStruct((T, SMALL_COLS), F32),
            jax.ShapeDtypeStruct((T, 2 * SGU_WIDTH), BF16),
            jax.ShapeDtypeStruct((T, 2 * D), BF16),
        ],
        scratch_shapes=[pltpu.VMEM((tm, D), BF16)],
        compiler_params=_cparams(("parallel", "arbitrary")),
        name="in_proj",
    )(x2, g_mix, w_t)


def _rope_block(x, cos, sin_a, sin_b):
    return x * cos + pltpu.roll(x, LANES - 16, 1) * sin_a + pltpu.roll(x, 16, 1) * sin_b


def _a_proj_kernel(small_ref, gcq_ref, gckv_ref, wq_ref, wiq_ref, wiqr_ref, wk_ref, wv_ref, tab_ref,
                   q_ref, k_ref, v_ref, qi_ref, kilo_ref, kihi_ref, wt_ref):
    c_q = _rms(small_ref[:, 0:Q_LORA], gcq_ref[...]).astype(BF16)
    c_kv = _rms(small_ref[:, COL_CKV:COL_CKV + KV_LORA], gckv_ref[...]).astype(BF16)

    q = _dot(c_q, wq_ref[...])
    cq, sq = tab_ref[TAB_Q], tab_ref[TAB_Q + 1]
    for h in range(N_HEADS):
        blk = q[:, h * LANES:(h + 1) * LANES]
        q_ref[0, h] = (blk * cq + pltpu.roll(blk, LANES - QK_ROPE, 1) * sq).astype(BF16)

    ci, si = tab_ref[TAB_QI], tab_ref[TAB_QI + 1]
    qi = _dot(c_q, wiq_ref[...])
    qir = _dot(c_q, wiqr_ref[...])
    for hp in range(IDX_HEADS // 2):
        cols = slice(hp * LANES, (hp + 1) * LANES)
        qi_ref[:, cols] = (qi[:, cols] * ci + qir[:, cols] * si).astype(BF16)

    misc = small_ref[:, COL_MISC:COL_MISC + LANES]
    ki_lo = _rope_block(pltpu.roll(misc, LANES - MISC_KIDX, 1),
                        tab_ref[TAB_KI], tab_ref[TAB_KI + 1], tab_ref[TAB_KI + 2])
    kilo_ref[...] = ki_lo.astype(BF16)
    kihi_ref[...] = pltpu.roll(ki_lo, IDX_DIM, 1).astype(BF16)

    k_rope = pltpu.roll(_rope_block(misc, tab_ref[TAB_KR], tab_ref[TAB_KR + 1], tab_ref[TAB_KR + 2]),
                        QK_NOPE, 1)
    k_nope = _dot(c_kv, wk_ref[...])
    for h in range(N_HEADS):
        k_ref[0, h] = (k_nope[:, h * LANES:(h + 1) * LANES] + k_rope).astype(BF16)

    v = _dot(c_kv, wv_ref[...])
    for p in range(N_HEADS // 2):
        v_ref[0, p] = v[:, p * LANES:(p + 1) * LANES].astype(BF16)

    w_scale = IDX_HEADS ** -0.5 * IDX_DIM ** -0.5
    wt_ref[...] = misc.T[MISC_WIDX:MISC_WIDX + IDX_HEADS, :] * w_scale


def _a_proj(small, g_cq, g_ckv, wq, wiq, wiqr, wk, wv, tables, *, B, S, tm):
    T = B * S
    nt = S // tm
    const2 = lambda i: (0, 0)
    head_spec = lambda nh: pl.BlockSpec((1, nh, tm, LANES), lambda i: (i // nt, 0, i % nt, 0))
    return pl.pallas_call(
        _a_proj_kernel,
        grid=(T // tm,),
        in_specs=[
            pl.BlockSpec((tm, SMALL_COLS), lambda i: (i, 0)),
            pl.BlockSpec((1, Q_LORA), const2),
            pl.BlockSpec((1, KV_LORA), const2),
            pl.BlockSpec(wq.shape, const2),
            pl.BlockSpec(wiq.shape, const2),
            pl.BlockSpec(wiqr.shape, const2),
            pl.BlockSpec(wk.shape, const2),
            pl.BlockSpec(wv.shape, const2),
            pl.BlockSpec((N_TABLES, tm, LANES), lambda i: (0, i % nt, 0)),
        ],
        out_specs=[
            head_spec(N_HEADS),
            head_spec(N_HEADS),
            head_spec(N_HEADS // 2),
            pl.BlockSpec((tm, IDX_HEADS * IDX_DIM), lambda i: (i, 0)),
            pl.BlockSpec((tm, LANES), lambda i: (i, 0)),
            pl.BlockSpec((tm, LANES), lambda i: (i, 0)),
            pl.BlockSpec((IDX_HEADS, tm), lambda i: (0, i)),
        ],
        out_shape=[
            jax.ShapeDtypeStruct((B, N_HEADS, S, LANES), BF16),
            jax.ShapeDtypeStruct((B, N_HEADS, S, LANES), BF16),
            jax.ShapeDtypeStruct((B, N_HEADS // 2, S, LANES), BF16),
            jax.ShapeDtypeStruct((T, IDX_HEADS * IDX_DIM), BF16),
            jax.ShapeDtypeStruct((T, LANES), BF16),
            jax.ShapeDtypeStruct((T, LANES), BF16),
            jax.ShapeDtypeStruct((IDX_HEADS, T), F32),
        ],
        compiler_params=_cparams(("parallel",)),
        name="a_proj",
    )(small, g_cq, g_ckv, wq, wiq, wiqr, wk, wv, tables)


def _dsa_index_kernel(kilo_ref, kihi_ref, qi_ref, wt_ref, bias_ref, isc_ref, mm_ref, js_ref,
                      *, S, TQ, KCH, topk):
    j = pl.program_id(1)
    q0 = j * TQ
    nkeys = q0 + TQ
    qidx = q0 + lax.broadcasted_iota(jnp.int32, (1, TQ), 1)
    kf = float(topk)

    mm_ref[0:8, :] = jnp.full((8, TQ), jnp.inf, F32)
    mm_ref[8:16, :] = jnp.full((8, TQ), -jnp.inf, F32)
    for c in range(S // KCH):
        @pl.when(c * KCH < nkeys)
        def _(c=c):
            klo = kilo_ref[c * KCH:(c + 1) * KCH, :]
            khi = kihi_ref[c * KCH:(c + 1) * KCH, :]
            acc = jnp.zeros((KCH, TQ), F32)
            for hp in range(IDX_HEADS // 2):
                qp = qi_ref[:, hp * LANES:(hp + 1) * LANES]
                s0 = _dot_nt(klo, qp)
                s1 = _dot_nt(khi, qp)
                acc = acc + jnp.maximum(s0, 0.0) * wt_ref[2 * hp:2 * hp + 1, :]
                acc = acc + jnp.maximum(s1, 0.0) * wt_ref[2 * hp + 1:2 * hp + 2, :]
            kidx = c * KCH + lax.broadcasted_iota(jnp.int32, (KCH, TQ), 0)
            causal = kidx <= qidx
            isc_ref[c * KCH:(c + 1) * KCH, :] = jnp.where(causal, acc, -jnp.inf)
            lo_part = jnp.where(causal, acc, jnp.inf).reshape(KCH // 8, 8, TQ).min(axis=0)
            hi_part = jnp.where(causal, acc, -jnp.inf).reshape(KCH // 8, 8, TQ).max(axis=0)
            mm_ref[0:8, :] = jnp.minimum(mm_ref[0:8, :], lo_part)
            mm_ref[8:16, :] = jnp.maximum(mm_ref[8:16, :], hi_part)

    def select(nch):
        def count(pred):
            acc = jnp.zeros((COUNT_CHAINS, 8, TQ), F32)
            for c in range(nch):
                ones = jnp.where(pred(isc_ref[c * TQ:(c + 1) * TQ, :], c * TQ), 1.0, 0.0)
                acc = acc + ones.reshape(COUNT_CHAINS, TQ // (8 * COUNT_CHAINS), 8, TQ).sum(axis=1)
            return acc.sum(axis=0).sum(axis=0, keepdims=True)

        row_min = mm_ref[0:8, :].min(axis=0, keepdims=True)
        row_max = mm_ref[8:16, :].max(axis=0, keepdims=True)
        full = (qidx + 1) <= topk
        c_max = count(lambda blk, k0: blk >= row_max)
        exact0 = c_max == kf
        tie0 = c_max > kf
        settled0 = full | exact0 | tie0
        lo0 = jnp.where(full, -F32_MAX, jnp.where(settled0, row_max, row_min))
        hi0 = jnp.where(full, -F32_MAX, jnp.where(tie0, jnp.inf, row_max))
        act0 = jnp.where(settled0, 0.0, 1.0)

        def step(lo, hi, act):
            mid = lo * 0.5 + hi * 0.5
            inside = (mid > lo) & (mid < hi)
            cnt = count(lambda blk, k0: blk >= mid)
            upd = (act > 0.0) & inside
            found = upd & (cnt == kf)
            lo2 = jnp.where(upd & (cnt >= kf), mid, lo)
            hi2 = jnp.where(upd & (cnt <= kf), mid, hi)
            return lo2, hi2, jnp.where(upd & jnp.logical_not(found), 1.0, 0.0)

        def any_active(act):
            return (jnp.max(act) > 0.0).astype(jnp.int32)

        def body(st):
            lo, hi, act, _ = st
            for _ in range(STEPS_PER_CHECK):
                lo, hi, act = step(lo, hi, act)
            return lo, hi, act, any_active(act)

        lo, hi, _, _ = lax.while_loop(lambda st: st[3] > 0, body, (lo0, hi0, act0, any_active(act0)))

        tie = lo < hi
        js_ref[...] = jnp.full((8, TQ), -1.0, F32)

        @pl.when(jnp.max(jnp.where(tie, 1.0, 0.0)) > 0.0)
        def _():
            need = kf - count(lambda blk, k0: blk >= hi)

            def kpos(k0):
                return (k0 + lax.broadcasted_iota(jnp.int32, (TQ, TQ), 0)).astype(F32)

            def tie_step(_, st):
                ilo, ihi = st
                imid = jnp.floor((ilo + ihi) * 0.5)
                cnt = count(lambda blk, k0: (blk >= lo) & (blk < hi) & (kpos(k0) <= imid))
                ge = cnt >= need
                return jnp.where(ge, ilo, imid), jnp.where(ge, imid, ihi)

            nsteps = int(math.ceil(math.log2(S))) + 1
            _, ihi = lax.fori_loop(0, nsteps, tie_step,
                                   (jnp.full((1, TQ), -1.0, F32), jnp.full((1, TQ), S - 1.0, F32)))
            js_ref[0:1, :] = jnp.where(tie, ihi, -1.0)

        jstar = js_ref[0:1, :]

        for c in range(S // TQ):
            if c < nch:
                blk = isc_ref[c * TQ:(c + 1) * TQ, :]
                kpos_c = (c * TQ + lax.broadcasted_iota(jnp.int32, (TQ, TQ), 0)).astype(F32)
                sel = (blk >= hi) | ((blk >= lo) & (kpos_c <= jstar))
                bias_ref[:, c * TQ:(c + 1) * TQ] = jnp.where(sel, 0.0, NEG_BIAS).T
            else:
                bias_ref[:, c * TQ:(c + 1) * TQ] = jnp.full((TQ, TQ), NEG_BIAS, F32)

    for jj in range(S // TQ):
        @pl.when(j == jj)
        def _(jj=jj):
            select(jj + 1)


def _dsa_index(ki_lo, ki_hi, qi, wt, *, B, S, TQ, KCH, topk):
    T = B * S
    nq = S // TQ
    kern = functools.partial(_dsa_index_kernel, S=S, TQ=TQ, KCH=KCH, topk=topk)
    return pl.pallas_call(
        kern,
        grid=(B, nq),
        in_specs=[
            pl.BlockSpec((S, LANES), lambda b, j: (b, 0)),
            pl.BlockSpec((S, LANES), lambda b, j: (b, 0)),
            pl.BlockSpec((TQ, IDX_HEADS * IDX_DIM), lambda b, j: (b * nq + j, 0)),
            pl.BlockSpec((IDX_HEADS, TQ), lambda b, j: (0, b * nq + j)),
        ],
        out_specs=pl.BlockSpec((TQ, S), lambda b, j: (b * nq + j, 0)),
        out_shape=jax.ShapeDtypeStruct((T, S), F32),
        scratch_shapes=[
            pltpu.VMEM((S, TQ), F32),
            pltpu.VMEM((16, TQ), F32),
            pltpu.VMEM((8, TQ), F32),
        ],
        compiler_params=_cparams(("parallel", "arbitrary")),
        name="dsa_index",
    )(ki_lo, ki_hi, qi, wt)


def _dsa_attn_kernel(q_ref, k_ref, v_ref, bias_ref, o_ref, *, S, TQ):
    j = pl.program_id(1)
    lane = lax.broadcasted_iota(jnp.int32, (TQ, LANES), 1)

    def variant(nk):
        ones = jnp.ones((nk, LANES), BF16)

        def pair(p, carry):
            vp = jnp.concatenate([v_ref[0, p, 0:nk, :], ones], axis=1)
            outs = []
            for e in range(2):
                h = 2 * p + e
                s = _dot_nt(q_ref[0, h], k_ref[0, h, 0:nk, :]) + bias_ref[:, 0:nk]
                m = s.max(axis=1, keepdims=True)
                pv = _dot(jnp.exp2(s - m).astype(BF16), vp)
                outs.append(pv[:, 0:LANES] * (1.0 / pv[:, LANES:2 * LANES]))
            o_ref[0, p] = jnp.where(lane < V_HEAD, outs[0], outs[1]).astype(BF16)
            return carry
        lax.fori_loop(0, N_HEADS // 2, pair, 0, unroll=2)

    for jj in range(S // TQ):
        @pl.when(j == jj)
        def _(jj=jj):
            variant((jj + 1) * TQ)


def _dsa_attn(q, k, v, bias, *, B, S, TQ):
    nq = S // TQ
    kern = functools.partial(_dsa_attn_kernel, S=S, TQ=TQ)
    return pl.pallas_call(
        kern,
        grid=(B, nq),
        in_specs=[
            pl.BlockSpec((1, N_HEADS, TQ, LANES), lambda b, j: (b, 0, j, 0)),
            pl.BlockSpec((1, N_HEADS, S, LANES), lambda b, j: (b, 0, 0, 0)),
            pl.BlockSpec((1, N_HEADS // 2, S, LANES), lambda b, j: (b, 0, 0, 0)),
            pl.BlockSpec((TQ, S), lambda b, j: (b * nq + j, 0)),
        ],
        out_specs=pl.BlockSpec((1, N_HEADS // 2, TQ, LANES), lambda b, j: (b, 0, j, 0)),
        out_shape=jax.ShapeDtypeStruct((B, N_HEADS // 2, S, LANES), BF16),
        compiler_params=_cparams(("parallel", "arbitrary")),
        name="dsa_attn",
    )(q, k, v, bias)


def _sgu_kernel(z_ref, g_ref, ws_ref, bt_ref, y_ref, *, tm):
    row = lax.broadcasted_iota(jnp.int32, (SGU_CHUNK, SGU_CHUNK), 0)
    col = lax.broadcasted_iota(jnp.int32, (SGU_CHUNK, SGU_CHUNK), 1)
    tril = col <= row
    w = [jnp.where(tril, ws_ref[g], 0.0).astype(BF16) for g in range(SGU_GROUPS)]
    for cc in range(tm // SGU_CHUNK):
        rows = slice(cc * SGU_CHUNK, (cc + 1) * SGU_CHUNK)
        vn = _rms(z_ref[rows, SGU_WIDTH:2 * SGU_WIDTH].astype(F32), g_ref[...]).astype(BF16)
        for g in range(SGU_GROUPS):
            cols = slice(g * SGU_GROUP_DIM, (g + 1) * SGU_GROUP_DIM)
            mixed = _dot(w[g], vn[:, cols]) + bt_ref[:, g:g + 1]
            y_ref[rows, cols] = (z_ref[rows, cols].astype(F32) * mixed).astype(BF16)


def _sgu(z, g_sgu, w_spatial, b_t, *, tm):
    T = z.shape[0]
    kern = functools.partial(_sgu_kernel, tm=tm)
    return pl.pallas_call(
        kern,
        grid=(T // tm,),
        in_specs=[
            pl.BlockSpec((tm, 2 * SGU_WIDTH), lambda i: (i, 0)),
            pl.BlockSpec((1, SGU_WIDTH), lambda i: (0, 0)),
            pl.BlockSpec((SGU_GROUPS, SGU_CHUNK, SGU_CHUNK), lambda i: (0, 0, 0)),
            pl.BlockSpec((SGU_CHUNK, SGU_GROUPS), lambda i: (0, 0)),
        ],
        out_specs=pl.BlockSpec((tm, SGU_WIDTH), lambda i: (i, 0)),
        out_shape=jax.ShapeDtypeStruct((T, SGU_WIDTH), BF16),
        compiler_params=_cparams(("parallel",)),
        name="sgu",
    )(z, g_sgu, w_spatial, b_t)


def _merge_kernel(o_ref, y_ref, ga_ref, gb_ref, wa_ref, wb_ref, out_ref, wa_bf, wb_bf):
    @pl.when(pl.program_id(1) == 0)
    def _():
        wa_bf[...] = wa_ref[...].astype(BF16)
        wb_bf[...] = wb_ref[...].astype(BF16)

    o_a = jnp.concatenate([o_ref[0, p] for p in range(N_HEADS // 2)], axis=1)
    ya = _dot(o_a, wa_bf[...])
    yb = _dot(y_ref[...], wb_bf[...])
    out_ref[...] = (ga_ref[...].astype(F32) * ya + gb_ref[...].astype(F32) * yb).astype(BF16)


def _merge(o, y_sgu, gates, wa, wb, *, B, S, tm, tn):
    T = B * S
    D = wa.shape[1]
    nt = S // tm
    nn = D // tn
    return pl.pallas_call(
        _merge_kernel,
        grid=(nn, T // tm),
        in_specs=[
            pl.BlockSpec((1, N_HEADS // 2, tm, LANES), lambda j, i: (i // nt, 0, i % nt, 0)),
            pl.BlockSpec((tm, SGU_WIDTH), lambda j, i: (i, 0)),
            pl.BlockSpec((tm, tn), lambda j, i: (i, j)),
            pl.BlockSpec((tm, tn), lambda j, i: (i, nn + j)),
            pl.BlockSpec((wa.shape[0], tn), lambda j, i: (0, j)),
            pl.BlockSpec((wb.shape[0], tn), lambda j, i: (0, j)),
        ],
        out_specs=pl.BlockSpec((tm, tn), lambda j, i: (i, j)),
        out_shape=jax.ShapeDtypeStruct((T, D), BF16),
        scratch_shapes=[pltpu.VMEM((wa.shape[0], tn), BF16), pltpu.VMEM((wb.shape[0], tn), BF16)],
        compiler_params=_cparams(("arbitrary", "arbitrary")),
        name="merge",
    )(o, y_sgu, gates, gates, wa, wb)


def _o_proj_kernel(x_ref, m_ref, w_ref, g_ref, h_ref, n_ref, w_bf):
    @pl.when(pl.program_id(0) == 0)
    def _():
        w_bf[...] = w_ref[...].astype(BF16)

    h = x_ref[...] + _dot(m_ref[...], w_bf[...])
    h_ref[...] = h
    n_ref[...] = _rms(h, g_ref[...]).astype(BF16)


def _o_proj(x2, merged, w_o, g_ffn, *, tm):
    T, D = x2.shape
    return pl.pallas_call(
        _o_proj_kernel,
        grid=(T // tm,),
        in_specs=[
            pl.BlockSpec((tm, D), lambda i: (i, 0)),
            pl.BlockSpec((tm, D), lambda i: (i, 0)),
            pl.BlockSpec((D, D), lambda i: (0, 0), pipeline_mode=pl.Buffered(1)),
            pl.BlockSpec((1, D), lambda i: (0, 0)),
        ],
        out_specs=[pl.BlockSpec((tm, D), lambda i: (i, 0)), pl.BlockSpec((tm, D), lambda i: (i, 0))],
        out_shape=[jax.ShapeDtypeStruct((T, D), F32), jax.ShapeDtypeStruct((T, D), BF16)],
        scratch_shapes=[pltpu.VMEM((D, D), BF16)],
        compiler_params=_cparams(("arbitrary",)),
        name="o_proj",
    )(x2, merged, w_o, g_ffn)


def _ffn_up_kernel(n_ref, wg_ref, wu_ref, a_ref, wg_bf, wu_bf):
    @pl.when(pl.program_id(1) == 0)
    def _():
        wg_bf[...] = wg_ref[...].astype(BF16)
        wu_bf[...] = wu_ref[...].astype(BF16)

    n = n_ref[...]
    g = _dot(n, wg_bf[...])
    u = _dot(n, wu_bf[...])
    a_ref[...] = (g * _sigmoid(g) * u).astype(BF16)


def _ffn_up(n2, w_gu, *, d_ff, tm, tn):
    T, D = n2.shape
    nn = d_ff // tn
    return pl.pallas_call(
        _ffn_up_kernel,
        grid=(nn, T // tm),
        in_specs=[
            pl.BlockSpec((tm, D), lambda j, i: (i, 0)),
            pl.BlockSpec((D, tn), lambda j, i: (0, j)),
            pl.BlockSpec((D, tn), lambda j, i: (0, nn + j)),
        ],
        out_specs=pl.BlockSpec((tm, tn), lambda j, i: (i, j)),
        out_shape=jax.ShapeDtypeStruct((T, d_ff), BF16),
        scratch_shapes=[pltpu.VMEM((D, tn), BF16), pltpu.VMEM((D, tn), BF16)],
        compiler_params=_cparams(("arbitrary", "arbitrary")),
        name="ffn_up",
    )(n2, w_gu, w_gu)


def _ffn_down_kernel(a_ref, w_ref, h1_ref, h2_ref, w_bf):
    @pl.when(pl.program_id(1) == 0)
    def _():
        w_bf[...] = w_ref[...].astype(BF16)

    h2_ref[...] = h1_ref[...] + _dot(a_ref[...], w_bf[...])


def _ffn_down(act, w_down, h1, *, tm, tn):
    T, D = h1.shape
    d_ff = act.shape[1]
    return pl.pallas_call(
        _ffn_down_kernel,
        grid=(D // tn, T // tm),
        in_specs=[
            pl.BlockSpec((tm, d_ff), lambda j, i: (i, 0)),
            pl.BlockSpec((d_ff, tn), lambda j, i: (0, j)),
            pl.BlockSpec((tm, tn), lambda j, i: (i, j)),
        ],
        out_specs=pl.BlockSpec((tm, tn), lambda j, i: (i, j)),
        out_shape=jax.ShapeDtypeStruct((T, D), F32),
        scratch_shapes=[pltpu.VMEM((d_ff, tn), BF16)],
        compiler_params=_cparams(("arbitrary", "arbitrary")),
        name="ffn_down",
    )(act, w_down, h1)


def _ple_final_kernel(h2_ref, p_ref, wg_ref, wp_ref, gp_ref, gf_ref, out_ref, wg_bf, wp_bf):
    @pl.when(pl.program_id(0) == 0)
    def _():
        wg_bf[...] = wg_ref[...].astype(BF16)
        wp_bf[...] = wp_ref[...].astype(BF16)

    h2 = h2_ref[...]
    gate = _sigmoid(_dot(_rms(h2, gp_ref[...]).astype(BF16), wg_bf[...]))
    pp = _dot(p_ref[...].astype(BF16), wp_bf[...])
    out_ref[...] = _rms(h2 + gate * pp, gf_ref[...])


def _ple_final(h2, p2, w_pg, w_pp, g_ple, g_final, *, tm):
    T, D = h2.shape
    P = p2.shape[1]
    once = pl.Buffered(1)
    return pl.pallas_call(
        _ple_final_kernel,
        grid=(T // tm,),
        in_specs=[
            pl.BlockSpec((tm, D), lambda i: (i, 0)),
            pl.BlockSpec((tm, P), lambda i: (i, 0)),
            pl.BlockSpec((D, D), lambda i: (0, 0), pipeline_mode=once),
            pl.BlockSpec((P, D), lambda i: (0, 0), pipeline_mode=once),
            pl.BlockSpec((1, D), lambda i: (0, 0)),
            pl.BlockSpec((1, D), lambda i: (0, 0)),
        ],
        out_specs=pl.BlockSpec((tm, D), lambda i: (i, 0)),
        out_shape=jax.ShapeDtypeStruct((T, D), F32),
        scratch_shapes=[pltpu.VMEM((D, D), BF16), pltpu.VMEM((P, D), BF16)],
        compiler_params=_cparams(("arbitrary",)),
        name="ple_final",
    )(h2, p2, w_pg, w_pp, g_ple, g_final)


TAB_Q = 0
TAB_QI = 2
TAB_KI = 4
TAB_KR = 7
N_TABLES = 10


def _lane_tables(S):
    assert QK_ROPE == IDX_ROPE == 32
    scale = (QK_NOPE + QK_ROPE) ** -0.5 * math.log2(math.e)
    a = np.zeros((N_TABLES, LANES), np.float32)
    bc = np.zeros_like(a)
    bs = np.zeros_like(a)
    a[TAB_Q, 0:QK_NOPE] = scale
    bc[TAB_Q, 64:96] = scale
    bs[TAB_Q + 1, 64:80] = -scale
    bs[TAB_Q + 1, 80:96] = scale
    for o in (0, IDX_DIM):
        bc[TAB_QI, o:o + 32] = 1.0
        a[TAB_QI, o + 32:o + 64] = 1.0
        bs[TAB_QI + 1, o:o + 32] = 1.0
    bc[TAB_KI, 0:32] = 1.0
    a[TAB_KI, 32:64] = 1.0
    bs[TAB_KI + 1, 0:16] = -1.0
    bs[TAB_KI + 2, 16:32] = 1.0
    bc[TAB_KR, 0:32] = 1.0
    bs[TAB_KR + 1, 0:16] = -1.0
    bs[TAB_KR + 2, 16:32] = 1.0
    inv = ROPE_THETA ** (-jnp.arange(0, QK_ROPE, 2, dtype=F32) / QK_ROPE)
    ang = jnp.arange(S, dtype=F32)[:, None] * jnp.tile(inv, LANES // inv.shape[0])[None, :]
    return a[:, None, :] + bc[:, None, :] * jnp.cos(ang)[None] + bs[:, None, :] * jnp.sin(ang)[None]


def kernel(x, p, g_mix, w_in, g_cq, g_ckv, w_uq, w_uk, w_uv, w_iq, w_a_proj, g_sgu, w_spatial,
           b_spatial, w_b_proj, w_o, g_ffn, w_gu, w_down, g_ple, w_ple_gate, w_ple_proj, g_final):
    B, S, D = x.shape
    T = B * S
    depth = w_in.shape[0]
    d_ff = w_down.shape[1]
    topk = min(TOPK_MAX, S // 4)
    TQ = 256
    KCH = min(512, S)
    tm = min(512, S)
    tables = _lane_tables(S)

    h = x.reshape(T, D)
    for i in range(depth):
        wq3 = w_uq[i].reshape(Q_LORA, N_HEADS, QK_NOPE + QK_ROPE)
        x1, x2 = wq3[..., QK_NOPE:QK_NOPE + QK_ROPE // 2], wq3[..., QK_NOPE + QK_ROPE // 2:]
        wq = jnp.concatenate([wq3, x2, x1], axis=-1).reshape(Q_LORA, N_HEADS * LANES).astype(BF16)
        wk = jnp.pad(w_uk[i], ((0, 0), (0, 0), (0, LANES - QK_NOPE))).reshape(KV_LORA, N_HEADS * LANES).astype(BF16)
        wv = w_uv[i].reshape(KV_LORA, N_HEADS * V_HEAD).astype(BF16)
        wi3 = w_iq[i].reshape(Q_LORA, IDX_HEADS, IDX_DIM)
        wiq = w_iq[i].astype(BF16)
        wiqr = jnp.concatenate([-wi3[..., IDX_ROPE // 2:IDX_ROPE], wi3[..., :IDX_ROPE // 2],
                                jnp.zeros_like(wi3[..., IDX_ROPE:])], axis=-1).reshape(w_iq[i].shape).astype(BF16)

        small, z, gates = _in_proj(h, g_mix[i][None], jnp.transpose(w_in[i]), tm=min(1024, S), tn=512)
        q, k, v, qi, ki_lo, ki_hi, wt = _a_proj(small, g_cq[i][None], g_ckv[i][None], wq, wiq, wiqr, wk, wv,
                                                 tables, B=B, S=S, tm=tm)
        bias = _dsa_index(ki_lo, ki_hi, qi, wt, B=B, S=S, TQ=TQ, KCH=KCH, topk=topk)
        o = _dsa_attn(q, k, v, bias, B=B, S=S, TQ=TQ)
        y_sgu = _sgu(z, g_sgu[i][None], w_spatial[i], jnp.transpose(b_spatial[i]), tm=tm)
        merged = _merge(o, y_sgu, gates, w_a_proj[i], w_b_proj[i], B=B, S=S, tm=tm, tn=1024)
        h1, n2 = _o_proj(h, merged, w_o[i], g_ffn[i][None], tm=min(256, S))
        act = _ffn_up(n2, w_gu[i], d_ff=d_ff, tm=min(1024, S), tn=512)
        h2 = _ffn_down(act, w_down[i], h1, tm=tm, tn=512)
        assert depth == 1
        h = _ple_final(h2, p[i].reshape(T, -1), w_ple_gate[i], w_ple_proj[i], g_ple[i][None],
                       g_final[None], tm=min(256, S))
    return h.reshape(B, S, D)
```

```python
import functools
import math

import numpy as np
import jax
import jax.numpy as jnp
from jax import lax
from jax.experimental import pallas as pl
from jax.experimental.pallas import tpu as pltpu

F32 = jnp.float32
BF16 = jnp.bfloat16

N_HEADS = 16
QK_NOPE = 64
QK_ROPE = 32
V_HEAD = 64
Q_LORA = 512
KV_LORA = 256
IDX_HEADS = 16
IDX_DIM = 64
IDX_ROPE = 32
TOPK_MAX = 256
SGU_CHUNK = 128
SGU_GROUPS = 8
SGU_GROUP_DIM = 128
SGU_WIDTH = SGU_GROUPS * SGU_GROUP_DIM
ROPE_THETA = 10000.0
EPS = 1e-6

LANES = 128
IN_SMALL = Q_LORA + KV_LORA + QK_ROPE + IDX_DIM + IDX_HEADS
SMALL_COLS = 1024
COL_CKV = Q_LORA
COL_MISC = Q_LORA + KV_LORA
MISC_KIDX = QK_ROPE
MISC_WIDX = QK_ROPE + IDX_DIM
VMEM_LIMIT_BYTES = 56 * 1024 * 1024
NEG_BIAS = -1e30
F32_MAX = 3.4028234663852886e38
STEPS_PER_CHECK = 4
COUNT_CHAINS = 4


def _cparams(semantics):
    return pltpu.CompilerParams(dimension_semantics=semantics, vmem_limit_bytes=VMEM_LIMIT_BYTES)


def _rms(x, g):
    return x * lax.rsqrt(jnp.mean(x * x, axis=-1, keepdims=True) + EPS) * g


def _dot(a, b):
    return jnp.dot(a, b, preferred_element_type=F32)


def _dot_nt(a, b):
    return lax.dot_general(a, b, (((1,), (1,)), ((), ())), preferred_element_type=F32)


def _gelu_exact(x):
    return 0.5 * x * (1.0 + lax.erf(x * (1.0 / math.sqrt(2.0))))


def _sigmoid(x):
    return 0.5 * jnp.tanh(0.5 * x) + 0.5


def _norm_small_kernel(x_ref, g_ref, w_ref, n_ref, small_ref, w_bf):
    @pl.when(pl.program_id(0) == 0)
    def _():
        w_bf[...] = w_ref[...].astype(BF16)

    n = _rms(x_ref[...], g_ref[...]).astype(BF16)
    n_ref[...] = n
    small_ref[...] = _dot_nt(n, w_bf[...])


def _norm_small(x2, g_mix, w_t, *, tm):
    T, D = x2.shape
    return pl.pallas_call(
        _norm_small_kernel,
        grid=(T // tm,),
        in_specs=[
            pl.BlockSpec((tm, D), lambda i: (i, 0)),
            pl.BlockSpec((1, D), lambda i: (0, 0)),
            pl.BlockSpec((SMALL_COLS, D), lambda i: (0, 0), pipeline_mode=pl.Buffered(1)),
        ],
        out_specs=[pl.BlockSpec((tm, D), lambda i: (i, 0)), pl.BlockSpec((tm, SMALL_COLS), lambda i: (i, 0))],
        out_shape=[jax.ShapeDtypeStruct((T, D), BF16), jax.ShapeDtypeStruct((T, SMALL_COLS), F32)],
        scratch_shapes=[pltpu.VMEM((SMALL_COLS, D), BF16)],
        compiler_params=_cparams(("arbitrary",)),
        name="norm_small",
    )(x2, g_mix, w_t)


def _uv_gate_kernel(n_ref, w_ref, out_ref, w_bf, *, n_z):
    j = pl.program_id(0)

    @pl.when(pl.program_id(1) == 0)
    def _():
        w_bf[...] = w_ref[...].astype(BF16)

    def proj():
        return _dot_nt(n_ref[...], w_bf[...])

    @pl.when(j < n_z)
    def _():
        out_ref[...] = _gelu_exact(proj()).astype(BF16)

    @pl.when(j >= n_z)
    def _():
        out_ref[...] = _sigmoid(proj()).astype(BF16)


def _uv_gate_proj(n, w_t, *, tm, tn):
    T, D = n.shape
    n_z = (2 * SGU_WIDTH) // tn
    n_out = 2 * SGU_WIDTH + 2 * D
    assert w_t.shape == (IN_SMALL + n_out, D) and IN_SMALL % 8 == 0 and tn % 8 == 0
    kern = functools.partial(_uv_gate_kernel, n_z=n_z)
    return pl.pallas_call(
        kern,
        grid=(n_out // tn, T // tm),
        in_specs=[
            pl.BlockSpec((tm, D), lambda j, i: (i, 0)),
            pl.BlockSpec((pl.Element(tn), pl.Element(D)), lambda j, i: ((IN_SMALL // 8 + j * (tn // 8)) * 8, 0)),
        ],
        out_specs=pl.BlockSpec((tm, tn), lambda j, i: (i, j)),
        out_shape=jax.ShapeDtypeStruct((T, n_out), BF16),
        scratch_shapes=[pltpu.VMEM((tn, D), BF16)],
        compiler_params=_cparams(("arbitrary", "arbitrary")),
        name="uv_gate_proj",
    )(n, w_t)


def _rope_block(x, cos, sin_a, sin_b):
    return x * cos + pltpu.roll(x, LANES - 16, 1) * sin_a + pltpu.roll(x, 16, 1) * sin_b


def _a_proj_kernel(small_ref, gcq_ref, gckv_ref, wq_ref, wiq_ref, wiqr_ref, wk_ref, wv_ref, tab_ref,
                   q_ref, k_ref, v_ref, qi_ref, kilo_ref, kihi_ref, wt_ref):
    c_q = _rms(small_ref[:, 0:Q_LORA], gcq_ref[...]).astype(BF16)
    c_kv = _rms(small_ref[:, COL_CKV:COL_CKV + KV_LORA], gckv_ref[...]).astype(BF16)

    q = _dot(c_q, wq_ref[...])
    cq, sq = tab_ref[TAB_Q], tab_ref[TAB_Q + 1]
    for h in range(N_HEADS):
        blk = q[:, h * LANES:(h + 1) * LANES]
        q_ref[0, h] = (blk * cq + pltpu.roll(blk, LANES - QK_ROPE, 1) * sq).astype(BF16)

    ci, si = tab_ref[TAB_QI], tab_ref[TAB_QI + 1]
    qi = _dot(c_q, wiq_ref[...])
    qir = _dot(c_q, wiqr_ref[...])
    for hp in range(IDX_HEADS // 2):
        cols = slice(hp * LANES, (hp + 1) * LANES)
        qi_ref[:, cols] = (qi[:, cols] * ci + qir[:, cols] * si).astype(BF16)

    misc = small_ref[:, COL_MISC:COL_MISC + LANES]
    ki_lo = _rope_block(pltpu.roll(misc, LANES - MISC_KIDX, 1),
                        tab_ref[TAB_KI], tab_ref[TAB_KI + 1], tab_ref[TAB_KI + 2])
    kilo_ref[...] = ki_lo.astype(BF16)
    kihi_ref[...] = pltpu.roll(ki_lo, IDX_DIM, 1).astype(BF16)

    k_rope = pltpu.roll(_rope_block(misc, tab_ref[TAB_KR], tab_ref[TAB_KR + 1], tab_ref[TAB_KR + 2]),
                        QK_NOPE, 1)
    k_nope = _dot(c_kv, wk_ref[...])
    for h in range(N_HEADS):
        k_ref[0, h] = (k_nope[:, h * LANES:(h + 1) * LANES] + k_rope).astype(BF16)

    v = _dot(c_kv, wv_ref[...])
    for p in range(N_HEADS // 2):
        v_ref[0, p] = v[:, p * LANES:(p + 1) * LANES].astype(BF16)

    w_scale = IDX_HEADS ** -0.5 * IDX_DIM ** -0.5
    wt_ref[...] = misc.T[MISC_WIDX:MISC_WIDX + IDX_HEADS, :] * w_scale


def _a_proj(small, g_cq, g_ckv, wq, wiq, wiqr, wk, wv, tables, *, B, S, tm):
    T = B * S
    nt = S // tm
    const2 = lambda i: (0, 0)
    head_spec = lambda nh: pl.BlockSpec((1, nh, tm, LANES), lambda i: (i // nt, 0, i % nt, 0))
    return pl.pallas_call(
        _a_proj_kernel,
        grid=(T // tm,),
        in_specs=[
            pl.BlockSpec((tm, SMALL_COLS), lambda i: (i, 0)),
            pl.BlockSpec((1, Q_LORA), const2),
            pl.BlockSpec((1, KV_LORA), const2),
            pl.BlockSpec(wq.shape, const2),
            pl.BlockSpec(wiq.shape, const2),
            pl.BlockSpec(wiqr.shape, const2),
            pl.BlockSpec(wk.shape, const2),
            pl.BlockSpec(wv.shape, const2),
            pl.BlockSpec((N_TABLES, tm, LANES), lambda i: (0, i % nt, 0)),
        ],
        out_specs=[
            head_spec(N_HEADS),
            head_spec(N_HEADS),
            head_spec(N_HEADS // 2),
            pl.BlockSpec((tm, IDX_HEADS * IDX_DIM), lambda i: (i, 0)),
            pl.BlockSpec((tm, LANES), lambda i: (i, 0)),
            pl.BlockSpec((tm, LANES), lambda i: (i, 0)),
            pl.BlockSpec((IDX_HEADS, tm), lambda i: (0, i)),
        ],
        out_shape=[
            jax.ShapeDtypeStruct((B, N_HEADS, S, LANES), BF16),
            jax.ShapeDtypeStruct((B, N_HEADS, S, LANES), BF16),
            jax.ShapeDtypeStruct((B, N_HEADS // 2, S, LANES), BF16),
            jax.ShapeDtypeStruct((T, IDX_HEADS * IDX_DIM), BF16),
            jax.ShapeDtypeStruct((T, LANES), BF16),
            jax.ShapeDtypeStruct((T, LANES), BF16),
            jax.ShapeDtypeStruct((IDX_HEADS, T), F32),
        ],
        compiler_params=_cparams(("parallel",)),
        name="a_proj",
    )(small, g_cq, g_ckv, wq, wiq, wiqr, wk, wv, tables)


def _dsa_index_kernel(kilo_ref, kihi_ref, qi_ref, wt_ref, bias_ref, isc_ref, mm_ref, js_ref,
                      *, S, TQ, KCH, topk):
    j = pl.program_id(1)
    q0 = j * TQ
    nkeys = q0 + TQ
    qidx = q0 + lax.broadcasted_iota(jnp.int32, (1, TQ), 1)
    kf = float(topk)

    mm_ref[0:8, :] = jnp.full((8, TQ), jnp.inf, F32)
    mm_ref[8:16, :] = jnp.full((8, TQ), -jnp.inf, F32)
    for c in range(S // KCH):
        @pl.when(c * KCH < nkeys)
        def _(c=c):
            klo = kilo_ref[c * KCH:(c + 1) * KCH, :]
            khi = kihi_ref[c * KCH:(c + 1) * KCH, :]
            acc = jnp.zeros((KCH, TQ), F32)
            for hp in range(IDX_HEADS // 2):
                qp = qi_ref[:, hp * LANES:(hp + 1) * LANES]
                s0 = _dot_nt(klo, qp)
                s1 = _dot_nt(khi, qp)
                acc = acc + jnp.maximum(s0, 0.0) * wt_ref[2 * hp:2 * hp + 1, :]
                acc = acc + jnp.maximum(s1, 0.0) * wt_ref[2 * hp + 1:2 * hp + 2, :]
            kidx = c * KCH + lax.broadcasted_iota(jnp.int32, (KCH, TQ), 0)
            causal = kidx <= qidx
            isc_ref[c * KCH:(c + 1) * KCH, :] = jnp.where(causal, acc, -jnp.inf)
            lo_part = jnp.where(causal, acc, jnp.inf).reshape(KCH // 8, 8, TQ).min(axis=0)
            hi_part = jnp.where(causal, acc, -jnp.inf).reshape(KCH // 8, 8, TQ).max(axis=0)
            mm_ref[0:8, :] = jnp.minimum(mm_ref[0:8, :], lo_part)
            mm_ref[8:16, :] = jnp.maximum(mm_ref[8:16, :], hi_part)

    def select(nch):
        def count(pred):
            acc = jnp.zeros((COUNT_CHAINS, 8, TQ), F32)
            for c in range(nch):
                ones = jnp.where(pred(isc_ref[c * TQ:(c + 1) * TQ, :], c * TQ), 1.0, 0.0)
                acc = acc + ones.reshape(COUNT_CHAINS, TQ // (8 * COUNT_CHAINS), 8, TQ).sum(axis=1)
            return acc.sum(axis=0).sum(axis=0, keepdims=True)

        row_min = mm_ref[0:8, :].min(axis=0, keepdims=True)
        row_max = mm_ref[8:16, :].max(axis=0, keepdims=True)
        full = (qidx + 1) <= topk
        c_max = count(lambda blk, k0: blk >= row_max)
        exact0 = c_max == kf
        tie0 = c_max > kf
        settled0 = full | exact0 | tie0
        lo0 = jnp.where(full, -F32_MAX, jnp.where(settled0, row_max, row_min))
        hi0 = jnp.where(full, -F32_MAX, jnp.where(tie0, jnp.inf, row_max))
        act0 = jnp.where(settled0, 0.0, 1.0)

        def step(lo, hi, act):
            mid = lo * 0.5 + hi * 0.5
            inside = (mid > lo) & (mid < hi)
            cnt = count(lambda blk, k0: blk >= mid)
            upd = (act > 0.0) & inside
            found = upd & (cnt == kf)
            lo2 = jnp.where(upd & (cnt >= kf), mid, lo)
            hi2 = jnp.where(upd & (cnt <= kf), mid, hi)
            return lo2, hi2, jnp.where(upd & jnp.logical_not(found), 1.0, 0.0)

        def any_active(act):
            return (jnp.max(act) > 0.0).astype(jnp.int32)

        def body(st):
            lo, hi, act, _ = st
            for _ in range(STEPS_PER_CHECK):
                lo, hi, act = step(lo, hi, act)
            return lo, hi, act, any_active(act)

        lo, hi, _, _ = lax.while_loop(lambda st: st[3] > 0, body, (lo0, hi0, act0, any_active(act0)))

        tie = lo < hi
        js_ref[...] = jnp.full((8, TQ), -1.0, F32)

        @pl.when(jnp.max(jnp.where(tie, 1.0, 0.0)) > 0.0)
        def _():
            need = kf - count(lambda blk, k0: blk >= hi)

            def kpos(k0):
                return (k0 + lax.broadcasted_iota(jnp.int32, (TQ, TQ), 0)).astype(F32)

            def tie_step(_, st):
                ilo, ihi = st
                imid = jnp.floor((ilo + ihi) * 0.5)
                cnt = count(lambda blk, k0: (blk >= lo) & (blk < hi) & (kpos(k0) <= imid))
                ge = cnt >= need
                return jnp.where(ge, ilo, imid), jnp.where(ge, imid, ihi)

            nsteps = int(math.ceil(math.log2(S))) + 1
            _, ihi = lax.fori_loop(0, nsteps, tie_step,
                                   (jnp.full((1, TQ), -1.0, F32), jnp.full((1, TQ), S - 1.0, F32)))
            js_ref[0:1, :] = jnp.where(tie, ihi, -1.0)

        jstar = js_ref[0:1, :]

        for c in range(S // TQ):
            if c < nch:
                blk = isc_ref[c * TQ:(c + 1) * TQ, :]
                kpos_c = (c * TQ + lax.broadcasted_iota(jnp.int32, (TQ, TQ), 0)).astype(F32)
                sel = (blk >= hi) | ((blk >= lo) & (kpos_c <= jstar))
                bias_ref[:, c * TQ:(c + 1) * TQ] = jnp.where(sel, 0.0, NEG_BIAS).T
            else:
                bias_ref[:, c * TQ:(c + 1) * TQ] = jnp.full((TQ, TQ), NEG_BIAS, F32)

    for jj in range(S // TQ):
        @pl.when(j == jj)
        def _(jj=jj):
            select(jj + 1)


def _dsa_index(ki_lo, ki_hi, qi, wt, *, B, S, TQ, KCH, topk):
    T = B * S
    nq = S // TQ
    kern = functools.partial(_dsa_index_kernel, S=S, TQ=TQ, KCH=KCH, topk=topk)
    return pl.pallas_call(
        kern,
        grid=(B, nq),
        in_specs=[
            pl.BlockSpec((S, LANES), lambda b, j: (b, 0)),
            pl.BlockSpec((S, LANES), lambda b, j: (b, 0)),
            pl.BlockSpec((TQ, IDX_HEADS * IDX_DIM), lambda b, j: (b * nq + j, 0)),
            pl.BlockSpec((IDX_HEADS, TQ), lambda b, j: (0, b * nq + j)),
        ],
        out_specs=pl.BlockSpec((TQ, S), lambda b, j: (b * nq + j, 0)),
        out_shape=jax.ShapeDtypeStruct((T, S), F32),
        scratch_shapes=[
            pltpu.VMEM((S, TQ), F32),
            pltpu.VMEM((16, TQ), F32),
            pltpu.VMEM((8, TQ), F32),
        ],
        compiler_params=_cparams(("parallel", "arbitrary")),
        name="dsa_index",
    )(ki_lo, ki_hi, qi, wt)


def _dsa_attn_kernel(q_ref, k_ref, v_ref, bias_ref, o_ref, *, S, TQ):
    j = pl.program_id(1)
    lane = lax.broadcasted_iota(jnp.int32, (TQ, LANES), 1)

    def variant(nk):
        ones = jnp.ones((nk, LANES), BF16)

        def pair(p, carry):
            vp = jnp.concatenate([v_ref[0, p, 0:nk, :], ones], axis=1)
            outs = []
            for e in range(2):
                h = 2 * p + e
                s = _dot_nt(q_ref[0, h], k_ref[0, h, 0:nk, :]) + bias_ref[:, 0:nk]
                m = s.max(axis=1, keepdims=True)
                pv = _dot(jnp.exp2(s - m).astype(BF16), vp)
                outs.append(pv[:, 0:LANES] * (1.0 / pv[:, LANES:2 * LANES]))
            o_ref[0, p] = jnp.where(lane < V_HEAD, outs[0], outs[1]).astype(BF16)
            return carry
        lax.fori_loop(0, N_HEADS // 2, pair, 0, unroll=4)

    for jj in range(S // TQ):
        @pl.when(j == jj)
        def _(jj=jj):
            variant((jj + 1) * TQ)


def _dsa_attn(q, k, v, bias, *, B, S, TQ):
    nq = S // TQ
    kern = functools.partial(_dsa_attn_kernel, S=S, TQ=TQ)
    return pl.pallas_call(
        kern,
        grid=(B, nq),
        in_specs=[
            pl.BlockSpec((1, N_HEADS, TQ, LANES), lambda b, j: (b, 0, j, 0)),
            pl.BlockSpec((1, N_HEADS, S, LANES), lambda b, j: (b, 0, 0, 0)),
            pl.BlockSpec((1, N_HEADS // 2, S, LANES), lambda b, j: (b, 0, 0, 0)),
            pl.BlockSpec((TQ, S), lambda b, j: (b * nq + j, 0)),
        ],
        out_specs=pl.BlockSpec((1, N_HEADS // 2, TQ, LANES), lambda b, j: (b, 0, j, 0)),
        out_shape=jax.ShapeDtypeStruct((B, N_HEADS // 2, S, LANES), BF16),
        compiler_params=_cparams(("parallel", "arbitrary")),
        name="dsa_attn",
    )(q, k, v, bias)


def _sgu_kernel(z_ref, g_ref, ws_ref, bt_ref, y_ref, *, tm):
    row = lax.broadcasted_iota(jnp.int32, (SGU_CHUNK, SGU_CHUNK), 0)
    col = lax.broadcasted_iota(jnp.int32, (SGU_CHUNK, SGU_CHUNK), 1)
    tril = col <= row
    w = [jnp.where(tril, ws_ref[g], 0.0).astype(BF16) for g in range(SGU_GROUPS)]
    for cc in range(tm // SGU_CHUNK):
        rows = slice(cc * SGU_CHUNK, (cc + 1) * SGU_CHUNK)
        vn = _rms(z_ref[rows, SGU_WIDTH:2 * SGU_WIDTH].astype(F32), g_ref[...]).astype(BF16)
        for g in range(SGU_GROUPS):
            cols = slice(g * SGU_GROUP_DIM, (g + 1) * SGU_GROUP_DIM)
            mixed = _dot(w[g], vn[:, cols]) + bt_ref[:, g:g + 1]
            y_ref[rows, cols] = (z_ref[rows, cols].astype(F32) * mixed).astype(BF16)


def _sgu(z, g_sgu, w_spatial, b_t, *, tm):
    T = z.shape[0]
    kern = functools.partial(_sgu_kernel, tm=tm)
    return pl.pallas_call(
        kern,
        grid=(T // tm,),
        in_specs=[
            pl.BlockSpec((tm, 2 * SGU_WIDTH), lambda i: (i, 0)),
            pl.BlockSpec((1, SGU_WIDTH), lambda i: (0, 0)),
            pl.BlockSpec((SGU_GROUPS, SGU_CHUNK, SGU_CHUNK), lambda i: (0, 0, 0)),
            pl.BlockSpec((SGU_CHUNK, SGU_GROUPS), lambda i: (0, 0)),
        ],
        out_specs=pl.BlockSpec((tm, SGU_WIDTH), lambda i: (i, 0)),
        out_shape=jax.ShapeDtypeStruct((T, SGU_WIDTH), BF16),
        compiler_params=_cparams(("parallel",)),
        name="sgu",
    )(z, g_sgu, w_spatial, b_t)


def _merge_kernel(o_ref, y_ref, ga_ref, gb_ref, wa_ref, wb_ref, out_ref, wa_bf, wb_bf):
    @pl.when(pl.program_id(1) == 0)
    def _():
        wa_bf[...] = wa_ref[...].astype(BF16)
        wb_bf[...] = wb_ref[...].astype(BF16)

    o_a = jnp.concatenate([o_ref[0, p] for p in range(N_HEADS // 2)], axis=1)
    ya = _dot(o_a, wa_bf[...])
    yb = _dot(y_ref[...], wb_bf[...])
    out_ref[...] = (ga_ref[...].astype(F32) * ya + gb_ref[...].astype(F32) * yb).astype(BF16)


def _merge(o, y_sgu, zg, wa, wb, *, B, S, tm, tn):
    T = B * S
    D = wa.shape[1]
    nt = S // tm
    nn = D // tn
    g0 = (2 * SGU_WIDTH) // tn
    return pl.pallas_call(
        _merge_kernel,
        grid=(nn, T // tm),
        in_specs=[
            pl.BlockSpec((1, N_HEADS // 2, tm, LANES), lambda j, i: (i // nt, 0, i % nt, 0)),
            pl.BlockSpec((tm, SGU_WIDTH), lambda j, i: (i, 0)),
            pl.BlockSpec((tm, tn), lambda j, i: (i, g0 + j)),
            pl.BlockSpec((tm, tn), lambda j, i: (i, g0 + nn + j)),
            pl.BlockSpec((wa.shape[0], tn), lambda j, i: (0, j)),
            pl.BlockSpec((wb.shape[0], tn), lambda j, i: (0, j)),
        ],
        out_specs=pl.BlockSpec((tm, tn), lambda j, i: (i, j)),
        out_shape=jax.ShapeDtypeStruct((T, D), BF16),
        scratch_shapes=[pltpu.VMEM((wa.shape[0], tn), BF16), pltpu.VMEM((wb.shape[0], tn), BF16)],
        compiler_params=_cparams(("arbitrary", "arbitrary")),
        name="merge",
    )(o, y_sgu, zg, zg, wa, wb)


def _o_proj_kernel(x_ref, m_ref, w_ref, g_ref, h_ref, n_ref, w_bf):
    @pl.when(pl.program_id(0) == 0)
    def _():
        w_bf[...] = w_ref[...].astype(BF16)

    h = x_ref[...] + _dot(m_ref[...], w_bf[...])
    h_ref[...] = h
    n_ref[...] = _rms(h, g_ref[...]).astype(BF16)


def _o_proj(x2, merged, w_o, g_ffn, *, tm):
    T, D = x2.shape
    return pl.pallas_call(
        _o_proj_kernel,
        grid=(T // tm,),
        in_specs=[
            pl.BlockSpec((tm, D), lambda i: (i, 0)),
            pl.BlockSpec((tm, D), lambda i: (i, 0)),
            pl.BlockSpec((D, D), lambda i: (0, 0), pipeline_mode=pl.Buffered(1)),
            pl.BlockSpec((1, D), lambda i: (0, 0)),
        ],
        out_specs=[pl.BlockSpec((tm, D), lambda i: (i, 0)), pl.BlockSpec((tm, D), lambda i: (i, 0))],
        out_shape=[jax.ShapeDtypeStruct((T, D), F32), jax.ShapeDtypeStruct((T, D), BF16)],
        scratch_shapes=[pltpu.VMEM((D, D), BF16)],
        compiler_params=_cparams(("arbitrary",)),
        name="o_proj",
    )(x2, merged, w_o, g_ffn)


def _ffn_up_kernel(n_ref, wg_ref, wu_ref, a_ref, wg_bf, wu_bf):
    @pl.when(pl.program_id(1) == 0)
    def _():
        wg_bf[...] = wg_ref[...].astype(BF16)
        wu_bf[...] = wu_ref[...].astype(BF16)

    n = n_ref[...]
    g = _dot(n, wg_bf[...])
    u = _dot(n, wu_bf[...])
    a_ref[...] = (g * _sigmoid(g) * u).astype(BF16)


def _ffn_up(n2, w_gu, *, d_ff, tm, tn):
    T, D = n2.shape
    nn = d_ff // tn
    return pl.pallas_call(
        _ffn_up_kernel,
        grid=(nn, T // tm),
        in_specs=[
            pl.BlockSpec((tm, D), lambda j, i: (i, 0)),
            pl.BlockSpec((D, tn), lambda j, i: (0, j)),
            pl.BlockSpec((D, tn), lambda j, i: (0, nn + j)),
        ],
        out_specs=pl.BlockSpec((tm, tn), lambda j, i: (i, j)),
        out_shape=jax.ShapeDtypeStruct((T, d_ff), BF16),
        scratch_shapes=[pltpu.VMEM((D, tn), BF16), pltpu.VMEM((D, tn), BF16)],
        compiler_params=_cparams(("arbitrary", "arbitrary")),
        name="ffn_up",
    )(n2, w_gu, w_gu)


def _ffn_down_kernel(a_ref, w_ref, h1_ref, h2_ref, w_bf):
    @pl.when(pl.program_id(1) == 0)
    def _():
        w_bf[...] = w_ref[...].astype(BF16)

    h2_ref[...] = h1_ref[...] + _dot(a_ref[...], w_bf[...])


def _ffn_down(act, w_down, h1, *, tm, tn):
    T, D = h1.shape
    d_ff = act.shape[1]
    return pl.pallas_call(
        _ffn_down_kernel,
        grid=(D // tn, T // tm),
        in_specs=[
            pl.BlockSpec((tm, d_ff), lambda j, i: (i, 0)),
            pl.BlockSpec((d_ff, tn), lambda j, i: (0, j)),
            pl.BlockSpec((tm, tn), lambda j, i: (i, j)),
        ],
        out_specs=pl.BlockSpec((tm, tn), lambda j, i: (i, j)),
        out_shape=jax.ShapeDtypeStruct((T, D), F32),
        scratch_shapes=[pltpu.VMEM((d_ff, tn), BF16)],
        compiler_params=_cparams(("arbitrary", "arbitrary")),
        name="ffn_down",
    )(act, w_down, h1)


def _ple_final_kernel(h2_ref, p_ref, wg_ref, wp_ref, gp_ref, gf_ref, out_ref, wg_bf, wp_bf):
    @pl.when(pl.program_id(0) == 0)
    def _():
        wg_bf[...] = wg_ref[...].astype(BF16)
        wp_bf[...] = wp_ref[...].astype(BF16)

    h2 = h2_ref[...]
    gate = _sigmoid(_dot(_rms(h2, gp_ref[...]).astype(BF16), wg_bf[...]))
    pp = _dot(p_ref[...].astype(BF16), wp_bf[...])
    out_ref[...] = _rms(h2 + gate * pp, gf_ref[...])


def _ple_final(h2, p2, w_pg, w_pp, g_ple, g_final, *, tm):
    T, D = h2.shape
    P = p2.shape[1]
    once = pl.Buffered(1)
    return pl.pallas_call(
        _ple_final_kernel,
        grid=(T // tm,),
        in_specs=[
            pl.BlockSpec((tm, D), lambda i: (i, 0)),
            pl.BlockSpec((tm, P), lambda i: (i, 0)),
            pl.BlockSpec((D, D), lambda i: (0, 0), pipeline_mode=once),
            pl.BlockSpec((P, D), lambda i: (0, 0), pipeline_mode=once),
            pl.BlockSpec((1, D), lambda i: (0, 0)),
            pl.BlockSpec((1, D), lambda i: (0, 0)),
        ],
        out_specs=pl.BlockSpec((tm, D), lambda i: (i, 0)),
        out_shape=jax.ShapeDtypeStruct((T, D), F32),
        scratch_shapes=[pltpu.VMEM((D, D), BF16), pltpu.VMEM((P, D), BF16)],
        compiler_params=_cparams(("arbitrary",)),
        name="ple_final",
    )(h2, p2, w_pg, w_pp, g_ple, g_final)


TAB_Q = 0
TAB_QI = 2
TAB_KI = 4
TAB_KR = 7
N_TABLES = 10


def _lane_tables(S):
    assert QK_ROPE == IDX_ROPE == 32
    scale = (QK_NOPE + QK_ROPE) ** -0.5 * math.log2(math.e)
    a = np.zeros((N_TABLES, LANES), np.float32)
    bc = np.zeros_like(a)
    bs = np.zeros_like(a)
    a[TAB_Q, 0:QK_NOPE] = scale
    bc[TAB_Q, 64:96] = scale
    bs[TAB_Q + 1, 64:80] = -scale
    bs[TAB_Q + 1, 80:96] = scale
    for o in (0, IDX_DIM):
        bc[TAB_QI, o:o + 32] = 1.0
        a[TAB_QI, o + 32:o + 64] = 1.0
        bs[TAB_QI + 1, o:o + 32] = 1.0
    bc[TAB_KI, 0:32] = 1.0
    a[TAB_KI, 32:64] = 1.0
    bs[TAB_KI + 1, 0:16] = -1.0
    bs[TAB_KI + 2, 16:32] = 1.0
    bc[TAB_KR, 0:32] = 1.0
    bs[TAB_KR + 1, 0:16] = -1.0
    bs[TAB_KR + 2, 16:32] = 1.0
    inv = ROPE_THETA ** (-jnp.arange(0, QK_ROPE, 2, dtype=F32) / QK_ROPE)
    ang = jnp.arange(S, dtype=F32)[:, None] * jnp.tile(inv, LANES // inv.shape[0])[None, :]
    return a[:, None, :] + bc[:, None, :] * jnp.cos(ang)[None] + bs[:, None, :] * jnp.sin(ang)[None]


def kernel(x, p, g_mix, w_in, g_cq, g_ckv, w_uq, w_uk, w_uv, w_iq, w_a_proj, g_sgu, w_spatial,
           b_spatial, w_b_proj, w_o, g_ffn, w_gu, w_down, g_ple, w_ple_gate, w_ple_proj, g_final):
    B, S, D = x.shape
    T = B * S
    depth = w_in.shape[0]
    d_ff = w_down.shape[1]
    topk = min(TOPK_MAX, S // 4)
    TQ = 256
    KCH = min(512, S)
    tm = min(512, S)
    tables = _lane_tables(S)

    h = x.reshape(T, D)
    for i in range(depth):
        wq3 = w_uq[i].reshape(Q_LORA, N_HEADS, QK_NOPE + QK_ROPE)
        x1, x2 = wq3[..., QK_NOPE:QK_NOPE + QK_ROPE // 2], wq3[..., QK_NOPE + QK_ROPE // 2:]
        wq = jnp.concatenate([wq3, x2, x1], axis=-1).reshape(Q_LORA, N_HEADS * LANES).astype(BF16)
        wk = jnp.pad(w_uk[i], ((0, 0), (0, 0), (0, LANES - QK_NOPE))).reshape(KV_LORA, N_HEADS * LANES).astype(BF16)
        wv = w_uv[i].reshape(KV_LORA, N_HEADS * V_HEAD).astype(BF16)
        wi3 = w_iq[i].reshape(Q_LORA, IDX_HEADS, IDX_DIM)
        wiq = w_iq[i].astype(BF16)
        wiqr = jnp.concatenate([-wi3[..., IDX_ROPE // 2:IDX_ROPE], wi3[..., :IDX_ROPE // 2],
                                jnp.zeros_like(wi3[..., IDX_ROPE:])], axis=-1).reshape(w_iq[i].shape).astype(BF16)

        w_t = jnp.transpose(w_in[i])
        n, small = _norm_small(h, g_mix[i][None], w_t, tm=tm)
        zg = _uv_gate_proj(n, w_t, tm=min(1024, S), tn=1024)
        q, k, v, qi, ki_lo, ki_hi, wt = _a_proj(small, g_cq[i][None], g_ckv[i][None], wq, wiq, wiqr, wk, wv,
                                                 tables, B=B, S=S, tm=tm)
        bias = _dsa_index(ki_lo, ki_hi, qi, wt, B=B, S=S, TQ=TQ, KCH=KCH, topk=topk)
        o = _dsa_attn(q, k, v, bias, B=B, S=S, TQ=TQ)
        y_sgu = _sgu(zg, g_sgu[i][None], w_spatial[i], jnp.transpose(b_spatial[i]), tm=tm)
        merged = _merge(o, y_sgu, zg, w_a_proj[i], w_b_proj[i], B=B, S=S, tm=tm, tn=1024)
        h1, n2 = _o_proj(h, merged, w_o[i], g_ffn[i][None], tm=min(256, S))
        act = _ffn_up(n2, w_gu[i], d_ff=d_ff, tm=min(1024, S), tn=512)
        h2 = _ffn_down(act, w_down[i], h1, tm=tm, tn=512)
        assert depth == 1
        h = _ple_final(h2, p[i].reshape(T, -1), w_ple_gate[i], w_ple_proj[i], g_ple[i][None],
                       g_final[None], tm=min(256, S))
    return h.reshape(B, S, D)
```

```python
import functools
import math

import numpy as np
import jax
import jax.numpy as jnp
from jax import lax
from jax.experimental import pallas as pl
from jax.experimental.pallas import tpu as pltpu

F32 = jnp.float32
BF16 = jnp.bfloat16

N_HEADS = 16
QK_NOPE = 64
QK_ROPE = 32
V_HEAD = 64
Q_LORA = 512
KV_LORA = 256
IDX_HEADS = 16
IDX_DIM = 64
IDX_ROPE = 32
TOPK_MAX = 256
SGU_CHUNK = 128
SGU_GROUPS = 8
SGU_GROUP_DIM = 128
SGU_WIDTH = SGU_GROUPS * SGU_GROUP_DIM
ROPE_THETA = 10000.0
EPS = 1e-6

LANES = 128
IN_SMALL = Q_LORA + KV_LORA + QK_ROPE + IDX_DIM + IDX_HEADS
SMALL_COLS = 1024
COL_CKV = Q_LORA
COL_MISC = Q_LORA + KV_LORA
MISC_KIDX = QK_ROPE
MISC_WIDX = QK_ROPE + IDX_DIM
VMEM_LIMIT_BYTES = 56 * 1024 * 1024
NEG_BIAS = -1e30
F32_MAX = 3.4028234663852886e38
STEPS_PER_CHECK = 4
COUNT_CHAINS = 4


def _cparams(semantics):
    return pltpu.CompilerParams(dimension_semantics=semantics, vmem_limit_bytes=VMEM_LIMIT_BYTES)


def _rms(x, g):
    return x * lax.rsqrt(jnp.mean(x * x, axis=-1, keepdims=True) + EPS) * g


def _dot(a, b):
    return jnp.dot(a, b, preferred_element_type=F32)


def _dot_nt(a, b):
    return lax.dot_general(a, b, (((1,), (1,)), ((), ())), preferred_element_type=F32)


def _gelu_exact(x):
    return 0.5 * x * (1.0 + lax.erf(x * (1.0 / math.sqrt(2.0))))


def _sigmoid(x):
    return 0.5 * jnp.tanh(0.5 * x) + 0.5


def _norm_small_kernel(x_ref, g_ref, w_ref, n_ref, small_ref, w_bf):
    @pl.when(pl.program_id(0) == 0)
    def _():
        w_bf[...] = w_ref[...].astype(BF16)

    n = _rms(x_ref[...], g_ref[...]).astype(BF16)
    n_ref[...] = n
    small_ref[...] = _dot_nt(n, w_bf[...])


def _norm_small(x2, g_mix, w_t, *, tm):
    T, D = x2.shape
    return pl.pallas_call(
        _norm_small_kernel,
        grid=(T // tm,),
        in_specs=[
            pl.BlockSpec((tm, D), lambda i: (i, 0)),
            pl.BlockSpec((1, D), lambda i: (0, 0)),
            pl.BlockSpec((SMALL_COLS, D), lambda i: (0, 0), pipeline_mode=pl.Buffered(1)),
        ],
        out_specs=[pl.BlockSpec((tm, D), lambda i: (i, 0)), pl.BlockSpec((tm, SMALL_COLS), lambda i: (i, 0))],
        out_shape=[jax.ShapeDtypeStruct((T, D), BF16), jax.ShapeDtypeStruct((T, SMALL_COLS), F32)],
        scratch_shapes=[pltpu.VMEM((SMALL_COLS, D), BF16)],
        compiler_params=_cparams(("arbitrary",)),
        name="norm_small",
    )(x2, g_mix, w_t)


def _uv_gate_kernel(n_ref, w_ref, out_ref, w_bf, *, n_z):
    j = pl.program_id(0)

    @pl.when(pl.program_id(1) == 0)
    def _():
        w_bf[...] = w_ref[...].astype(BF16)

    def proj():
        return _dot_nt(n_ref[...], w_bf[...])

    @pl.when(j < n_z)
    def _():
        out_ref[...] = _gelu_exact(proj()).astype(BF16)

    @pl.when(j >= n_z)
    def _():
        out_ref[...] = _sigmoid(proj()).astype(BF16)


def _uv_gate_proj(n, w_t, *, tm, tn):
    T, D = n.shape
    n_z = (2 * SGU_WIDTH) // tn
    n_out = 2 * SGU_WIDTH + 2 * D
    assert w_t.shape == (IN_SMALL + n_out, D) and IN_SMALL % 8 == 0 and tn % 8 == 0
    kern = functools.partial(_uv_gate_kernel, n_z=n_z)
    return pl.pallas_call(
        kern,
        grid=(n_out // tn, T // tm),
        in_specs=[
            pl.BlockSpec((tm, D), lambda j, i: (i, 0)),
            pl.BlockSpec((pl.Element(tn), pl.Element(D)), lambda j, i: ((IN_SMALL // 8 + j * (tn // 8)) * 8, 0)),
        ],
        out_specs=pl.BlockSpec((tm, tn), lambda j, i: (i, j)),
        out_shape=jax.ShapeDtypeStruct((T, n_out), BF16),
        scratch_shapes=[pltpu.VMEM((tn, D), BF16)],
        compiler_params=_cparams(("arbitrary", "arbitrary")),
        name="uv_gate_proj",
    )(n, w_t)


def _rope_block(x, cos, sin_a, sin_b):
    return x * cos + pltpu.roll(x, LANES - 16, 1) * sin_a + pltpu.roll(x, 16, 1) * sin_b


def _a_proj_kernel(small_ref, gcq_ref, gckv_ref, wq_ref, wiq_ref, wiqr_ref, wk_ref, wv_ref, tab_ref,
                   q_ref, k_ref, v_ref, qi_ref, kilo_ref, kihi_ref, wt_ref):
    c_q = _rms(small_ref[:, 0:Q_LORA], gcq_ref[...]).astype(BF16)
    c_kv = _rms(small_ref[:, COL_CKV:COL_CKV + KV_LORA], gckv_ref[...]).astype(BF16)

    q = _dot(c_q, wq_ref[...])
    cq, sq = tab_ref[TAB_Q], tab_ref[TAB_Q + 1]
    for h in range(N_HEADS):
        blk = q[:, h * LANES:(h + 1) * LANES]
        q_ref[0, h] = (blk * cq + pltpu.roll(blk, LANES - QK_ROPE, 1) * sq).astype(BF16)

    ci, si = tab_ref[TAB_QI], tab_ref[TAB_QI + 1]
    qi = _dot(c_q, wiq_ref[...])
    qir = _dot(c_q, wiqr_ref[...])
    for hp in range(IDX_HEADS // 2):
        cols = slice(hp * LANES, (hp + 1) * LANES)
        qi_ref[:, cols] = (qi[:, cols] * ci + qir[:, cols] * si).astype(BF16)

    misc = small_ref[:, COL_MISC:COL_MISC + LANES]
    ki_lo = _rope_block(pltpu.roll(misc, LANES - MISC_KIDX, 1),
                        tab_ref[TAB_KI], tab_ref[TAB_KI + 1], tab_ref[TAB_KI + 2])
    kilo_ref[...] = ki_lo.astype(BF16)
    kihi_ref[...] = pltpu.roll(ki_lo, IDX_DIM, 1).astype(BF16)

    k_rope = pltpu.roll(_rope_block(misc, tab_ref[TAB_KR], tab_ref[TAB_KR + 1], tab_ref[TAB_KR + 2]),
                        QK_NOPE, 1)
    k_nope = _dot(c_kv, wk_ref[...])
    for h in range(N_HEADS):
        k_ref[0, h] = (k_nope[:, h * LANES:(h + 1) * LANES] + k_rope).astype(BF16)

    v = _dot(c_kv, wv_ref[...])
    for p in range(N_HEADS // 2):
        v_ref[0, p] = v[:, p * LANES:(p + 1) * LANES].astype(BF16)

    w_scale = IDX_HEADS ** -0.5 * IDX_DIM ** -0.5
    wt_ref[...] = misc.T[MISC_WIDX:MISC_WIDX + IDX_HEADS, :] * w_scale


def _a_proj(small, g_cq, g_ckv, wq, wiq, wiqr, wk, wv, tables, *, B, S, tm):
    T = B * S
    nt = S // tm
    const2 = lambda i: (0, 0)
    head_spec = lambda nh: pl.BlockSpec((1, nh, tm, LANES), lambda i: (i // nt, 0, i % nt, 0))
    return pl.pallas_call(
        _a_proj_kernel,
        grid=(T // tm,),
        in_specs=[
            pl.BlockSpec((tm, SMALL_COLS), lambda i: (i, 0)),
            pl.BlockSpec((1, Q_LORA), const2),
            pl.BlockSpec((1, KV_LORA), const2),
            pl.BlockSpec(wq.shape, const2),
            pl.BlockSpec(wiq.shape, const2),
            pl.BlockSpec(wiqr.shape, const2),
            pl.BlockSpec(wk.shape, const2),
            pl.BlockSpec(wv.shape, const2),
            pl.BlockSpec((N_TABLES, tm, LANES), lambda i: (0, i % nt, 0)),
        ],
        out_specs=[
            head_spec(N_HEADS),
            head_spec(N_HEADS),
            head_spec(N_HEADS // 2),
            pl.BlockSpec((tm, IDX_HEADS * IDX_DIM), lambda i: (i, 0)),
            pl.BlockSpec((tm, LANES), lambda i: (i, 0)),
            pl.BlockSpec((tm, LANES), lambda i: (i, 0)),
            pl.BlockSpec((IDX_HEADS, tm), lambda i: (0, i)),
        ],
        out_shape=[
            jax.ShapeDtypeStruct((B, N_HEADS, S, LANES), BF16),
            jax.ShapeDtypeStruct((B, N_HEADS, S, LANES), BF16),
            jax.ShapeDtypeStruct((B, N_HEADS // 2, S, LANES), BF16),
            jax.ShapeDtypeStruct((T, IDX_HEADS * IDX_DIM), BF16),
            jax.ShapeDtypeStruct((T, LANES), BF16),
            jax.ShapeDtypeStruct((T, LANES), BF16),
            jax.ShapeDtypeStruct((IDX_HEADS, T), F32),
        ],
        compiler_params=_cparams(("parallel",)),
        name="a_proj",
    )(small, g_cq, g_ckv, wq, wiq, wiqr, wk, wv, tables)


def _dsa_index_kernel(kilo_ref, kihi_ref, qi_ref, wt_ref, bias_ref, isc_ref, mm_ref, js_ref,
                      *, S, TQ, KCH, topk):
    j = pl.program_id(1)
    q0 = j * TQ
    nkeys = q0 + TQ
    qidx = q0 + lax.broadcasted_iota(jnp.int32, (1, TQ), 1)
    kf = float(topk)

    mm_ref[0:8, :] = jnp.full((8, TQ), jnp.inf, F32)
    mm_ref[8:16, :] = jnp.full((8, TQ), -jnp.inf, F32)
    for c in range(S // KCH):
        @pl.when(c * KCH < nkeys)
        def _(c=c):
            klo = kilo_ref[c * KCH:(c + 1) * KCH, :]
            khi = kihi_ref[c * KCH:(c + 1) * KCH, :]
            acc = jnp.zeros((KCH, TQ), F32)
            for hp in range(IDX_HEADS // 2):
                qp = qi_ref[:, hp * LANES:(hp + 1) * LANES]
                s0 = _dot_nt(klo, qp)
                s1 = _dot_nt(khi, qp)
                acc = acc + jnp.maximum(s0, 0.0) * wt_ref[2 * hp:2 * hp + 1, :]
                acc = acc + jnp.maximum(s1, 0.0) * wt_ref[2 * hp + 1:2 * hp + 2, :]
            kidx = c * KCH + lax.broadcasted_iota(jnp.int32, (KCH, TQ), 0)
            causal = kidx <= qidx
            isc_ref[c * KCH:(c + 1) * KCH, :] = jnp.where(causal, acc, -jnp.inf)
            lo_part = jnp.where(causal, acc, jnp.inf).reshape(KCH // 8, 8, TQ).min(axis=0)
            hi_part = jnp.where(causal, acc, -jnp.inf).reshape(KCH // 8, 8, TQ).max(axis=0)
            mm_ref[0:8, :] = jnp.minimum(mm_ref[0:8, :], lo_part)
            mm_ref[8:16, :] = jnp.maximum(mm_ref[8:16, :], hi_part)

    def select(nch):
        def count(pred):
            acc = jnp.zeros((COUNT_CHAINS, 8, TQ), F32)
            for c in range(nch):
                ones = jnp.where(pred(isc_ref[c * TQ:(c + 1) * TQ, :], c * TQ), 1.0, 0.0)
                acc = acc + ones.reshape(COUNT_CHAINS, TQ // (8 * COUNT_CHAINS), 8, TQ).sum(axis=1)
            return acc.sum(axis=0).sum(axis=0, keepdims=True)

        row_min = mm_ref[0:8, :].min(axis=0, keepdims=True)
        row_max = mm_ref[8:16, :].max(axis=0, keepdims=True)
        full = (qidx + 1) <= topk
        c_max = count(lambda blk, k0: blk >= row_max)
        exact0 = c_max == kf
        tie0 = c_max > kf
        settled0 = full | exact0 | tie0
        lo0 = jnp.where(full, -F32_MAX, jnp.where(settled0, row_max, row_min))
        hi0 = jnp.where(full, -F32_MAX, jnp.where(tie0, jnp.inf, row_max))
        act0 = jnp.where(settled0, 0.0, 1.0)

        def step(lo, hi, act):
            mid = lo * 0.5 + hi * 0.5
            inside = (mid > lo) & (mid < hi)
            cnt = count(lambda blk, k0: blk >= mid)
            upd = (act > 0.0) & inside
            found = upd & (cnt == kf)
            lo2 = jnp.where(upd & (cnt >= kf), mid, lo)
            hi2 = jnp.where(upd & (cnt <= kf), mid, hi)
            return lo2, hi2, jnp.where(upd & jnp.logical_not(found), 1.0, 0.0)

        def any_active(act):
            return (jnp.max(act) > 0.0).astype(jnp.int32)

        def body(st):
            lo, hi, act, _ = st
            for _ in range(STEPS_PER_CHECK):
                lo, hi, act = step(lo, hi, act)
            return lo, hi, act, any_active(act)

        lo, hi, _, _ = lax.while_loop(lambda st: st[3] > 0, body, (lo0, hi0, act0, any_active(act0)))

        tie = lo < hi
        js_ref[...] = jnp.full((8, TQ), -1.0, F32)

        @pl.when(jnp.max(jnp.where(tie, 1.0, 0.0)) > 0.0)
        def _():
            need = kf - count(lambda blk, k0: blk >= hi)

            def kpos(k0):
                return (k0 + lax.broadcasted_iota(jnp.int32, (TQ, TQ), 0)).astype(F32)

            def tie_step(_, st):
                ilo, ihi = st
                imid = jnp.floor((ilo + ihi) * 0.5)
                cnt = count(lambda blk, k0: (blk >= lo) & (blk < hi) & (kpos(k0) <= imid))
                ge = cnt >= need
                return jnp.where(ge, ilo, imid), jnp.where(ge, imid, ihi)

            nsteps = int(math.ceil(math.log2(S))) + 1
            _, ihi = lax.fori_loop(0, nsteps, tie_step,
                                   (jnp.full((1, TQ), -1.0, F32), jnp.full((1, TQ), S - 1.0, F32)))
            js_ref[0:1, :] = jnp.where(tie, ihi, -1.0)

        jstar = js_ref[0:1, :]

        for c in range(S // TQ):
            if c < nch:
                blk = isc_ref[c * TQ:(c + 1) * TQ, :]
                kpos_c = (c * TQ + lax.broadcasted_iota(jnp.int32, (TQ, TQ), 0)).astype(F32)
                sel = (blk >= hi) | ((blk >= lo) & (kpos_c <= jstar))
                bias_ref[:, c * TQ:(c + 1) * TQ] = jnp.where(sel, 0.0, NEG_BIAS).T
            else:
                bias_ref[:, c * TQ:(c + 1) * TQ] = jnp.full((TQ, TQ), NEG_BIAS, F32)

    for jj in range(S // TQ):
        @pl.when(j == jj)
        def _(jj=jj):
            select(jj + 1)


def _dsa_index(ki_lo, ki_hi, qi, wt, *, B, S, TQ, KCH, topk):
    T = B * S
    nq = S // TQ
    kern = functools.partial(_dsa_index_kernel, S=S, TQ=TQ, KCH=KCH, topk=topk)
    return pl.pallas_call(
        kern,
        grid=(B, nq),
        in_specs=[
            pl.BlockSpec((S, LANES), lambda b, j: (b, 0)),
            pl.BlockSpec((S, LANES), lambda b, j: (b, 0)),
            pl.BlockSpec((TQ, IDX_HEADS * IDX_DIM), lambda b, j: (b * nq + j, 0)),
            pl.BlockSpec((IDX_HEADS, TQ), lambda b, j: (0, b * nq + j)),
        ],
        out_specs=pl.BlockSpec((TQ, S), lambda b, j: (b * nq + j, 0)),
        out_shape=jax.ShapeDtypeStruct((T, S), F32),
        scratch_shapes=[
            pltpu.VMEM((S, TQ), F32),
            pltpu.VMEM((16, TQ), F32),
            pltpu.VMEM((8, TQ), F32),
        ],
        compiler_params=_cparams(("parallel", "arbitrary")),
        name="dsa_index",
    )(ki_lo, ki_hi, qi, wt)


def _dsa_attn_kernel(q_ref, k_ref, v_ref, bias_ref, o_ref, *, S, TQ):
    j = pl.program_id(1)
    lane = lax.broadcasted_iota(jnp.int32, (TQ, LANES), 1)

    def variant(nk):
        ones = jnp.ones((nk, LANES), BF16)

        def pair(p, carry):
            vp = jnp.concatenate([v_ref[0, p, 0:nk, :], ones], axis=1)
            outs = []
            for e in range(2):
                h = 2 * p + e
                s = _dot_nt(q_ref[0, h], k_ref[0, h, 0:nk, :]) + bias_ref[:, 0:nk]
                m = s.max(axis=1, keepdims=True)
                pv = _dot(jnp.exp2(s - m).astype(BF16), vp)
                outs.append(pv[:, 0:LANES] * (1.0 / pv[:, LANES:2 * LANES]))
            o_ref[0, p] = jnp.where(lane < V_HEAD, outs[0], outs[1]).astype(BF16)
            return carry
        lax.fori_loop(0, N_HEADS // 2, pair, 0, unroll=4)

    for jj in range(S // TQ):
        @pl.when(j == jj)
        def _(jj=jj):
            variant((jj + 1) * TQ)


def _dsa_attn(q, k, v, bias, *, B, S, TQ):
    nq = S // TQ
    kern = functools.partial(_dsa_attn_kernel, S=S, TQ=TQ)
    return pl.pallas_call(
        kern,
        grid=(B, nq),
        in_specs=[
            pl.BlockSpec((1, N_HEADS, TQ, LANES), lambda b, j: (b, 0, j, 0)),
            pl.BlockSpec((1, N_HEADS, S, LANES), lambda b, j: (b, 0, 0, 0)),
            pl.BlockSpec((1, N_HEADS // 2, S, LANES), lambda b, j: (b, 0, 0, 0)),
            pl.BlockSpec((TQ, S), lambda b, j: (b * nq + j, 0)),
        ],
        out_specs=pl.BlockSpec((1, N_HEADS // 2, TQ, LANES), lambda b, j: (b, 0, j, 0)),
        out_shape=jax.ShapeDtypeStruct((B, N_HEADS // 2, S, LANES), BF16),
        compiler_params=_cparams(("parallel", "arbitrary")),
        name="dsa_attn",
    )(q, k, v, bias)


STAGE_ROWS = 512


def _stage_weights_bf16(pairs, stage, sem):
    chunks = [(src, dst, r) for src, dst in pairs for r in range(0, src.shape[0], STAGE_ROWS)]

    def copy(k):
        src, _, r = chunks[k]
        return pltpu.make_async_copy(src.at[r:r + STAGE_ROWS, :], stage.at[k % 2], sem.at[k % 2])

    copy(0).start()
    for k, (_, dst, r) in enumerate(chunks):
        if k + 1 < len(chunks):
            copy(k + 1).start()
        copy(k).wait()
        dst[r:r + STAGE_ROWS, :] = stage[k % 2].astype(BF16)


def _branch_out_kernel(o_ref, z_ref, ga_ref, gb_ref, x_ref, gs_ref, ws_ref, bt_ref, gf_ref,
                       wa_hbm, wb_hbm, wo_hbm, h_ref, n_ref,
                       wa_bf, wb_bf, wo_bf, y_scr, stage, sem, *, tm):
    @pl.when(pl.program_id(0) == 0)
    def _():
        _stage_weights_bf16([(wa_hbm, wa_bf), (wb_hbm, wb_bf), (wo_hbm, wo_bf)], stage, sem)

    row = lax.broadcasted_iota(jnp.int32, (SGU_CHUNK, SGU_CHUNK), 0)
    col = lax.broadcasted_iota(jnp.int32, (SGU_CHUNK, SGU_CHUNK), 1)
    tril = col <= row
    w = [jnp.where(tril, ws_ref[g], 0.0).astype(BF16) for g in range(SGU_GROUPS)]
    for cc in range(tm // SGU_CHUNK):
        rows = slice(cc * SGU_CHUNK, (cc + 1) * SGU_CHUNK)
        vn = _rms(z_ref[rows, SGU_WIDTH:2 * SGU_WIDTH].astype(F32), gs_ref[...]).astype(BF16)
        for g in range(SGU_GROUPS):
            cols = slice(g * SGU_GROUP_DIM, (g + 1) * SGU_GROUP_DIM)
            mixed = _dot(w[g], vn[:, cols]) + bt_ref[:, g:g + 1]
            y_scr[rows, cols] = (z_ref[rows, cols].astype(F32) * mixed).astype(BF16)

    o_a = jnp.concatenate([o_ref[0, p] for p in range(N_HEADS // 2)], axis=1)
    ya = _dot(o_a, wa_bf[...])
    yb = _dot(y_scr[...], wb_bf[...])
    merged = (ga_ref[...].astype(F32) * ya + gb_ref[...].astype(F32) * yb).astype(BF16)
    h = x_ref[...] + _dot(merged, wo_bf[...])
    h_ref[...] = h
    n_ref[...] = _rms(h, gf_ref[...]).astype(BF16)


def _branch_out(o, zg, x2, g_sgu, w_spatial, b_t, g_ffn, wa, wb, wo, *, B, S, tm):
    T, D = x2.shape
    nt = S // tm
    assert 2 * SGU_WIDTH == D and wa.shape == wb.shape == (SGU_WIDTH, D) and wo.shape == (D, D)
    row = lambda c: pl.BlockSpec((tm, D), lambda i: (i, c))
    const = lambda shape: pl.BlockSpec(shape, lambda i: (0,) * len(shape))
    hbm = pl.BlockSpec(memory_space=pl.ANY)
    kern = functools.partial(_branch_out_kernel, tm=tm)
    return pl.pallas_call(
        kern,
        grid=(T // tm,),
        in_specs=[
            pl.BlockSpec((1, N_HEADS // 2, tm, LANES), lambda i: (i // nt, 0, i % nt, 0)),
            row(0), row(1), row(2),
            row(0),
            const((1, SGU_WIDTH)),
            const((SGU_GROUPS, SGU_CHUNK, SGU_CHUNK)),
            const((SGU_CHUNK, SGU_GROUPS)),
            const((1, D)),
            hbm, hbm, hbm,
        ],
        out_specs=[pl.BlockSpec((tm, D), lambda i: (i, 0)), pl.BlockSpec((tm, D), lambda i: (i, 0))],
        out_shape=[jax.ShapeDtypeStruct((T, D), F32), jax.ShapeDtypeStruct((T, D), BF16)],
        scratch_shapes=[
            pltpu.VMEM((SGU_WIDTH, D), BF16),
            pltpu.VMEM((SGU_WIDTH, D), BF16),
            pltpu.VMEM((D, D), BF16),
            pltpu.VMEM((tm, SGU_WIDTH), BF16),
            pltpu.VMEM((2, STAGE_ROWS, D), F32),
            pltpu.SemaphoreType.DMA((2,)),
        ],
        compiler_params=_cparams(("arbitrary",)),
        name="branch_out",
    )(o, zg, zg, zg, x2, g_sgu, w_spatial, b_t, g_ffn, wa, wb, wo)


def _ffn_up_kernel(n_ref, wg_ref, wu_ref, a_ref, wg_bf, wu_bf):
    @pl.when(pl.program_id(1) == 0)
    def _():
        wg_bf[...] = wg_ref[...].astype(BF16)
        wu_bf[...] = wu_ref[...].astype(BF16)

    n = n_ref[...]
    g = _dot(n, wg_bf[...])
    u = _dot(n, wu_bf[...])
    a_ref[...] = (g * _sigmoid(g) * u).astype(BF16)


def _ffn_up(n2, w_gu, *, d_ff, tm, tn):
    T, D = n2.shape
    nn = d_ff // tn
    return pl.pallas_call(
        _ffn_up_kernel,
        grid=(nn, T // tm),
        in_specs=[
            pl.BlockSpec((tm, D), lambda j, i: (i, 0)),
            pl.BlockSpec((D, tn), lambda j, i: (0, j)),
            pl.BlockSpec((D, tn), lambda j, i: (0, nn + j)),
        ],
        out_specs=pl.BlockSpec((tm, tn), lambda j, i: (i, j)),
        out_shape=jax.ShapeDtypeStruct((T, d_ff), BF16),
        scratch_shapes=[pltpu.VMEM((D, tn), BF16), pltpu.VMEM((D, tn), BF16)],
        compiler_params=_cparams(("arbitrary", "arbitrary")),
        name="ffn_up",
    )(n2, w_gu, w_gu)


def _ffn_down_kernel(a_ref, w_ref, h1_ref, h2_ref, w_bf):
    @pl.when(pl.program_id(1) == 0)
    def _():
        w_bf[...] = w_ref[...].astype(BF16)

    h2_ref[...] = h1_ref[...] + _dot(a_ref[...], w_bf[...])


def _ffn_down(act, w_down, h1, *, tm, tn):
    T, D = h1.shape
    d_ff = act.shape[1]
    return pl.pallas_call(
        _ffn_down_kernel,
        grid=(D // tn, T // tm),
        in_specs=[
            pl.BlockSpec((tm, d_ff), lambda j, i: (i, 0)),
            pl.BlockSpec((d_ff, tn), lambda j, i: (0, j)),
            pl.BlockSpec((tm, tn), lambda j, i: (i, j)),
        ],
        out_specs=pl.BlockSpec((tm, tn), lambda j, i: (i, j)),
        out_shape=jax.ShapeDtypeStruct((T, D), F32),
        scratch_shapes=[pltpu.VMEM((d_ff, tn), BF16)],
        compiler_params=_cparams(("arbitrary", "arbitrary")),
        name="ffn_down",
    )(act, w_down, h1)


def _ple_final_kernel(h2_ref, p_ref, wg_ref, wp_ref, gp_ref, gf_ref, out_ref, wg_bf, wp_bf):
    @pl.when(pl.program_id(0) == 0)
    def _():
        wg_bf[...] = wg_ref[...].astype(BF16)
        wp_bf[...] = wp_ref[...].astype(BF16)

    h2 = h2_ref[...]
    gate = _sigmoid(_dot(_rms(h2, gp_ref[...]).astype(BF16), wg_bf[...]))
    pp = _dot(p_ref[...].astype(BF16), wp_bf[...])
    out_ref[...] = _rms(h2 + gate * pp, gf_ref[...])


def _ple_final(h2, p2, w_pg, w_pp, g_ple, g_final, *, tm):
    T, D = h2.shape
    P = p2.shape[1]
    once = pl.Buffered(1)
    return pl.pallas_call(
        _ple_final_kernel,
        grid=(T // tm,),
        in_specs=[
            pl.BlockSpec((tm, D), lambda i: (i, 0)),
            pl.BlockSpec((tm, P), lambda i: (i, 0)),
            pl.BlockSpec((D, D), lambda i: (0, 0), pipeline_mode=once),
            pl.BlockSpec((P, D), lambda i: (0, 0), pipeline_mode=once),
            pl.BlockSpec((1, D), lambda i: (0, 0)),
            pl.BlockSpec((1, D), lambda i: (0, 0)),
        ],
        out_specs=pl.BlockSpec((tm, D), lambda i: (i, 0)),
        out_shape=jax.ShapeDtypeStruct((T, D), F32),
        scratch_shapes=[pltpu.VMEM((D, D), BF16), pltpu.VMEM((P, D), BF16)],
        compiler_params=_cparams(("arbitrary",)),
        name="ple_final",
    )(h2, p2, w_pg, w_pp, g_ple, g_final)


TAB_Q = 0
TAB_QI = 2
TAB_KI = 4
TAB_KR = 7
N_TABLES = 10


def _lane_tables(S):
    assert QK_ROPE == IDX_ROPE == 32
    scale = (QK_NOPE + QK_ROPE) ** -0.5 * math.log2(math.e)
    a = np.zeros((N_TABLES, LANES), np.float32)
    bc = np.zeros_like(a)
    bs = np.zeros_like(a)
    a[TAB_Q, 0:QK_NOPE] = scale
    bc[TAB_Q, 64:96] = scale
    bs[TAB_Q + 1, 64:80] = -scale
    bs[TAB_Q + 1, 80:96] = scale
    for o in (0, IDX_DIM):
        bc[TAB_QI, o:o + 32] = 1.0
        a[TAB_QI, o + 32:o + 64] = 1.0
        bs[TAB_QI + 1, o:o + 32] = 1.0
    bc[TAB_KI, 0:32] = 1.0
    a[TAB_KI, 32:64] = 1.0
    bs[TAB_KI + 1, 0:16] = -1.0
    bs[TAB_KI + 2, 16:32] = 1.0
    bc[TAB_KR, 0:32] = 1.0
    bs[TAB_KR + 1, 0:16] = -1.0
    bs[TAB_KR + 2, 16:32] = 1.0
    inv = ROPE_THETA ** (-jnp.arange(0, QK_ROPE, 2, dtype=F32) / QK_ROPE)
    ang = jnp.arange(S, dtype=F32)[:, None] * jnp.tile(inv, LANES // inv.shape[0])[None, :]
    return a[:, None, :] + bc[:, None, :] * jnp.cos(ang)[None] + bs[:, None, :] * jnp.sin(ang)[None]


def kernel(x, p, g_mix, w_in, g_cq, g_ckv, w_uq, w_uk, w_uv, w_iq, w_a_proj, g_sgu, w_spatial,
           b_spatial, w_b_proj, w_o, g_ffn, w_gu, w_down, g_ple, w_ple_gate, w_ple_proj, g_final):
    B, S, D = x.shape
    T = B * S
    depth = w_in.shape[0]
    d_ff = w_down.shape[1]
    topk = min(TOPK_MAX, S // 4)
    TQ = 256
    KCH = min(512, S)
    tm = min(512, S)
    tables = _lane_tables(S)

    h = x.reshape(T, D)
    for i in range(depth):
        wq3 = w_uq[i].reshape(Q_LORA, N_HEADS, QK_NOPE + QK_ROPE)
        x1, x2 = wq3[..., QK_NOPE:QK_NOPE + QK_ROPE // 2], wq3[..., QK_NOPE + QK_ROPE // 2:]
        wq = jnp.concatenate([wq3, x2, x1], axis=-1).reshape(Q_LORA, N_HEADS * LANES).astype(BF16)
        wk = jnp.pad(w_uk[i], ((0, 0), (0, 0), (0, LANES - QK_NOPE))).reshape(KV_LORA, N_HEADS * LANES).astype(BF16)
        wv = w_uv[i].reshape(KV_LORA, N_HEADS * V_HEAD).astype(BF16)
        wi3 = w_iq[i].reshape(Q_LORA, IDX_HEADS, IDX_DIM)
        wiq = w_iq[i].astype(BF16)
        wiqr = jnp.concatenate([-wi3[..., IDX_ROPE // 2:IDX_ROPE], wi3[..., :IDX_ROPE // 2],
                                jnp.zeros_like(wi3[..., IDX_ROPE:])], axis=-1).reshape(w_iq[i].shape).astype(BF16)

        w_t = jnp.transpose(w_in[i])
        n, small = _norm_small(h, g_mix[i][None], w_t, tm=tm)
        zg = _uv_gate_proj(n, w_t, tm=min(1024, S), tn=1024)
        q, k, v, qi, ki_lo, ki_hi, wt = _a_proj(small, g_cq[i][None], g_ckv[i][None], wq, wiq, wiqr, wk, wv,
                                                 tables, B=B, S=S, tm=tm)
        bias = _dsa_index(ki_lo, ki_hi, qi, wt, B=B, S=S, TQ=TQ, KCH=KCH, topk=topk)
        o = _dsa_attn(q, k, v, bias, B=B, S=S, TQ=TQ)
        h1, n2 = _branch_out(o, zg, h, g_sgu[i][None], w_spatial[i], jnp.transpose(b_spatial[i]),
                             g_ffn[i][None], w_a_proj[i], w_b_proj[i], w_o[i], B=B, S=S, tm=min(256, S))
        act = _ffn_up(n2, w_gu[i], d_ff=d_ff, tm=min(1024, S), tn=512)
        h2 = _ffn_down(act, w_down[i], h1, tm=tm, tn=512)
        assert depth == 1
        h = _ple_final(h2, p[i].reshape(T, -1), w_ple_gate[i], w_ple_proj[i], g_ple[i][None],
                       g_final[None], tm=min(256, S))
    return h.reshape(B, S, D)
```

```python
import functools
import math

import numpy as np
import jax
import jax.numpy as jnp
from jax import lax
from jax.experimental import pallas as pl
from jax.experimental.pallas import tpu as pltpu

F32 = jnp.float32
BF16 = jnp.bfloat16

N_HEADS = 16
QK_NOPE = 64
QK_ROPE = 32
V_HEAD = 64
Q_LORA = 512
KV_LORA = 256
IDX_HEADS = 16
IDX_DIM = 64
IDX_ROPE = 32
TOPK_MAX = 256
SGU_CHUNK = 128
SGU_GROUPS = 8
SGU_GROUP_DIM = 128
SGU_WIDTH = SGU_GROUPS * SGU_GROUP_DIM
ROPE_THETA = 10000.0
EPS = 1e-6

LANES = 128
IN_SMALL = Q_LORA + KV_LORA + QK_ROPE + IDX_DIM + IDX_HEADS
SMALL_COLS = 1024
COL_CKV = Q_LORA
COL_MISC = Q_LORA + KV_LORA
MISC_KIDX = QK_ROPE
MISC_WIDX = QK_ROPE + IDX_DIM
VMEM_LIMIT_BYTES = 56 * 1024 * 1024
NEG_BIAS = -1e30
F32_MAX = 3.4028234663852886e38
STEPS_PER_CHECK = 4
COUNT_CHAINS = 4


def _cparams(semantics):
    return pltpu.CompilerParams(dimension_semantics=semantics, vmem_limit_bytes=VMEM_LIMIT_BYTES)


def _rms(x, g):
    return x * lax.rsqrt(jnp.mean(x * x, axis=-1, keepdims=True) + EPS) * g


def _dot(a, b):
    return jnp.dot(a, b, preferred_element_type=F32)


def _dot_nt(a, b):
    return lax.dot_general(a, b, (((1,), (1,)), ((), ())), preferred_element_type=F32)


def _gelu_exact(x):
    return 0.5 * x * (1.0 + lax.erf(x * (1.0 / math.sqrt(2.0))))


def _sigmoid(x):
    return 0.5 * jnp.tanh(0.5 * x) + 0.5


def _uv_gate_kernel(n_ref, w_ref, out_ref, w_bf, *, n_z):
    j = pl.program_id(0)

    @pl.when(pl.program_id(1) == 0)
    def _():
        w_bf[...] = w_ref[...].astype(BF16)

    def proj():
        return _dot_nt(n_ref[...], w_bf[...])

    @pl.when(j < n_z)
    def _():
        out_ref[...] = _gelu_exact(proj()).astype(BF16)

    @pl.when(j >= n_z)
    def _():
        out_ref[...] = _sigmoid(proj()).astype(BF16)


def _uv_gate_proj(n, w_t, *, tm, tn):
    T, D = n.shape
    n_z = (2 * SGU_WIDTH) // tn
    n_out = 2 * SGU_WIDTH + 2 * D
    assert w_t.shape == (IN_SMALL + n_out, D) and IN_SMALL % 8 == 0 and tn % 8 == 0
    kern = functools.partial(_uv_gate_kernel, n_z=n_z)
    return pl.pallas_call(
        kern,
        grid=(n_out // tn, T // tm),
        in_specs=[
            pl.BlockSpec((tm, D), lambda j, i: (i, 0)),
            pl.BlockSpec((pl.Element(tn), pl.Element(D)), lambda j, i: ((IN_SMALL // 8 + j * (tn // 8)) * 8, 0)),
        ],
        out_specs=pl.BlockSpec((tm, tn), lambda j, i: (i, j)),
        out_shape=jax.ShapeDtypeStruct((T, n_out), BF16),
        scratch_shapes=[pltpu.VMEM((tn, D), BF16)],
        compiler_params=_cparams(("arbitrary", "arbitrary")),
        name="uv_gate_proj",
    )(n, w_t)


def _rope_block(x, cos, sin_a, sin_b):
    return x * cos + pltpu.roll(x, LANES - 16, 1) * sin_a + pltpu.roll(x, 16, 1) * sin_b


def _a_proj_kernel(x_ref, gmix_ref, ws_ref, gcq_ref, gckv_ref, wq_ref, wiq_ref, wiqr_ref, wk_ref, wv_ref, tab_ref,
                   n_ref, q_ref, k_ref, v_ref, qi_ref, kilo_ref, kihi_ref, wt_ref, ws_bf):
    @pl.when(pl.program_id(0) == 0)
    def _():
        ws_bf[...] = ws_ref[...].astype(BF16)

    n = _rms(x_ref[...], gmix_ref[...]).astype(BF16)
    n_ref[...] = n
    small = _dot_nt(n, ws_bf[...])
    c_q = _rms(small[:, 0:Q_LORA], gcq_ref[...]).astype(BF16)
    c_kv = _rms(small[:, COL_CKV:COL_CKV + KV_LORA], gckv_ref[...]).astype(BF16)

    q = _dot(c_q, wq_ref[...])
    cq, sq = tab_ref[TAB_Q], tab_ref[TAB_Q + 1]
    for h in range(N_HEADS):
        blk = q[:, h * LANES:(h + 1) * LANES]
        q_ref[0, h] = (blk * cq + pltpu.roll(blk, LANES - QK_ROPE, 1) * sq).astype(BF16)

    ci, si = tab_ref[TAB_QI], tab_ref[TAB_QI + 1]
    qi = _dot(c_q, wiq_ref[...])
    qir = _dot(c_q, wiqr_ref[...])
    for hp in range(IDX_HEADS // 2):
        cols = slice(hp * LANES, (hp + 1) * LANES)
        qi_ref[:, cols] = (qi[:, cols] * ci + qir[:, cols] * si).astype(BF16)

    misc = small[:, COL_MISC:COL_MISC + LANES]
    ki_lo = _rope_block(pltpu.roll(misc, LANES - MISC_KIDX, 1),
                        tab_ref[TAB_KI], tab_ref[TAB_KI + 1], tab_ref[TAB_KI + 2])
    kilo_ref[...] = ki_lo.astype(BF16)
    kihi_ref[...] = pltpu.roll(ki_lo, IDX_DIM, 1).astype(BF16)

    k_rope = pltpu.roll(_rope_block(misc, tab_ref[TAB_KR], tab_ref[TAB_KR + 1], tab_ref[TAB_KR + 2]),
                        QK_NOPE, 1)
    k_nope = _dot(c_kv, wk_ref[...])
    for h in range(N_HEADS):
        k_ref[0, h] = (k_nope[:, h * LANES:(h + 1) * LANES] + k_rope).astype(BF16)

    v = _dot(c_kv, wv_ref[...])
    for p in range(N_HEADS // 2):
        v_ref[0, p] = v[:, p * LANES:(p + 1) * LANES].astype(BF16)

    w_scale = IDX_HEADS ** -0.5 * IDX_DIM ** -0.5
    wt_ref[...] = misc.T[MISC_WIDX:MISC_WIDX + IDX_HEADS, :] * w_scale


def _a_proj(x2, g_mix, w_t, g_cq, g_ckv, wq, wiq, wiqr, wk, wv, tables, *, B, S, tm):
    T, D = x2.shape
    nt = S // tm
    const2 = lambda i: (0, 0)
    head_spec = lambda nh: pl.BlockSpec((1, nh, tm, LANES), lambda i: (i // nt, 0, i % nt, 0))
    return pl.pallas_call(
        _a_proj_kernel,
        grid=(T // tm,),
        in_specs=[
            pl.BlockSpec((tm, D), lambda i: (i, 0)),
            pl.BlockSpec((1, D), const2),
            pl.BlockSpec((SMALL_COLS, D), const2, pipeline_mode=pl.Buffered(1)),
            pl.BlockSpec((1, Q_LORA), const2),
            pl.BlockSpec((1, KV_LORA), const2),
            pl.BlockSpec(wq.shape, const2),
            pl.BlockSpec(wiq.shape, const2),
            pl.BlockSpec(wiqr.shape, const2),
            pl.BlockSpec(wk.shape, const2),
            pl.BlockSpec(wv.shape, const2),
            pl.BlockSpec((N_TABLES, tm, LANES), lambda i: (0, i % nt, 0)),
        ],
        out_specs=[
            pl.BlockSpec((tm, D), lambda i: (i, 0)),
            head_spec(N_HEADS),
            head_spec(N_HEADS),
            head_spec(N_HEADS // 2),
            pl.BlockSpec((tm, IDX_HEADS * IDX_DIM), lambda i: (i, 0)),
            pl.BlockSpec((tm, LANES), lambda i: (i, 0)),
            pl.BlockSpec((tm, LANES), lambda i: (i, 0)),
            pl.BlockSpec((IDX_HEADS, tm), lambda i: (0, i)),
        ],
        out_shape=[
            jax.ShapeDtypeStruct((T, D), BF16),
            jax.ShapeDtypeStruct((B, N_HEADS, S, LANES), BF16),
            jax.ShapeDtypeStruct((B, N_HEADS, S, LANES), BF16),
            jax.ShapeDtypeStruct((B, N_HEADS // 2, S, LANES), BF16),
            jax.ShapeDtypeStruct((T, IDX_HEADS * IDX_DIM), BF16),
            jax.ShapeDtypeStruct((T, LANES), BF16),
            jax.ShapeDtypeStruct((T, LANES), BF16),
            jax.ShapeDtypeStruct((IDX_HEADS, T), F32),
        ],
        scratch_shapes=[pltpu.VMEM((SMALL_COLS, D), BF16)],
        compiler_params=_cparams(("arbitrary",)),
        name="norm_a_proj",
    )(x2, g_mix, w_t, g_cq, g_ckv, wq, wiq, wiqr, wk, wv, tables)


def _dsa_index_kernel(kilo_ref, kihi_ref, qi_ref, wt_ref, bias_ref, isc_ref, mm_ref, js_ref,
                      *, S, TQ, KCH, topk):
    j = pl.program_id(1)
    q0 = j * TQ
    nkeys = q0 + TQ
    qidx = q0 + lax.broadcasted_iota(jnp.int32, (1, TQ), 1)
    kf = float(topk)

    mm_ref[0:8, :] = jnp.full((8, TQ), jnp.inf, F32)
    mm_ref[8:16, :] = jnp.full((8, TQ), -jnp.inf, F32)
    for c in range(S // KCH):
        @pl.when(c * KCH < nkeys)
        def _(c=c):
            klo = kilo_ref[c * KCH:(c + 1) * KCH, :]
            khi = kihi_ref[c * KCH:(c + 1) * KCH, :]
            acc = jnp.zeros((KCH, TQ), F32)
            for hp in range(IDX_HEADS // 2):
                qp = qi_ref[:, hp * LANES:(hp + 1) * LANES]
                s0 = _dot_nt(klo, qp)
                s1 = _dot_nt(khi, qp)
                acc = acc + jnp.maximum(s0, 0.0) * wt_ref[2 * hp:2 * hp + 1, :]
                acc = acc + jnp.maximum(s1, 0.0) * wt_ref[2 * hp + 1:2 * hp + 2, :]
            kidx = c * KCH + lax.broadcasted_iota(jnp.int32, (KCH, TQ), 0)
            causal = kidx <= qidx
            isc_ref[c * KCH:(c + 1) * KCH, :] = jnp.where(causal, acc, -jnp.inf)
            lo_part = jnp.where(causal, acc, jnp.inf).reshape(KCH // 8, 8, TQ).min(axis=0)
            hi_part = jnp.where(causal, acc, -jnp.inf).reshape(KCH // 8, 8, TQ).max(axis=0)
            mm_ref[0:8, :] = jnp.minimum(mm_ref[0:8, :], lo_part)
            mm_ref[8:16, :] = jnp.maximum(mm_ref[8:16, :], hi_part)

    def select(nch):
        def count(pred):
            acc = jnp.zeros((COUNT_CHAINS, 8, TQ), F32)
            for c in range(nch):
                ones = jnp.where(pred(isc_ref[c * TQ:(c + 1) * TQ, :], c * TQ), 1.0, 0.0)
                acc = acc + ones.reshape(COUNT_CHAINS, TQ // (8 * COUNT_CHAINS), 8, TQ).sum(axis=1)
            return acc.sum(axis=0).sum(axis=0, keepdims=True)

        row_min = mm_ref[0:8, :].min(axis=0, keepdims=True)
        row_max = mm_ref[8:16, :].max(axis=0, keepdims=True)
        full = (qidx + 1) <= topk
        c_max = count(lambda blk, k0: blk >= row_max)
        exact0 = c_max == kf
        tie0 = c_max > kf
        settled0 = full | exact0 | tie0
        lo0 = jnp.where(full, -F32_MAX, jnp.where(settled0, row_max, row_min))
        hi0 = jnp.where(full, -F32_MAX, jnp.where(tie0, jnp.inf, row_max))
        act0 = jnp.where(settled0, 0.0, 1.0)

        def step(lo, hi, act):
            mid = lo * 0.5 + hi * 0.5
            inside = (mid > lo) & (mid < hi)
            cnt = count(lambda blk, k0: blk >= mid)
            upd = (act > 0.0) & inside
            found = upd & (cnt == kf)
            lo2 = jnp.where(upd & (cnt >= kf), mid, lo)
            hi2 = jnp.where(upd & (cnt <= kf), mid, hi)
            return lo2, hi2, jnp.where(upd & jnp.logical_not(found), 1.0, 0.0)

        def any_active(act):
            return (jnp.max(act) > 0.0).astype(jnp.int32)

        def body(st):
            lo, hi, act, _ = st
            for _ in range(STEPS_PER_CHECK):
                lo, hi, act = step(lo, hi, act)
            return lo, hi, act, any_active(act)

        lo, hi, _, _ = lax.while_loop(lambda st: st[3] > 0, body, (lo0, hi0, act0, any_active(act0)))

        tie = lo < hi
        js_ref[...] = jnp.full((8, TQ), -1.0, F32)

        @pl.when(jnp.max(jnp.where(tie, 1.0, 0.0)) > 0.0)
        def _():
            need = kf - count(lambda blk, k0: blk >= hi)

            def kpos(k0):
                return (k0 + lax.broadcasted_iota(jnp.int32, (TQ, TQ), 0)).astype(F32)

            def tie_step(_, st):
                ilo, ihi = st
                imid = jnp.floor((ilo + ihi) * 0.5)
                cnt = count(lambda blk, k0: (blk >= lo) & (blk < hi) & (kpos(k0) <= imid))
                ge = cnt >= need
                return jnp.where(ge, ilo, imid), jnp.where(ge, imid, ihi)

            nsteps = int(math.ceil(math.log2(S))) + 1
            _, ihi = lax.fori_loop(0, nsteps, tie_step,
                                   (jnp.full((1, TQ), -1.0, F32), jnp.full((1, TQ), S - 1.0, F32)))
            js_ref[0:1, :] = jnp.where(tie, ihi, -1.0)

        jstar = js_ref[0:1, :]

        for c in range(S // TQ):
            if c < nch:
                blk = isc_ref[c * TQ:(c + 1) * TQ, :]
                kpos_c = (c * TQ + lax.broadcasted_iota(jnp.int32, (TQ, TQ), 0)).astype(F32)
                sel = (blk >= hi) | ((blk >= lo) & (kpos_c <= jstar))
                bias_ref[:, c * TQ:(c + 1) * TQ] = jnp.where(sel, 0.0, NEG_BIAS).T
            else:
                bias_ref[:, c * TQ:(c + 1) * TQ] = jnp.full((TQ, TQ), NEG_BIAS, F32)

    for jj in range(S // TQ):
        @pl.when(j == jj)
        def _(jj=jj):
            select(jj + 1)


def _dsa_index(ki_lo, ki_hi, qi, wt, *, B, S, TQ, KCH, topk):
    T = B * S
    nq = S // TQ
    kern = functools.partial(_dsa_index_kernel, S=S, TQ=TQ, KCH=KCH, topk=topk)
    return pl.pallas_call(
        kern,
        grid=(B, nq),
        in_specs=[
            pl.BlockSpec((S, LANES), lambda b, j: (b, 0)),
            pl.BlockSpec((S, LANES), lambda b, j: (b, 0)),
            pl.BlockSpec((TQ, IDX_HEADS * IDX_DIM), lambda b, j: (b * nq + j, 0)),
            pl.BlockSpec((IDX_HEADS, TQ), lambda b, j: (0, b * nq + j)),
        ],
        out_specs=pl.BlockSpec((TQ, S), lambda b, j: (b * nq + j, 0)),
        out_shape=jax.ShapeDtypeStruct((T, S), F32),
        scratch_shapes=[
            pltpu.VMEM((S, TQ), F32),
            pltpu.VMEM((16, TQ), F32),
            pltpu.VMEM((8, TQ), F32),
        ],
        compiler_params=_cparams(("parallel", "arbitrary")),
        name="dsa_index",
    )(ki_lo, ki_hi, qi, wt)


def _dsa_attn_kernel(q_ref, k_ref, v_ref, bias_ref, o_ref, *, S, TQ):
    j = pl.program_id(1)
    lane = lax.broadcasted_iota(jnp.int32, (TQ, LANES), 1)

    def variant(nk):
        ones = jnp.ones((nk, LANES), BF16)

        def pair(p, carry):
            vp = jnp.concatenate([v_ref[0, p, 0:nk, :], ones], axis=1)
            outs = []
            for e in range(2):
                h = 2 * p + e
                s = _dot_nt(q_ref[0, h], k_ref[0, h, 0:nk, :]) + bias_ref[:, 0:nk]
                m = s.max(axis=1, keepdims=True)
                pv = _dot(jnp.exp2(s - m).astype(BF16), vp)
                outs.append(pv[:, 0:LANES] * (1.0 / pv[:, LANES:2 * LANES]))
            o_ref[0, p] = jnp.where(lane < V_HEAD, outs[0], outs[1]).astype(BF16)
            return carry
        lax.fori_loop(0, N_HEADS // 2, pair, 0, unroll=4)

    for jj in range(S // TQ):
        @pl.when(j == jj)
        def _(jj=jj):
            variant((jj + 1) * TQ)


def _dsa_attn(q, k, v, bias, *, B, S, TQ):
    nq = S // TQ
    kern = functools.partial(_dsa_attn_kernel, S=S, TQ=TQ)
    return pl.pallas_call(
        kern,
        grid=(B, nq),
        in_specs=[
            pl.BlockSpec((1, N_HEADS, TQ, LANES), lambda b, j: (b, 0, j, 0)),
            pl.BlockSpec((1, N_HEADS, S, LANES), lambda b, j: (b, 0, 0, 0)),
            pl.BlockSpec((1, N_HEADS // 2, S, LANES), lambda b, j: (b, 0, 0, 0)),
            pl.BlockSpec((TQ, S), lambda b, j: (b * nq + j, 0)),
        ],
        out_specs=pl.BlockSpec((1, N_HEADS // 2, TQ, LANES), lambda b, j: (b, 0, j, 0)),
        out_shape=jax.ShapeDtypeStruct((B, N_HEADS // 2, S, LANES), BF16),
        compiler_params=_cparams(("parallel", "arbitrary")),
        name="dsa_attn",
    )(q, k, v, bias)


STAGE_ROWS = 512


def _stage_weights_bf16(pairs, stage, sem):
    chunks = []
    for src, dst in pairs:
        rows = min(src.shape[0], STAGE_ROWS)
        assert src.shape[0] % rows == 0 and src.shape == dst.shape
        chunks += [(src, dst, r, rows) for r in range(0, src.shape[0], rows)]

    def copy(k):
        src, _, r, rows = chunks[k]
        return pltpu.make_async_copy(src.at[r:r + rows, :], stage.at[k % 2, 0:rows, :], sem.at[k % 2])

    copy(0).start()
    for k, (_, dst, r, rows) in enumerate(chunks):
        if k + 1 < len(chunks):
            copy(k + 1).start()
        copy(k).wait()
        dst[r:r + rows, :] = stage[k % 2, 0:rows, :].astype(BF16)


def _branch_out_kernel(o_ref, z_ref, ga_ref, gb_ref, x_ref, gs_ref, ws_ref, bt_ref, gf_ref,
                       wa_hbm, wb_hbm, wo_hbm, h_ref, n_ref,
                       wa_bf, wb_bf, wo_bf, y_scr, stage, sem, *, tm):
    @pl.when(pl.program_id(0) == 0)
    def _():
        _stage_weights_bf16([(wa_hbm, wa_bf), (wb_hbm, wb_bf), (wo_hbm, wo_bf)], stage, sem)

    row = lax.broadcasted_iota(jnp.int32, (SGU_CHUNK, SGU_CHUNK), 0)
    col = lax.broadcasted_iota(jnp.int32, (SGU_CHUNK, SGU_CHUNK), 1)
    tril = col <= row
    w = [jnp.where(tril, ws_ref[g], 0.0).astype(BF16) for g in range(SGU_GROUPS)]
    for cc in range(tm // SGU_CHUNK):
        rows = slice(cc * SGU_CHUNK, (cc + 1) * SGU_CHUNK)
        vn = _rms(z_ref[rows, SGU_WIDTH:2 * SGU_WIDTH].astype(F32), gs_ref[...]).astype(BF16)
        for g in range(SGU_GROUPS):
            cols = slice(g * SGU_GROUP_DIM, (g + 1) * SGU_GROUP_DIM)
            mixed = _dot(w[g], vn[:, cols]) + bt_ref[:, g:g + 1]
            y_scr[rows, cols] = (z_ref[rows, cols].astype(F32) * mixed).astype(BF16)

    o_a = jnp.concatenate([o_ref[0, p] for p in range(N_HEADS // 2)], axis=1)
    ya = _dot(o_a, wa_bf[...])
    yb = _dot(y_scr[...], wb_bf[...])
    merged = (ga_ref[...].astype(F32) * ya + gb_ref[...].astype(F32) * yb).astype(BF16)
    h = x_ref[...] + _dot(merged, wo_bf[...])
    h_ref[...] = h
    n_ref[...] = _rms(h, gf_ref[...]).astype(BF16)


def _branch_out(o, zg, x2, g_sgu, w_spatial, b_t, g_ffn, wa, wb, wo, *, B, S, tm):
    T, D = x2.shape
    nt = S // tm
    assert 2 * SGU_WIDTH == D and wa.shape == wb.shape == (SGU_WIDTH, D) and wo.shape == (D, D)
    row = lambda c: pl.BlockSpec((tm, D), lambda i: (i, c))
    const = lambda shape: pl.BlockSpec(shape, lambda i: (0,) * len(shape))
    hbm = pl.BlockSpec(memory_space=pl.ANY)
    kern = functools.partial(_branch_out_kernel, tm=tm)
    return pl.pallas_call(
        kern,
        grid=(T // tm,),
        in_specs=[
            pl.BlockSpec((1, N_HEADS // 2, tm, LANES), lambda i: (i // nt, 0, i % nt, 0)),
            row(0), row(1), row(2),
            row(0),
            const((1, SGU_WIDTH)),
            const((SGU_GROUPS, SGU_CHUNK, SGU_CHUNK)),
            const((SGU_CHUNK, SGU_GROUPS)),
            const((1, D)),
            hbm, hbm, hbm,
        ],
        out_specs=[pl.BlockSpec((tm, D), lambda i: (i, 0)), pl.BlockSpec((tm, D), lambda i: (i, 0))],
        out_shape=[jax.ShapeDtypeStruct((T, D), F32), jax.ShapeDtypeStruct((T, D), BF16)],
        scratch_shapes=[
            pltpu.VMEM((SGU_WIDTH, D), BF16),
            pltpu.VMEM((SGU_WIDTH, D), BF16),
            pltpu.VMEM((D, D), BF16),
            pltpu.VMEM((tm, SGU_WIDTH), BF16),
            pltpu.VMEM((2, STAGE_ROWS, D), F32),
            pltpu.SemaphoreType.DMA((2,)),
        ],
        compiler_params=_cparams(("arbitrary",)),
        name="branch_out",
    )(o, zg, zg, zg, x2, g_sgu, w_spatial, b_t, g_ffn, wa, wb, wo)


def _ffn_up_kernel(n_ref, wg_ref, wu_ref, a_ref, wg_bf, wu_bf):
    @pl.when(pl.program_id(1) == 0)
    def _():
        wg_bf[...] = wg_ref[...].astype(BF16)
        wu_bf[...] = wu_ref[...].astype(BF16)

    n = n_ref[...]
    g = _dot(n, wg_bf[...])
    u = _dot(n, wu_bf[...])
    a_ref[...] = (g * _sigmoid(g) * u).astype(BF16)


def _ffn_up(n2, w_gu, *, d_ff, tm, tn):
    T, D = n2.shape
    nn = d_ff // tn
    return pl.pallas_call(
        _ffn_up_kernel,
        grid=(nn, T // tm),
        in_specs=[
            pl.BlockSpec((tm, D), lambda j, i: (i, 0)),
            pl.BlockSpec((D, tn), lambda j, i: (0, j)),
            pl.BlockSpec((D, tn), lambda j, i: (0, nn + j)),
        ],
        out_specs=pl.BlockSpec((tm, tn), lambda j, i: (i, j)),
        out_shape=jax.ShapeDtypeStruct((T, d_ff), BF16),
        scratch_shapes=[pltpu.VMEM((D, tn), BF16), pltpu.VMEM((D, tn), BF16)],
        compiler_params=_cparams(("arbitrary", "arbitrary")),
        name="ffn_up",
    )(n2, w_gu, w_gu)


def _ffn_down_kernel(a_ref, w_ref, h1_ref, h2_ref, w_bf):
    @pl.when(pl.program_id(1) == 0)
    def _():
        w_bf[...] = w_ref[...].astype(BF16)

    h2_ref[...] = h1_ref[...] + _dot(a_ref[...], w_bf[...])


def _ffn_down(act, w_down, h1, *, tm, tn):
    T, D = h1.shape
    d_ff = act.shape[1]
    return pl.pallas_call(
        _ffn_down_kernel,
        grid=(D // tn, T // tm),
        in_specs=[
            pl.BlockSpec((tm, d_ff), lambda j, i: (i, 0)),
            pl.BlockSpec((d_ff, tn), lambda j, i: (0, j)),
            pl.BlockSpec((tm, tn), lambda j, i: (i, j)),
        ],
        out_specs=pl.BlockSpec((tm, tn), lambda j, i: (i, j)),
        out_shape=jax.ShapeDtypeStruct((T, D), F32),
        scratch_shapes=[pltpu.VMEM((d_ff, tn), BF16)],
        compiler_params=_cparams(("arbitrary", "arbitrary")),
        name="ffn_down",
    )(act, w_down, h1)


def _ple_final_kernel(h2_ref, p_ref, gp_ref, gf_ref, wg_hbm, wp_hbm, out_ref, wg_bf, wp_bf, stage, sem):
    @pl.when(pl.program_id(0) == 0)
    def _():
        _stage_weights_bf16([(wg_hbm, wg_bf), (wp_hbm, wp_bf)], stage, sem)

    h2 = h2_ref[...]
    gate = _sigmoid(_dot(_rms(h2, gp_ref[...]).astype(BF16), wg_bf[...]))
    pp = _dot(p_ref[...].astype(BF16), wp_bf[...])
    out_ref[...] = _rms(h2 + gate * pp, gf_ref[...])


def _ple_final(h2, p2, w_pg, w_pp, g_ple, g_final, *, tm):
    T, D = h2.shape
    P = p2.shape[1]
    hbm = pl.BlockSpec(memory_space=pl.ANY)
    return pl.pallas_call(
        _ple_final_kernel,
        grid=(T // tm,),
        in_specs=[
            pl.BlockSpec((tm, D), lambda i: (i, 0)),
            pl.BlockSpec((tm, P), lambda i: (i, 0)),
            pl.BlockSpec((1, D), lambda i: (0, 0)),
            pl.BlockSpec((1, D), lambda i: (0, 0)),
            hbm, hbm,
        ],
        out_specs=pl.BlockSpec((tm, D), lambda i: (i, 0)),
        out_shape=jax.ShapeDtypeStruct((T, D), F32),
        scratch_shapes=[
            pltpu.VMEM((D, D), BF16),
            pltpu.VMEM((P, D), BF16),
            pltpu.VMEM((2, STAGE_ROWS, D), F32),
            pltpu.SemaphoreType.DMA((2,)),
        ],
        compiler_params=_cparams(("arbitrary",)),
        name="ple_final",
    )(h2, p2, g_ple, g_final, w_pg, w_pp)


TAB_Q = 0
TAB_QI = 2
TAB_KI = 4
TAB_KR = 7
N_TABLES = 10


def _lane_tables(S):
    assert QK_ROPE == IDX_ROPE == 32
    scale = (QK_NOPE + QK_ROPE) ** -0.5 * math.log2(math.e)
    a = np.zeros((N_TABLES, LANES), np.float32)
    bc = np.zeros_like(a)
    bs = np.zeros_like(a)
    a[TAB_Q, 0:QK_NOPE] = scale
    bc[TAB_Q, 64:96] = scale
    bs[TAB_Q + 1, 64:80] = -scale
    bs[TAB_Q + 1, 80:96] = scale
    for o in (0, IDX_DIM):
        bc[TAB_QI, o:o + 32] = 1.0
        a[TAB_QI, o + 32:o + 64] = 1.0
        bs[TAB_QI + 1, o:o + 32] = 1.0
    bc[TAB_KI, 0:32] = 1.0
    a[TAB_KI, 32:64] = 1.0
    bs[TAB_KI + 1, 0:16] = -1.0
    bs[TAB_KI + 2, 16:32] = 1.0
    bc[TAB_KR, 0:32] = 1.0
    bs[TAB_KR + 1, 0:16] = -1.0
    bs[TAB_KR + 2, 16:32] = 1.0
    inv = ROPE_THETA ** (-jnp.arange(0, QK_ROPE, 2, dtype=F32) / QK_ROPE)
    ang = jnp.arange(S, dtype=F32)[:, None] * jnp.tile(inv, LANES // inv.shape[0])[None, :]
    return a[:, None, :] + bc[:, None, :] * jnp.cos(ang)[None] + bs[:, None, :] * jnp.sin(ang)[None]


def kernel(x, p, g_mix, w_in, g_cq, g_ckv, w_uq, w_uk, w_uv, w_iq, w_a_proj, g_sgu, w_spatial,
           b_spatial, w_b_proj, w_o, g_ffn, w_gu, w_down, g_ple, w_ple_gate, w_ple_proj, g_final):
    B, S, D = x.shape
    T = B * S
    depth = w_in.shape[0]
    d_ff = w_down.shape[1]
    topk = min(TOPK_MAX, S // 4)
    TQ = 256
    KCH = min(512, S)
    tm = min(512, S)
    tables = _lane_tables(S)

    h = x.reshape(T, D)
    for i in range(depth):
        wq3 = w_uq[i].reshape(Q_LORA, N_HEADS, QK_NOPE + QK_ROPE)
        x1, x2 = wq3[..., QK_NOPE:QK_NOPE + QK_ROPE // 2], wq3[..., QK_NOPE + QK_ROPE // 2:]
        wq = jnp.concatenate([wq3, x2, x1], axis=-1).reshape(Q_LORA, N_HEADS * LANES).astype(BF16)
        wk = jnp.pad(w_uk[i], ((0, 0), (0, 0), (0, LANES - QK_NOPE))).reshape(KV_LORA, N_HEADS * LANES).astype(BF16)
        wv = w_uv[i].reshape(KV_LORA, N_HEADS * V_HEAD).astype(BF16)
        wi3 = w_iq[i].reshape(Q_LORA, IDX_HEADS, IDX_DIM)
        wiq = w_iq[i].astype(BF16)
        wiqr = jnp.concatenate([-wi3[..., IDX_ROPE // 2:IDX_ROPE], wi3[..., :IDX_ROPE // 2],
                                jnp.zeros_like(wi3[..., IDX_ROPE:])], axis=-1).reshape(w_iq[i].shape).astype(BF16)

        w_t = jnp.transpose(w_in[i])
        n, q, k, v, qi, ki_lo, ki_hi, wt = _a_proj(h, g_mix[i][None], w_t, g_cq[i][None], g_ckv[i][None],
                                                    wq, wiq, wiqr, wk, wv, tables, B=B, S=S, tm=min(256, S))
        zg = _uv_gate_proj(n, w_t, tm=min(1024, S), tn=1024)
        bias = _dsa_index(ki_lo, ki_hi, qi, wt, B=B, S=S, TQ=TQ, KCH=KCH, topk=topk)
        o = _dsa_attn(q, k, v, bias, B=B, S=S, TQ=TQ)
        h1, n2 = _branch_out(o, zg, h, g_sgu[i][None], w_spatial[i], jnp.transpose(b_spatial[i]),
                             g_ffn[i][None], w_a_proj[i], w_b_proj[i], w_o[i], B=B, S=S, tm=min(256, S))
        act = _ffn_up(n2, w_gu[i], d_ff=d_ff, tm=min(1024, S), tn=512)
        h2 = _ffn_down(act, w_down[i], h1, tm=tm, tn=512)
        assert depth == 1
        h = _ple_final(h2, p[i].reshape(T, -1), w_ple_gate[i], w_ple_proj[i], g_ple[i][None],
                       g_final[None], tm=tm)
    return h.reshape(B, S, D)
```

```python
import functools
import math

import numpy as np
import jax
import jax.numpy as jnp
from jax import lax
from jax.experimental import pallas as pl
from jax.experimental.pallas import tpu as pltpu

F32 = jnp.float32
BF16 = jnp.bfloat16

N_HEADS = 16
QK_NOPE = 64
QK_ROPE = 32
V_HEAD = 64
Q_LORA = 512
KV_LORA = 256
IDX_HEADS = 16
IDX_DIM = 64
IDX_ROPE = 32
TOPK_MAX = 256
SGU_CHUNK = 128
SGU_GROUPS = 8
SGU_GROUP_DIM = 128
SGU_WIDTH = SGU_GROUPS * SGU_GROUP_DIM
ROPE_THETA = 10000.0
EPS = 1e-6

LANES = 128
IN_SMALL = Q_LORA + KV_LORA + QK_ROPE + IDX_DIM + IDX_HEADS
SMALL_COLS = 1024
COL_CKV = Q_LORA
COL_MISC = Q_LORA + KV_LORA
MISC_KIDX = QK_ROPE
MISC_WIDX = QK_ROPE + IDX_DIM
VMEM_LIMIT_BYTES = 56 * 1024 * 1024
NEG_BIAS = -1e30
F32_MAX = 3.4028234663852886e38
STEPS_PER_CHECK = 4
COUNT_CHAINS = 4


def _cparams(semantics):
    return pltpu.CompilerParams(dimension_semantics=semantics, vmem_limit_bytes=VMEM_LIMIT_BYTES)


def _rms(x, g):
    return x * lax.rsqrt(jnp.mean(x * x, axis=-1, keepdims=True) + EPS) * g


def _dot(a, b):
    return jnp.dot(a, b, preferred_element_type=F32)


def _dot_nt(a, b):
    return lax.dot_general(a, b, (((1,), (1,)), ((), ())), preferred_element_type=F32)


def _gelu_exact(x):
    return 0.5 * x * (1.0 + lax.erf(x * (1.0 / math.sqrt(2.0))))


def _sigmoid(x):
    return 0.5 * jnp.tanh(0.5 * x) + 0.5


def _uv_gate_kernel(n_ref, w_ref, out_ref, w_bf, *, n_z):
    j = pl.program_id(0)

    @pl.when(pl.program_id(1) == 0)
    def _():
        w_bf[...] = w_ref[...].astype(BF16)

    def proj():
        return _dot_nt(n_ref[...], w_bf[...])

    @pl.when(j < n_z)
    def _():
        out_ref[...] = _gelu_exact(proj()).astype(BF16)

    @pl.when(j >= n_z)
    def _():
        out_ref[...] = _sigmoid(proj()).astype(BF16)


def _uv_gate_proj(n, w_t, *, tm, tn):
    T, D = n.shape
    n_z = (2 * SGU_WIDTH) // tn
    n_out = 2 * SGU_WIDTH + 2 * D
    assert w_t.shape == (IN_SMALL + n_out, D) and IN_SMALL % 8 == 0 and tn % 8 == 0
    kern = functools.partial(_uv_gate_kernel, n_z=n_z)
    return pl.pallas_call(
        kern,
        grid=(n_out // tn, T // tm),
        in_specs=[
            pl.BlockSpec((tm, D), lambda j, i: (i, 0)),
            pl.BlockSpec((pl.Element(tn), pl.Element(D)), lambda j, i: ((IN_SMALL // 8 + j * (tn // 8)) * 8, 0)),
        ],
        out_specs=pl.BlockSpec((tm, tn), lambda j, i: (i, j)),
        out_shape=jax.ShapeDtypeStruct((T, n_out), BF16),
        scratch_shapes=[pltpu.VMEM((tn, D), BF16)],
        compiler_params=_cparams(("arbitrary", "arbitrary")),
        name="uv_gate_proj",
    )(n, w_t)


def _rope_block(x, cos, sin_a, sin_b):
    return x * cos + pltpu.roll(x, LANES - 16, 1) * sin_a + pltpu.roll(x, 16, 1) * sin_b


def _a_proj_kernel(x_ref, gmix_ref, ws_ref, gcq_ref, gckv_ref, wq_ref, wiq_ref, wiqr_ref, wk_ref, wv_ref, tab_ref,
                   n_ref, q_ref, k_ref, v_ref, qi_ref, kilo_ref, kihi_ref, wt_ref, ws_bf):
    @pl.when(pl.program_id(0) == 0)
    def _():
        ws_bf[...] = ws_ref[...].astype(BF16)

    n = _rms(x_ref[...], gmix_ref[...]).astype(BF16)
    n_ref[...] = n
    small = _dot_nt(n, ws_bf[...])
    c_q = _rms(small[:, 0:Q_LORA], gcq_ref[...]).astype(BF16)
    c_kv = _rms(small[:, COL_CKV:COL_CKV + KV_LORA], gckv_ref[...]).astype(BF16)

    q = _dot(c_q, wq_ref[...])
    cq, sq = tab_ref[TAB_Q], tab_ref[TAB_Q + 1]
    for h in range(N_HEADS):
        blk = q[:, h * LANES:(h + 1) * LANES]
        q_ref[0, h] = (blk * cq + pltpu.roll(blk, LANES - QK_ROPE, 1) * sq).astype(BF16)

    ci, si = tab_ref[TAB_QI], tab_ref[TAB_QI + 1]
    qi = _dot(c_q, wiq_ref[...])
    qir = _dot(c_q, wiqr_ref[...])
    for hp in range(IDX_HEADS // 2):
        cols = slice(hp * LANES, (hp + 1) * LANES)
        qi_ref[:, cols] = (qi[:, cols] * ci + qir[:, cols] * si).astype(BF16)

    misc = small[:, COL_MISC:COL_MISC + LANES]
    ki_lo = _rope_block(pltpu.roll(misc, LANES - MISC_KIDX, 1),
                        tab_ref[TAB_KI], tab_ref[TAB_KI + 1], tab_ref[TAB_KI + 2])
    kilo_ref[...] = ki_lo.astype(BF16)
    kihi_ref[...] = pltpu.roll(ki_lo, IDX_DIM, 1).astype(BF16)

    k_rope = pltpu.roll(_rope_block(misc, tab_ref[TAB_KR], tab_ref[TAB_KR + 1], tab_ref[TAB_KR + 2]),
                        QK_NOPE, 1)
    k_nope = _dot(c_kv, wk_ref[...])
    for h in range(N_HEADS):
        k_ref[0, h] = (k_nope[:, h * LANES:(h + 1) * LANES] + k_rope).astype(BF16)

    v = _dot(c_kv, wv_ref[...])
    for p in range(N_HEADS // 2):
        v_ref[0, p] = v[:, p * LANES:(p + 1) * LANES].astype(BF16)

    w_scale = IDX_HEADS ** -0.5 * IDX_DIM ** -0.5
    wt_ref[...] = misc.T[MISC_WIDX:MISC_WIDX + IDX_HEADS, :] * w_scale


def _a_proj(x2, g_mix, w_t, g_cq, g_ckv, wq, wiq, wiqr, wk, wv, tables, *, B, S, tm):
    T, D = x2.shape
    nt = S // tm
    const2 = lambda i: (0, 0)
    head_spec = lambda nh: pl.BlockSpec((1, nh, tm, LANES), lambda i: (i // nt, 0, i % nt, 0))
    return pl.pallas_call(
        _a_proj_kernel,
        grid=(T // tm,),
        in_specs=[
            pl.BlockSpec((tm, D), lambda i: (i, 0)),
            pl.BlockSpec((1, D), const2),
            pl.BlockSpec((SMALL_COLS, D), const2, pipeline_mode=pl.Buffered(1)),
            pl.BlockSpec((1, Q_LORA), const2),
            pl.BlockSpec((1, KV_LORA), const2),
            pl.BlockSpec(wq.shape, const2),
            pl.BlockSpec(wiq.shape, const2),
            pl.BlockSpec(wiqr.shape, const2),
            pl.BlockSpec(wk.shape, const2),
            pl.BlockSpec(wv.shape, const2),
            pl.BlockSpec((N_TABLES, tm, LANES), lambda i: (0, i % nt, 0)),
        ],
        out_specs=[
            pl.BlockSpec((tm, D), lambda i: (i, 0)),
            head_spec(N_HEADS),
            head_spec(N_HEADS),
            head_spec(N_HEADS // 2),
            pl.BlockSpec((tm, IDX_HEADS * IDX_DIM), lambda i: (i, 0)),
            pl.BlockSpec((tm, LANES), lambda i: (i, 0)),
            pl.BlockSpec((tm, LANES), lambda i: (i, 0)),
            pl.BlockSpec((IDX_HEADS, tm), lambda i: (0, i)),
        ],
        out_shape=[
            jax.ShapeDtypeStruct((T, D), BF16),
            jax.ShapeDtypeStruct((B, N_HEADS, S, LANES), BF16),
            jax.ShapeDtypeStruct((B, N_HEADS, S, LANES), BF16),
            jax.ShapeDtypeStruct((B, N_HEADS // 2, S, LANES), BF16),
            jax.ShapeDtypeStruct((T, IDX_HEADS * IDX_DIM), BF16),
            jax.ShapeDtypeStruct((T, LANES), BF16),
            jax.ShapeDtypeStruct((T, LANES), BF16),
            jax.ShapeDtypeStruct((IDX_HEADS, T), F32),
        ],
        scratch_shapes=[pltpu.VMEM((SMALL_COLS, D), BF16)],
        compiler_params=_cparams(("arbitrary",)),
        name="norm_a_proj",
    )(x2, g_mix, w_t, g_cq, g_ckv, wq, wiq, wiqr, wk, wv, tables)


def _dsa_index_kernel(kilo_ref, kihi_ref, qi_ref, wt_ref, bias_ref, isc_ref, mm_ref, js_ref,
                      *, S, TQ, KCH, topk):
    j = pl.program_id(1)
    q0 = j * TQ
    nkeys = q0 + TQ
    qidx = q0 + lax.broadcasted_iota(jnp.int32, (1, TQ), 1)
    kf = float(topk)

    mm_ref[0:8, :] = jnp.full((8, TQ), jnp.inf, F32)
    mm_ref[8:16, :] = jnp.full((8, TQ), -jnp.inf, F32)
    for c in range(S // KCH):
        @pl.when(c * KCH < nkeys)
        def _(c=c):
            klo = kilo_ref[c * KCH:(c + 1) * KCH, :]
            khi = kihi_ref[c * KCH:(c + 1) * KCH, :]
            acc = jnp.zeros((KCH, TQ), F32)
            for hp in range(IDX_HEADS // 2):
                qp = qi_ref[:, hp * LANES:(hp + 1) * LANES]
                s0 = _dot_nt(klo, qp)
                s1 = _dot_nt(khi, qp)
                acc = acc + jnp.maximum(s0, 0.0) * wt_ref[2 * hp:2 * hp + 1, :]
                acc = acc + jnp.maximum(s1, 0.0) * wt_ref[2 * hp + 1:2 * hp + 2, :]
            kidx = c * KCH + lax.broadcasted_iota(jnp.int32, (KCH, TQ), 0)
            causal = kidx <= qidx
            isc_ref[c * KCH:(c + 1) * KCH, :] = jnp.where(causal, acc, -jnp.inf)
            lo_part = jnp.where(causal, acc, jnp.inf).reshape(KCH // 8, 8, TQ).min(axis=0)
            hi_part = jnp.where(causal, acc, -jnp.inf).reshape(KCH // 8, 8, TQ).max(axis=0)
            mm_ref[0:8, :] = jnp.minimum(mm_ref[0:8, :], lo_part)
            mm_ref[8:16, :] = jnp.maximum(mm_ref[8:16, :], hi_part)

    def select(nch):
        def count(pred):
            acc = jnp.zeros((COUNT_CHAINS, 8, TQ), F32)
            for c in range(nch):
                ones = jnp.where(pred(isc_ref[c * TQ:(c + 1) * TQ, :], c * TQ), 1.0, 0.0)
                acc = acc + ones.reshape(COUNT_CHAINS, TQ // (8 * COUNT_CHAINS), 8, TQ).sum(axis=1)
            return acc.sum(axis=0).sum(axis=0, keepdims=True)

        row_min = mm_ref[0:8, :].min(axis=0, keepdims=True)
        row_max = mm_ref[8:16, :].max(axis=0, keepdims=True)
        full = (qidx + 1) <= topk
        c_max = count(lambda blk, k0: blk >= row_max)
        exact0 = c_max == kf
        tie0 = c_max > kf
        settled0 = full | exact0 | tie0
        lo0 = jnp.where(full, -F32_MAX, jnp.where(settled0, row_max, row_min))
        hi0 = jnp.where(full, -F32_MAX, jnp.where(tie0, jnp.inf, row_max))
        act0 = jnp.where(settled0, 0.0, 1.0)

        def step(lo, hi, act):
            mid = lo * 0.5 + hi * 0.5
            inside = (mid > lo) & (mid < hi)
            cnt = count(lambda blk, k0: blk >= mid)
            upd = (act > 0.0) & inside
            found = upd & (cnt == kf)
            lo2 = jnp.where(upd & (cnt >= kf), mid, lo)
            hi2 = jnp.where(upd & (cnt <= kf), mid, hi)
            return lo2, hi2, jnp.where(upd & jnp.logical_not(found), 1.0, 0.0)

        def any_active(act):
            return (jnp.max(act) > 0.0).astype(jnp.int32)

        def body(st):
            lo, hi, act, _ = st
            for _ in range(STEPS_PER_CHECK):
                lo, hi, act = step(lo, hi, act)
            return lo, hi, act, any_active(act)

        lo, hi, _, _ = lax.while_loop(lambda st: st[3] > 0, body, (lo0, hi0, act0, any_active(act0)))

        tie = lo < hi
        js_ref[...] = jnp.full((8, TQ), -1.0, F32)

        @pl.when(jnp.max(jnp.where(tie, 1.0, 0.0)) > 0.0)
        def _():
            need = kf - count(lambda blk, k0: blk >= hi)

            def kpos(k0):
                return (k0 + lax.broadcasted_iota(jnp.int32, (TQ, TQ), 0)).astype(F32)

            def tie_step(_, st):
                ilo, ihi = st
                imid = jnp.floor((ilo + ihi) * 0.5)
                cnt = count(lambda blk, k0: (blk >= lo) & (blk < hi) & (kpos(k0) <= imid))
                ge = cnt >= need
                return jnp.where(ge, ilo, imid), jnp.where(ge, imid, ihi)

            nsteps = int(math.ceil(math.log2(S))) + 1
            _, ihi = lax.fori_loop(0, nsteps, tie_step,
                                   (jnp.full((1, TQ), -1.0, F32), jnp.full((1, TQ), S - 1.0, F32)))
            js_ref[0:1, :] = jnp.where(tie, ihi, -1.0)

        jstar = js_ref[0:1, :]

        for c in range(S // TQ):
            if c < nch:
                blk = isc_ref[c * TQ:(c + 1) * TQ, :]
                kpos_c = (c * TQ + lax.broadcasted_iota(jnp.int32, (TQ, TQ), 0)).astype(F32)
                sel = (blk >= hi) | ((blk >= lo) & (kpos_c <= jstar))
                bias_ref[:, c * TQ:(c + 1) * TQ] = jnp.where(sel, 0.0, NEG_BIAS).T
            else:
                bias_ref[:, c * TQ:(c + 1) * TQ] = jnp.full((TQ, TQ), NEG_BIAS, F32)

    for jj in range(S // TQ):
        @pl.when(j == jj)
        def _(jj=jj):
            select(jj + 1)


def _dsa_index(ki_lo, ki_hi, qi, wt, *, B, S, TQ, KCH, topk):
    T = B * S
    nq = S // TQ
    kern = functools.partial(_dsa_index_kernel, S=S, TQ=TQ, KCH=KCH, topk=topk)
    return pl.pallas_call(
        kern,
        grid=(B, nq),
        in_specs=[
            pl.BlockSpec((S, LANES), lambda b, j: (b, 0)),
            pl.BlockSpec((S, LANES), lambda b, j: (b, 0)),
            pl.BlockSpec((TQ, IDX_HEADS * IDX_DIM), lambda b, j: (b * nq + j, 0)),
            pl.BlockSpec((IDX_HEADS, TQ), lambda b, j: (0, b * nq + j)),
        ],
        out_specs=pl.BlockSpec((TQ, S), lambda b, j: (b * nq + j, 0)),
        out_shape=jax.ShapeDtypeStruct((T, S), F32),
        scratch_shapes=[
            pltpu.VMEM((S, TQ), F32),
            pltpu.VMEM((16, TQ), F32),
            pltpu.VMEM((8, TQ), F32),
        ],
        compiler_params=_cparams(("parallel", "arbitrary")),
        name="dsa_index",
    )(ki_lo, ki_hi, qi, wt)


def _dsa_attn_kernel(q_ref, k_ref, v_ref, bias_ref, o_ref, *, S, TQ):
    j = pl.program_id(1)
    lane = lax.broadcasted_iota(jnp.int32, (TQ, LANES), 1)

    def variant(nk):
        ones = jnp.ones((nk, LANES), BF16)

        def pair(p, carry):
            vp = jnp.concatenate([v_ref[0, p, 0:nk, :], ones], axis=1)
            outs = []
            for e in range(2):
                h = 2 * p + e
                s = _dot_nt(q_ref[0, h], k_ref[0, h, 0:nk, :]) + bias_ref[:, 0:nk]
                m = s.max(axis=1, keepdims=True)
                pv = _dot(jnp.exp2(s - m).astype(BF16), vp)
                outs.append(pv[:, 0:LANES] * (1.0 / pv[:, LANES:2 * LANES]))
            o_ref[0, p] = jnp.where(lane < V_HEAD, outs[0], outs[1]).astype(BF16)
            return carry
        lax.fori_loop(0, N_HEADS // 2, pair, 0, unroll=4)

    for jj in range(S // TQ):
        @pl.when(j == jj)
        def _(jj=jj):
            variant((jj + 1) * TQ)


def _dsa_attn(q, k, v, bias, *, B, S, TQ):
    nq = S // TQ
    kern = functools.partial(_dsa_attn_kernel, S=S, TQ=TQ)
    return pl.pallas_call(
        kern,
        grid=(B, nq),
        in_specs=[
            pl.BlockSpec((1, N_HEADS, TQ, LANES), lambda b, j: (b, 0, j, 0)),
            pl.BlockSpec((1, N_HEADS, S, LANES), lambda b, j: (b, 0, 0, 0)),
            pl.BlockSpec((1, N_HEADS // 2, S, LANES), lambda b, j: (b, 0, 0, 0)),
            pl.BlockSpec((TQ, S), lambda b, j: (b * nq + j, 0)),
        ],
        out_specs=pl.BlockSpec((1, N_HEADS // 2, TQ, LANES), lambda b, j: (b, 0, j, 0)),
        out_shape=jax.ShapeDtypeStruct((B, N_HEADS // 2, S, LANES), BF16),
        compiler_params=_cparams(("parallel", "arbitrary")),
        name="dsa_attn",
    )(q, k, v, bias)


STAGE_ROWS = 512


def _stage_weights_bf16(pairs, stage, sem):
    chunks = []
    for src, dst in pairs:
        rows = min(src.shape[0], STAGE_ROWS)
        assert src.shape[0] % rows == 0 and src.shape == dst.shape
        chunks += [(src, dst, r, rows) for r in range(0, src.shape[0], rows)]

    def copy(k):
        src, _, r, rows = chunks[k]
        return pltpu.make_async_copy(src.at[r:r + rows, :], stage.at[k % 2, 0:rows, :], sem.at[k % 2])

    copy(0).start()
    for k, (_, dst, r, rows) in enumerate(chunks):
        if k + 1 < len(chunks):
            copy(k + 1).start()
        copy(k).wait()
        dst[r:r + rows, :] = stage[k % 2, 0:rows, :].astype(BF16)


def _branch_out_kernel(o_ref, z_ref, ga_ref, gb_ref, x_ref, gs_ref, ws_ref, bt_ref, gf_ref,
                       wa_hbm, wb_hbm, wo_hbm, h_ref, n_ref,
                       wa_bf, wb_bf, wo_bf, y_scr, stage, sem, *, tm):
    @pl.when(pl.program_id(0) == 0)
    def _():
        _stage_weights_bf16([(wa_hbm, wa_bf), (wb_hbm, wb_bf), (wo_hbm, wo_bf)], stage, sem)

    row = lax.broadcasted_iota(jnp.int32, (SGU_CHUNK, SGU_CHUNK), 0)
    col = lax.broadcasted_iota(jnp.int32, (SGU_CHUNK, SGU_CHUNK), 1)
    tril = col <= row
    w = [jnp.where(tril, ws_ref[g], 0.0).astype(BF16) for g in range(SGU_GROUPS)]
    for cc in range(tm // SGU_CHUNK):
        rows = slice(cc * SGU_CHUNK, (cc + 1) * SGU_CHUNK)
        vn = _rms(z_ref[rows, SGU_WIDTH:2 * SGU_WIDTH].astype(F32), gs_ref[...]).astype(BF16)
        for g in range(SGU_GROUPS):
            cols = slice(g * SGU_GROUP_DIM, (g + 1) * SGU_GROUP_DIM)
            mixed = _dot(w[g], vn[:, cols]) + bt_ref[:, g:g + 1]
            y_scr[rows, cols] = (z_ref[rows, cols].astype(F32) * mixed).astype(BF16)

    o_a = jnp.concatenate([o_ref[0, p] for p in range(N_HEADS // 2)], axis=1)
    ya = _dot(o_a, wa_bf[...])
    yb = _dot(y_scr[...], wb_bf[...])
    merged = (ga_ref[...].astype(F32) * ya + gb_ref[...].astype(F32) * yb).astype(BF16)
    h = x_ref[...] + _dot(merged, wo_bf[...])
    h_ref[...] = h
    n_ref[...] = _rms(h, gf_ref[...]).astype(BF16)


def _branch_out(o, zg, x2, g_sgu, w_spatial, b_t, g_ffn, wa, wb, wo, *, B, S, tm):
    T, D = x2.shape
    nt = S // tm
    assert 2 * SGU_WIDTH == D and wa.shape == wb.shape == (SGU_WIDTH, D) and wo.shape == (D, D)
    row = lambda c: pl.BlockSpec((tm, D), lambda i: (i, c))
    const = lambda shape: pl.BlockSpec(shape, lambda i: (0,) * len(shape))
    hbm = pl.BlockSpec(memory_space=pl.ANY)
    kern = functools.partial(_branch_out_kernel, tm=tm)
    return pl.pallas_call(
        kern,
        grid=(T // tm,),
        in_specs=[
            pl.BlockSpec((1, N_HEADS // 2, tm, LANES), lambda i: (i // nt, 0, i % nt, 0)),
            row(0), row(1), row(2),
            row(0),
            const((1, SGU_WIDTH)),
            const((SGU_GROUPS, SGU_CHUNK, SGU_CHUNK)),
            const((SGU_CHUNK, SGU_GROUPS)),
            const((1, D)),
            hbm, hbm, hbm,
        ],
        out_specs=[pl.BlockSpec((tm, D), lambda i: (i, 0)), pl.BlockSpec((tm, D), lambda i: (i, 0))],
        out_shape=[jax.ShapeDtypeStruct((T, D), F32), jax.ShapeDtypeStruct((T, D), BF16)],
        scratch_shapes=[
            pltpu.VMEM((SGU_WIDTH, D), BF16),
            pltpu.VMEM((SGU_WIDTH, D), BF16),
            pltpu.VMEM((D, D), BF16),
            pltpu.VMEM((tm, SGU_WIDTH), BF16),
            pltpu.VMEM((2, STAGE_ROWS, D), F32),
            pltpu.SemaphoreType.DMA((2,)),
        ],
        compiler_params=_cparams(("arbitrary",)),
        name="branch_out",
    )(o, zg, zg, zg, x2, g_sgu, w_spatial, b_t, g_ffn, wa, wb, wo)


def _ffn_up_kernel(n_ref, wg_ref, wu_ref, wd_ref, a_ref, wd_bf_ref, wg_bf, wu_bf):
    @pl.when(pl.program_id(1) == 0)
    def _():
        wg_bf[...] = wg_ref[...].astype(BF16)
        wu_bf[...] = wu_ref[...].astype(BF16)

    wd_bf_ref[...] = wd_ref[...].astype(BF16)
    n = n_ref[...]
    g = _dot(n, wg_bf[...])
    u = _dot(n, wu_bf[...])
    a_ref[...] = (g * _sigmoid(g) * u).astype(BF16)


def _ffn_up(n2, w_gu, w_down, *, tm, tn):
    T, D = n2.shape
    d_ff = w_down.shape[0]
    nn, nm = d_ff // tn, T // tm
    slab = d_ff // (nn * nm)
    assert slab * nn * nm == d_ff and slab % 16 == 0
    return pl.pallas_call(
        _ffn_up_kernel,
        grid=(nn, nm),
        in_specs=[
            pl.BlockSpec((tm, D), lambda j, i: (i, 0)),
            pl.BlockSpec((D, tn), lambda j, i: (0, j)),
            pl.BlockSpec((D, tn), lambda j, i: (0, nn + j)),
            pl.BlockSpec((slab, D), lambda j, i: (j * nm + i, 0)),
        ],
        out_specs=[
            pl.BlockSpec((tm, tn), lambda j, i: (i, j)),
            pl.BlockSpec((slab, D), lambda j, i: (j * nm + i, 0)),
        ],
        out_shape=[jax.ShapeDtypeStruct((T, d_ff), BF16), jax.ShapeDtypeStruct((d_ff, D), BF16)],
        scratch_shapes=[pltpu.VMEM((D, tn), BF16), pltpu.VMEM((D, tn), BF16)],
        compiler_params=_cparams(("arbitrary", "arbitrary")),
        name="ffn_up",
    )(n2, w_gu, w_gu, w_down)


def _ffn_down_kernel(a_ref, w_ref, h1_ref, h2_ref):
    h2_ref[...] = h1_ref[...] + _dot(a_ref[...], w_ref[...])


def _ffn_down(act, w_down_bf, h1, *, tm, tn):
    T, D = h1.shape
    d_ff = act.shape[1]
    return pl.pallas_call(
        _ffn_down_kernel,
        grid=(D // tn, T // tm),
        in_specs=[
            pl.BlockSpec((tm, d_ff), lambda j, i: (i, 0)),
            pl.BlockSpec((d_ff, tn), lambda j, i: (0, j)),
            pl.BlockSpec((tm, tn), lambda j, i: (i, j)),
        ],
        out_specs=pl.BlockSpec((tm, tn), lambda j, i: (i, j)),
        out_shape=jax.ShapeDtypeStruct((T, D), F32),
        compiler_params=_cparams(("arbitrary", "arbitrary")),
        name="ffn_down",
    )(act, w_down_bf, h1)


def _ple_final_kernel(h2_ref, p_ref, gp_ref, gf_ref, wg_hbm, wp_hbm, out_ref, wg_bf, wp_bf, stage, sem):
    @pl.when(pl.program_id(0) == 0)
    def _():
        _stage_weights_bf16([(wg_hbm, wg_bf), (wp_hbm, wp_bf)], stage, sem)

    h2 = h2_ref[...]
    gate = _sigmoid(_dot(_rms(h2, gp_ref[...]).astype(BF16), wg_bf[...]))
    pp = _dot(p_ref[...].astype(BF16), wp_bf[...])
    out_ref[...] = _rms(h2 + gate * pp, gf_ref[...])


def _ple_final(h2, p2, w_pg, w_pp, g_ple, g_final, *, tm):
    T, D = h2.shape
    P = p2.shape[1]
    hbm = pl.BlockSpec(memory_space=pl.ANY)
    return pl.pallas_call(
        _ple_final_kernel,
        grid=(T // tm,),
        in_specs=[
            pl.BlockSpec((tm, D), lambda i: (i, 0)),
            pl.BlockSpec((tm, P), lambda i: (i, 0)),
            pl.BlockSpec((1, D), lambda i: (0, 0)),
            pl.BlockSpec((1, D), lambda i: (0, 0)),
            hbm, hbm,
        ],
        out_specs=pl.BlockSpec((tm, D), lambda i: (i, 0)),
        out_shape=jax.ShapeDtypeStruct((T, D), F32),
        scratch_shapes=[
            pltpu.VMEM((D, D), BF16),
            pltpu.VMEM((P, D), BF16),
            pltpu.VMEM((2, STAGE_ROWS, D), F32),
            pltpu.SemaphoreType.DMA((2,)),
        ],
        compiler_params=_cparams(("arbitrary",)),
        name="ple_final",
    )(h2, p2, g_ple, g_final, w_pg, w_pp)


TAB_Q = 0
TAB_QI = 2
TAB_KI = 4
TAB_KR = 7
N_TABLES = 10


def _lane_tables(S):
    assert QK_ROPE == IDX_ROPE == 32
    scale = (QK_NOPE + QK_ROPE) ** -0.5 * math.log2(math.e)
    a = np.zeros((N_TABLES, LANES), np.float32)
    bc = np.zeros_like(a)
    bs = np.zeros_like(a)
    a[TAB_Q, 0:QK_NOPE] = scale
    bc[TAB_Q, 64:96] = scale
    bs[TAB_Q + 1, 64:80] = -scale
    bs[TAB_Q + 1, 80:96] = scale
    for o in (0, IDX_DIM):
        bc[TAB_QI, o:o + 32] = 1.0
        a[TAB_QI, o + 32:o + 64] = 1.0
        bs[TAB_QI + 1, o:o + 32] = 1.0
    bc[TAB_KI, 0:32] = 1.0
    a[TAB_KI, 32:64] = 1.0
    bs[TAB_KI + 1, 0:16] = -1.0
    bs[TAB_KI + 2, 16:32] = 1.0
    bc[TAB_KR, 0:32] = 1.0
    bs[TAB_KR + 1, 0:16] = -1.0
    bs[TAB_KR + 2, 16:32] = 1.0
    inv = ROPE_THETA ** (-jnp.arange(0, QK_ROPE, 2, dtype=F32) / QK_ROPE)
    ang = jnp.arange(S, dtype=F32)[:, None] * jnp.tile(inv, LANES // inv.shape[0])[None, :]
    return a[:, None, :] + bc[:, None, :] * jnp.cos(ang)[None] + bs[:, None, :] * jnp.sin(ang)[None]


def kernel(x, p, g_mix, w_in, g_cq, g_ckv, w_uq, w_uk, w_uv, w_iq, w_a_proj, g_sgu, w_spatial,
           b_spatial, w_b_proj, w_o, g_ffn, w_gu, w_down, g_ple, w_ple_gate, w_ple_proj, g_final):
    B, S, D = x.shape
    T = B * S
    depth = w_in.shape[0]
    topk = min(TOPK_MAX, S // 4)
    TQ = 256
    KCH = min(512, S)
    tm = min(512, S)
    tables = _lane_tables(S)

    h = x.reshape(T, D)
    for i in range(depth):
        wq3 = w_uq[i].reshape(Q_LORA, N_HEADS, QK_NOPE + QK_ROPE)
        x1, x2 = wq3[..., QK_NOPE:QK_NOPE + QK_ROPE // 2], wq3[..., QK_NOPE + QK_ROPE // 2:]
        wq = jnp.concatenate([wq3, x2, x1], axis=-1).reshape(Q_LORA, N_HEADS * LANES).astype(BF16)
        wk = jnp.pad(w_uk[i], ((0, 0), (0, 0), (0, LANES - QK_NOPE))).reshape(KV_LORA, N_HEADS * LANES).astype(BF16)
        wv = w_uv[i].reshape(KV_LORA, N_HEADS * V_HEAD).astype(BF16)
        wi3 = w_iq[i].reshape(Q_LORA, IDX_HEADS, IDX_DIM)
        wiq = w_iq[i].astype(BF16)
        wiqr = jnp.concatenate([-wi3[..., IDX_ROPE // 2:IDX_ROPE], wi3[..., :IDX_ROPE // 2],
                                jnp.zeros_like(wi3[..., IDX_ROPE:])], axis=-1).reshape(w_iq[i].shape).astype(BF16)

        w_t = jnp.transpose(w_in[i])
        n, q, k, v, qi, ki_lo, ki_hi, wt = _a_proj(h, g_mix[i][None], w_t, g_cq[i][None], g_ckv[i][None],
                                                    wq, wiq, wiqr, wk, wv, tables, B=B, S=S, tm=min(256, S))
        zg = _uv_gate_proj(n, w_t, tm=min(1024, S), tn=1024)
        bias = _dsa_index(ki_lo, ki_hi, qi, wt, B=B, S=S, TQ=TQ, KCH=KCH, topk=topk)
        o = _dsa_attn(q, k, v, bias, B=B, S=S, TQ=TQ)
        h1, n2 = _branch_out(o, zg, h, g_sgu[i][None], w_spatial[i], jnp.transpose(b_spatial[i]),
                             g_ffn[i][None], w_a_proj[i], w_b_proj[i], w_o[i], B=B, S=S, tm=min(256, S))
        act, w_down_bf = _ffn_up(n2, w_gu[i], w_down[i], tm=min(1024, S), tn=512)
        h2 = _ffn_down(act, w_down_bf, h1, tm=tm, tn=1024)
        assert depth == 1
        h = _ple_final(h2, p[i].reshape(T, -1), w_ple_gate[i], w_ple_proj[i], g_ple[i][None],
                       g_final[None], tm=tm)
    return h.reshape(B, S, D)
```

```python
import functools
import math

import numpy as np
import jax
import jax.numpy as jnp
from jax import lax
from jax.experimental import pallas as pl
from jax.experimental.pallas import tpu as pltpu

F32 = jnp.float32
BF16 = jnp.bfloat16

N_HEADS = 16
QK_NOPE = 64
QK_ROPE = 32
V_HEAD = 64
Q_LORA = 512
KV_LORA = 256
IDX_HEADS = 16
IDX_DIM = 64
IDX_ROPE = 32
TOPK_MAX = 256
SGU_CHUNK = 128
SGU_GROUPS = 8
SGU_GROUP_DIM = 128
SGU_WIDTH = SGU_GROUPS * SGU_GROUP_DIM
ROPE_THETA = 10000.0
EPS = 1e-6

LANES = 128
IN_SMALL = Q_LORA + KV_LORA + QK_ROPE + IDX_DIM + IDX_HEADS
SMALL_COLS = 1024
COL_CKV = Q_LORA
COL_MISC = Q_LORA + KV_LORA
MISC_KIDX = QK_ROPE
MISC_WIDX = QK_ROPE + IDX_DIM
VMEM_LIMIT_BYTES = 56 * 1024 * 1024
NEG_BIAS = -1e30
F32_MAX = 3.4028234663852886e38
STEPS_PER_CHECK = 4
COUNT_CHAINS = 4
STAGE_ROWS = 512


def _cparams(semantics):
    return pltpu.CompilerParams(dimension_semantics=semantics, vmem_limit_bytes=VMEM_LIMIT_BYTES)


def _rms(x, g):
    return x * lax.rsqrt(jnp.mean(x * x, axis=-1, keepdims=True) + EPS) * g


def _dot(a, b):
    return jnp.dot(a, b, preferred_element_type=F32)


def _dot_nt(a, b):
    return lax.dot_general(a, b, (((1,), (1,)), ((), ())), preferred_element_type=F32)


def _gelu_exact(x):
    return 0.5 * x * (1.0 + lax.erf(x * (1.0 / math.sqrt(2.0))))


def _sigmoid(x):
    return 0.5 * jnp.tanh(0.5 * x) + 0.5


def _stage_weights_bf16(pairs, stage, sem):
    chunks = []
    for src, dst in pairs:
        rows = min(src.shape[0], STAGE_ROWS)
        assert src.shape[0] % rows == 0 and src.shape == dst.shape
        chunks += [(src, dst, r, rows) for r in range(0, src.shape[0], rows)]

    def copy(k):
        src, _, r, rows = chunks[k]
        return pltpu.make_async_copy(src.at[r:r + rows, :], stage.at[k % 2, 0:rows, :], sem.at[k % 2])

    copy(0).start()
    for k, (_, dst, r, rows) in enumerate(chunks):
        if k + 1 < len(chunks):
            copy(k + 1).start()
        copy(k).wait()
        dst[r:r + rows, :] = stage[k % 2, 0:rows, :].astype(BF16)


TAB_Q = 0
TAB_QI = 2
TAB_KI = 4
TAB_KR = 7
N_TABLES = 10


def _rope_block(x, cos, sin_a, sin_b):
    return x * cos + pltpu.roll(x, LANES - 16, 1) * sin_a + pltpu.roll(x, 16, 1) * sin_b


def _a_proj_kernel(x_ref, gmix_ref, ws_ref, gcq_ref, gckv_ref, wq_ref, wiq_ref, wiqr_ref, wk_ref, wv_ref, tab_ref,
                   n_ref, q_ref, k_ref, v_ref, qi_ref, kilo_ref, kihi_ref, wt_ref, ws_bf):
    @pl.when(pl.program_id(0) == 0)
    def _():
        ws_bf[...] = ws_ref[...].astype(BF16)

    n = _rms(x_ref[...], gmix_ref[...]).astype(BF16)
    n_ref[...] = n
    small = _dot_nt(n, ws_bf[...])
    c_q = _rms(small[:, 0:Q_LORA], gcq_ref[...]).astype(BF16)
    c_kv = _rms(small[:, COL_CKV:COL_CKV + KV_LORA], gckv_ref[...]).astype(BF16)

    q = _dot(c_q, wq_ref[...])
    cq, sq = tab_ref[TAB_Q], tab_ref[TAB_Q + 1]
    for h in range(N_HEADS):
        blk = q[:, h * LANES:(h + 1) * LANES]
        q_ref[0, h] = (blk * cq + pltpu.roll(blk, LANES - QK_ROPE, 1) * sq).astype(BF16)

    ci, si = tab_ref[TAB_QI], tab_ref[TAB_QI + 1]
    qi = _dot(c_q, wiq_ref[...])
    qir = _dot(c_q, wiqr_ref[...])
    for hp in range(IDX_HEADS // 2):
        cols = slice(hp * LANES, (hp + 1) * LANES)
        qi_ref[:, cols] = (qi[:, cols] * ci + qir[:, cols] * si).astype(BF16)

    misc = small[:, COL_MISC:COL_MISC + LANES]
    ki_lo = _rope_block(pltpu.roll(misc, LANES - MISC_KIDX, 1),
                        tab_ref[TAB_KI], tab_ref[TAB_KI + 1], tab_ref[TAB_KI + 2])
    kilo_ref[...] = ki_lo.astype(BF16)
    kihi_ref[...] = pltpu.roll(ki_lo, IDX_DIM, 1).astype(BF16)

    k_rope = pltpu.roll(_rope_block(misc, tab_ref[TAB_KR], tab_ref[TAB_KR + 1], tab_ref[TAB_KR + 2]),
                        QK_NOPE, 1)
    k_nope = _dot(c_kv, wk_ref[...])
    for h in range(N_HEADS):
        k_ref[0, h] = (k_nope[:, h * LANES:(h + 1) * LANES] + k_rope).astype(BF16)

    v = _dot(c_kv, wv_ref[...])
    for p in range(N_HEADS // 2):
        v_ref[0, p] = v[:, p * LANES:(p + 1) * LANES].astype(BF16)

    w_scale = IDX_HEADS ** -0.5 * IDX_DIM ** -0.5
    wt_ref[...] = misc.T[MISC_WIDX:MISC_WIDX + IDX_HEADS, :] * w_scale


def _a_proj(x2, g_mix, w_t, g_cq, g_ckv, wq, wiq, wiqr, wk, wv, tables, *, B, S, tm):
    T, D = x2.shape
    nt = S // tm
    const2 = lambda i: (0, 0)
    head_spec = lambda nh: pl.BlockSpec((1, nh, tm, LANES), lambda i: (i // nt, 0, i % nt, 0))
    return pl.pallas_call(
        _a_proj_kernel,
        grid=(T // tm,),
        in_specs=[
            pl.BlockSpec((tm, D), lambda i: (i, 0)),
            pl.BlockSpec((1, D), const2),
            pl.BlockSpec((SMALL_COLS, D), const2, pipeline_mode=pl.Buffered(1)),
            pl.BlockSpec((1, Q_LORA), const2),
            pl.BlockSpec((1, KV_LORA), const2),
            pl.BlockSpec(wq.shape, const2),
            pl.BlockSpec(wiq.shape, const2),
            pl.BlockSpec(wiqr.shape, const2),
            pl.BlockSpec(wk.shape, const2),
            pl.BlockSpec(wv.shape, const2),
            pl.BlockSpec((N_TABLES, tm, LANES), lambda i: (0, i % nt, 0)),
        ],
        out_specs=[
            pl.BlockSpec((tm, D), lambda i: (i, 0)),
            head_spec(N_HEADS),
            head_spec(N_HEADS),
            head_spec(N_HEADS // 2),
            pl.BlockSpec((tm, IDX_HEADS * IDX_DIM), lambda i: (i, 0)),
            pl.BlockSpec((tm, LANES), lambda i: (i, 0)),
            pl.BlockSpec((tm, LANES), lambda i: (i, 0)),
            pl.BlockSpec((IDX_HEADS, tm), lambda i: (0, i)),
        ],
        out_shape=[
            jax.ShapeDtypeStruct((T, D), BF16),
            jax.ShapeDtypeStruct((B, N_HEADS, S, LANES), BF16),
            jax.ShapeDtypeStruct((B, N_HEADS, S, LANES), BF16),
            jax.ShapeDtypeStruct((B, N_HEADS // 2, S, LANES), BF16),
            jax.ShapeDtypeStruct((T, IDX_HEADS * IDX_DIM), BF16),
            jax.ShapeDtypeStruct((T, LANES), BF16),
            jax.ShapeDtypeStruct((T, LANES), BF16),
            jax.ShapeDtypeStruct((IDX_HEADS, T), F32),
        ],
        scratch_shapes=[pltpu.VMEM((SMALL_COLS, D), BF16)],
        compiler_params=_cparams(("arbitrary",)),
        name="norm_a_proj",
    )(x2, g_mix, w_t, g_cq, g_ckv, wq, wiq, wiqr, wk, wv, tables)


def _uv_gate_kernel(n_ref, w_ref, out_ref, w_bf, *, n_z):
    j = pl.program_id(0)

    @pl.when(pl.program_id(1) == 0)
    def _():
        w_bf[...] = w_ref[...].astype(BF16)

    def proj():
        return _dot_nt(n_ref[...], w_bf[...])

    @pl.when(j < n_z)
    def _():
        out_ref[...] = _gelu_exact(proj()).astype(BF16)

    @pl.when(j >= n_z)
    def _():
        out_ref[...] = _sigmoid(proj()).astype(BF16)


def _uv_gate_proj(n, w_t, *, tm, tn):
    T, D = n.shape
    n_z = (2 * SGU_WIDTH) // tn
    n_out = 2 * SGU_WIDTH + 2 * D
    assert w_t.shape == (IN_SMALL + n_out, D) and IN_SMALL % 8 == 0 and tn % 8 == 0
    kern = functools.partial(_uv_gate_kernel, n_z=n_z)
    return pl.pallas_call(
        kern,
        grid=(n_out // tn, T // tm),
        in_specs=[
            pl.BlockSpec((tm, D), lambda j, i: (i, 0)),
            pl.BlockSpec((pl.Element(tn), pl.Element(D)), lambda j, i: ((IN_SMALL // 8 + j * (tn // 8)) * 8, 0)),
        ],
        out_specs=pl.BlockSpec((tm, tn), lambda j, i: (i, j)),
        out_shape=jax.ShapeDtypeStruct((T, n_out), BF16),
        scratch_shapes=[pltpu.VMEM((tn, D), BF16)],
        compiler_params=_cparams(("arbitrary", "arbitrary")),
        name="uv_gate_proj",
    )(n, w_t)


def _dsa_index_kernel(kilo_ref, kihi_ref, qi_ref, wt_ref, bias_ref, isc_ref, mm_ref, js_ref,
                      *, S, TQ, KCH, topk):
    j = pl.program_id(1)
    q0 = j * TQ
    nkeys = q0 + TQ
    qidx = q0 + lax.broadcasted_iota(jnp.int32, (1, TQ), 1)
    kf = float(topk)

    mm_ref[0:8, :] = jnp.full((8, TQ), jnp.inf, F32)
    mm_ref[8:16, :] = jnp.full((8, TQ), -jnp.inf, F32)
    for c in range(S // KCH):
        @pl.when(c * KCH < nkeys)
        def _(c=c):
            klo = kilo_ref[c * KCH:(c + 1) * KCH, :]
            khi = kihi_ref[c * KCH:(c + 1) * KCH, :]
            acc = jnp.zeros((KCH, TQ), F32)
            for hp in range(IDX_HEADS // 2):
                qp = qi_ref[:, hp * LANES:(hp + 1) * LANES]
                s0 = _dot_nt(klo, qp)
                s1 = _dot_nt(khi, qp)
                acc = acc + jnp.maximum(s0, 0.0) * wt_ref[2 * hp:2 * hp + 1, :]
                acc = acc + jnp.maximum(s1, 0.0) * wt_ref[2 * hp + 1:2 * hp + 2, :]
            kidx = c * KCH + lax.broadcasted_iota(jnp.int32, (KCH, TQ), 0)
            causal = kidx <= qidx
            isc_ref[c * KCH:(c + 1) * KCH, :] = jnp.where(causal, acc, -jnp.inf)
            lo_part = jnp.where(causal, acc, jnp.inf).reshape(KCH // 8, 8, TQ).min(axis=0)
            hi_part = jnp.where(causal, acc, -jnp.inf).reshape(KCH // 8, 8, TQ).max(axis=0)
            mm_ref[0:8, :] = jnp.minimum(mm_ref[0:8, :], lo_part)
            mm_ref[8:16, :] = jnp.maximum(mm_ref[8:16, :], hi_part)

    def select(nch):
        def count(pred):
            groups = []
            for lg in range(TQ // LANES):
                lanes = slice(lg * LANES, (lg + 1) * LANES)
                acc = jnp.zeros((COUNT_CHAINS, 8, LANES), F32)
                for c in range(nch):
                    ones = jnp.where(pred(isc_ref[c * TQ:(c + 1) * TQ, lanes], c * TQ, lanes), 1.0, 0.0)
                    acc = acc + ones.reshape(COUNT_CHAINS, TQ // (8 * COUNT_CHAINS), 8, LANES).sum(axis=1)
                groups.append(acc.sum(axis=0).sum(axis=0, keepdims=True))
            return jnp.concatenate(groups, axis=1)

        row_min = mm_ref[0:8, :].min(axis=0, keepdims=True)
        row_max = mm_ref[8:16, :].max(axis=0, keepdims=True)
        full = (qidx + 1) <= topk
        c_max = count(lambda blk, k0, lanes: blk >= row_max[:, lanes])
        exact0 = c_max == kf
        tie0 = c_max > kf
        settled0 = full | exact0 | tie0
        lo0 = jnp.where(full, -F32_MAX, jnp.where(settled0, row_max, row_min))
        hi0 = jnp.where(full, -F32_MAX, jnp.where(tie0, jnp.inf, row_max))
        act0 = jnp.where(settled0, 0.0, 1.0)

        def step(lo, hi, act):
            mid = lo * 0.5 + hi * 0.5
            inside = (mid > lo) & (mid < hi)
            cnt = count(lambda blk, k0, lanes: blk >= mid[:, lanes])
            upd = (act > 0.0) & inside
            found = upd & (cnt == kf)
            lo2 = jnp.where(upd & (cnt >= kf), mid, lo)
            hi2 = jnp.where(upd & (cnt <= kf), mid, hi)
            return lo2, hi2, jnp.where(upd & jnp.logical_not(found), 1.0, 0.0)

        def any_active(act):
            return (jnp.max(act) > 0.0).astype(jnp.int32)

        def body(st):
            lo, hi, act, _ = st
            for _ in range(STEPS_PER_CHECK):
                lo, hi, act = step(lo, hi, act)
            return lo, hi, act, any_active(act)

        lo, hi, _, _ = lax.while_loop(lambda st: st[3] > 0, body, (lo0, hi0, act0, any_active(act0)))

        tie = lo < hi
        js_ref[...] = jnp.full((8, TQ), -1.0, F32)

        @pl.when(jnp.max(jnp.where(tie, 1.0, 0.0)) > 0.0)
        def _():
            need = kf - count(lambda blk, k0, lanes: blk >= hi[:, lanes])

            def kpos(k0):
                return (k0 + lax.broadcasted_iota(jnp.int32, (TQ, LANES), 0)).astype(F32)

            def tie_step(_, st):
                ilo, ihi = st
                imid = jnp.floor((ilo + ihi) * 0.5)
                cnt = count(lambda blk, k0, lanes: (blk >= lo[:, lanes]) & (blk < hi[:, lanes])
                            & (kpos(k0) <= imid[:, lanes]))
                ge = cnt >= need
                return jnp.where(ge, ilo, imid), jnp.where(ge, imid, ihi)

            nsteps = int(math.ceil(math.log2(S))) + 1
            _, ihi = lax.fori_loop(0, nsteps, tie_step,
                                   (jnp.full((1, TQ), -1.0, F32), jnp.full((1, TQ), S - 1.0, F32)))
            js_ref[0:1, :] = jnp.where(tie, ihi, -1.0)

        jstar = js_ref[0:1, :]

        for c in range(S // TQ):
            if c < nch:
                blk = isc_ref[c * TQ:(c + 1) * TQ, :]
                kpos_c = (c * TQ + lax.broadcasted_iota(jnp.int32, (TQ, TQ), 0)).astype(F32)
                sel = (blk >= hi) | ((blk >= lo) & (kpos_c <= jstar))
                bias_ref[:, c * TQ:(c + 1) * TQ] = jnp.where(sel, 0.0, NEG_BIAS).T
            else:
                bias_ref[:, c * TQ:(c + 1) * TQ] = jnp.full((TQ, TQ), NEG_BIAS, F32)

    for jj in range(S // TQ):
        @pl.when(j == jj)
        def _(jj=jj):
            select(jj + 1)


def _dsa_index(ki_lo, ki_hi, qi, wt, *, B, S, TQ, KCH, topk):
    T = B * S
    nq = S // TQ
    kern = functools.partial(_dsa_index_kernel, S=S, TQ=TQ, KCH=KCH, topk=topk)
    return pl.pallas_call(
        kern,
        grid=(B, nq),
        in_specs=[
            pl.BlockSpec((S, LANES), lambda b, j: (b, 0)),
            pl.BlockSpec((S, LANES), lambda b, j: (b, 0)),
            pl.BlockSpec((TQ, IDX_HEADS * IDX_DIM), lambda b, j: (b * nq + j, 0)),
            pl.BlockSpec((IDX_HEADS, TQ), lambda b, j: (0, b * nq + j)),
        ],
        out_specs=pl.BlockSpec((TQ, S), lambda b, j: (b * nq + j, 0)),
        out_shape=jax.ShapeDtypeStruct((T, S), F32),
        scratch_shapes=[
            pltpu.VMEM((S, TQ), F32),
            pltpu.VMEM((16, TQ), F32),
            pltpu.VMEM((8, TQ), F32),
        ],
        compiler_params=_cparams(("parallel", "arbitrary")),
        name="dsa_index",
    )(ki_lo, ki_hi, qi, wt)


def _dsa_attn_kernel(q_ref, k_ref, v_ref, bias_ref, o_ref, *, S, TQ):
    j = pl.program_id(1)
    lane = lax.broadcasted_iota(jnp.int32, (TQ, LANES), 1)

    def variant(nk):
        ones = jnp.ones((nk, LANES), BF16)

        def pair(p, carry):
            vp = jnp.concatenate([v_ref[0, p, 0:nk, :], ones], axis=1)
            outs = []
            for e in range(2):
                h = 2 * p + e
                s = _dot_nt(q_ref[0, h], k_ref[0, h, 0:nk, :]) + bias_ref[:, 0:nk]
                m = s.max(axis=1, keepdims=True)
                pv = _dot(jnp.exp2(s - m).astype(BF16), vp)
                outs.append(pv[:, 0:LANES] * (1.0 / pv[:, LANES:2 * LANES]))
            o_ref[0, p] = jnp.where(lane < V_HEAD, outs[0], outs[1]).astype(BF16)
            return carry
        lax.fori_loop(0, N_HEADS // 2, pair, 0, unroll=4)

    for jj in range(S // TQ):
        @pl.when(j == jj)
        def _(jj=jj):
            variant((jj + 1) * TQ)


def _dsa_attn(q, k, v, bias, *, B, S, TQ):
    nq = S // TQ
    kern = functools.partial(_dsa_attn_kernel, S=S, TQ=TQ)
    return pl.pallas_call(
        kern,
        grid=(B, nq),
        in_specs=[
            pl.BlockSpec((1, N_HEADS, TQ, LANES), lambda b, j: (b, 0, j, 0)),
            pl.BlockSpec((1, N_HEADS, S, LANES), lambda b, j: (b, 0, 0, 0)),
            pl.BlockSpec((1, N_HEADS // 2, S, LANES), lambda b, j: (b, 0, 0, 0)),
            pl.BlockSpec((TQ, S), lambda b, j: (b * nq + j, 0)),
        ],
        out_specs=pl.BlockSpec((1, N_HEADS // 2, TQ, LANES), lambda b, j: (b, 0, j, 0)),
        out_shape=jax.ShapeDtypeStruct((B, N_HEADS // 2, S, LANES), BF16),
        compiler_params=_cparams(("parallel", "arbitrary")),
        name="dsa_attn",
    )(q, k, v, bias)


def _branch_out_kernel(o_ref, z_ref, ga_ref, gb_ref, x_ref, gs_ref, ws_ref, bt_ref, gf_ref,
                       wa_hbm, wb_hbm, wo_hbm, h_ref, n_ref,
                       wa_bf, wb_bf, wo_bf, y_scr, stage, sem, *, tm):
    @pl.when(pl.program_id(0) == 0)
    def _():
        _stage_weights_bf16([(wa_hbm, wa_bf), (wb_hbm, wb_bf), (wo_hbm, wo_bf)], stage, sem)

    row = lax.broadcasted_iota(jnp.int32, (SGU_CHUNK, SGU_CHUNK), 0)
    col = lax.broadcasted_iota(jnp.int32, (SGU_CHUNK, SGU_CHUNK), 1)
    tril = col <= row
    w = [jnp.where(tril, ws_ref[g], 0.0).astype(BF16) for g in range(SGU_GROUPS)]
    for cc in range(tm // SGU_CHUNK):
        rows = slice(cc * SGU_CHUNK, (cc + 1) * SGU_CHUNK)
        vn = _rms(z_ref[rows, SGU_WIDTH:2 * SGU_WIDTH].astype(F32), gs_ref[...]).astype(BF16)
        for g in range(SGU_GROUPS):
            cols = slice(g * SGU_GROUP_DIM, (g + 1) * SGU_GROUP_DIM)
            mixed = _dot(w[g], vn[:, cols]) + bt_ref[:, g:g + 1]
            y_scr[rows, cols] = (z_ref[rows, cols].astype(F32) * mixed).astype(BF16)

    o_a = jnp.concatenate([o_ref[0, p] for p in range(N_HEADS // 2)], axis=1)
    ya = _dot(o_a, wa_bf[...])
    yb = _dot(y_scr[...], wb_bf[...])
    merged = (ga_ref[...].astype(F32) * ya + gb_ref[...].astype(F32) * yb).astype(BF16)
    h = x_ref[...] + _dot(merged, wo_bf[...])
    h_ref[...] = h
    n_ref[...] = _rms(h, gf_ref[...]).astype(BF16)


def _branch_out(o, zg, x2, g_sgu, w_spatial, b_t, g_ffn, wa, wb, wo, *, B, S, tm):
    T, D = x2.shape
    nt = S // tm
    assert 2 * SGU_WIDTH == D and wa.shape == wb.shape == (SGU_WIDTH, D) and wo.shape == (D, D)
    row = lambda c: pl.BlockSpec((tm, D), lambda i: (i, c))
    const = lambda shape: pl.BlockSpec(shape, lambda i: (0,) * len(shape))
    hbm = pl.BlockSpec(memory_space=pl.ANY)
    kern = functools.partial(_branch_out_kernel, tm=tm)
    return pl.pallas_call(
        kern,
        grid=(T // tm,),
        in_specs=[
            pl.BlockSpec((1, N_HEADS // 2, tm, LANES), lambda i: (i // nt, 0, i % nt, 0)),
            row(0), row(1), row(2),
            row(0),
            const((1, SGU_WIDTH)),
            const((SGU_GROUPS, SGU_CHUNK, SGU_CHUNK)),
            const((SGU_CHUNK, SGU_GROUPS)),
            const((1, D)),
            hbm, hbm, hbm,
        ],
        out_specs=[pl.BlockSpec((tm, D), lambda i: (i, 0)), pl.BlockSpec((tm, D), lambda i: (i, 0))],
        out_shape=[jax.ShapeDtypeStruct((T, D), F32), jax.ShapeDtypeStruct((T, D), BF16)],
        scratch_shapes=[
            pltpu.VMEM((SGU_WIDTH, D), BF16),
            pltpu.VMEM((SGU_WIDTH, D), BF16),
            pltpu.VMEM((D, D), BF16),
            pltpu.VMEM((tm, SGU_WIDTH), BF16),
            pltpu.VMEM((2, STAGE_ROWS, D), F32),
            pltpu.SemaphoreType.DMA((2,)),
        ],
        compiler_params=_cparams(("arbitrary",)),
        name="branch_out",
    )(o, zg, zg, zg, x2, g_sgu, w_spatial, b_t, g_ffn, wa, wb, wo)


def _ffn_up_kernel(n_ref, wg_ref, wu_ref, wd_ref, a_ref, wd_bf_ref, wg_bf, wu_bf):
    @pl.when(pl.program_id(1) == 0)
    def _():
        wg_bf[...] = wg_ref[...].astype(BF16)
        wu_bf[...] = wu_ref[...].astype(BF16)

    wd_bf_ref[...] = wd_ref[...].astype(BF16)
    n = n_ref[...]
    g = _dot(n, wg_bf[...])
    u = _dot(n, wu_bf[...])
    a_ref[...] = (g * _sigmoid(g) * u).astype(BF16)


def _ffn_up(n2, w_gu, w_down, *, tm, tn):
    T, D = n2.shape
    d_ff = w_down.shape[0]
    nn, nm = d_ff // tn, T // tm
    slab = d_ff // (nn * nm)
    assert slab * nn * nm == d_ff and slab % 16 == 0
    return pl.pallas_call(
        _ffn_up_kernel,
        grid=(nn, nm),
        in_specs=[
            pl.BlockSpec((tm, D), lambda j, i: (i, 0)),
            pl.BlockSpec((D, tn), lambda j, i: (0, j)),
            pl.BlockSpec((D, tn), lambda j, i: (0, nn + j)),
            pl.BlockSpec((slab, D), lambda j, i: (j * nm + i, 0)),
        ],
        out_specs=[
            pl.BlockSpec((tm, tn), lambda j, i: (i, j)),
            pl.BlockSpec((slab, D), lambda j, i: (j * nm + i, 0)),
        ],
        out_shape=[jax.ShapeDtypeStruct((T, d_ff), BF16), jax.ShapeDtypeStruct((d_ff, D), BF16)],
        scratch_shapes=[pltpu.VMEM((D, tn), BF16), pltpu.VMEM((D, tn), BF16)],
        compiler_params=_cparams(("arbitrary", "arbitrary")),
        name="ffn_up",
    )(n2, w_gu, w_gu, w_down)


def _ffn_down_kernel(a_ref, w_ref, h1_ref, h2_ref):
    h2_ref[...] = h1_ref[...] + _dot(a_ref[...], w_ref[...])


def _ffn_down(act, w_down_bf, h1, *, tm, tn):
    T, D = h1.shape
    d_ff = act.shape[1]
    return pl.pallas_call(
        _ffn_down_kernel,
        grid=(D // tn, T // tm),
        in_specs=[
            pl.BlockSpec((tm, d_ff), lambda j, i: (i, 0)),
            pl.BlockSpec((d_ff, tn), lambda j, i: (0, j)),
            pl.BlockSpec((tm, tn), lambda j, i: (i, j)),
        ],
        out_specs=pl.BlockSpec((tm, tn), lambda j, i: (i, j)),
        out_shape=jax.ShapeDtypeStruct((T, D), F32),
        compiler_params=_cparams(("arbitrary", "arbitrary")),
        name="ffn_down",
    )(act, w_down_bf, h1)


def _ple_final_kernel(h2_ref, p_ref, gp_ref, gf_ref, wg_hbm, wp_hbm, out_ref, wg_bf, wp_bf, stage, sem):
    @pl.when(pl.program_id(0) == 0)
    def _():
        _stage_weights_bf16([(wg_hbm, wg_bf), (wp_hbm, wp_bf)], stage, sem)

    h2 = h2_ref[...]
    gate = _sigmoid(_dot(_rms(h2, gp_ref[...]).astype(BF16), wg_bf[...]))
    pp = _dot(p_ref[...].astype(BF16), wp_bf[...])
    out_ref[...] = _rms(h2 + gate * pp, gf_ref[...])


def _ple_final(h2, p2, w_pg, w_pp, g_ple, g_final, *, tm):
    T, D = h2.shape
    P = p2.shape[1]
    hbm = pl.BlockSpec(memory_space=pl.ANY)
    return pl.pallas_call(
        _ple_final_kernel,
        grid=(T // tm,),
        in_specs=[
            pl.BlockSpec((tm, D), lambda i: (i, 0)),
            pl.BlockSpec((tm, P), lambda i: (i, 0)),
            pl.BlockSpec((1, D), lambda i: (0, 0)),
            pl.BlockSpec((1, D), lambda i: (0, 0)),
            hbm, hbm,
        ],
        out_specs=pl.BlockSpec((tm, D), lambda i: (i, 0)),
        out_shape=jax.ShapeDtypeStruct((T, D), F32),
        scratch_shapes=[
            pltpu.VMEM((D, D), BF16),
            pltpu.VMEM((P, D), BF16),
            pltpu.VMEM((2, STAGE_ROWS, D), F32),
            pltpu.SemaphoreType.DMA((2,)),
        ],
        compiler_params=_cparams(("arbitrary",)),
        name="ple_final",
    )(h2, p2, g_ple, g_final, w_pg, w_pp)


def _lane_tables(S):
    assert QK_ROPE == IDX_ROPE == 32
    scale = (QK_NOPE + QK_ROPE) ** -0.5 * math.log2(math.e)
    a = np.zeros((N_TABLES, LANES), np.float32)
    bc = np.zeros_like(a)
    bs = np.zeros_like(a)
    a[TAB_Q, 0:QK_NOPE] = scale
    bc[TAB_Q, 64:96] = scale
    bs[TAB_Q + 1, 64:80] = -scale
    bs[TAB_Q + 1, 80:96] = scale
    for o in (0, IDX_DIM):
        bc[TAB_QI, o:o + 32] = 1.0
        a[TAB_QI, o + 32:o + 64] = 1.0
        bs[TAB_QI + 1, o:o + 32] = 1.0
    bc[TAB_KI, 0:32] = 1.0
    a[TAB_KI, 32:64] = 1.0
    bs[TAB_KI + 1, 0:16] = -1.0
    bs[TAB_KI + 2, 16:32] = 1.0
    bc[TAB_KR, 0:32] = 1.0
    bs[TAB_KR + 1, 0:16] = -1.0
    bs[TAB_KR + 2, 16:32] = 1.0
    inv = ROPE_THETA ** (-jnp.arange(0, QK_ROPE, 2, dtype=F32) / QK_ROPE)
    ang = jnp.arange(S, dtype=F32)[:, None] * jnp.tile(inv, LANES // inv.shape[0])[None, :]
    return a[:, None, :] + bc[:, None, :] * jnp.cos(ang)[None] + bs[:, None, :] * jnp.sin(ang)[None]


def _tiles(S):
    return dict(
        attn_q=256,
        index_keys=min(512, S),
        a_proj_rows=min(256, S),
        uv_gate_rows=min(1024, S), uv_gate_cols=1024,
        branch_out_rows=min(256, S),
        ffn_up_rows=min(1024, S), ffn_up_cols=512,
        ffn_down_rows=min(512, S), ffn_down_cols=1024,
        ple_rows=min(512, S),
    )


def kernel(x, p, g_mix, w_in, g_cq, g_ckv, w_uq, w_uk, w_uv, w_iq, w_a_proj, g_sgu, w_spatial,
           b_spatial, w_b_proj, w_o, g_ffn, w_gu, w_down, g_ple, w_ple_gate, w_ple_proj, g_final):
    B, S, D = x.shape
    T = B * S
    depth = w_in.shape[0]
    topk = min(TOPK_MAX, S // 4)
    t = _tiles(S)
    tables = _lane_tables(S)

    h = x.reshape(T, D)
    for i in range(depth):
        wq3 = w_uq[i].reshape(Q_LORA, N_HEADS, QK_NOPE + QK_ROPE)
        x1, x2 = wq3[..., QK_NOPE:QK_NOPE + QK_ROPE // 2], wq3[..., QK_NOPE + QK_ROPE // 2:]
        wq = jnp.concatenate([wq3, x2, x1], axis=-1).reshape(Q_LORA, N_HEADS * LANES).astype(BF16)
        wk = jnp.pad(w_uk[i], ((0, 0), (0, 0), (0, LANES - QK_NOPE))).reshape(KV_LORA, N_HEADS * LANES).astype(BF16)
        wv = w_uv[i].reshape(KV_LORA, N_HEADS * V_HEAD).astype(BF16)
        wi3 = w_iq[i].reshape(Q_LORA, IDX_HEADS, IDX_DIM)
        wiq = w_iq[i].astype(BF16)
        wiqr = jnp.concatenate([-wi3[..., IDX_ROPE // 2:IDX_ROPE], wi3[..., :IDX_ROPE // 2],
                                jnp.zeros_like(wi3[..., IDX_ROPE:])], axis=-1).reshape(w_iq[i].shape).astype(BF16)

        w_t = jnp.transpose(w_in[i])
        n, q, k, v, qi, ki_lo, ki_hi, wt = _a_proj(h, g_mix[i][None], w_t, g_cq[i][None], g_ckv[i][None],
                                                    wq, wiq, wiqr, wk, wv, tables, B=B, S=S, tm=t["a_proj_rows"])
        zg = _uv_gate_proj(n, w_t, tm=t["uv_gate_rows"], tn=t["uv_gate_cols"])
        bias = _dsa_index(ki_lo, ki_hi, qi, wt, B=B, S=S, TQ=t["attn_q"], KCH=t["index_keys"], topk=topk)
        o = _dsa_attn(q, k, v, bias, B=B, S=S, TQ=t["attn_q"])
        h1, n2 = _branch_out(o, zg, h, g_sgu[i][None], w_spatial[i], jnp.transpose(b_spatial[i]),
                             g_ffn[i][None], w_a_proj[i], w_b_proj[i], w_o[i], B=B, S=S, tm=t["branch_out_rows"])
        act, w_down_bf = _ffn_up(n2, w_gu[i], w_down[i], tm=t["ffn_up_rows"], tn=t["ffn_up_cols"])
        h2 = _ffn_down(act, w_down_bf, h1, tm=t["ffn_down_rows"], tn=t["ffn_down_cols"])
        assert depth == 1
        h = _ple_final(h2, p[i].reshape(T, -1), w_ple_gate[i], w_ple_proj[i], g_ple[i][None],
                       g_final[None], tm=t["ple_rows"])
    return h.reshape(B, S, D)
```

```python
import functools
import math

import numpy as np
import jax
import jax.numpy as jnp
from jax import lax
from jax.experimental import pallas as pl
from jax.experimental.pallas import tpu as pltpu

F32 = jnp.float32
BF16 = jnp.bfloat16

N_HEADS = 16
QK_NOPE = 64
QK_ROPE = 32
V_HEAD = 64
Q_LORA = 512
KV_LORA = 256
IDX_HEADS = 16
IDX_DIM = 64
IDX_ROPE = 32
TOPK_MAX = 256
SGU_CHUNK = 128
SGU_GROUPS = 8
SGU_GROUP_DIM = 128
SGU_WIDTH = SGU_GROUPS * SGU_GROUP_DIM
ROPE_THETA = 10000.0
EPS = 1e-6

LANES = 128
IN_SMALL = Q_LORA + KV_LORA + QK_ROPE + IDX_DIM + IDX_HEADS
SMALL_COLS = 1024
COL_CKV = Q_LORA
COL_MISC = Q_LORA + KV_LORA
MISC_KIDX = QK_ROPE
MISC_WIDX = QK_ROPE + IDX_DIM
VMEM_LIMIT_BYTES = 56 * 1024 * 1024
NEG_BIAS = -1e30
F32_MAX = 3.4028234663852886e38
STEPS_PER_CHECK = 4
COUNT_CHAINS = 4


def _cparams(semantics):
    return pltpu.CompilerParams(dimension_semantics=semantics, vmem_limit_bytes=VMEM_LIMIT_BYTES)


def _rms(x, g):
    return x * lax.rsqrt(jnp.mean(x * x, axis=-1, keepdims=True) + EPS) * g


def _dot(a, b):
    return jnp.dot(a, b, preferred_element_type=F32)


def _dot_nt(a, b):
    return lax.dot_general(a, b, (((1,), (1,)), ((), ())), preferred_element_type=F32)


def _gelu_exact(x):
    return 0.5 * x * (1.0 + lax.erf(x * (1.0 / math.sqrt(2.0))))


def _sigmoid(x):
    return 0.5 * jnp.tanh(0.5 * x) + 0.5


class _SideRound:
    def __init__(self, w, first, steps, step_of):
        rows, cols = w.shape
        slab = rows // steps
        assert slab * steps == rows and slab % 16 == 0
        self.first, self.steps = first, steps
        index = lambda *g: (jnp.clip(step_of(*g) - first, 0, steps - 1), 0)
        self.spec = pl.BlockSpec((slab, cols), index)
        self.out_shape = jax.ShapeDtypeStruct((rows, cols), BF16)

    def emit(self, step, src_ref, dst_ref):
        @pl.when((step >= self.first) & (step < self.first + self.steps))
        def _():
            dst_ref[...] = src_ref[...].astype(BF16)


def _pow2_floor(n):
    return 1 << (max(int(n), 1).bit_length() - 1)


TAB_Q = 0
TAB_QI = 2
TAB_KI = 4
TAB_KR = 7
N_TABLES = 10


def _rope_block(x, cos, sin_a, sin_b):
    return x * cos + pltpu.roll(x, LANES - 16, 1) * sin_a + pltpu.roll(x, 16, 1) * sin_b


def _a_proj_kernel(x_ref, gmix_ref, ws_ref, gcq_ref, gckv_ref, wq_ref, wiq_ref, wiqr_ref, wk_ref, wv_ref, tab_ref,
                   n_ref, q_ref, k_ref, v_ref, qi_ref, kilo_ref, kihi_ref, wt_ref, ws_bf):
    @pl.when(pl.program_id(0) == 0)
    def _():
        ws_bf[...] = ws_ref[...].astype(BF16)

    n = _rms(x_ref[...], gmix_ref[...]).astype(BF16)
    n_ref[...] = n
    small = _dot_nt(n, ws_bf[...])
    c_q = _rms(small[:, 0:Q_LORA], gcq_ref[...]).astype(BF16)
    c_kv = _rms(small[:, COL_CKV:COL_CKV + KV_LORA], gckv_ref[...]).astype(BF16)

    q = _dot(c_q, wq_ref[...])
    cq, sq = tab_ref[TAB_Q], tab_ref[TAB_Q + 1]
    for h in range(N_HEADS):
        blk = q[:, h * LANES:(h + 1) * LANES]
        q_ref[0, h] = (blk * cq + pltpu.roll(blk, LANES - QK_ROPE, 1) * sq).astype(BF16)

    ci, si = tab_ref[TAB_QI], tab_ref[TAB_QI + 1]
    qi = _dot(c_q, wiq_ref[...])
    qir = _dot(c_q, wiqr_ref[...])
    for hp in range(IDX_HEADS // 2):
        cols = slice(hp * LANES, (hp + 1) * LANES)
        qi_ref[:, cols] = (qi[:, cols] * ci + qir[:, cols] * si).astype(BF16)

    misc = small[:, COL_MISC:COL_MISC + LANES]
    ki_lo = _rope_block(pltpu.roll(misc, LANES - MISC_KIDX, 1),
                        tab_ref[TAB_KI], tab_ref[TAB_KI + 1], tab_ref[TAB_KI + 2])
    kilo_ref[...] = ki_lo.astype(BF16)
    kihi_ref[...] = pltpu.roll(ki_lo, IDX_DIM, 1).astype(BF16)

    k_rope = pltpu.roll(_rope_block(misc, tab_ref[TAB_KR], tab_ref[TAB_KR + 1], tab_ref[TAB_KR + 2]),
                        QK_NOPE, 1)
    k_nope = _dot(c_kv, wk_ref[...])
    for h in range(N_HEADS):
        k_ref[0, h] = (k_nope[:, h * LANES:(h + 1) * LANES] + k_rope).astype(BF16)

    v = _dot(c_kv, wv_ref[...])
    for p in range(N_HEADS // 2):
        v_ref[0, p] = v[:, p * LANES:(p + 1) * LANES].astype(BF16)

    w_scale = IDX_HEADS ** -0.5 * IDX_DIM ** -0.5
    wt_ref[...] = misc.T[MISC_WIDX:MISC_WIDX + IDX_HEADS, :] * w_scale


def _a_proj(x2, g_mix, w_t, g_cq, g_ckv, wq, wiq, wiqr, wk, wv, tables, *, B, S, tm):
    T, D = x2.shape
    nt = S // tm
    const2 = lambda i: (0, 0)
    head_spec = lambda nh: pl.BlockSpec((1, nh, tm, LANES), lambda i: (i // nt, 0, i % nt, 0))
    return pl.pallas_call(
        _a_proj_kernel,
        grid=(T // tm,),
        in_specs=[
            pl.BlockSpec((tm, D), lambda i: (i, 0)),
            pl.BlockSpec((1, D), const2),
            pl.BlockSpec((SMALL_COLS, D), const2, pipeline_mode=pl.Buffered(1)),
            pl.BlockSpec((1, Q_LORA), const2),
            pl.BlockSpec((1, KV_LORA), const2),
            pl.BlockSpec(wq.shape, const2),
            pl.BlockSpec(wiq.shape, const2),
            pl.BlockSpec(wiqr.shape, const2),
            pl.BlockSpec(wk.shape, const2),
            pl.BlockSpec(wv.shape, const2),
            pl.BlockSpec((N_TABLES, tm, LANES), lambda i: (0, i % nt, 0)),
        ],
        out_specs=[
            pl.BlockSpec((tm, D), lambda i: (i, 0)),
            head_spec(N_HEADS),
            head_spec(N_HEADS),
            head_spec(N_HEADS // 2),
            pl.BlockSpec((tm, IDX_HEADS * IDX_DIM), lambda i: (i, 0)),
            pl.BlockSpec((tm, LANES), lambda i: (i, 0)),
            pl.BlockSpec((tm, LANES), lambda i: (i, 0)),
            pl.BlockSpec((IDX_HEADS, tm), lambda i: (0, i)),
        ],
        out_shape=[
            jax.ShapeDtypeStruct((T, D), BF16),
            jax.ShapeDtypeStruct((B, N_HEADS, S, LANES), BF16),
            jax.ShapeDtypeStruct((B, N_HEADS, S, LANES), BF16),
            jax.ShapeDtypeStruct((B, N_HEADS // 2, S, LANES), BF16),
            jax.ShapeDtypeStruct((T, IDX_HEADS * IDX_DIM), BF16),
            jax.ShapeDtypeStruct((T, LANES), BF16),
            jax.ShapeDtypeStruct((T, LANES), BF16),
            jax.ShapeDtypeStruct((IDX_HEADS, T), F32),
        ],
        scratch_shapes=[pltpu.VMEM((SMALL_COLS, D), BF16)],
        compiler_params=_cparams(("arbitrary",)),
        name="norm_a_proj",
    )(x2, g_mix, w_t, g_cq, g_ckv, wq, wiq, wiqr, wk, wv, tables)


def _uv_gate_kernel(n_ref, w_ref, *rest, n_z, sides):
    side_in, (out_ref, *side_out), w_bf = rest[:len(sides)], rest[len(sides):-1], rest[-1]
    j = pl.program_id(0)
    step = j * pl.num_programs(1) + pl.program_id(1)
    for side, src, dst in zip(sides, side_in, side_out):
        side.emit(step, src, dst)

    @pl.when(pl.program_id(1) == 0)
    def _():
        w_bf[...] = w_ref[...].astype(BF16)

    def proj():
        return _dot_nt(n_ref[...], w_bf[...])

    @pl.when(j < n_z)
    def _():
        out_ref[...] = _gelu_exact(proj()).astype(BF16)

    @pl.when(j >= n_z)
    def _():
        out_ref[...] = _sigmoid(proj()).astype(BF16)


def _uv_gate_proj(n, w_t, wa, wb, wo, *, tm, tn):
    T, D = n.shape
    n_z = (2 * SGU_WIDTH) // tn
    n_out = 2 * SGU_WIDTH + 2 * D
    nm = T // tm
    assert w_t.shape == (IN_SMALL + n_out, D) and IN_SMALL % 8 == 0 and tn % 8 == 0
    step_of = lambda j, i: j * nm + i
    a = _pow2_floor((n_out // tn) * nm // 4)
    sides = [_SideRound(wa, 0, a, step_of), _SideRound(wb, a, a, step_of), _SideRound(wo, 2 * a, 2 * a, step_of)]
    kern = functools.partial(_uv_gate_kernel, n_z=n_z, sides=sides)
    return pl.pallas_call(
        kern,
        grid=(n_out // tn, nm),
        in_specs=[
            pl.BlockSpec((tm, D), lambda j, i: (i, 0)),
            pl.BlockSpec((pl.Element(tn), pl.Element(D)), lambda j, i: ((IN_SMALL // 8 + j * (tn // 8)) * 8, 0)),
        ] + [s.spec for s in sides],
        out_specs=[pl.BlockSpec((tm, tn), lambda j, i: (i, j))] + [s.spec for s in sides],
        out_shape=[jax.ShapeDtypeStruct((T, n_out), BF16)] + [s.out_shape for s in sides],
        scratch_shapes=[pltpu.VMEM((tn, D), BF16)],
        compiler_params=_cparams(("arbitrary", "arbitrary")),
        name="uv_gate_proj",
    )(n, w_t, wa, wb, wo)


def _dsa_index_kernel(kilo_ref, kihi_ref, qi_ref, wt_ref, bias_ref, isc_ref, mm_ref, js_ref,
                      *, S, TQ, KCH, topk):
    j = pl.program_id(1)
    q0 = j * TQ
    nkeys = q0 + TQ
    qidx = q0 + lax.broadcasted_iota(jnp.int32, (1, TQ), 1)
    kf = float(topk)

    mm_ref[0:8, :] = jnp.full((8, TQ), jnp.inf, F32)
    mm_ref[8:16, :] = jnp.full((8, TQ), -jnp.inf, F32)
    for c in range(S // KCH):
        @pl.when(c * KCH < nkeys)
        def _(c=c):
            klo = kilo_ref[c * KCH:(c + 1) * KCH, :]
            khi = kihi_ref[c * KCH:(c + 1) * KCH, :]
            acc = jnp.zeros((KCH, TQ), F32)
            for hp in range(IDX_HEADS // 2):
                qp = qi_ref[:, hp * LANES:(hp + 1) * LANES]
                s0 = _dot_nt(klo, qp)
                s1 = _dot_nt(khi, qp)
                acc = acc + jnp.maximum(s0, 0.0) * wt_ref[2 * hp:2 * hp + 1, :]
                acc = acc + jnp.maximum(s1, 0.0) * wt_ref[2 * hp + 1:2 * hp + 2, :]
            kidx = c * KCH + lax.broadcasted_iota(jnp.int32, (KCH, TQ), 0)
            causal = kidx <= qidx
            isc_ref[c * KCH:(c + 1) * KCH, :] = jnp.where(causal, acc, -jnp.inf)
            lo_part = jnp.where(causal, acc, jnp.inf).reshape(KCH // 8, 8, TQ).min(axis=0)
            hi_part = jnp.where(causal, acc, -jnp.inf).reshape(KCH // 8, 8, TQ).max(axis=0)
            mm_ref[0:8, :] = jnp.minimum(mm_ref[0:8, :], lo_part)
            mm_ref[8:16, :] = jnp.maximum(mm_ref[8:16, :], hi_part)

    def select(nch):
        def count(pred):
            groups = []
            for lg in range(TQ // LANES):
                lanes = slice(lg * LANES, (lg + 1) * LANES)
                acc = jnp.zeros((COUNT_CHAINS, 8, LANES), F32)
                for c in range(nch):
                    ones = jnp.where(pred(isc_ref[c * TQ:(c + 1) * TQ, lanes], c * TQ, lanes), 1.0, 0.0)
                    acc = acc + ones.reshape(COUNT_CHAINS, TQ // (8 * COUNT_CHAINS), 8, LANES).sum(axis=1)
                groups.append(acc.sum(axis=0).sum(axis=0, keepdims=True))
            return jnp.concatenate(groups, axis=1)

        row_min = mm_ref[0:8, :].min(axis=0, keepdims=True)
        row_max = mm_ref[8:16, :].max(axis=0, keepdims=True)
        full = (qidx + 1) <= topk
        c_max = count(lambda blk, k0, lanes: blk >= row_max[:, lanes])
        exact0 = c_max == kf
        tie0 = c_max > kf
        settled0 = full | exact0 | tie0
        lo0 = jnp.where(full, -F32_MAX, jnp.where(settled0, row_max, row_min))
        hi0 = jnp.where(full, -F32_MAX, jnp.where(tie0, jnp.inf, row_max))
        act0 = jnp.where(settled0, 0.0, 1.0)

        def step(lo, hi, act):
            mid = lo * 0.5 + hi * 0.5
            inside = (mid > lo) & (mid < hi)
            cnt = count(lambda blk, k0, lanes: blk >= mid[:, lanes])
            upd = (act > 0.0) & inside
            found = upd & (cnt == kf)
            lo2 = jnp.where(upd & (cnt >= kf), mid, lo)
            hi2 = jnp.where(upd & (cnt <= kf), mid, hi)
            return lo2, hi2, jnp.where(upd & jnp.logical_not(found), 1.0, 0.0)

        def any_active(act):
            return (jnp.max(act) > 0.0).astype(jnp.int32)

        def body(st):
            lo, hi, act, _ = st
            for _ in range(STEPS_PER_CHECK):
                lo, hi, act = step(lo, hi, act)
            return lo, hi, act, any_active(act)

        lo, hi, _, _ = lax.while_loop(lambda st: st[3] > 0, body, (lo0, hi0, act0, any_active(act0)))

        tie = lo < hi
        js_ref[...] = jnp.full((8, TQ), -1.0, F32)

        @pl.when(jnp.max(jnp.where(tie, 1.0, 0.0)) > 0.0)
        def _():
            need = kf - count(lambda blk, k0, lanes: blk >= hi[:, lanes])

            def kpos(k0):
                return (k0 + lax.broadcasted_iota(jnp.int32, (TQ, LANES), 0)).astype(F32)

            def tie_step(_, st):
                ilo, ihi = st
                imid = jnp.floor((ilo + ihi) * 0.5)
                cnt = count(lambda blk, k0, lanes: (blk >= lo[:, lanes]) & (blk < hi[:, lanes])
                            & (kpos(k0) <= imid[:, lanes]))
                ge = cnt >= need
                return jnp.where(ge, ilo, imid), jnp.where(ge, imid, ihi)

            nsteps = int(math.ceil(math.log2(S))) + 1
            _, ihi = lax.fori_loop(0, nsteps, tie_step,
                                   (jnp.full((1, TQ), -1.0, F32), jnp.full((1, TQ), S - 1.0, F32)))
            js_ref[0:1, :] = jnp.where(tie, ihi, -1.0)

        jstar = js_ref[0:1, :]

        for c in range(S // TQ):
            if c < nch:
                blk = isc_ref[c * TQ:(c + 1) * TQ, :]
                kpos_c = (c * TQ + lax.broadcasted_iota(jnp.int32, (TQ, TQ), 0)).astype(F32)
                sel = (blk >= hi) | ((blk >= lo) & (kpos_c <= jstar))
                bias_ref[:, c * TQ:(c + 1) * TQ] = jnp.where(sel, 0.0, NEG_BIAS).T
            else:
                bias_ref[:, c * TQ:(c + 1) * TQ] = jnp.full((TQ, TQ), NEG_BIAS, F32)

    for jj in range(S // TQ):
        @pl.when(j == jj)
        def _(jj=jj):
            select(jj + 1)


def _dsa_index(ki_lo, ki_hi, qi, wt, *, B, S, TQ, KCH, topk):
    T = B * S
    nq = S // TQ
    kern = functools.partial(_dsa_index_kernel, S=S, TQ=TQ, KCH=KCH, topk=topk)
    return pl.pallas_call(
        kern,
        grid=(B, nq),
        in_specs=[
            pl.BlockSpec((S, LANES), lambda b, j: (b, 0)),
            pl.BlockSpec((S, LANES), lambda b, j: (b, 0)),
            pl.BlockSpec((TQ, IDX_HEADS * IDX_DIM), lambda b, j: (b * nq + j, 0)),
            pl.BlockSpec((IDX_HEADS, TQ), lambda b, j: (0, b * nq + j)),
        ],
        out_specs=pl.BlockSpec((TQ, S), lambda b, j: (b * nq + j, 0)),
        out_shape=jax.ShapeDtypeStruct((T, S), F32),
        scratch_shapes=[
            pltpu.VMEM((S, TQ), F32),
            pltpu.VMEM((16, TQ), F32),
            pltpu.VMEM((8, TQ), F32),
        ],
        compiler_params=_cparams(("parallel", "arbitrary")),
        name="dsa_index",
    )(ki_lo, ki_hi, qi, wt)


def _dsa_attn_kernel(q_ref, k_ref, v_ref, bias_ref, o_ref, *, S, TQ):
    j = pl.program_id(1)
    lane = lax.broadcasted_iota(jnp.int32, (TQ, LANES), 1)

    def variant(nk):
        ones = jnp.ones((nk, LANES), BF16)

        def pair(p, carry):
            vp = jnp.concatenate([v_ref[0, p, 0:nk, :], ones], axis=1)
            outs = []
            for e in range(2):
                h = 2 * p + e
                s = _dot_nt(q_ref[0, h], k_ref[0, h, 0:nk, :]) + bias_ref[:, 0:nk]
                m = s.max(axis=1, keepdims=True)
                pv = _dot(jnp.exp2(s - m).astype(BF16), vp)
                outs.append(pv[:, 0:LANES] * (1.0 / pv[:, LANES:2 * LANES]))
            o_ref[0, p] = jnp.where(lane < V_HEAD, outs[0], outs[1]).astype(BF16)
            return carry
        lax.fori_loop(0, N_HEADS // 2, pair, 0, unroll=4)

    for jj in range(S // TQ):
        @pl.when(j == jj)
        def _(jj=jj):
            variant((jj + 1) * TQ)


def _dsa_attn(q, k, v, bias, *, B, S, TQ):
    nq = S // TQ
    kern = functools.partial(_dsa_attn_kernel, S=S, TQ=TQ)
    return pl.pallas_call(
        kern,
        grid=(B, nq),
        in_specs=[
            pl.BlockSpec((1, N_HEADS, TQ, LANES), lambda b, j: (b, 0, j, 0)),
            pl.BlockSpec((1, N_HEADS, S, LANES), lambda b, j: (b, 0, 0, 0)),
            pl.BlockSpec((1, N_HEADS // 2, S, LANES), lambda b, j: (b, 0, 0, 0)),
            pl.BlockSpec((TQ, S), lambda b, j: (b * nq + j, 0)),
        ],
        out_specs=pl.BlockSpec((1, N_HEADS // 2, TQ, LANES), lambda b, j: (b, 0, j, 0)),
        out_shape=jax.ShapeDtypeStruct((B, N_HEADS // 2, S, LANES), BF16),
        compiler_params=_cparams(("parallel", "arbitrary")),
        name="dsa_attn",
    )(q, k, v, bias)


def _branch_out_kernel(o_ref, z_ref, ga_ref, gb_ref, x_ref, gs_ref, ws_ref, bt_ref, gf_ref,
                       wa_bf, wb_bf, wo_bf, h_ref, n_ref, y_scr, *, tm):
    row = lax.broadcasted_iota(jnp.int32, (SGU_CHUNK, SGU_CHUNK), 0)
    col = lax.broadcasted_iota(jnp.int32, (SGU_CHUNK, SGU_CHUNK), 1)
    tril = col <= row
    w = [jnp.where(tril, ws_ref[g], 0.0).astype(BF16) for g in range(SGU_GROUPS)]
    for cc in range(tm // SGU_CHUNK):
        rows = slice(cc * SGU_CHUNK, (cc + 1) * SGU_CHUNK)
        vn = _rms(z_ref[rows, SGU_WIDTH:2 * SGU_WIDTH].astype(F32), gs_ref[...]).astype(BF16)
        for g in range(SGU_GROUPS):
            cols = slice(g * SGU_GROUP_DIM, (g + 1) * SGU_GROUP_DIM)
            mixed = _dot(w[g], vn[:, cols]) + bt_ref[:, g:g + 1]
            y_scr[rows, cols] = (z_ref[rows, cols].astype(F32) * mixed).astype(BF16)

    o_a = jnp.concatenate([o_ref[0, p] for p in range(N_HEADS // 2)], axis=1)
    ya = _dot(o_a, wa_bf[...])
    yb = _dot(y_scr[...], wb_bf[...])
    merged = (ga_ref[...].astype(F32) * ya + gb_ref[...].astype(F32) * yb).astype(BF16)
    h = x_ref[...] + _dot(merged, wo_bf[...])
    h_ref[...] = h
    n_ref[...] = _rms(h, gf_ref[...]).astype(BF16)


def _branch_out(o, zg, x2, g_sgu, w_spatial, b_t, g_ffn, wa, wb, wo, *, B, S, tm):
    T, D = x2.shape
    nt = S // tm
    assert 2 * SGU_WIDTH == D and wa.shape == wb.shape == (SGU_WIDTH, D) and wo.shape == (D, D)
    row = lambda c: pl.BlockSpec((tm, D), lambda i: (i, c))
    const = lambda shape: pl.BlockSpec(shape, lambda i: (0,) * len(shape))
    resident = lambda w: pl.BlockSpec(w.shape, lambda i: (0, 0), pipeline_mode=pl.Buffered(1))
    kern = functools.partial(_branch_out_kernel, tm=tm)
    return pl.pallas_call(
        kern,
        grid=(T // tm,),
        in_specs=[
            pl.BlockSpec((1, N_HEADS // 2, tm, LANES), lambda i: (i // nt, 0, i % nt, 0)),
            row(0), row(1), row(2),
            row(0),
            const((1, SGU_WIDTH)),
            const((SGU_GROUPS, SGU_CHUNK, SGU_CHUNK)),
            const((SGU_CHUNK, SGU_GROUPS)),
            const((1, D)),
            resident(wa), resident(wb), resident(wo),
        ],
        out_specs=[pl.BlockSpec((tm, D), lambda i: (i, 0)), pl.BlockSpec((tm, D), lambda i: (i, 0))],
        out_shape=[jax.ShapeDtypeStruct((T, D), F32), jax.ShapeDtypeStruct((T, D), BF16)],
        scratch_shapes=[pltpu.VMEM((tm, SGU_WIDTH), BF16)],
        compiler_params=_cparams(("parallel",)),
        name="branch_out",
    )(o, zg, zg, zg, x2, g_sgu, w_spatial, b_t, g_ffn, wa, wb, wo)


def _ffn_up_kernel(n_ref, wg_ref, wu_ref, wd_ref, a_ref, wd_bf_ref, wg_bf, wu_bf):
    @pl.when(pl.program_id(1) == 0)
    def _():
        wg_bf[...] = wg_ref[...].astype(BF16)
        wu_bf[...] = wu_ref[...].astype(BF16)

    wd_bf_ref[...] = wd_ref[...].astype(BF16)
    n = n_ref[...]
    g = _dot(n, wg_bf[...])
    u = _dot(n, wu_bf[...])
    a_ref[...] = (g * _sigmoid(g) * u).astype(BF16)


def _ffn_up(n2, w_gu, w_down, *, tm, tn):
    T, D = n2.shape
    d_ff = w_down.shape[0]
    nn, nm = d_ff // tn, T // tm
    slab = d_ff // (nn * nm)
    assert slab * nn * nm == d_ff and slab % 16 == 0
    return pl.pallas_call(
        _ffn_up_kernel,
        grid=(nn, nm),
        in_specs=[
            pl.BlockSpec((tm, D), lambda j, i: (i, 0)),
            pl.BlockSpec((D, tn), lambda j, i: (0, j)),
            pl.BlockSpec((D, tn), lambda j, i: (0, nn + j)),
            pl.BlockSpec((slab, D), lambda j, i: (j * nm + i, 0)),
        ],
        out_specs=[
            pl.BlockSpec((tm, tn), lambda j, i: (i, j)),
            pl.BlockSpec((slab, D), lambda j, i: (j * nm + i, 0)),
        ],
        out_shape=[jax.ShapeDtypeStruct((T, d_ff), BF16), jax.ShapeDtypeStruct((d_ff, D), BF16)],
        scratch_shapes=[pltpu.VMEM((D, tn), BF16), pltpu.VMEM((D, tn), BF16)],
        compiler_params=_cparams(("arbitrary", "arbitrary")),
        name="ffn_up",
    )(n2, w_gu, w_gu, w_down)


def _ffn_down_kernel(a_ref, w_ref, h1_ref, wg_ref, wp_ref, h2_ref, wg_bf_ref, wp_bf_ref, *, sides):
    step = pl.program_id(0) * pl.num_programs(1) + pl.program_id(1)
    for side, src, dst in zip(sides, (wg_ref, wp_ref), (wg_bf_ref, wp_bf_ref)):
        side.emit(step, src, dst)
    h2_ref[...] = h1_ref[...] + _dot(a_ref[...], w_ref[...])


def _ffn_down(act, w_down_bf, h1, w_pg, w_pp, *, tm, tn):
    T, D = h1.shape
    d_ff = act.shape[1]
    nn, nm = D // tn, T // tm
    step_of = lambda j, i: j * nm + i
    a = _pow2_floor(nn * nm // 2)
    b = min(a, w_pp.shape[0] // 16)
    sides = [_SideRound(w_pg, 0, a, step_of), _SideRound(w_pp, a, b, step_of)]
    kern = functools.partial(_ffn_down_kernel, sides=sides)
    return pl.pallas_call(
        kern,
        grid=(nn, nm),
        in_specs=[
            pl.BlockSpec((tm, d_ff), lambda j, i: (i, 0)),
            pl.BlockSpec((d_ff, tn), lambda j, i: (0, j)),
            pl.BlockSpec((tm, tn), lambda j, i: (i, j)),
        ] + [s.spec for s in sides],
        out_specs=[pl.BlockSpec((tm, tn), lambda j, i: (i, j))] + [s.spec for s in sides],
        out_shape=[jax.ShapeDtypeStruct((T, D), F32)] + [s.out_shape for s in sides],
        compiler_params=_cparams(("arbitrary", "arbitrary")),
        name="ffn_down",
    )(act, w_down_bf, h1, w_pg, w_pp)


def _ple_final_kernel(h2_ref, p_ref, gp_ref, gf_ref, wg_bf, wp_bf, out_ref):
    h2 = h2_ref[...]
    gate = _sigmoid(_dot(_rms(h2, gp_ref[...]).astype(BF16), wg_bf[...]))
    pp = _dot(p_ref[...].astype(BF16), wp_bf[...])
    out_ref[...] = _rms(h2 + gate * pp, gf_ref[...])


def _ple_final(h2, p2, w_pg, w_pp, g_ple, g_final, *, tm):
    T, D = h2.shape
    P = p2.shape[1]
    resident = lambda w: pl.BlockSpec(w.shape, lambda i: (0, 0), pipeline_mode=pl.Buffered(1))
    return pl.pallas_call(
        _ple_final_kernel,
        grid=(T // tm,),
        in_specs=[
            pl.BlockSpec((tm, D), lambda i: (i, 0)),
            pl.BlockSpec((tm, P), lambda i: (i, 0)),
            pl.BlockSpec((1, D), lambda i: (0, 0)),
            pl.BlockSpec((1, D), lambda i: (0, 0)),
            resident(w_pg), resident(w_pp),
        ],
        out_specs=pl.BlockSpec((tm, D), lambda i: (i, 0)),
        out_shape=jax.ShapeDtypeStruct((T, D), F32),
        compiler_params=_cparams(("parallel",)),
        name="ple_final",
    )(h2, p2, g_ple, g_final, w_pg, w_pp)


def _lane_tables(S):
    assert QK_ROPE == IDX_ROPE == 32
    scale = (QK_NOPE + QK_ROPE) ** -0.5 * math.log2(math.e)
    a = np.zeros((N_TABLES, LANES), np.float32)
    bc = np.zeros_like(a)
    bs = np.zeros_like(a)
    a[TAB_Q, 0:QK_NOPE] = scale
    bc[TAB_Q, 64:96] = scale
    bs[TAB_Q + 1, 64:80] = -scale
    bs[TAB_Q + 1, 80:96] = scale
    for o in (0, IDX_DIM):
        bc[TAB_QI, o:o + 32] = 1.0
        a[TAB_QI, o + 32:o + 64] = 1.0
        bs[TAB_QI + 1, o:o + 32] = 1.0
    bc[TAB_KI, 0:32] = 1.0
    a[TAB_KI, 32:64] = 1.0
    bs[TAB_KI + 1, 0:16] = -1.0
    bs[TAB_KI + 2, 16:32] = 1.0
    bc[TAB_KR, 0:32] = 1.0
    bs[TAB_KR + 1, 0:16] = -1.0
    bs[TAB_KR + 2, 16:32] = 1.0
    inv = ROPE_THETA ** (-jnp.arange(0, QK_ROPE, 2, dtype=F32) / QK_ROPE)
    ang = jnp.arange(S, dtype=F32)[:, None] * jnp.tile(inv, LANES // inv.shape[0])[None, :]
    return a[:, None, :] + bc[:, None, :] * jnp.cos(ang)[None] + bs[:, None, :] * jnp.sin(ang)[None]


def _tiles(S):
    return dict(
        attn_q=256,
        index_keys=min(512, S),
        a_proj_rows=min(256, S),
        uv_gate_rows=min(1024, S), uv_gate_cols=1024,
        branch_out_rows=min(256, S),
        ffn_up_rows=min(1024, S), ffn_up_cols=512,
        ffn_down_rows=min(512, S), ffn_down_cols=1024,
        ple_rows=min(512, S),
    )


def kernel(x, p, g_mix, w_in, g_cq, g_ckv, w_uq, w_uk, w_uv, w_iq, w_a_proj, g_sgu, w_spatial,
           b_spatial, w_b_proj, w_o, g_ffn, w_gu, w_down, g_ple, w_ple_gate, w_ple_proj, g_final):
    B, S, D = x.shape
    T = B * S
    depth = w_in.shape[0]
    topk = min(TOPK_MAX, S // 4)
    t = _tiles(S)
    tables = _lane_tables(S)

    h = x.reshape(T, D)
    for i in range(depth):
        wq3 = w_uq[i].reshape(Q_LORA, N_HEADS, QK_NOPE + QK_ROPE)
        x1, x2 = wq3[..., QK_NOPE:QK_NOPE + QK_ROPE // 2], wq3[..., QK_NOPE + QK_ROPE // 2:]
        wq = jnp.concatenate([wq3, x2, x1], axis=-1).reshape(Q_LORA, N_HEADS * LANES).astype(BF16)
        wk = jnp.pad(w_uk[i], ((0, 0), (0, 0), (0, LANES - QK_NOPE))).reshape(KV_LORA, N_HEADS * LANES).astype(BF16)
        wv = w_uv[i].reshape(KV_LORA, N_HEADS * V_HEAD).astype(BF16)
        wi3 = w_iq[i].reshape(Q_LORA, IDX_HEADS, IDX_DIM)
        wiq = w_iq[i].astype(BF16)
        wiqr = jnp.concatenate([-wi3[..., IDX_ROPE // 2:IDX_ROPE], wi3[..., :IDX_ROPE // 2],
                                jnp.zeros_like(wi3[..., IDX_ROPE:])], axis=-1).reshape(w_iq[i].shape).astype(BF16)

        w_t = jnp.transpose(w_in[i])
        n, q, k, v, qi, ki_lo, ki_hi, wt = _a_proj(h, g_mix[i][None], w_t, g_cq[i][None], g_ckv[i][None],
                                                    wq, wiq, wiqr, wk, wv, tables, B=B, S=S, tm=t["a_proj_rows"])
        zg, wa_bf, wb_bf, wo_bf = _uv_gate_proj(n, w_t, w_a_proj[i], w_b_proj[i], w_o[i],
                                                tm=t["uv_gate_rows"], tn=t["uv_gate_cols"])
        bias = _dsa_index(ki_lo, ki_hi, qi, wt, B=B, S=S, TQ=t["attn_q"], KCH=t["index_keys"], topk=topk)
        o = _dsa_attn(q, k, v, bias, B=B, S=S, TQ=t["attn_q"])
        h1, n2 = _branch_out(o, zg, h, g_sgu[i][None], w_spatial[i], jnp.transpose(b_spatial[i]),
                             g_ffn[i][None], wa_bf, wb_bf, wo_bf, B=B, S=S, tm=t["branch_out_rows"])
        act, w_down_bf = _ffn_up(n2, w_gu[i], w_down[i], tm=t["ffn_up_rows"], tn=t["ffn_up_cols"])
        h2, wpg_bf, wpp_bf = _ffn_down(act, w_down_bf, h1, w_ple_gate[i], w_ple_proj[i],
                                       tm=t["ffn_down_rows"], tn=t["ffn_down_cols"])
        assert depth == 1
        h = _ple_final(h2, p[i].reshape(T, -1), wpg_bf, wpp_bf, g_ple[i][None], g_final[None], tm=t["ple_rows"])
    return h.reshape(B, S, D)
```

```python
import functools
import math

import numpy as np
import jax
import jax.numpy as jnp
from jax import lax
from jax.experimental import pallas as pl
from jax.experimental.pallas import tpu as pltpu

F32 = jnp.float32
BF16 = jnp.bfloat16

N_HEADS = 16
QK_NOPE = 64
QK_ROPE = 32
V_HEAD = 64
Q_LORA = 512
KV_LORA = 256
IDX_HEADS = 16
IDX_DIM = 64
IDX_ROPE = 32
TOPK_MAX = 256
SGU_CHUNK = 128
SGU_GROUPS = 8
SGU_GROUP_DIM = 128
SGU_WIDTH = SGU_GROUPS * SGU_GROUP_DIM
ROPE_THETA = 10000.0
EPS = 1e-6

LANES = 128
IN_SMALL = Q_LORA + KV_LORA + QK_ROPE + IDX_DIM + IDX_HEADS
SMALL_COLS = 1024
COL_CKV = Q_LORA
COL_MISC = Q_LORA + KV_LORA
MISC_KIDX = QK_ROPE
MISC_WIDX = QK_ROPE + IDX_DIM
VMEM_LIMIT_BYTES = 56 * 1024 * 1024
NEG_BIAS = -1e30
F32_MAX = 3.4028234663852886e38
STEPS_PER_CHECK = 4
COUNT_CHAINS = 4


def _cparams(semantics):
    return pltpu.CompilerParams(dimension_semantics=semantics, vmem_limit_bytes=VMEM_LIMIT_BYTES)


def _rms(x, g):
    return x * lax.rsqrt(jnp.mean(x * x, axis=-1, keepdims=True) + EPS) * g


def _dot(a, b):
    return jnp.dot(a, b, preferred_element_type=F32)


def _dot_nt(a, b):
    return lax.dot_general(a, b, (((1,), (1,)), ((), ())), preferred_element_type=F32)


def _gelu_exact(x):
    return 0.5 * x * (1.0 + lax.erf(x * (1.0 / math.sqrt(2.0))))


def _sigmoid(x):
    return 0.5 * jnp.tanh(0.5 * x) + 0.5


class _SideRound:
    def __init__(self, w, first, steps, step_of):
        rows, cols = w.shape
        slab = rows // steps
        assert slab * steps == rows and slab % 16 == 0
        self.first, self.steps = first, steps
        index = lambda *g: (jnp.clip(step_of(*g) - first, 0, steps - 1), 0)
        self.spec = pl.BlockSpec((slab, cols), index)
        self.out_shape = jax.ShapeDtypeStruct((rows, cols), BF16)

    def emit(self, step, src_ref, dst_ref):
        @pl.when((step >= self.first) & (step < self.first + self.steps))
        def _():
            dst_ref[...] = src_ref[...].astype(BF16)


def _pow2_floor(n):
    return 1 << (max(int(n), 1).bit_length() - 1)


TAB_Q = 0
TAB_QI = 2
TAB_KI = 4
TAB_KR = 7
N_TABLES = 10


def _rope_block(x, cos, sin_a, sin_b):
    return x * cos + pltpu.roll(x, LANES - 16, 1) * sin_a + pltpu.roll(x, 16, 1) * sin_b


def _a_proj_kernel(x_ref, gmix_ref, ws_ref, gcq_ref, gckv_ref, wuq_ref, wiq_ref, wuk_ref, wuv_ref, tab_ref,
                   n_ref, q_ref, k_ref, v_ref, qi_ref, kilo_ref, kihi_ref, wt_ref,
                   ws_bf, wq_bf, wiq_bf, wiqr_bf, wk_bf, wv_bf):
    @pl.when(pl.program_id(0) == 0)
    def _():
        ws_bf[...] = ws_ref[...].astype(BF16)
        wiq_bf[...] = wiq_ref[...].astype(BF16)
        wv_bf[...] = wuv_ref[...].astype(BF16)
        r1 = QK_ROPE // 2
        hd = QK_NOPE + QK_ROPE
        for h in range(N_HEADS):
            src, dst = h * hd, h * LANES
            wq_bf[dst:dst + hd, :] = wuq_ref[src:src + hd, :].astype(BF16)
            wq_bf[dst + hd:dst + hd + r1, :] = wuq_ref[src + QK_NOPE + r1:src + hd, :].astype(BF16)
            wq_bf[dst + hd + r1:dst + LANES, :] = wuq_ref[src + QK_NOPE:src + QK_NOPE + r1, :].astype(BF16)
            wk_bf[dst:dst + QK_NOPE, :] = wuk_ref[h * QK_NOPE:(h + 1) * QK_NOPE, :].astype(BF16)
            wk_bf[dst + QK_NOPE:dst + LANES, :] = jnp.zeros((LANES - QK_NOPE, wk_bf.shape[1]), BF16)
        ri = IDX_ROPE // 2
        for h in range(IDX_HEADS):
            o = h * IDX_DIM
            wiqr_bf[o:o + ri, :] = (-wiq_ref[o + ri:o + 2 * ri, :]).astype(BF16)
            wiqr_bf[o + ri:o + 2 * ri, :] = wiq_ref[o:o + ri, :].astype(BF16)
            wiqr_bf[o + 2 * ri:o + IDX_DIM, :] = jnp.zeros((IDX_DIM - 2 * ri, wiqr_bf.shape[1]), BF16)

    n = _rms(x_ref[...], gmix_ref[...]).astype(BF16)
    n_ref[...] = n
    small = _dot_nt(n, ws_bf[...])
    c_q = _rms(small[:, 0:Q_LORA], gcq_ref[...]).astype(BF16)
    c_kv = _rms(small[:, COL_CKV:COL_CKV + KV_LORA], gckv_ref[...]).astype(BF16)

    q = _dot_nt(c_q, wq_bf[...])
    cq, sq = tab_ref[TAB_Q], tab_ref[TAB_Q + 1]
    for h in range(N_HEADS):
        blk = q[:, h * LANES:(h + 1) * LANES]
        q_ref[0, h] = (blk * cq + pltpu.roll(blk, LANES - QK_ROPE, 1) * sq).astype(BF16)

    ci, si = tab_ref[TAB_QI], tab_ref[TAB_QI + 1]
    qi = _dot_nt(c_q, wiq_bf[...])
    qir = _dot_nt(c_q, wiqr_bf[...])
    for hp in range(IDX_HEADS // 2):
        cols = slice(hp * LANES, (hp + 1) * LANES)
        qi_ref[:, cols] = (qi[:, cols] * ci + qir[:, cols] * si).astype(BF16)

    misc = small[:, COL_MISC:COL_MISC + LANES]
    ki_lo = _rope_block(pltpu.roll(misc, LANES - MISC_KIDX, 1),
                        tab_ref[TAB_KI], tab_ref[TAB_KI + 1], tab_ref[TAB_KI + 2])
    kilo_ref[...] = ki_lo.astype(BF16)
    kihi_ref[...] = pltpu.roll(ki_lo, IDX_DIM, 1).astype(BF16)

    k_rope = pltpu.roll(_rope_block(misc, tab_ref[TAB_KR], tab_ref[TAB_KR + 1], tab_ref[TAB_KR + 2]),
                        QK_NOPE, 1)
    k_nope = _dot_nt(c_kv, wk_bf[...])
    for h in range(N_HEADS):
        k_ref[0, h] = (k_nope[:, h * LANES:(h + 1) * LANES] + k_rope).astype(BF16)

    v = _dot_nt(c_kv, wv_bf[...])
    for p in range(N_HEADS // 2):
        v_ref[0, p] = v[:, p * LANES:(p + 1) * LANES].astype(BF16)

    w_scale = IDX_HEADS ** -0.5 * IDX_DIM ** -0.5
    wt_ref[...] = misc.T[MISC_WIDX:MISC_WIDX + IDX_HEADS, :] * w_scale


def _a_proj(x2, g_mix, w_t, g_cq, g_ckv, wuq_t, wiq_t, wuk_t, wuv_t, tables, *, B, S, tm):
    T, D = x2.shape
    nt = S // tm
    const2 = lambda i: (0, 0)
    once = lambda w: pl.BlockSpec(w.shape, const2, pipeline_mode=pl.Buffered(1))
    head_spec = lambda nh: pl.BlockSpec((1, nh, tm, LANES), lambda i: (i // nt, 0, i % nt, 0))
    return pl.pallas_call(
        _a_proj_kernel,
        grid=(T // tm,),
        in_specs=[
            pl.BlockSpec((tm, D), lambda i: (i, 0)),
            pl.BlockSpec((1, D), const2),
            pl.BlockSpec((SMALL_COLS, D), const2, pipeline_mode=pl.Buffered(1)),
            pl.BlockSpec((1, Q_LORA), const2),
            pl.BlockSpec((1, KV_LORA), const2),
            once(wuq_t), once(wiq_t), once(wuk_t), once(wuv_t),
            pl.BlockSpec((N_TABLES, tm, LANES), lambda i: (0, i % nt, 0)),
        ],
        out_specs=[
            pl.BlockSpec((tm, D), lambda i: (i, 0)),
            head_spec(N_HEADS),
            head_spec(N_HEADS),
            head_spec(N_HEADS // 2),
            pl.BlockSpec((tm, IDX_HEADS * IDX_DIM), lambda i: (i, 0)),
            pl.BlockSpec((tm, LANES), lambda i: (i, 0)),
            pl.BlockSpec((tm, LANES), lambda i: (i, 0)),
            pl.BlockSpec((IDX_HEADS, tm), lambda i: (0, i)),
        ],
        out_shape=[
            jax.ShapeDtypeStruct((T, D), BF16),
            jax.ShapeDtypeStruct((B, N_HEADS, S, LANES), BF16),
            jax.ShapeDtypeStruct((B, N_HEADS, S, LANES), BF16),
            jax.ShapeDtypeStruct((B, N_HEADS // 2, S, LANES), BF16),
            jax.ShapeDtypeStruct((T, IDX_HEADS * IDX_DIM), BF16),
            jax.ShapeDtypeStruct((T, LANES), BF16),
            jax.ShapeDtypeStruct((T, LANES), BF16),
            jax.ShapeDtypeStruct((IDX_HEADS, T), F32),
        ],
        scratch_shapes=[
            pltpu.VMEM((SMALL_COLS, D), BF16),
            pltpu.VMEM((N_HEADS * LANES, Q_LORA), BF16),
            pltpu.VMEM((IDX_HEADS * IDX_DIM, Q_LORA), BF16),
            pltpu.VMEM((IDX_HEADS * IDX_DIM, Q_LORA), BF16),
            pltpu.VMEM((N_HEADS * LANES, KV_LORA), BF16),
            pltpu.VMEM((N_HEADS * V_HEAD, KV_LORA), BF16),
        ],
        compiler_params=_cparams(("arbitrary",)),
        name="norm_a_proj",
    )(x2, g_mix, w_t, g_cq, g_ckv, wuq_t, wiq_t, wuk_t, wuv_t, tables)


def _uv_gate_kernel(n_ref, w_ref, *rest, n_z, sides):
    side_in, (out_ref, *side_out), w_bf = rest[:len(sides)], rest[len(sides):-1], rest[-1]
    j = pl.program_id(0)
    step = j * pl.num_programs(1) + pl.program_id(1)
    for side, src, dst in zip(sides, side_in, side_out):
        side.emit(step, src, dst)

    @pl.when(pl.program_id(1) == 0)
    def _():
        w_bf[...] = w_ref[...].astype(BF16)

    def proj():
        return _dot_nt(n_ref[...], w_bf[...])

    @pl.when(j < n_z)
    def _():
        out_ref[...] = _gelu_exact(proj()).astype(BF16)

    @pl.when(j >= n_z)
    def _():
        out_ref[...] = _sigmoid(proj()).astype(BF16)


def _uv_gate_proj(n, w_t, wa, wb, wo, *, tm, tn):
    T, D = n.shape
    n_z = (2 * SGU_WIDTH) // tn
    n_out = 2 * SGU_WIDTH + 2 * D
    nm = T // tm
    assert w_t.shape == (IN_SMALL + n_out, D) and IN_SMALL % 8 == 0 and tn % 8 == 0
    step_of = lambda j, i: j * nm + i
    a = _pow2_floor((n_out // tn) * nm // 4)
    sides = [_SideRound(wa, 0, a, step_of), _SideRound(wb, a, a, step_of), _SideRound(wo, 2 * a, 2 * a, step_of)]
    kern = functools.partial(_uv_gate_kernel, n_z=n_z, sides=sides)
    return pl.pallas_call(
        kern,
        grid=(n_out // tn, nm),
        in_specs=[
            pl.BlockSpec((tm, D), lambda j, i: (i, 0)),
            pl.BlockSpec((pl.Element(tn), pl.Element(D)), lambda j, i: ((IN_SMALL // 8 + j * (tn // 8)) * 8, 0)),
        ] + [s.spec for s in sides],
        out_specs=[pl.BlockSpec((tm, tn), lambda j, i: (i, j))] + [s.spec for s in sides],
        out_shape=[jax.ShapeDtypeStruct((T, n_out), BF16)] + [s.out_shape for s in sides],
        scratch_shapes=[pltpu.VMEM((tn, D), BF16)],
        compiler_params=_cparams(("arbitrary", "arbitrary")),
        name="uv_gate_proj",
    )(n, w_t, wa, wb, wo)


def _dsa_index_kernel(kilo_ref, kihi_ref, qi_ref, wt_ref, bias_ref, isc_ref, mm_ref, js_ref,
                      *, S, TQ, KCH, topk):
    j = pl.program_id(1)
    q0 = j * TQ
    nkeys = q0 + TQ
    qidx = q0 + lax.broadcasted_iota(jnp.int32, (1, TQ), 1)
    kf = float(topk)

    mm_ref[0:8, :] = jnp.full((8, TQ), jnp.inf, F32)
    mm_ref[8:16, :] = jnp.full((8, TQ), -jnp.inf, F32)
    for c in range(S // KCH):
        @pl.when(c * KCH < nkeys)
        def _(c=c):
            klo = kilo_ref[c * KCH:(c + 1) * KCH, :]
            khi = kihi_ref[c * KCH:(c + 1) * KCH, :]
            acc = jnp.zeros((KCH, TQ), F32)
            for hp in range(IDX_HEADS // 2):
                qp = qi_ref[:, hp * LANES:(hp + 1) * LANES]
                s0 = _dot_nt(klo, qp)
                s1 = _dot_nt(khi, qp)
                acc = acc + jnp.maximum(s0, 0.0) * wt_ref[2 * hp:2 * hp + 1, :]
                acc = acc + jnp.maximum(s1, 0.0) * wt_ref[2 * hp + 1:2 * hp + 2, :]
            kidx = c * KCH + lax.broadcasted_iota(jnp.int32, (KCH, TQ), 0)
            causal = kidx <= qidx
            isc_ref[c * KCH:(c + 1) * KCH, :] = jnp.where(causal, acc, -jnp.inf)
            lo_part = jnp.where(causal, acc, jnp.inf).reshape(KCH // 8, 8, TQ).min(axis=0)
            hi_part = jnp.where(causal, acc, -jnp.inf).reshape(KCH // 8, 8, TQ).max(axis=0)
            mm_ref[0:8, :] = jnp.minimum(mm_ref[0:8, :], lo_part)
            mm_ref[8:16, :] = jnp.maximum(mm_ref[8:16, :], hi_part)

    def select(nch):
        def count(pred):
            groups = []
            for lg in range(TQ // LANES):
                lanes = slice(lg * LANES, (lg + 1) * LANES)
                acc = jnp.zeros((COUNT_CHAINS, 8, LANES), F32)
                for c in range(nch):
                    ones = jnp.where(pred(isc_ref[c * TQ:(c + 1) * TQ, lanes], c * TQ, lanes), 1.0, 0.0)
                    acc = acc + ones.reshape(COUNT_CHAINS, TQ // (8 * COUNT_CHAINS), 8, LANES).sum(axis=1)
                groups.append(acc.sum(axis=0).sum(axis=0, keepdims=True))
            return jnp.concatenate(groups, axis=1)

        row_min = mm_ref[0:8, :].min(axis=0, keepdims=True)
        row_max = mm_ref[8:16, :].max(axis=0, keepdims=True)
        full = (qidx + 1) <= topk
        c_max = count(lambda blk, k0, lanes: blk >= row_max[:, lanes])
        exact0 = c_max == kf
        tie0 = c_max > kf
        settled0 = full | exact0 | tie0
        lo0 = jnp.where(full, -F32_MAX, jnp.where(settled0, row_max, row_min))
        hi0 = jnp.where(full, -F32_MAX, jnp.where(tie0, jnp.inf, row_max))
        act0 = jnp.where(settled0, 0.0, 1.0)

        def step(lo, hi, act):
            mid = lo * 0.5 + hi * 0.5
            inside = (mid > lo) & (mid < hi)
            cnt = count(lambda blk, k0, lanes: blk >= mid[:, lanes])
            upd = (act > 0.0) & inside
            found = upd & (cnt == kf)
            lo2 = jnp.where(upd & (cnt >= kf), mid, lo)
            hi2 = jnp.where(upd & (cnt <= kf), mid, hi)
            return lo2, hi2, jnp.where(upd & jnp.logical_not(found), 1.0, 0.0)

        def any_active(act):
            return (jnp.max(act) > 0.0).astype(jnp.int32)

        def body(st):
            lo, hi, act, _ = st
            for _ in range(STEPS_PER_CHECK):
                lo, hi, act = step(lo, hi, act)
            return lo, hi, act, any_active(act)

        lo, hi, _, _ = lax.while_loop(lambda st: st[3] > 0, body, (lo0, hi0, act0, any_active(act0)))

        tie = lo < hi
        js_ref[...] = jnp.full((8, TQ), -1.0, F32)

        @pl.when(jnp.max(jnp.where(tie, 1.0, 0.0)) > 0.0)
        def _():
            need = kf - count(lambda blk, k0, lanes: blk >= hi[:, lanes])

            def kpos(k0):
                return (k0 + lax.broadcasted_iota(jnp.int32, (TQ, LANES), 0)).astype(F32)

            def tie_step(_, st):
                ilo, ihi = st
                imid = jnp.floor((ilo + ihi) * 0.5)
                cnt = count(lambda blk, k0, lanes: (blk >= lo[:, lanes]) & (blk < hi[:, lanes])
                            & (kpos(k0) <= imid[:, lanes]))
                ge = cnt >= need
                return jnp.where(ge, ilo, imid), jnp.where(ge, imid, ihi)

            nsteps = int(math.ceil(math.log2(S))) + 1
            _, ihi = lax.fori_loop(0, nsteps, tie_step,
                                   (jnp.full((1, TQ), -1.0, F32), jnp.full((1, TQ), S - 1.0, F32)))
            js_ref[0:1, :] = jnp.where(tie, ihi, -1.0)

        jstar = js_ref[0:1, :]

        for c in range(S // TQ):
            if c < nch:
                blk = isc_ref[c * TQ:(c + 1) * TQ, :]
                kpos_c = (c * TQ + lax.broadcasted_iota(jnp.int32, (TQ, TQ), 0)).astype(F32)
                sel = (blk >= hi) | ((blk >= lo) & (kpos_c <= jstar))
                bias_ref[:, c * TQ:(c + 1) * TQ] = jnp.where(sel, 0.0, NEG_BIAS).T
            else:
                bias_ref[:, c * TQ:(c + 1) * TQ] = jnp.full((TQ, TQ), NEG_BIAS, F32)

    for jj in range(S // TQ):
        @pl.when(j == jj)
        def _(jj=jj):
            select(jj + 1)


def _dsa_index(ki_lo, ki_hi, qi, wt, *, B, S, TQ, KCH, topk):
    T = B * S
    nq = S // TQ
    kern = functools.partial(_dsa_index_kernel, S=S, TQ=TQ, KCH=KCH, topk=topk)
    return pl.pallas_call(
        kern,
        grid=(B, nq),
        in_specs=[
            pl.BlockSpec((S, LANES), lambda b, j: (b, 0)),
            pl.BlockSpec((S, LANES), lambda b, j: (b, 0)),
            pl.BlockSpec((TQ, IDX_HEADS * IDX_DIM), lambda b, j: (b * nq + j, 0)),
            pl.BlockSpec((IDX_HEADS, TQ), lambda b, j: (0, b * nq + j)),
        ],
        out_specs=pl.BlockSpec((TQ, S), lambda b, j: (b * nq + j, 0)),
        out_shape=jax.ShapeDtypeStruct((T, S), F32),
        scratch_shapes=[
            pltpu.VMEM((S, TQ), F32),
            pltpu.VMEM((16, TQ), F32),
            pltpu.VMEM((8, TQ), F32),
        ],
        compiler_params=_cparams(("parallel", "arbitrary")),
        name="dsa_index",
    )(ki_lo, ki_hi, qi, wt)


def _dsa_attn_kernel(q_ref, k_ref, v_ref, bias_ref, o_ref, *, S, TQ):
    j = pl.program_id(1)
    lane = lax.broadcasted_iota(jnp.int32, (TQ, LANES), 1)

    def variant(nk):
        ones = jnp.ones((nk, LANES), BF16)

        def pair(p, carry):
            vp = jnp.concatenate([v_ref[0, p, 0:nk, :], ones], axis=1)
            outs = []
            for e in range(2):
                h = 2 * p + e
                s = _dot_nt(q_ref[0, h], k_ref[0, h, 0:nk, :]) + bias_ref[:, 0:nk]
                m = s.max(axis=1, keepdims=True)
                pv = _dot(jnp.exp2(s - m).astype(BF16), vp)
                outs.append(pv[:, 0:LANES] * (1.0 / pv[:, LANES:2 * LANES]))
            o_ref[0, p] = jnp.where(lane < V_HEAD, outs[0], outs[1]).astype(BF16)
            return carry
        lax.fori_loop(0, N_HEADS // 2, pair, 0, unroll=4)

    for jj in range(S // TQ):
        @pl.when(j == jj)
        def _(jj=jj):
            variant((jj + 1) * TQ)


def _dsa_attn(q, k, v, bias, *, B, S, TQ):
    nq = S // TQ
    kern = functools.partial(_dsa_attn_kernel, S=S, TQ=TQ)
    return pl.pallas_call(
        kern,
        grid=(B, nq),
        in_specs=[
            pl.BlockSpec((1, N_HEADS, TQ, LANES), lambda b, j: (b, 0, j, 0)),
            pl.BlockSpec((1, N_HEADS, S, LANES), lambda b, j: (b, 0, 0, 0)),
            pl.BlockSpec((1, N_HEADS // 2, S, LANES), lambda b, j: (b, 0, 0, 0)),
            pl.BlockSpec((TQ, S), lambda b, j: (b * nq + j, 0)),
        ],
        out_specs=pl.BlockSpec((1, N_HEADS // 2, TQ, LANES), lambda b, j: (b, 0, j, 0)),
        out_shape=jax.ShapeDtypeStruct((B, N_HEADS // 2, S, LANES), BF16),
        compiler_params=_cparams(("parallel", "arbitrary")),
        name="dsa_attn",
    )(q, k, v, bias)


def _branch_out_kernel(o_ref, z_ref, ga_ref, gb_ref, x_ref, gs_ref, ws_ref, bt_ref, gf_ref,
                       wa_bf, wb_bf, wo_bf, h_ref, n_ref, y_scr, *, tm):
    row = lax.broadcasted_iota(jnp.int32, (SGU_CHUNK, SGU_CHUNK), 0)
    col = lax.broadcasted_iota(jnp.int32, (SGU_CHUNK, SGU_CHUNK), 1)
    tril = col <= row
    w = [jnp.where(tril, ws_ref[g], 0.0).astype(BF16) for g in range(SGU_GROUPS)]
    for cc in range(tm // SGU_CHUNK):
        rows = slice(cc * SGU_CHUNK, (cc + 1) * SGU_CHUNK)
        vn = _rms(z_ref[rows, SGU_WIDTH:2 * SGU_WIDTH].astype(F32), gs_ref[...]).astype(BF16)
        for g in range(SGU_GROUPS):
            cols = slice(g * SGU_GROUP_DIM, (g + 1) * SGU_GROUP_DIM)
            mixed = _dot(w[g], vn[:, cols]) + bt_ref[:, g:g + 1]
            y_scr[rows, cols] = (z_ref[rows, cols].astype(F32) * mixed).astype(BF16)

    o_a = jnp.concatenate([o_ref[0, p] for p in range(N_HEADS // 2)], axis=1)
    ya = _dot(o_a, wa_bf[...])
    yb = _dot(y_scr[...], wb_bf[...])
    merged = (ga_ref[...].astype(F32) * ya + gb_ref[...].astype(F32) * yb).astype(BF16)
    h = x_ref[...] + _dot(merged, wo_bf[...])
    h_ref[...] = h
    n_ref[...] = _rms(h, gf_ref[...]).astype(BF16)


def _branch_out(o, zg, x2, g_sgu, w_spatial, b_t, g_ffn, wa, wb, wo, *, B, S, tm):
    T, D = x2.shape
    nt = S // tm
    assert 2 * SGU_WIDTH == D and wa.shape == wb.shape == (SGU_WIDTH, D) and wo.shape == (D, D)
    row = lambda c: pl.BlockSpec((tm, D), lambda i: (i, c))
    const = lambda shape: pl.BlockSpec(shape, lambda i: (0,) * len(shape))
    resident = lambda w: pl.BlockSpec(w.shape, lambda i: (0, 0), pipeline_mode=pl.Buffered(1))
    kern = functools.partial(_branch_out_kernel, tm=tm)
    return pl.pallas_call(
        kern,
        grid=(T // tm,),
        in_specs=[
            pl.BlockSpec((1, N_HEADS // 2, tm, LANES), lambda i: (i // nt, 0, i % nt, 0)),
            row(0), row(1), row(2),
            row(0),
            const((1, SGU_WIDTH)),
            const((SGU_GROUPS, SGU_CHUNK, SGU_CHUNK)),
            const((SGU_CHUNK, SGU_GROUPS)),
            const((1, D)),
            resident(wa), resident(wb), resident(wo),
        ],
        out_specs=[pl.BlockSpec((tm, D), lambda i: (i, 0)), pl.BlockSpec((tm, D), lambda i: (i, 0))],
        out_shape=[jax.ShapeDtypeStruct((T, D), F32), jax.ShapeDtypeStruct((T, D), BF16)],
        scratch_shapes=[pltpu.VMEM((tm, SGU_WIDTH), BF16)],
        compiler_params=_cparams(("parallel",)),
        name="branch_out",
    )(o, zg, zg, zg, x2, g_sgu, w_spatial, b_t, g_ffn, wa, wb, wo)


def _ffn_up_kernel(n_ref, wg_ref, wu_ref, wd_ref, a_ref, wd_bf_ref, wg_bf, wu_bf):
    @pl.when(pl.program_id(1) == 0)
    def _():
        wg_bf[...] = wg_ref[...].astype(BF16)
        wu_bf[...] = wu_ref[...].astype(BF16)

    wd_bf_ref[...] = wd_ref[...].astype(BF16)
    n = n_ref[...]
    g = _dot(n, wg_bf[...])
    u = _dot(n, wu_bf[...])
    a_ref[...] = (g * _sigmoid(g) * u).astype(BF16)


def _ffn_up(n2, w_gu, w_down, *, tm, tn):
    T, D = n2.shape
    d_ff = w_down.shape[0]
    nn, nm = d_ff // tn, T // tm
    slab = d_ff // (nn * nm)
    assert slab * nn * nm == d_ff and slab % 16 == 0
    return pl.pallas_call(
        _ffn_up_kernel,
        grid=(nn, nm),
        in_specs=[
            pl.BlockSpec((tm, D), lambda j, i: (i, 0)),
            pl.BlockSpec((D, tn), lambda j, i: (0, j)),
            pl.BlockSpec((D, tn), lambda j, i: (0, nn + j)),
            pl.BlockSpec((slab, D), lambda j, i: (j * nm + i, 0)),
        ],
        out_specs=[
            pl.BlockSpec((tm, tn), lambda j, i: (i, j)),
            pl.BlockSpec((slab, D), lambda j, i: (j * nm + i, 0)),
        ],
        out_shape=[jax.ShapeDtypeStruct((T, d_ff), BF16), jax.ShapeDtypeStruct((d_ff, D), BF16)],
        scratch_shapes=[pltpu.VMEM((D, tn), BF16), pltpu.VMEM((D, tn), BF16)],
        compiler_params=_cparams(("arbitrary", "arbitrary")),
        name="ffn_up",
    )(n2, w_gu, w_gu, w_down)


def _ffn_down_kernel(a_ref, w_ref, h1_ref, wg_ref, wp_ref, h2_ref, wg_bf_ref, wp_bf_ref, *, sides):
    step = pl.program_id(0) * pl.num_programs(1) + pl.program_id(1)
    for side, src, dst in zip(sides, (wg_ref, wp_ref), (wg_bf_ref, wp_bf_ref)):
        side.emit(step, src, dst)
    h2_ref[...] = h1_ref[...] + _dot(a_ref[...], w_ref[...])


def _ffn_down(act, w_down_bf, h1, w_pg, w_pp, *, tm, tn):
    T, D = h1.shape
    d_ff = act.shape[1]
    nn, nm = D // tn, T // tm
    step_of = lambda j, i: j * nm + i
    a = _pow2_floor(nn * nm // 2)
    b = min(a, w_pp.shape[0] // 16)
    sides = [_SideRound(w_pg, 0, a, step_of), _SideRound(w_pp, a, b, step_of)]
    kern = functools.partial(_ffn_down_kernel, sides=sides)
    return pl.pallas_call(
        kern,
        grid=(nn, nm),
        in_specs=[
            pl.BlockSpec((tm, d_ff), lambda j, i: (i, 0)),
            pl.BlockSpec((d_ff, tn), lambda j, i: (0, j)),
            pl.BlockSpec((tm, tn), lambda j, i: (i, j)),
        ] + [s.spec for s in sides],
        out_specs=[pl.BlockSpec((tm, tn), lambda j, i: (i, j))] + [s.spec for s in sides],
        out_shape=[jax.ShapeDtypeStruct((T, D), F32)] + [s.out_shape for s in sides],
        compiler_params=_cparams(("arbitrary", "arbitrary")),
        name="ffn_down",
    )(act, w_down_bf, h1, w_pg, w_pp)


def _ple_final_kernel(h2_ref, p_ref, gp_ref, gf_ref, wg_bf, wp_bf, out_ref):
    h2 = h2_ref[...]
    gate = _sigmoid(_dot(_rms(h2, gp_ref[...]).astype(BF16), wg_bf[...]))
    pp = _dot(p_ref[...].astype(BF16), wp_bf[...])
    out_ref[...] = _rms(h2 + gate * pp, gf_ref[...])


def _ple_final(h2, p2, w_pg, w_pp, g_ple, g_final, *, tm):
    T, D = h2.shape
    P = p2.shape[1]
    resident = lambda w: pl.BlockSpec(w.shape, lambda i: (0, 0), pipeline_mode=pl.Buffered(1))
    return pl.pallas_call(
        _ple_final_kernel,
        grid=(T // tm,),
        in_specs=[
            pl.BlockSpec((tm, D), lambda i: (i, 0)),
            pl.BlockSpec((tm, P), lambda i: (i, 0)),
            pl.BlockSpec((1, D), lambda i: (0, 0)),
            pl.BlockSpec((1, D), lambda i: (0, 0)),
            resident(w_pg), resident(w_pp),
        ],
        out_specs=pl.BlockSpec((tm, D), lambda i: (i, 0)),
        out_shape=jax.ShapeDtypeStruct((T, D), F32),
        compiler_params=_cparams(("parallel",)),
        name="ple_final",
    )(h2, p2, g_ple, g_final, w_pg, w_pp)


def _lane_tables(S):
    assert QK_ROPE == IDX_ROPE == 32
    scale = (QK_NOPE + QK_ROPE) ** -0.5 * math.log2(math.e)
    a = np.zeros((N_TABLES, LANES), np.float32)
    bc = np.zeros_like(a)
    bs = np.zeros_like(a)
    a[TAB_Q, 0:QK_NOPE] = scale
    bc[TAB_Q, 64:96] = scale
    bs[TAB_Q + 1, 64:80] = -scale
    bs[TAB_Q + 1, 80:96] = scale
    for o in (0, IDX_DIM):
        bc[TAB_QI, o:o + 32] = 1.0
        a[TAB_QI, o + 32:o + 64] = 1.0
        bs[TAB_QI + 1, o:o + 32] = 1.0
    bc[TAB_KI, 0:32] = 1.0
    a[TAB_KI, 32:64] = 1.0
    bs[TAB_KI + 1, 0:16] = -1.0
    bs[TAB_KI + 2, 16:32] = 1.0
    bc[TAB_KR, 0:32] = 1.0
    bs[TAB_KR + 1, 0:16] = -1.0
    bs[TAB_KR + 2, 16:32] = 1.0
    inv = ROPE_THETA ** (-jnp.arange(0, QK_ROPE, 2, dtype=F32) / QK_ROPE)
    ang = jnp.arange(S, dtype=F32)[:, None] * jnp.tile(inv, LANES // inv.shape[0])[None, :]
    return a[:, None, :] + bc[:, None, :] * jnp.cos(ang)[None] + bs[:, None, :] * jnp.sin(ang)[None]


def _tiles(S):
    return dict(
        attn_q=256,
        index_keys=min(512, S),
        a_proj_rows=min(256, S),
        uv_gate_rows=min(1024, S), uv_gate_cols=1024,
        branch_out_rows=min(256, S),
        ffn_up_rows=min(1024, S), ffn_up_cols=512,
        ffn_down_rows=min(512, S), ffn_down_cols=1024,
        ple_rows=min(512, S),
    )


def kernel(x, p, g_mix, w_in, g_cq, g_ckv, w_uq, w_uk, w_uv, w_iq, w_a_proj, g_sgu, w_spatial,
           b_spatial, w_b_proj, w_o, g_ffn, w_gu, w_down, g_ple, w_ple_gate, w_ple_proj, g_final):
    B, S, D = x.shape
    T = B * S
    depth = w_in.shape[0]
    topk = min(TOPK_MAX, S // 4)
    t = _tiles(S)
    tables = _lane_tables(S)

    h = x.reshape(T, D)
    for i in range(depth):
        w_t = jnp.transpose(w_in[i])
        wuk_t = jnp.transpose(w_uk[i], (1, 2, 0)).reshape(N_HEADS * QK_NOPE, KV_LORA)
        wuv_t = jnp.transpose(w_uv[i], (1, 2, 0)).reshape(N_HEADS * V_HEAD, KV_LORA)
        n, q, k, v, qi, ki_lo, ki_hi, wt = _a_proj(h, g_mix[i][None], w_t, g_cq[i][None], g_ckv[i][None],
                                                    jnp.transpose(w_uq[i]), jnp.transpose(w_iq[i]), wuk_t, wuv_t,
                                                    tables, B=B, S=S, tm=t["a_proj_rows"])
        zg, wa_bf, wb_bf, wo_bf = _uv_gate_proj(n, w_t, w_a_proj[i], w_b_proj[i], w_o[i],
                                                tm=t["uv_gate_rows"], tn=t["uv_gate_cols"])
        bias = _dsa_index(ki_lo, ki_hi, qi, wt, B=B, S=S, TQ=t["attn_q"], KCH=t["index_keys"], topk=topk)
        o = _dsa_attn(q, k, v, bias, B=B, S=S, TQ=t["attn_q"])
        h1, n2 = _branch_out(o, zg, h, g_sgu[i][None], w_spatial[i], jnp.transpose(b_spatial[i]),
                             g_ffn[i][None], wa_bf, wb_bf, wo_bf, B=B, S=S, tm=t["branch_out_rows"])
        act, w_down_bf = _ffn_up(n2, w_gu[i], w_down[i], tm=t["ffn_up_rows"], tn=t["ffn_up_cols"])
        h2, wpg_bf, wpp_bf = _ffn_down(act, w_down_bf, h1, w_ple_gate[i], w_ple_proj[i],
                                       tm=t["ffn_down_rows"], tn=t["ffn_down_cols"])
        assert depth == 1
        h = _ple_final(h2, p[i].reshape(T, -1), wpg_bf, wpp_bf, g_ple[i][None], g_final[None], tm=t["ple_rows"])
    return h.reshape(B, S, D)
```

```python
import functools
import math

import numpy as np
import jax
import jax.numpy as jnp
from jax import lax
from jax.experimental import pallas as pl
from jax.experimental.pallas import tpu as pltpu

F32 = jnp.float32
BF16 = jnp.bfloat16

N_HEADS = 16
QK_NOPE = 64
QK_ROPE = 32
V_HEAD = 64
Q_LORA = 512
KV_LORA = 256
IDX_HEADS = 16
IDX_DIM = 64
IDX_ROPE = 32
TOPK_MAX = 256
SGU_CHUNK = 128
SGU_GROUPS = 8
SGU_GROUP_DIM = 128
SGU_WIDTH = SGU_GROUPS * SGU_GROUP_DIM
ROPE_THETA = 10000.0
EPS = 1e-6

LANES = 128
IN_SMALL = Q_LORA + KV_LORA + QK_ROPE + IDX_DIM + IDX_HEADS
SMALL_COLS = 1024
COL_CKV = Q_LORA
COL_MISC = Q_LORA + KV_LORA
MISC_KIDX = QK_ROPE
MISC_WIDX = QK_ROPE + IDX_DIM
VMEM_LIMIT_BYTES = 56 * 1024 * 1024
NEG_BIAS = -1e30
F32_MAX = 3.4028234663852886e38
STEPS_PER_CHECK = 4
COUNT_CHAINS = 4


def _cparams(semantics):
    return pltpu.CompilerParams(dimension_semantics=semantics, vmem_limit_bytes=VMEM_LIMIT_BYTES)


def _rms(x, g):
    return x * lax.rsqrt(jnp.mean(x * x, axis=-1, keepdims=True) + EPS) * g


def _dot(a, b):
    return jnp.dot(a, b, preferred_element_type=F32)


def _dot_nt(a, b):
    return lax.dot_general(a, b, (((1,), (1,)), ((), ())), preferred_element_type=F32)


def _gelu_exact(x):
    return 0.5 * x * (1.0 + lax.erf(x * (1.0 / math.sqrt(2.0))))


def _sigmoid(x):
    return 0.5 * jnp.tanh(0.5 * x) + 0.5


class _SideRound:
    def __init__(self, w, first, steps, step_of):
        rows, cols = w.shape
        slab = rows // steps
        assert slab * steps == rows and slab % 16 == 0
        self.first, self.steps = first, steps
        index = lambda *g: (jnp.clip(step_of(*g) - first, 0, steps - 1), 0)
        self.spec = pl.BlockSpec((slab, cols), index)
        self.out_shape = jax.ShapeDtypeStruct((rows, cols), BF16)

    def emit(self, step, src_ref, dst_ref):
        @pl.when((step >= self.first) & (step < self.first + self.steps))
        def _():
            dst_ref[...] = src_ref[...].astype(BF16)


def _pow2_floor(n):
    return 1 << (max(int(n), 1).bit_length() - 1)


TAB_Q = 0
TAB_QI = 2
TAB_KI = 4
TAB_KR = 7
N_TABLES = 10


def _rope_block(x, cos, sin_a, sin_b):
    return x * cos + pltpu.roll(x, LANES - 16, 1) * sin_a + pltpu.roll(x, 16, 1) * sin_b


def _a_proj_kernel(x_ref, gmix_ref, ws_ref, gcq_ref, gckv_ref, wuq_ref, wiq_ref, wuk_ref, wuv_ref, tab_ref,
                   n_ref, q_ref, k_ref, v_ref, qi_ref, kilo_ref, kihi_ref, wt_ref,
                   ws_bf, wq_bf, wiq_bf, wiqr_bf, wk_bf, wv_bf):
    @pl.when(pl.program_id(0) == 0)
    def _():
        ws_bf[...] = ws_ref[...].astype(BF16)
        wiq_bf[...] = wiq_ref[...].astype(BF16)
        wv_bf[...] = wuv_ref[...].astype(BF16)
        r1 = QK_ROPE // 2
        hd = QK_NOPE + QK_ROPE
        for h in range(N_HEADS):
            src, dst = h * hd, h * LANES
            wq_bf[dst:dst + hd, :] = wuq_ref[src:src + hd, :].astype(BF16)
            wq_bf[dst + hd:dst + hd + r1, :] = wuq_ref[src + QK_NOPE + r1:src + hd, :].astype(BF16)
            wq_bf[dst + hd + r1:dst + LANES, :] = wuq_ref[src + QK_NOPE:src + QK_NOPE + r1, :].astype(BF16)
            wk_bf[dst:dst + QK_NOPE, :] = wuk_ref[h * QK_NOPE:(h + 1) * QK_NOPE, :].astype(BF16)
            wk_bf[dst + QK_NOPE:dst + LANES, :] = jnp.zeros((LANES - QK_NOPE, wk_bf.shape[1]), BF16)
        ri = IDX_ROPE // 2
        for h in range(IDX_HEADS):
            o = h * IDX_DIM
            wiqr_bf[o:o + ri, :] = (-wiq_ref[o + ri:o + 2 * ri, :]).astype(BF16)
            wiqr_bf[o + ri:o + 2 * ri, :] = wiq_ref[o:o + ri, :].astype(BF16)
            wiqr_bf[o + 2 * ri:o + IDX_DIM, :] = jnp.zeros((IDX_DIM - 2 * ri, wiqr_bf.shape[1]), BF16)

    n = _rms(x_ref[...], gmix_ref[...]).astype(BF16)
    n_ref[...] = n
    small = _dot_nt(n, ws_bf[...])
    c_q = _rms(small[:, 0:Q_LORA], gcq_ref[...]).astype(BF16)
    c_kv = _rms(small[:, COL_CKV:COL_CKV + KV_LORA], gckv_ref[...]).astype(BF16)

    q = _dot_nt(c_q, wq_bf[...])
    cq, sq = tab_ref[TAB_Q], tab_ref[TAB_Q + 1]
    for h in range(N_HEADS):
        blk = q[:, h * LANES:(h + 1) * LANES]
        q_ref[0, h] = (blk * cq + pltpu.roll(blk, LANES - QK_ROPE, 1) * sq).astype(BF16)

    ci, si = tab_ref[TAB_QI], tab_ref[TAB_QI + 1]
    qi = _dot_nt(c_q, wiq_bf[...])
    qir = _dot_nt(c_q, wiqr_bf[...])
    for hp in range(IDX_HEADS // 2):
        cols = slice(hp * LANES, (hp + 1) * LANES)
        qi_ref[:, cols] = (qi[:, cols] * ci + qir[:, cols] * si).astype(BF16)

    misc = small[:, COL_MISC:COL_MISC + LANES]
    ki_lo = _rope_block(pltpu.roll(misc, LANES - MISC_KIDX, 1),
                        tab_ref[TAB_KI], tab_ref[TAB_KI + 1], tab_ref[TAB_KI + 2])
    kilo_ref[...] = ki_lo.astype(BF16)
    kihi_ref[...] = pltpu.roll(ki_lo, IDX_DIM, 1).astype(BF16)

    k_rope = pltpu.roll(_rope_block(misc, tab_ref[TAB_KR], tab_ref[TAB_KR + 1], tab_ref[TAB_KR + 2]),
                        QK_NOPE, 1)
    k_nope = _dot_nt(c_kv, wk_bf[...])
    for h in range(N_HEADS):
        k_ref[0, h] = (k_nope[:, h * LANES:(h + 1) * LANES] + k_rope).astype(BF16)

    v = _dot_nt(c_kv, wv_bf[...])
    for p in range(N_HEADS // 2):
        v_ref[0, p] = v[:, p * LANES:(p + 1) * LANES].astype(BF16)

    w_scale = IDX_HEADS ** -0.5 * IDX_DIM ** -0.5
    wt_ref[...] = misc.T[MISC_WIDX:MISC_WIDX + IDX_HEADS, :] * w_scale


def _a_proj(x2, g_mix, w_t, g_cq, g_ckv, wuq_t, wiq_t, wuk_t, wuv_t, tables, *, B, S, tm):
    T, D = x2.shape
    nt = S // tm
    const2 = lambda i: (0, 0)
    once = lambda w: pl.BlockSpec(w.shape, const2, pipeline_mode=pl.Buffered(1))
    head_spec = lambda nh: pl.BlockSpec((1, nh, tm, LANES), lambda i: (i // nt, 0, i % nt, 0))
    return pl.pallas_call(
        _a_proj_kernel,
        grid=(T // tm,),
        in_specs=[
            pl.BlockSpec((tm, D), lambda i: (i, 0)),
            pl.BlockSpec((1, D), const2),
            pl.BlockSpec((SMALL_COLS, D), const2, pipeline_mode=pl.Buffered(1)),
            pl.BlockSpec((1, Q_LORA), const2),
            pl.BlockSpec((1, KV_LORA), const2),
            once(wuq_t), once(wiq_t), once(wuk_t), once(wuv_t),
            pl.BlockSpec((N_TABLES, tm, LANES), lambda i: (0, i % nt, 0)),
        ],
        out_specs=[
            pl.BlockSpec((tm, D), lambda i: (i, 0)),
            head_spec(N_HEADS),
            head_spec(N_HEADS),
            head_spec(N_HEADS // 2),
            pl.BlockSpec((tm, IDX_HEADS * IDX_DIM), lambda i: (i, 0)),
            pl.BlockSpec((tm, LANES), lambda i: (i, 0)),
            pl.BlockSpec((tm, LANES), lambda i: (i, 0)),
            pl.BlockSpec((IDX_HEADS, tm), lambda i: (0, i)),
        ],
        out_shape=[
            jax.ShapeDtypeStruct((T, D), BF16),
            jax.ShapeDtypeStruct((B, N_HEADS, S, LANES), BF16),
            jax.ShapeDtypeStruct((B, N_HEADS, S, LANES), BF16),
            jax.ShapeDtypeStruct((B, N_HEADS // 2, S, LANES), BF16),
            jax.ShapeDtypeStruct((T, IDX_HEADS * IDX_DIM), BF16),
            jax.ShapeDtypeStruct((T, LANES), BF16),
            jax.ShapeDtypeStruct((T, LANES), BF16),
            jax.ShapeDtypeStruct((IDX_HEADS, T), F32),
        ],
        scratch_shapes=[
            pltpu.VMEM((SMALL_COLS, D), BF16),
            pltpu.VMEM((N_HEADS * LANES, Q_LORA), BF16),
            pltpu.VMEM((IDX_HEADS * IDX_DIM, Q_LORA), BF16),
            pltpu.VMEM((IDX_HEADS * IDX_DIM, Q_LORA), BF16),
            pltpu.VMEM((N_HEADS * LANES, KV_LORA), BF16),
            pltpu.VMEM((N_HEADS * V_HEAD, KV_LORA), BF16),
        ],
        compiler_params=_cparams(("arbitrary",)),
        name="norm_a_proj",
    )(x2, g_mix, w_t, g_cq, g_ckv, wuq_t, wiq_t, wuk_t, wuv_t, tables)


def _uv_gate_kernel(n_ref, w_ref, *rest, n_z, sides):
    side_in, (out_ref, *side_out), w_bf = rest[:len(sides)], rest[len(sides):-1], rest[-1]
    j = pl.program_id(0)
    step = j * pl.num_programs(1) + pl.program_id(1)
    for side, src, dst in zip(sides, side_in, side_out):
        side.emit(step, src, dst)

    @pl.when(pl.program_id(1) == 0)
    def _():
        w_bf[...] = w_ref[...].astype(BF16)

    def proj():
        return _dot_nt(n_ref[...], w_bf[...])

    @pl.when(j < n_z)
    def _():
        out_ref[...] = _gelu_exact(proj()).astype(BF16)

    @pl.when(j >= n_z)
    def _():
        out_ref[...] = _sigmoid(proj()).astype(BF16)


def _uv_gate_proj(n, w_t, wa, wb, wo, *, tm, tn):
    T, D = n.shape
    n_z = (2 * SGU_WIDTH) // tn
    n_out = 2 * SGU_WIDTH + 2 * D
    nm = T // tm
    assert w_t.shape == (IN_SMALL + n_out, D) and IN_SMALL % 8 == 0 and tn % 8 == 0
    step_of = lambda j, i: j * nm + i
    a = _pow2_floor((n_out // tn) * nm // 4)
    sides = [_SideRound(wa, 0, a, step_of), _SideRound(wb, a, a, step_of), _SideRound(wo, 2 * a, 2 * a, step_of)]
    kern = functools.partial(_uv_gate_kernel, n_z=n_z, sides=sides)
    return pl.pallas_call(
        kern,
        grid=(n_out // tn, nm),
        in_specs=[
            pl.BlockSpec((tm, D), lambda j, i: (i, 0)),
            pl.BlockSpec((pl.Element(tn), pl.Element(D)), lambda j, i: ((IN_SMALL // 8 + j * (tn // 8)) * 8, 0)),
        ] + [s.spec for s in sides],
        out_specs=[pl.BlockSpec((tm, tn), lambda j, i: (i, j))] + [s.spec for s in sides],
        out_shape=[jax.ShapeDtypeStruct((T, n_out), BF16)] + [s.out_shape for s in sides],
        scratch_shapes=[pltpu.VMEM((tn, D), BF16)],
        compiler_params=_cparams(("arbitrary", "arbitrary")),
        name="uv_gate_proj",
    )(n, w_t, wa, wb, wo)


def _dsa_index_kernel(kilo_ref, kihi_ref, qi_ref, wt_ref, bias_ref, isc_ref, mm_ref, js_ref,
                      *, S, TQ, KCH, topk):
    j = pl.program_id(1)
    q0 = j * TQ
    nkeys = q0 + TQ
    qidx = q0 + lax.broadcasted_iota(jnp.int32, (1, TQ), 1)
    kf = float(topk)

    mm_ref[0:8, :] = jnp.full((8, TQ), jnp.inf, F32)
    mm_ref[8:16, :] = jnp.full((8, TQ), -jnp.inf, F32)
    for c in range(S // KCH):
        @pl.when(c * KCH < nkeys)
        def _(c=c):
            klo = kilo_ref[c * KCH:(c + 1) * KCH, :]
            khi = kihi_ref[c * KCH:(c + 1) * KCH, :]
            acc = jnp.zeros((KCH, TQ), F32)
            for hp in range(IDX_HEADS // 2):
                qp = qi_ref[:, hp * LANES:(hp + 1) * LANES]
                s0 = _dot_nt(klo, qp)
                s1 = _dot_nt(khi, qp)
                acc = acc + jnp.maximum(s0, 0.0) * wt_ref[2 * hp:2 * hp + 1, :]
                acc = acc + jnp.maximum(s1, 0.0) * wt_ref[2 * hp + 1:2 * hp + 2, :]
            kidx = c * KCH + lax.broadcasted_iota(jnp.int32, (KCH, TQ), 0)
            causal = kidx <= qidx
            isc_ref[c * KCH:(c + 1) * KCH, :] = jnp.where(causal, acc, -jnp.inf)
            lo_part = jnp.where(causal, acc, jnp.inf).reshape(KCH // 8, 8, TQ).min(axis=0)
            hi_part = jnp.where(causal, acc, -jnp.inf).reshape(KCH // 8, 8, TQ).max(axis=0)
            mm_ref[0:8, :] = jnp.minimum(mm_ref[0:8, :], lo_part)
            mm_ref[8:16, :] = jnp.maximum(mm_ref[8:16, :], hi_part)

    def select(nch):
        def count(pred):
            groups = []
            for lg in range(TQ // LANES):
                lanes = slice(lg * LANES, (lg + 1) * LANES)
                acc = jnp.zeros((COUNT_CHAINS, 8, LANES), F32)
                for c in range(nch):
                    ones = jnp.where(pred(isc_ref[c * TQ:(c + 1) * TQ, lanes], c * TQ, lanes), 1.0, 0.0)
                    acc = acc + ones.reshape(COUNT_CHAINS, TQ // (8 * COUNT_CHAINS), 8, LANES).sum(axis=1)
                groups.append(acc.sum(axis=0).sum(axis=0, keepdims=True))
            return jnp.concatenate(groups, axis=1)

        row_min = mm_ref[0:8, :].min(axis=0, keepdims=True)
        row_max = mm_ref[8:16, :].max(axis=0, keepdims=True)
        full = (qidx + 1) <= topk
        c_max = count(lambda blk, k0, lanes: blk >= row_max[:, lanes])
        exact0 = c_max == kf
        tie0 = c_max > kf
        settled0 = full | exact0 | tie0
        lo0 = jnp.where(full, -F32_MAX, jnp.where(settled0, row_max, row_min))
        hi0 = jnp.where(full, -F32_MAX, jnp.where(tie0, jnp.inf, row_max))
        act0 = jnp.where(settled0, 0.0, 1.0)

        def step(lo, hi, act):
            mid = lo * 0.5 + hi * 0.5
            inside = (mid > lo) & (mid < hi)
            cnt = count(lambda blk, k0, lanes: blk >= mid[:, lanes])
            upd = (act > 0.0) & inside
            found = upd & (cnt == kf)
            lo2 = jnp.where(upd & (cnt >= kf), mid, lo)
            hi2 = jnp.where(upd & (cnt <= kf), mid, hi)
            return lo2, hi2, jnp.where(upd & jnp.logical_not(found), 1.0, 0.0)

        def any_active(act):
            return (jnp.max(act) > 0.0).astype(jnp.int32)

        def body(st):
            lo, hi, act, _ = st
            for _ in range(STEPS_PER_CHECK):
                lo, hi, act = step(lo, hi, act)
            return lo, hi, act, any_active(act)

        lo, hi, _, _ = lax.while_loop(lambda st: st[3] > 0, body, (lo0, hi0, act0, any_active(act0)))

        tie = lo < hi
        js_ref[...] = jnp.full((8, TQ), -1.0, F32)

        @pl.when(jnp.max(jnp.where(tie, 1.0, 0.0)) > 0.0)
        def _():
            need = kf - count(lambda blk, k0, lanes: blk >= hi[:, lanes])

            def kpos(k0):
                return (k0 + lax.broadcasted_iota(jnp.int32, (TQ, LANES), 0)).astype(F32)

            def tie_step(_, st):
                ilo, ihi = st
                imid = jnp.floor((ilo + ihi) * 0.5)
                cnt = count(lambda blk, k0, lanes: (blk >= lo[:, lanes]) & (blk < hi[:, lanes])
                            & (kpos(k0) <= imid[:, lanes]))
                ge = cnt >= need
                return jnp.where(ge, ilo, imid), jnp.where(ge, imid, ihi)

            nsteps = int(math.ceil(math.log2(S))) + 1
            _, ihi = lax.fori_loop(0, nsteps, tie_step,
                                   (jnp.full((1, TQ), -1.0, F32), jnp.full((1, TQ), S - 1.0, F32)))
            js_ref[0:1, :] = jnp.where(tie, ihi, -1.0)

        js_ref[1:2, :] = lo
        js_ref[2:3, :] = hi

    for jj in range(S // TQ):
        @pl.when(j == jj)
        def _(jj=jj):
            select(jj + 1)

    jstar, lo, hi = js_ref[0:1, :], js_ref[1:2, :], js_ref[2:3, :]
    for c in range(S // TQ):
        @pl.when(c <= j)
        def _(c=c):
            blk = isc_ref[c * TQ:(c + 1) * TQ, :]
            kpos_c = (c * TQ + lax.broadcasted_iota(jnp.int32, (TQ, TQ), 0)).astype(F32)
            sel = (blk >= hi) | ((blk >= lo) & (kpos_c <= jstar))
            bias_ref[:, c * TQ:(c + 1) * TQ] = jnp.where(sel, 0.0, NEG_BIAS).T

        @pl.when(c > j)
        def _(c=c):
            bias_ref[:, c * TQ:(c + 1) * TQ] = jnp.full((TQ, TQ), NEG_BIAS, F32)


def _dsa_index(ki_lo, ki_hi, qi, wt, *, B, S, TQ, KCH, topk):
    T = B * S
    nq = S // TQ
    kern = functools.partial(_dsa_index_kernel, S=S, TQ=TQ, KCH=KCH, topk=topk)
    return pl.pallas_call(
        kern,
        grid=(B, nq),
        in_specs=[
            pl.BlockSpec((S, LANES), lambda b, j: (b, 0)),
            pl.BlockSpec((S, LANES), lambda b, j: (b, 0)),
            pl.BlockSpec((TQ, IDX_HEADS * IDX_DIM), lambda b, j: (b * nq + j, 0)),
            pl.BlockSpec((IDX_HEADS, TQ), lambda b, j: (0, b * nq + j)),
        ],
        out_specs=pl.BlockSpec((TQ, S), lambda b, j: (b * nq + j, 0)),
        out_shape=jax.ShapeDtypeStruct((T, S), F32),
        scratch_shapes=[
            pltpu.VMEM((S, TQ), F32),
            pltpu.VMEM((16, TQ), F32),
            pltpu.VMEM((8, TQ), F32),
        ],
        compiler_params=_cparams(("parallel", "arbitrary")),
        name="dsa_index",
    )(ki_lo, ki_hi, qi, wt)


def _dsa_attn_kernel(q_ref, k_ref, v_ref, bias_ref, o_ref, *, S, TQ):
    j = pl.program_id(1)
    lane = lax.broadcasted_iota(jnp.int32, (TQ, LANES), 1)

    def variant(nk):
        ones = jnp.ones((nk, LANES), BF16)

        def pair(p, carry):
            vp = jnp.concatenate([v_ref[0, p, 0:nk, :], ones], axis=1)
            outs = []
            for e in range(2):
                h = 2 * p + e
                s = _dot_nt(q_ref[0, h], k_ref[0, h, 0:nk, :]) + bias_ref[:, 0:nk]
                m = s.max(axis=1, keepdims=True)
                pv = _dot(jnp.exp2(s - m).astype(BF16), vp)
                outs.append(pv[:, 0:LANES] * (1.0 / pv[:, LANES:2 * LANES]))
            o_ref[0, p] = jnp.where(lane < V_HEAD, outs[0], outs[1]).astype(BF16)
            return carry
        lax.fori_loop(0, N_HEADS // 2, pair, 0, unroll=4)

    for jj in range(S // TQ):
        @pl.when(j == jj)
        def _(jj=jj):
            variant((jj + 1) * TQ)


def _dsa_attn(q, k, v, bias, *, B, S, TQ):
    nq = S // TQ
    kern = functools.partial(_dsa_attn_kernel, S=S, TQ=TQ)
    return pl.pallas_call(
        kern,
        grid=(B, nq),
        in_specs=[
            pl.BlockSpec((1, N_HEADS, TQ, LANES), lambda b, j: (b, 0, j, 0)),
            pl.BlockSpec((1, N_HEADS, S, LANES), lambda b, j: (b, 0, 0, 0)),
            pl.BlockSpec((1, N_HEADS // 2, S, LANES), lambda b, j: (b, 0, 0, 0)),
            pl.BlockSpec((TQ, S), lambda b, j: (b * nq + j, 0)),
        ],
        out_specs=pl.BlockSpec((1, N_HEADS // 2, TQ, LANES), lambda b, j: (b, 0, j, 0)),
        out_shape=jax.ShapeDtypeStruct((B, N_HEADS // 2, S, LANES), BF16),
        compiler_params=_cparams(("parallel", "arbitrary")),
        name="dsa_attn",
    )(q, k, v, bias)


def _branch_out_kernel(o_ref, z_ref, ga_ref, gb_ref, x_ref, gs_ref, ws_ref, bt_ref, gf_ref,
                       wa_bf, wb_bf, wo_bf, h_ref, n_ref, y_scr, *, tm):
    row = lax.broadcasted_iota(jnp.int32, (SGU_CHUNK, SGU_CHUNK), 0)
    col = lax.broadcasted_iota(jnp.int32, (SGU_CHUNK, SGU_CHUNK), 1)
    tril = col <= row
    w = [jnp.where(tril, ws_ref[g], 0.0).astype(BF16) for g in range(SGU_GROUPS)]
    for cc in range(tm // SGU_CHUNK):
        rows = slice(cc * SGU_CHUNK, (cc + 1) * SGU_CHUNK)
        vn = _rms(z_ref[rows, SGU_WIDTH:2 * SGU_WIDTH].astype(F32), gs_ref[...]).astype(BF16)
        for g in range(SGU_GROUPS):
            cols = slice(g * SGU_GROUP_DIM, (g + 1) * SGU_GROUP_DIM)
            mixed = _dot(w[g], vn[:, cols]) + bt_ref[:, g:g + 1]
            y_scr[rows, cols] = (z_ref[rows, cols].astype(F32) * mixed).astype(BF16)

    o_a = jnp.concatenate([o_ref[0, p] for p in range(N_HEADS // 2)], axis=1)
    ya = _dot(o_a, wa_bf[...])
    yb = _dot(y_scr[...], wb_bf[...])
    merged = (ga_ref[...].astype(F32) * ya + gb_ref[...].astype(F32) * yb).astype(BF16)
    h = x_ref[...] + _dot(merged, wo_bf[...])
    h_ref[...] = h
    n_ref[...] = _rms(h, gf_ref[...]).astype(BF16)


def _branch_out(o, zg, x2, g_sgu, w_spatial, b_t, g_ffn, wa, wb, wo, *, B, S, tm):
    T, D = x2.shape
    nt = S // tm
    assert 2 * SGU_WIDTH == D and wa.shape == wb.shape == (SGU_WIDTH, D) and wo.shape == (D, D)
    row = lambda c: pl.BlockSpec((tm, D), lambda i: (i, c))
    const = lambda shape: pl.BlockSpec(shape, lambda i: (0,) * len(shape))
    resident = lambda w: pl.BlockSpec(w.shape, lambda i: (0, 0), pipeline_mode=pl.Buffered(1))
    kern = functools.partial(_branch_out_kernel, tm=tm)
    return pl.pallas_call(
        kern,
        grid=(T // tm,),
        in_specs=[
            pl.BlockSpec((1, N_HEADS // 2, tm, LANES), lambda i: (i // nt, 0, i % nt, 0)),
            row(0), row(1), row(2),
            row(0),
            const((1, SGU_WIDTH)),
            const((SGU_GROUPS, SGU_CHUNK, SGU_CHUNK)),
            const((SGU_CHUNK, SGU_GROUPS)),
            const((1, D)),
            resident(wa), resident(wb), resident(wo),
        ],
        out_specs=[pl.BlockSpec((tm, D), lambda i: (i, 0)), pl.BlockSpec((tm, D), lambda i: (i, 0))],
        out_shape=[jax.ShapeDtypeStruct((T, D), F32), jax.ShapeDtypeStruct((T, D), BF16)],
        scratch_shapes=[pltpu.VMEM((tm, SGU_WIDTH), BF16)],
        compiler_params=_cparams(("parallel",)),
        name="branch_out",
    )(o, zg, zg, zg, x2, g_sgu, w_spatial, b_t, g_ffn, wa, wb, wo)


def _ffn_up_kernel(n_ref, wg_ref, wu_ref, wd_ref, a_ref, wd_bf_ref, wg_bf, wu_bf):
    @pl.when(pl.program_id(1) == 0)
    def _():
        wg_bf[...] = wg_ref[...].astype(BF16)
        wu_bf[...] = wu_ref[...].astype(BF16)

    wd_bf_ref[...] = wd_ref[...].astype(BF16)
    n = n_ref[...]
    g = _dot(n, wg_bf[...])
    u = _dot(n, wu_bf[...])
    a_ref[...] = (g * _sigmoid(g) * u).astype(BF16)


def _ffn_up(n2, w_gu, w_down, *, tm, tn):
    T, D = n2.shape
    d_ff = w_down.shape[0]
    nn, nm = d_ff // tn, T // tm
    slab = d_ff // (nn * nm)
    assert slab * nn * nm == d_ff and slab % 16 == 0
    return pl.pallas_call(
        _ffn_up_kernel,
        grid=(nn, nm),
        in_specs=[
            pl.BlockSpec((tm, D), lambda j, i: (i, 0)),
            pl.BlockSpec((D, tn), lambda j, i: (0, j)),
            pl.BlockSpec((D, tn), lambda j, i: (0, nn + j)),
            pl.BlockSpec((slab, D), lambda j, i: (j * nm + i, 0)),
        ],
        out_specs=[
            pl.BlockSpec((tm, tn), lambda j, i: (i, j)),
            pl.BlockSpec((slab, D), lambda j, i: (j * nm + i, 0)),
        ],
        out_shape=[jax.ShapeDtypeStruct((T, d_ff), BF16), jax.ShapeDtypeStruct((d_ff, D), BF16)],
        scratch_shapes=[pltpu.VMEM((D, tn), BF16), pltpu.VMEM((D, tn), BF16)],
        compiler_params=_cparams(("arbitrary", "arbitrary")),
        name="ffn_up",
    )(n2, w_gu, w_gu, w_down)


def _ffn_down_kernel(a_ref, w_ref, h1_ref, wg_ref, wp_ref, h2_ref, wg_bf_ref, wp_bf_ref, *, sides):
    step = pl.program_id(0) * pl.num_programs(1) + pl.program_id(1)
    for side, src, dst in zip(sides, (wg_ref, wp_ref), (wg_bf_ref, wp_bf_ref)):
        side.emit(step, src, dst)
    h2_ref[...] = h1_ref[...] + _dot(a_ref[...], w_ref[...])


def _ffn_down(act, w_down_bf, h1, w_pg, w_pp, *, tm, tn):
    T, D = h1.shape
    d_ff = act.shape[1]
    nn, nm = D // tn, T // tm
    step_of = lambda j, i: j * nm + i
    a = _pow2_floor(nn * nm // 2)
    b = min(a, w_pp.shape[0] // 16)
    sides = [_SideRound(w_pg, 0, a, step_of), _SideRound(w_pp, a, b, step_of)]
    kern = functools.partial(_ffn_down_kernel, sides=sides)
    return pl.pallas_call(
        kern,
        grid=(nn, nm),
        in_specs=[
            pl.BlockSpec((tm, d_ff), lambda j, i: (i, 0)),
            pl.BlockSpec((d_ff, tn), lambda j, i: (0, j)),
            pl.BlockSpec((tm, tn), lambda j, i: (i, j)),
        ] + [s.spec for s in sides],
        out_specs=[pl.BlockSpec((tm, tn), lambda j, i: (i, j))] + [s.spec for s in sides],
        out_shape=[jax.ShapeDtypeStruct((T, D), F32)] + [s.out_shape for s in sides],
        compiler_params=_cparams(("arbitrary", "arbitrary")),
        name="ffn_down",
    )(act, w_down_bf, h1, w_pg, w_pp)


def _ple_final_kernel(h2_ref, p_ref, gp_ref, gf_ref, wg_bf, wp_bf, out_ref):
    h2 = h2_ref[...]
    gate = _sigmoid(_dot(_rms(h2, gp_ref[...]).astype(BF16), wg_bf[...]))
    pp = _dot(p_ref[...].astype(BF16), wp_bf[...])
    out_ref[...] = _rms(h2 + gate * pp, gf_ref[...])


def _ple_final(h2, p2, w_pg, w_pp, g_ple, g_final, *, tm):
    T, D = h2.shape
    P = p2.shape[1]
    resident = lambda w: pl.BlockSpec(w.shape, lambda i: (0, 0), pipeline_mode=pl.Buffered(1))
    return pl.pallas_call(
        _ple_final_kernel,
        grid=(T // tm,),
        in_specs=[
            pl.BlockSpec((tm, D), lambda i: (i, 0)),
            pl.BlockSpec((tm, P), lambda i: (i, 0)),
            pl.BlockSpec((1, D), lambda i: (0, 0)),
            pl.BlockSpec((1, D), lambda i: (0, 0)),
            resident(w_pg), resident(w_pp),
        ],
        out_specs=pl.BlockSpec((tm, D), lambda i: (i, 0)),
        out_shape=jax.ShapeDtypeStruct((T, D), F32),
        compiler_params=_cparams(("parallel",)),
        name="ple_final",
    )(h2, p2, g_ple, g_final, w_pg, w_pp)


def _lane_tables(S):
    assert QK_ROPE == IDX_ROPE == 32
    scale = (QK_NOPE + QK_ROPE) ** -0.5 * math.log2(math.e)
    a = np.zeros((N_TABLES, LANES), np.float32)
    bc = np.zeros_like(a)
    bs = np.zeros_like(a)
    a[TAB_Q, 0:QK_NOPE] = scale
    bc[TAB_Q, 64:96] = scale
    bs[TAB_Q + 1, 64:80] = -scale
    bs[TAB_Q + 1, 80:96] = scale
    for o in (0, IDX_DIM):
        bc[TAB_QI, o:o + 32] = 1.0
        a[TAB_QI, o + 32:o + 64] = 1.0
        bs[TAB_QI + 1, o:o + 32] = 1.0
    bc[TAB_KI, 0:32] = 1.0
    a[TAB_KI, 32:64] = 1.0
    bs[TAB_KI + 1, 0:16] = -1.0
    bs[TAB_KI + 2, 16:32] = 1.0
    bc[TAB_KR, 0:32] = 1.0
    bs[TAB_KR + 1, 0:16] = -1.0
    bs[TAB_KR + 2, 16:32] = 1.0
    inv = ROPE_THETA ** (-jnp.arange(0, QK_ROPE, 2, dtype=F32) / QK_ROPE)
    ang = jnp.arange(S, dtype=F32)[:, None] * jnp.tile(inv, LANES // inv.shape[0])[None, :]
    return a[:, None, :] + bc[:, None, :] * jnp.cos(ang)[None] + bs[:, None, :] * jnp.sin(ang)[None]


def _tiles(S):
    return dict(
        attn_q=256,
        index_keys=min(512, S),
        a_proj_rows=min(256, S),
        uv_gate_rows=min(1024, S), uv_gate_cols=1024,
        branch_out_rows=min(256, S),
        ffn_up_rows=min(1024, S), ffn_up_cols=512,
        ffn_down_rows=min(512, S), ffn_down_cols=1024,
        ple_rows=min(512, S),
    )


def kernel(x, p, g_mix, w_in, g_cq, g_ckv, w_uq, w_uk, w_uv, w_iq, w_a_proj, g_sgu, w_spatial,
           b_spatial, w_b_proj, w_o, g_ffn, w_gu, w_down, g_ple, w_ple_gate, w_ple_proj, g_final):
    B, S, D = x.shape
    T = B * S
    depth = w_in.shape[0]
    topk = min(TOPK_MAX, S // 4)
    t = _tiles(S)
    tables = _lane_tables(S)

    h = x.reshape(T, D)
    for i in range(depth):
        w_t = jnp.transpose(w_in[i])
        wuk_t = jnp.transpose(w_uk[i], (1, 2, 0)).reshape(N_HEADS * QK_NOPE, KV_LORA)
        wuv_t = jnp.transpose(w_uv[i], (1, 2, 0)).reshape(N_HEADS * V_HEAD, KV_LORA)
        n, q, k, v, qi, ki_lo, ki_hi, wt = _a_proj(h, g_mix[i][None], w_t, g_cq[i][None], g_ckv[i][None],
                                                    jnp.transpose(w_uq[i]), jnp.transpose(w_iq[i]), wuk_t, wuv_t,
                                                    tables, B=B, S=S, tm=t["a_proj_rows"])
        zg, wa_bf, wb_bf, wo_bf = _uv_gate_proj(n, w_t, w_a_proj[i], w_b_proj[i], w_o[i],
                                                tm=t["uv_gate_rows"], tn=t["uv_gate_cols"])
        bias = _dsa_index(ki_lo, ki_hi, qi, wt, B=B, S=S, TQ=t["attn_q"], KCH=t["index_keys"], topk=topk)
        o = _dsa_attn(q, k, v, bias, B=B, S=S, TQ=t["attn_q"])
        h1, n2 = _branch_out(o, zg, h, g_sgu[i][None], w_spatial[i], jnp.transpose(b_spatial[i]),
                             g_ffn[i][None], wa_bf, wb_bf, wo_bf, B=B, S=S, tm=t["branch_out_rows"])
        act, w_down_bf = _ffn_up(n2, w_gu[i], w_down[i], tm=t["ffn_up_rows"], tn=t["ffn_up_cols"])
        h2, wpg_bf, wpp_bf = _ffn_down(act, w_down_bf, h1, w_ple_gate[i], w_ple_proj[i],
                                       tm=t["ffn_down_rows"], tn=t["ffn_down_cols"])
        assert depth == 1
        h = _ple_final(h2, p[i].reshape(T, -1), wpg_bf, wpp_bf, g_ple[i][None], g_final[None], tm=t["ple_rows"])
    return h.reshape(B, S, D)
```

```python
import functools
import math

import numpy as np
import jax
import jax.numpy as jnp
from jax import lax
from jax.experimental import pallas as pl
from jax.experimental.pallas import tpu as pltpu

F32 = jnp.float32
BF16 = jnp.bfloat16

N_HEADS = 16
QK_NOPE = 64
QK_ROPE = 32
V_HEAD = 64
Q_LORA = 512
KV_LORA = 256
IDX_HEADS = 16
IDX_DIM = 64
IDX_ROPE = 32
TOPK_MAX = 256
SGU_CHUNK = 128
SGU_GROUPS = 8
SGU_GROUP_DIM = 128
SGU_WIDTH = SGU_GROUPS * SGU_GROUP_DIM
ROPE_THETA = 10000.0
EPS = 1e-6

LANES = 128
IN_SMALL = Q_LORA + KV_LORA + QK_ROPE + IDX_DIM + IDX_HEADS
SMALL_COLS = 1024
COL_CKV = Q_LORA
COL_MISC = Q_LORA + KV_LORA
MISC_KIDX = QK_ROPE
MISC_WIDX = QK_ROPE + IDX_DIM
VMEM_LIMIT_BYTES = 56 * 1024 * 1024
NEG_BIAS = -1e30
F32_MAX = 3.4028234663852886e38
STEPS_PER_CHECK = 4
COUNT_CHAINS = 4


def _cparams(semantics):
    return pltpu.CompilerParams(dimension_semantics=semantics, vmem_limit_bytes=VMEM_LIMIT_BYTES)


def _rms(x, g):
    return x * lax.rsqrt(jnp.mean(x * x, axis=-1, keepdims=True) + EPS) * g


def _dot(a, b):
    return jnp.dot(a, b, preferred_element_type=F32)


def _dot_nt(a, b):
    return lax.dot_general(a, b, (((1,), (1,)), ((), ())), preferred_element_type=F32)


def _gelu_exact(x):
    return 0.5 * x * (1.0 + lax.erf(x * (1.0 / math.sqrt(2.0))))


def _sigmoid(x):
    return 0.5 * jnp.tanh(0.5 * x) + 0.5


class _SideRound:
    def __init__(self, w, first, steps, step_of):
        rows, cols = w.shape
        slab = rows // steps
        assert slab * steps == rows and slab % 16 == 0
        self.first, self.steps = first, steps
        index = lambda *g: (jnp.clip(step_of(*g) - first, 0, steps - 1), 0)
        self.spec = pl.BlockSpec((slab, cols), index)
        self.out_shape = jax.ShapeDtypeStruct((rows, cols), BF16)

    def emit(self, step, src_ref, dst_ref):
        @pl.when((step >= self.first) & (step < self.first + self.steps))
        def _():
            dst_ref[...] = src_ref[...].astype(BF16)


def _pow2_floor(n):
    return 1 << (max(int(n), 1).bit_length() - 1)


TAB_Q = 0
TAB_QI = 2
TAB_KI = 4
TAB_KR = 7
N_TABLES = 10


def _rope_block(x, cos, sin_a, sin_b):
    return x * cos + pltpu.roll(x, LANES - 16, 1) * sin_a + pltpu.roll(x, 16, 1) * sin_b


def _a_proj_kernel(x_ref, gmix_ref, ws_ref, gcq_ref, gckv_ref, wuq_ref, wiq_ref, wuk_ref, wuv_ref, tab_ref,
                   n_ref, q_ref, k_ref, v_ref, qi_ref, kilo_ref, kihi_ref, wt_ref,
                   ws_bf, wq_bf, wiq_bf, wiqr_bf, wk_bf, wv_bf):
    @pl.when(pl.program_id(0) == 0)
    def _():
        ws_bf[...] = ws_ref[...].astype(BF16)
        wiq_bf[...] = wiq_ref[...].astype(BF16)
        wv_bf[...] = wuv_ref[...].astype(BF16)
        r1 = QK_ROPE // 2
        hd = QK_NOPE + QK_ROPE
        for h in range(N_HEADS):
            src, dst = h * hd, h * LANES
            wq_bf[dst:dst + hd, :] = wuq_ref[src:src + hd, :].astype(BF16)
            wq_bf[dst + hd:dst + hd + r1, :] = wuq_ref[src + QK_NOPE + r1:src + hd, :].astype(BF16)
            wq_bf[dst + hd + r1:dst + LANES, :] = wuq_ref[src + QK_NOPE:src + QK_NOPE + r1, :].astype(BF16)
            wk_bf[dst:dst + QK_NOPE, :] = wuk_ref[h * QK_NOPE:(h + 1) * QK_NOPE, :].astype(BF16)
            wk_bf[dst + QK_NOPE:dst + LANES, :] = jnp.zeros((LANES - QK_NOPE, wk_bf.shape[1]), BF16)
        ri = IDX_ROPE // 2
        for h in range(IDX_HEADS):
            o = h * IDX_DIM
            wiqr_bf[o:o + ri, :] = (-wiq_ref[o + ri:o + 2 * ri, :]).astype(BF16)
            wiqr_bf[o + ri:o + 2 * ri, :] = wiq_ref[o:o + ri, :].astype(BF16)
            wiqr_bf[o + 2 * ri:o + IDX_DIM, :] = jnp.zeros((IDX_DIM - 2 * ri, wiqr_bf.shape[1]), BF16)

    n = _rms(x_ref[...], gmix_ref[...]).astype(BF16)
    n_ref[...] = n
    small = _dot_nt(n, ws_bf[...])
    c_q = _rms(small[:, 0:Q_LORA], gcq_ref[...]).astype(BF16)
    c_kv = _rms(small[:, COL_CKV:COL_CKV + KV_LORA], gckv_ref[...]).astype(BF16)

    q = _dot_nt(c_q, wq_bf[...])
    cq, sq = tab_ref[TAB_Q], tab_ref[TAB_Q + 1]
    for h in range(N_HEADS):
        blk = q[:, h * LANES:(h + 1) * LANES]
        q_ref[0, h] = (blk * cq + pltpu.roll(blk, LANES - QK_ROPE, 1) * sq).astype(BF16)

    ci, si = tab_ref[TAB_QI], tab_ref[TAB_QI + 1]
    qi = _dot_nt(c_q, wiq_bf[...])
    qir = _dot_nt(c_q, wiqr_bf[...])
    for hp in range(IDX_HEADS // 2):
        cols = slice(hp * LANES, (hp + 1) * LANES)
        qi_ref[:, cols] = (qi[:, cols] * ci + qir[:, cols] * si).astype(BF16)

    misc = small[:, COL_MISC:COL_MISC + LANES]
    ki_lo = _rope_block(pltpu.roll(misc, LANES - MISC_KIDX, 1),
                        tab_ref[TAB_KI], tab_ref[TAB_KI + 1], tab_ref[TAB_KI + 2])
    kilo_ref[...] = ki_lo.astype(BF16)
    kihi_ref[...] = pltpu.roll(ki_lo, IDX_DIM, 1).astype(BF16)

    k_rope = pltpu.roll(_rope_block(misc, tab_ref[TAB_KR], tab_ref[TAB_KR + 1], tab_ref[TAB_KR + 2]),
                        QK_NOPE, 1)
    k_nope = _dot_nt(c_kv, wk_bf[...])
    for h in range(N_HEADS):
        k_ref[0, h] = (k_nope[:, h * LANES:(h + 1) * LANES] + k_rope).astype(BF16)

    v = _dot_nt(c_kv, wv_bf[...])
    for p in range(N_HEADS // 2):
        v_ref[0, p] = v[:, p * LANES:(p + 1) * LANES].astype(BF16)

    w_scale = IDX_HEADS ** -0.5 * IDX_DIM ** -0.5
    wt_ref[...] = misc.T[MISC_WIDX:MISC_WIDX + IDX_HEADS, :] * w_scale


def _a_proj(x2, g_mix, w_t, g_cq, g_ckv, wuq_t, wiq_t, wuk_t, wuv_t, tables, *, B, S, tm):
    T, D = x2.shape
    nt = S // tm
    const2 = lambda i: (0, 0)
    once = lambda w: pl.BlockSpec(w.shape, const2, pipeline_mode=pl.Buffered(1))
    head_spec = lambda nh: pl.BlockSpec((1, nh, tm, LANES), lambda i: (i // nt, 0, i % nt, 0))
    return pl.pallas_call(
        _a_proj_kernel,
        grid=(T // tm,),
        in_specs=[
            pl.BlockSpec((tm, D), lambda i: (i, 0)),
            pl.BlockSpec((1, D), const2),
            pl.BlockSpec((SMALL_COLS, D), const2, pipeline_mode=pl.Buffered(1)),
            pl.BlockSpec((1, Q_LORA), const2),
            pl.BlockSpec((1, KV_LORA), const2),
            once(wuq_t), once(wiq_t), once(wuk_t), once(wuv_t),
            pl.BlockSpec((N_TABLES, tm, LANES), lambda i: (0, i % nt, 0)),
        ],
        out_specs=[
            pl.BlockSpec((tm, D), lambda i: (i, 0)),
            head_spec(N_HEADS),
            head_spec(N_HEADS),
            head_spec(N_HEADS // 2),
            pl.BlockSpec((tm, IDX_HEADS * IDX_DIM), lambda i: (i, 0)),
            pl.BlockSpec((tm, LANES), lambda i: (i, 0)),
            pl.BlockSpec((tm, LANES), lambda i: (i, 0)),
            pl.BlockSpec((IDX_HEADS, tm), lambda i: (0, i)),
        ],
        out_shape=[
            jax.ShapeDtypeStruct((T, D), BF16),
            jax.ShapeDtypeStruct((B, N_HEADS, S, LANES), BF16),
            jax.ShapeDtypeStruct((B, N_HEADS, S, LANES), BF16),
            jax.ShapeDtypeStruct((B, N_HEADS // 2, S, LANES), BF16),
            jax.ShapeDtypeStruct((T, IDX_HEADS * IDX_DIM), BF16),
            jax.ShapeDtypeStruct((T, LANES), BF16),
            jax.ShapeDtypeStruct((T, LANES), BF16),
            jax.ShapeDtypeStruct((IDX_HEADS, T), F32),
        ],
        scratch_shapes=[
            pltpu.VMEM((SMALL_COLS, D), BF16),
            pltpu.VMEM((N_HEADS * LANES, Q_LORA), BF16),
            pltpu.VMEM((IDX_HEADS * IDX_DIM, Q_LORA), BF16),
            pltpu.VMEM((IDX_HEADS * IDX_DIM, Q_LORA), BF16),
            pltpu.VMEM((N_HEADS * LANES, KV_LORA), BF16),
            pltpu.VMEM((N_HEADS * V_HEAD, KV_LORA), BF16),
        ],
        compiler_params=_cparams(("arbitrary",)),
        name="norm_a_proj",
    )(x2, g_mix, w_t, g_cq, g_ckv, wuq_t, wiq_t, wuk_t, wuv_t, tables)


def _uv_gate_kernel(n_ref, w_ref, *rest, n_z, sides):
    side_in, (out_ref, *side_out), w_bf = rest[:len(sides)], rest[len(sides):-1], rest[-1]
    j = pl.program_id(0)
    step = j * pl.num_programs(1) + pl.program_id(1)
    for side, src, dst in zip(sides, side_in, side_out):
        side.emit(step, src, dst)

    @pl.when(pl.program_id(1) == 0)
    def _():
        w_bf[...] = w_ref[...].astype(BF16)

    def proj():
        return _dot_nt(n_ref[...], w_bf[...])

    @pl.when(j < n_z)
    def _():
        out_ref[...] = _gelu_exact(proj()).astype(BF16)

    @pl.when(j >= n_z)
    def _():
        out_ref[...] = _sigmoid(proj()).astype(BF16)


def _uv_gate_proj(n, w_t, wa, wb, wo, *, tm, tn):
    T, D = n.shape
    n_z = (2 * SGU_WIDTH) // tn
    n_out = 2 * SGU_WIDTH + 2 * D
    nm = T // tm
    assert w_t.shape == (IN_SMALL + n_out, D) and IN_SMALL % 8 == 0 and tn % 8 == 0
    step_of = lambda j, i: j * nm + i
    a = _pow2_floor((n_out // tn) * nm // 4)
    sides = [_SideRound(wa, 0, a, step_of), _SideRound(wb, a, a, step_of), _SideRound(wo, 2 * a, 2 * a, step_of)]
    kern = functools.partial(_uv_gate_kernel, n_z=n_z, sides=sides)
    return pl.pallas_call(
        kern,
        grid=(n_out // tn, nm),
        in_specs=[
            pl.BlockSpec((tm, D), lambda j, i: (i, 0)),
            pl.BlockSpec((pl.Element(tn), pl.Element(D)), lambda j, i: ((IN_SMALL // 8 + j * (tn // 8)) * 8, 0)),
        ] + [s.spec for s in sides],
        out_specs=[pl.BlockSpec((tm, tn), lambda j, i: (i, j))] + [s.spec for s in sides],
        out_shape=[jax.ShapeDtypeStruct((T, n_out), BF16)] + [s.out_shape for s in sides],
        scratch_shapes=[pltpu.VMEM((tn, D), BF16)],
        compiler_params=_cparams(("arbitrary", "arbitrary")),
        name="uv_gate_proj",
    )(n, w_t, wa, wb, wo)


def _dsa_index_kernel(kilo_ref, kihi_ref, qi_ref, wt_ref, bias_ref, isc_ref, mm_ref, js_ref,
                      *, S, TQ, KCH, topk):
    j = pl.program_id(1)
    q0 = j * TQ
    nkeys = q0 + TQ
    qidx = q0 + lax.broadcasted_iota(jnp.int32, (1, TQ), 1)
    kf = float(topk)

    mm_ref[0:8, :] = jnp.full((8, TQ), jnp.inf, F32)
    mm_ref[8:16, :] = jnp.full((8, TQ), -jnp.inf, F32)
    for c in range(S // KCH):
        @pl.when(c * KCH < nkeys)
        def _(c=c):
            klo = kilo_ref[c * KCH:(c + 1) * KCH, :]
            khi = kihi_ref[c * KCH:(c + 1) * KCH, :]
            acc = jnp.zeros((KCH, TQ), F32)
            for hp in range(IDX_HEADS // 2):
                qp = qi_ref[:, hp * LANES:(hp + 1) * LANES]
                s0 = _dot_nt(klo, qp)
                s1 = _dot_nt(khi, qp)
                acc = acc + jnp.maximum(s0, 0.0) * wt_ref[2 * hp:2 * hp + 1, :]
                acc = acc + jnp.maximum(s1, 0.0) * wt_ref[2 * hp + 1:2 * hp + 2, :]
            kidx = c * KCH + lax.broadcasted_iota(jnp.int32, (KCH, TQ), 0)
            causal = kidx <= qidx
            isc_ref[c * KCH:(c + 1) * KCH, :] = jnp.where(causal, acc, -jnp.inf)
            lo_part = jnp.where(causal, acc, jnp.inf).reshape(KCH // 8, 8, TQ).min(axis=0)
            hi_part = jnp.where(causal, acc, -jnp.inf).reshape(KCH // 8, 8, TQ).max(axis=0)
            mm_ref[0:8, :] = jnp.minimum(mm_ref[0:8, :], lo_part)
            mm_ref[8:16, :] = jnp.maximum(mm_ref[8:16, :], hi_part)

    def select(nch):
        def count(pred):
            groups = []
            for lg in range(TQ // LANES):
                lanes = slice(lg * LANES, (lg + 1) * LANES)
                acc = jnp.zeros((COUNT_CHAINS, 8, LANES), F32)
                for c in range(nch):
                    ones = jnp.where(pred(isc_ref[c * TQ:(c + 1) * TQ, lanes], c * TQ, lanes), 1.0, 0.0)
                    acc = acc + ones.reshape(COUNT_CHAINS, TQ // (8 * COUNT_CHAINS), 8, LANES).sum(axis=1)
                groups.append(acc.sum(axis=0).sum(axis=0, keepdims=True))
            return jnp.concatenate(groups, axis=1)

        row_min = mm_ref[0:8, :].min(axis=0, keepdims=True)
        row_max = mm_ref[8:16, :].max(axis=0, keepdims=True)
        full = (qidx + 1) <= topk
        c_max = count(lambda blk, k0, lanes: blk >= row_max[:, lanes])
        exact0 = c_max == kf
        tie0 = c_max > kf
        settled0 = full | exact0 | tie0
        lo0 = jnp.where(full, -F32_MAX, jnp.where(settled0, row_max, row_min))
        hi0 = jnp.where(full, -F32_MAX, jnp.where(tie0, jnp.inf, row_max))
        act0 = jnp.where(settled0, 0.0, 1.0)

        def step(lo, hi, act):
            mid = lo * 0.5 + hi * 0.5
            inside = (mid > lo) & (mid < hi)
            cnt = count(lambda blk, k0, lanes: blk >= mid[:, lanes])
            upd = (act > 0.0) & inside
            found = upd & (cnt == kf)
            lo2 = jnp.where(upd & (cnt >= kf), mid, lo)
            hi2 = jnp.where(upd & (cnt <= kf), mid, hi)
            return lo2, hi2, jnp.where(upd & jnp.logical_not(found), 1.0, 0.0)

        def any_active(act):
            return (jnp.max(act) > 0.0).astype(jnp.int32)

        def body(st):
            lo, hi, act, _ = st
            for _ in range(STEPS_PER_CHECK):
                lo, hi, act = step(lo, hi, act)
            return lo, hi, act, any_active(act)

        lo, hi, _, _ = lax.while_loop(lambda st: st[3] > 0, body, (lo0, hi0, act0, any_active(act0)))

        tie = lo < hi
        js_ref[0:8, :] = jnp.full((8, TQ), -1.0, F32)

        @pl.when(jnp.max(jnp.where(tie, 1.0, 0.0)) > 0.0)
        def _():
            need = kf - count(lambda blk, k0, lanes: blk >= hi[:, lanes])

            def kpos(k0):
                return (k0 + lax.broadcasted_iota(jnp.int32, (TQ, LANES), 0)).astype(F32)

            def tie_step(_, st):
                ilo, ihi = st
                imid = jnp.floor((ilo + ihi) * 0.5)
                cnt = count(lambda blk, k0, lanes: (blk >= lo[:, lanes]) & (blk < hi[:, lanes])
                            & (kpos(k0) <= imid[:, lanes]))
                ge = cnt >= need
                return jnp.where(ge, ilo, imid), jnp.where(ge, imid, ihi)

            nsteps = int(math.ceil(math.log2(S))) + 1
            _, ihi = lax.fori_loop(0, nsteps, tie_step,
                                   (jnp.full((1, TQ), -1.0, F32), jnp.full((1, TQ), S - 1.0, F32)))
            js_ref[0:1, :] = jnp.where(tie, ihi, -1.0)

        jstar = js_ref[0:1, :]

        for c in range(S // TQ):
            if c < nch:
                blk = isc_ref[c * TQ:(c + 1) * TQ, :]
                kpos_c = (c * TQ + lax.broadcasted_iota(jnp.int32, (TQ, TQ), 0)).astype(F32)
                sel = (blk >= hi) | ((blk >= lo) & (kpos_c <= jstar))
                bias_ref[:, c * TQ:(c + 1) * TQ] = jnp.where(sel, 0.0, NEG_BIAS).T
            else:
                bias_ref[:, c * TQ:(c + 1) * TQ] = jnp.full((TQ, TQ), NEG_BIAS, F32)

    for jj in range(S // TQ):
        @pl.when(j == jj)
        def _(jj=jj):
            select(jj + 1)


def _dsa_index(ki_lo, ki_hi, qi, wt, *, B, S, TQ, KCH, topk):
    T = B * S
    nq = S // TQ
    kern = functools.partial(_dsa_index_kernel, S=S, TQ=TQ, KCH=KCH, topk=topk)
    return pl.pallas_call(
        kern,
        grid=(B, nq),
        in_specs=[
            pl.BlockSpec((S, LANES), lambda b, j: (b, 0)),
            pl.BlockSpec((S, LANES), lambda b, j: (b, 0)),
            pl.BlockSpec((TQ, IDX_HEADS * IDX_DIM), lambda b, j: (b * nq + j, 0)),
            pl.BlockSpec((IDX_HEADS, TQ), lambda b, j: (0, b * nq + j)),
        ],
        out_specs=pl.BlockSpec((TQ, S), lambda b, j: (b * nq + j, 0)),
        out_shape=jax.ShapeDtypeStruct((T, S), F32),
        scratch_shapes=[
            pltpu.VMEM((S, TQ), F32),
            pltpu.VMEM((16, TQ), F32),
            pltpu.VMEM((16, TQ), F32),
        ],
        compiler_params=_cparams(("parallel", "arbitrary")),
        name="dsa_index",
    )(ki_lo, ki_hi, qi, wt)


def _dsa_attn_kernel(q_ref, k_ref, v_ref, bias_ref, o_ref, *, S, TQ):
    j = pl.program_id(1)
    lane = lax.broadcasted_iota(jnp.int32, (TQ, LANES), 1)

    def variant(nk):
        ones = jnp.ones((nk, LANES), BF16)

        def pair(p, carry):
            vp = jnp.concatenate([v_ref[0, p, 0:nk, :], ones], axis=1)
            outs = []
            for e in range(2):
                h = 2 * p + e
                s = _dot_nt(q_ref[0, h], k_ref[0, h, 0:nk, :]) + bias_ref[:, 0:nk]
                m = s.max(axis=1, keepdims=True)
                pv = _dot(jnp.exp2(s - m).astype(BF16), vp)
                outs.append(pv[:, 0:LANES] * (1.0 / pv[:, LANES:2 * LANES]))
            o_ref[0, p] = jnp.where(lane < V_HEAD, outs[0], outs[1]).astype(BF16)
            return carry
        lax.fori_loop(0, N_HEADS // 2, pair, 0, unroll=4)

    for jj in range(S // TQ):
        @pl.when(j == jj)
        def _(jj=jj):
            variant((jj + 1) * TQ)


def _dsa_attn(q, k, v, bias, *, B, S, TQ):
    nq = S // TQ
    kern = functools.partial(_dsa_attn_kernel, S=S, TQ=TQ)
    return pl.pallas_call(
        kern,
        grid=(B, nq),
        in_specs=[
            pl.BlockSpec((1, N_HEADS, TQ, LANES), lambda b, j: (b, 0, j, 0)),
            pl.BlockSpec((1, N_HEADS, S, LANES), lambda b, j: (b, 0, 0, 0)),
            pl.BlockSpec((1, N_HEADS // 2, S, LANES), lambda b, j: (b, 0, 0, 0)),
            pl.BlockSpec((TQ, S), lambda b, j: (b * nq + j, 0)),
        ],
        out_specs=pl.BlockSpec((1, N_HEADS // 2, TQ, LANES), lambda b, j: (b, 0, j, 0)),
        out_shape=jax.ShapeDtypeStruct((B, N_HEADS // 2, S, LANES), BF16),
        compiler_params=_cparams(("parallel", "arbitrary")),
        name="dsa_attn",
    )(q, k, v, bias)


def _branch_out_kernel(o_ref, z_ref, ga_ref, gb_ref, x_ref, gs_ref, ws_ref, bt_ref, gf_ref,
                       wa_bf, wb_bf, wo_bf, h_ref, n_ref, y_scr, *, tm):
    row = lax.broadcasted_iota(jnp.int32, (SGU_CHUNK, SGU_CHUNK), 0)
    col = lax.broadcasted_iota(jnp.int32, (SGU_CHUNK, SGU_CHUNK), 1)
    tril = col <= row
    w = [jnp.where(tril, ws_ref[g], 0.0).astype(BF16) for g in range(SGU_GROUPS)]
    for cc in range(tm // SGU_CHUNK):
        rows = slice(cc * SGU_CHUNK, (cc + 1) * SGU_CHUNK)
        vn = _rms(z_ref[rows, SGU_WIDTH:2 * SGU_WIDTH].astype(F32), gs_ref[...]).astype(BF16)
        for g in range(SGU_GROUPS):
            cols = slice(g * SGU_GROUP_DIM, (g + 1) * SGU_GROUP_DIM)
            mixed = _dot(w[g], vn[:, cols]) + bt_ref[:, g:g + 1]
            y_scr[rows, cols] = (z_ref[rows, cols].astype(F32) * mixed).astype(BF16)

    o_a = jnp.concatenate([o_ref[0, p] for p in range(N_HEADS // 2)], axis=1)
    ya = _dot(o_a, wa_bf[...])
    yb = _dot(y_scr[...], wb_bf[...])
    merged = (ga_ref[...].astype(F32) * ya + gb_ref[...].astype(F32) * yb).astype(BF16)
    h = x_ref[...] + _dot(merged, wo_bf[...])
    h_ref[...] = h
    n_ref[...] = _rms(h, gf_ref[...]).astype(BF16)


def _branch_out(o, zg, x2, g_sgu, w_spatial, b_t, g_ffn, wa, wb, wo, *, B, S, tm):
    T, D = x2.shape
    nt = S // tm
    assert 2 * SGU_WIDTH == D and wa.shape == wb.shape == (SGU_WIDTH, D) and wo.shape == (D, D)
    row = lambda c: pl.BlockSpec((tm, D), lambda i: (i, c))
    const = lambda shape: pl.BlockSpec(shape, lambda i: (0,) * len(shape))
    resident = lambda w: pl.BlockSpec(w.shape, lambda i: (0, 0), pipeline_mode=pl.Buffered(1))
    kern = functools.partial(_branch_out_kernel, tm=tm)
    return pl.pallas_call(
        kern,
        grid=(T // tm,),
        in_specs=[
            pl.BlockSpec((1, N_HEADS // 2, tm, LANES), lambda i: (i // nt, 0, i % nt, 0)),
            row(0), row(1), row(2),
            row(0),
            const((1, SGU_WIDTH)),
            const((SGU_GROUPS, SGU_CHUNK, SGU_CHUNK)),
            const((SGU_CHUNK, SGU_GROUPS)),
            const((1, D)),
            resident(wa), resident(wb), resident(wo),
        ],
        out_specs=[pl.BlockSpec((tm, D), lambda i: (i, 0)), pl.BlockSpec((tm, D), lambda i: (i, 0))],
        out_shape=[jax.ShapeDtypeStruct((T, D), F32), jax.ShapeDtypeStruct((T, D), BF16)],
        scratch_shapes=[pltpu.VMEM((tm, SGU_WIDTH), BF16)],
        compiler_params=_cparams(("parallel",)),
        name="branch_out",
    )(o, zg, zg, zg, x2, g_sgu, w_spatial, b_t, g_ffn, wa, wb, wo)


def _ffn_up_kernel(n_ref, wg_ref, wu_ref, wd_ref, a_ref, wd_bf_ref, wg_bf, wu_bf):
    @pl.when(pl.program_id(1) == 0)
    def _():
        wg_bf[...] = wg_ref[...].astype(BF16)
        wu_bf[...] = wu_ref[...].astype(BF16)

    wd_bf_ref[...] = wd_ref[...].astype(BF16)
    n = n_ref[...]
    g = _dot(n, wg_bf[...])
    u = _dot(n, wu_bf[...])
    a_ref[...] = (g * _sigmoid(g) * u).astype(BF16)


def _ffn_up(n2, w_gu, w_down, *, tm, tn):
    T, D = n2.shape
    d_ff = w_down.shape[0]
    nn, nm = d_ff // tn, T // tm
    slab = d_ff // (nn * nm)
    assert slab * nn * nm == d_ff and slab % 16 == 0
    return pl.pallas_call(
        _ffn_up_kernel,
        grid=(nn, nm),
        in_specs=[
            pl.BlockSpec((tm, D), lambda j, i: (i, 0)),
            pl.BlockSpec((D, tn), lambda j, i: (0, j)),
            pl.BlockSpec((D, tn), lambda j, i: (0, nn + j)),
            pl.BlockSpec((slab, D), lambda j, i: (j * nm + i, 0)),
        ],
        out_specs=[
            pl.BlockSpec((tm, tn), lambda j, i: (i, j)),
            pl.BlockSpec((slab, D), lambda j, i: (j * nm + i, 0)),
        ],
        out_shape=[jax.ShapeDtypeStruct((T, d_ff), BF16), jax.ShapeDtypeStruct((d_ff, D), BF16)],
        scratch_shapes=[pltpu.VMEM((D, tn), BF16), pltpu.VMEM((D, tn), BF16)],
        compiler_params=_cparams(("arbitrary", "arbitrary")),
        name="ffn_up",
    )(n2, w_gu, w_gu, w_down)


def _ffn_down_kernel(a_ref, w_ref, h1_ref, wg_ref, wp_ref, h2_ref, wg_bf_ref, wp_bf_ref, *, sides):
    step = pl.program_id(0) * pl.num_programs(1) + pl.program_id(1)
    for side, src, dst in zip(sides, (wg_ref, wp_ref), (wg_bf_ref, wp_bf_ref)):
        side.emit(step, src, dst)
    h2_ref[...] = h1_ref[...] + _dot(a_ref[...], w_ref[...])


def _ffn_down(act, w_down_bf, h1, w_pg, w_pp, *, tm, tn):
    T, D = h1.shape
    d_ff = act.shape[1]
    nn, nm = D // tn, T // tm
    step_of = lambda j, i: j * nm + i
    a = _pow2_floor(nn * nm // 2)
    b = min(a, w_pp.shape[0] // 16)
    sides = [_SideRound(w_pg, 0, a, step_of), _SideRound(w_pp, a, b, step_of)]
    kern = functools.partial(_ffn_down_kernel, sides=sides)
    return pl.pallas_call(
        kern,
        grid=(nn, nm),
        in_specs=[
            pl.BlockSpec((tm, d_ff), lambda j, i: (i, 0)),
            pl.BlockSpec((d_ff, tn), lambda j, i: (0, j)),
            pl.BlockSpec((tm, tn), lambda j, i: (i, j)),
        ] + [s.spec for s in sides],
        out_specs=[pl.BlockSpec((tm, tn), lambda j, i: (i, j))] + [s.spec for s in sides],
        out_shape=[jax.ShapeDtypeStruct((T, D), F32)] + [s.out_shape for s in sides],
        compiler_params=_cparams(("arbitrary", "arbitrary")),
        name="ffn_down",
    )(act, w_down_bf, h1, w_pg, w_pp)


def _ple_final_kernel(h2_ref, p_ref, gp_ref, gf_ref, wg_bf, wp_bf, out_ref):
    h2 = h2_ref[...]
    gate = _sigmoid(_dot(_rms(h2, gp_ref[...]).astype(BF16), wg_bf[...]))
    pp = _dot(p_ref[...].astype(BF16), wp_bf[...])
    out_ref[...] = _rms(h2 + gate * pp, gf_ref[...])


def _ple_final(h2, p2, w_pg, w_pp, g_ple, g_final, *, tm):
    T, D = h2.shape
    P = p2.shape[1]
    resident = lambda w: pl.BlockSpec(w.shape, lambda i: (0, 0), pipeline_mode=pl.Buffered(1))
    return pl.pallas_call(
        _ple_final_kernel,
        grid=(T // tm,),
        in_specs=[
            pl.BlockSpec((tm, D), lambda i: (i, 0)),
            pl.BlockSpec((tm, P), lambda i: (i, 0)),
            pl.BlockSpec((1, D), lambda i: (0, 0)),
            pl.BlockSpec((1, D), lambda i: (0, 0)),
            resident(w_pg), resident(w_pp),
        ],
        out_specs=pl.BlockSpec((tm, D), lambda i: (i, 0)),
        out_shape=jax.ShapeDtypeStruct((T, D), F32),
        compiler_params=_cparams(("parallel",)),
        name="ple_final",
    )(h2, p2, g_ple, g_final, w_pg, w_pp)


def _lane_tables(S):
    assert QK_ROPE == IDX_ROPE == 32
    scale = (QK_NOPE + QK_ROPE) ** -0.5 * math.log2(math.e)
    a = np.zeros((N_TABLES, LANES), np.float32)
    bc = np.zeros_like(a)
    bs = np.zeros_like(a)
    a[TAB_Q, 0:QK_NOPE] = scale
    bc[TAB_Q, 64:96] = scale
    bs[TAB_Q + 1, 64:80] = -scale
    bs[TAB_Q + 1, 80:96] = scale
    for o in (0, IDX_DIM):
        bc[TAB_QI, o:o + 32] = 1.0
        a[TAB_QI, o + 32:o + 64] = 1.0
        bs[TAB_QI + 1, o:o + 32] = 1.0
    bc[TAB_KI, 0:32] = 1.0
    a[TAB_KI, 32:64] = 1.0
    bs[TAB_KI + 1, 0:16] = -1.0
    bs[TAB_KI + 2, 16:32] = 1.0
    bc[TAB_KR, 0:32] = 1.0
    bs[TAB_KR + 1, 0:16] = -1.0
    bs[TAB_KR + 2, 16:32] = 1.0
    inv = ROPE_THETA ** (-jnp.arange(0, QK_ROPE, 2, dtype=F32) / QK_ROPE)
    ang = jnp.arange(S, dtype=F32)[:, None] * jnp.tile(inv, LANES // inv.shape[0])[None, :]
    return a[:, None, :] + bc[:, None, :] * jnp.cos(ang)[None] + bs[:, None, :] * jnp.sin(ang)[None]


def _tiles(S):
    return dict(
        attn_q=256,
        index_keys=min(512, S),
        a_proj_rows=min(256, S),
        uv_gate_rows=min(1024, S), uv_gate_cols=1024,
        branch_out_rows=min(256, S),
        ffn_up_rows=min(1024, S), ffn_up_cols=512,
        ffn_down_rows=min(512, S), ffn_down_cols=1024,
        ple_rows=min(512, S),
    )


def kernel(x, p, g_mix, w_in, g_cq, g_ckv, w_uq, w_uk, w_uv, w_iq, w_a_proj, g_sgu, w_spatial,
           b_spatial, w_b_proj, w_o, g_ffn, w_gu, w_down, g_ple, w_ple_gate, w_ple_proj, g_final):
    B, S, D = x.shape
    T = B * S
    depth = w_in.shape[0]
    topk = min(TOPK_MAX, S // 4)
    t = _tiles(S)
    tables = _lane_tables(S)

    h = x.reshape(T, D)
    for i in range(depth):
        w_t = jnp.transpose(w_in[i])
        wuk_t = jnp.transpose(w_uk[i], (1, 2, 0)).reshape(N_HEADS * QK_NOPE, KV_LORA)
        wuv_t = jnp.transpose(w_uv[i], (1, 2, 0)).reshape(N_HEADS * V_HEAD, KV_LORA)
        n, q, k, v, qi, ki_lo, ki_hi, wt = _a_proj(h, g_mix[i][None], w_t, g_cq[i][None], g_ckv[i][None],
                                                    jnp.transpose(w_uq[i]), jnp.transpose(w_iq[i]), wuk_t, wuv_t,
                                                    tables, B=B, S=S, tm=t["a_proj_rows"])
        zg, wa_bf, wb_bf, wo_bf = _uv_gate_proj(n, w_t, w_a_proj[i], w_b_proj[i], w_o[i],
                                                tm=t["uv_gate_rows"], tn=t["uv_gate_cols"])
        bias = _dsa_index(ki_lo, ki_hi, qi, wt, B=B, S=S, TQ=t["attn_q"], KCH=t["index_keys"], topk=topk)
        o = _dsa_attn(q, k, v, bias, B=B, S=S, TQ=t["attn_q"])
        h1, n2 = _branch_out(o, zg, h, g_sgu[i][None], w_spatial[i], jnp.transpose(b_spatial[i]),
                             g_ffn[i][None], wa_bf, wb_bf, wo_bf, B=B, S=S, tm=t["branch_out_rows"])
        act, w_down_bf = _ffn_up(n2, w_gu[i], w_down[i], tm=t["ffn_up_rows"], tn=t["ffn_up_cols"])
        h2, wpg_bf, wpp_bf = _ffn_down(act, w_down_bf, h1, w_ple_gate[i], w_ple_proj[i],
                                       tm=t["ffn_down_rows"], tn=t["ffn_down_cols"])
        assert depth == 1
        h = _ple_final(h2, p[i].reshape(T, -1), wpg_bf, wpp_bf, g_ple[i][None], g_final[None], tm=t["ple_rows"])
    return h.reshape(B, S, D)
```

```python
import functools
import math

import numpy as np
import jax
import jax.numpy as jnp
from jax import lax
from jax.experimental import pallas as pl
from jax.experimental.pallas import tpu as pltpu

F32 = jnp.float32
BF16 = jnp.bfloat16

N_HEADS = 16
QK_NOPE = 64
QK_ROPE = 32
V_HEAD = 64
Q_LORA = 512
KV_LORA = 256
IDX_HEADS = 16
IDX_DIM = 64
IDX_ROPE = 32
TOPK_MAX = 256
SGU_CHUNK = 128
SGU_GROUPS = 8
SGU_GROUP_DIM = 128
SGU_WIDTH = SGU_GROUPS * SGU_GROUP_DIM
ROPE_THETA = 10000.0
EPS = 1e-6

LANES = 128
IN_SMALL = Q_LORA + KV_LORA + QK_ROPE + IDX_DIM + IDX_HEADS
SMALL_COLS = 1024
COL_CKV = Q_LORA
COL_MISC = Q_LORA + KV_LORA
MISC_KIDX = QK_ROPE
MISC_WIDX = QK_ROPE + IDX_DIM
VMEM_LIMIT_BYTES = 56 * 1024 * 1024
NEG_BIAS = -1e30
F32_MAX = 3.4028234663852886e38
STEPS_PER_CHECK = 4
COUNT_CHAINS = 4


def _cparams(semantics):
    return pltpu.CompilerParams(dimension_semantics=semantics, vmem_limit_bytes=VMEM_LIMIT_BYTES)


def _rms(x, g):
    return x * lax.rsqrt(jnp.mean(x * x, axis=-1, keepdims=True) + EPS) * g


def _dot(a, b):
    return jnp.dot(a, b, preferred_element_type=F32)


def _dot_nt(a, b):
    return lax.dot_general(a, b, (((1,), (1,)), ((), ())), preferred_element_type=F32)


def _gelu_exact(x):
    return 0.5 * x * (1.0 + lax.erf(x * (1.0 / math.sqrt(2.0))))


def _sigmoid(x):
    return 0.5 * jnp.tanh(0.5 * x) + 0.5


class _SideRound:
    def __init__(self, w, first, steps, step_of):
        rows, cols = w.shape
        slab = rows // steps
        assert slab * steps == rows and slab % 16 == 0
        self.first, self.steps = first, steps
        index = lambda *g: (jnp.clip(step_of(*g) - first, 0, steps - 1), 0)
        self.spec = pl.BlockSpec((slab, cols), index)
        self.out_shape = jax.ShapeDtypeStruct((rows, cols), BF16)

    def emit(self, step, src_ref, dst_ref):
        @pl.when((step >= self.first) & (step < self.first + self.steps))
        def _():
            dst_ref[...] = src_ref[...].astype(BF16)


def _pow2_floor(n):
    return 1 << (max(int(n), 1).bit_length() - 1)


TAB_Q = 0
TAB_QI = 2
TAB_KI = 4
TAB_KR = 7
N_TABLES = 10


def _rope_block(x, cos, sin_a, sin_b):
    return x * cos + pltpu.roll(x, LANES - 16, 1) * sin_a + pltpu.roll(x, 16, 1) * sin_b


def _a_proj_kernel(x_ref, gmix_ref, ws_ref, gcq_ref, gckv_ref, wuq_ref, wiq_ref, wuk_ref, wuv_ref, tab_ref,
                   n_ref, q_ref, k_ref, v_ref, qi_ref, kilo_ref, kihi_ref, wt_ref,
                   ws_bf, wq_bf, wiq_bf, wiqr_bf, wk_bf, wv_bf):
    @pl.when(pl.program_id(0) == 0)
    def _():
        ws_bf[...] = ws_ref[...].astype(BF16)
        wv_bf[...] = wuv_ref[...].astype(BF16)
        wuq_t = wuq_ref[...].T
        wiq_t = wiq_ref[...].T
        wiq_bf[...] = wiq_t.astype(BF16)
        r1 = QK_ROPE // 2
        hd = QK_NOPE + QK_ROPE
        for h in range(N_HEADS):
            src, dst = h * hd, h * LANES
            wq_bf[dst:dst + hd, :] = wuq_t[src:src + hd, :].astype(BF16)
            wq_bf[dst + hd:dst + hd + r1, :] = wuq_t[src + QK_NOPE + r1:src + hd, :].astype(BF16)
            wq_bf[dst + hd + r1:dst + LANES, :] = wuq_t[src + QK_NOPE:src + QK_NOPE + r1, :].astype(BF16)
            wk_bf[dst:dst + QK_NOPE, :] = wuk_ref[h * QK_NOPE:(h + 1) * QK_NOPE, :].astype(BF16)
            wk_bf[dst + QK_NOPE:dst + LANES, :] = jnp.zeros((LANES - QK_NOPE, wk_bf.shape[1]), BF16)
        ri = IDX_ROPE // 2
        for h in range(IDX_HEADS):
            o = h * IDX_DIM
            wiqr_bf[o:o + ri, :] = (-wiq_t[o + ri:o + 2 * ri, :]).astype(BF16)
            wiqr_bf[o + ri:o + 2 * ri, :] = wiq_t[o:o + ri, :].astype(BF16)
            wiqr_bf[o + 2 * ri:o + IDX_DIM, :] = jnp.zeros((IDX_DIM - 2 * ri, wiqr_bf.shape[1]), BF16)

    n = _rms(x_ref[...], gmix_ref[...]).astype(BF16)
    n_ref[...] = n
    small = _dot_nt(n, ws_bf[...])
    c_q = _rms(small[:, 0:Q_LORA], gcq_ref[...]).astype(BF16)
    c_kv = _rms(small[:, COL_CKV:COL_CKV + KV_LORA], gckv_ref[...]).astype(BF16)

    q = _dot_nt(c_q, wq_bf[...])
    cq, sq = tab_ref[TAB_Q], tab_ref[TAB_Q + 1]
    for h in range(N_HEADS):
        blk = q[:, h * LANES:(h + 1) * LANES]
        q_ref[0, h] = (blk * cq + pltpu.roll(blk, LANES - QK_ROPE, 1) * sq).astype(BF16)

    ci, si = tab_ref[TAB_QI], tab_ref[TAB_QI + 1]
    qi = _dot_nt(c_q, wiq_bf[...])
    qir = _dot_nt(c_q, wiqr_bf[...])
    for hp in range(IDX_HEADS // 2):
        cols = slice(hp * LANES, (hp + 1) * LANES)
        qi_ref[:, cols] = (qi[:, cols] * ci + qir[:, cols] * si).astype(BF16)

    misc = small[:, COL_MISC:COL_MISC + LANES]
    ki_lo = _rope_block(pltpu.roll(misc, LANES - MISC_KIDX, 1),
                        tab_ref[TAB_KI], tab_ref[TAB_KI + 1], tab_ref[TAB_KI + 2])
    kilo_ref[...] = ki_lo.astype(BF16)
    kihi_ref[...] = pltpu.roll(ki_lo, IDX_DIM, 1).astype(BF16)

    k_rope = pltpu.roll(_rope_block(misc, tab_ref[TAB_KR], tab_ref[TAB_KR + 1], tab_ref[TAB_KR + 2]),
                        QK_NOPE, 1)
    k_nope = _dot_nt(c_kv, wk_bf[...])
    for h in range(N_HEADS):
        k_ref[0, h] = (k_nope[:, h * LANES:(h + 1) * LANES] + k_rope).astype(BF16)

    v = _dot_nt(c_kv, wv_bf[...])
    for p in range(N_HEADS // 2):
        v_ref[0, p] = v[:, p * LANES:(p + 1) * LANES].astype(BF16)

    w_scale = IDX_HEADS ** -0.5 * IDX_DIM ** -0.5
    wt_ref[...] = misc.T[MISC_WIDX:MISC_WIDX + IDX_HEADS, :] * w_scale


def _a_proj(x2, g_mix, w_t, g_cq, g_ckv, wuq, wiq, wuk_t, wuv_t, tables, *, B, S, tm):
    T, D = x2.shape
    nt = S // tm
    const2 = lambda i: (0, 0)
    once = lambda w: pl.BlockSpec(w.shape, const2, pipeline_mode=pl.Buffered(1))
    head_spec = lambda nh: pl.BlockSpec((1, nh, tm, LANES), lambda i: (i // nt, 0, i % nt, 0))
    return pl.pallas_call(
        _a_proj_kernel,
        grid=(T // tm,),
        in_specs=[
            pl.BlockSpec((tm, D), lambda i: (i, 0)),
            pl.BlockSpec((1, D), const2),
            pl.BlockSpec((SMALL_COLS, D), const2, pipeline_mode=pl.Buffered(1)),
            pl.BlockSpec((1, Q_LORA), const2),
            pl.BlockSpec((1, KV_LORA), const2),
            once(wuq), once(wiq), once(wuk_t), once(wuv_t),
            pl.BlockSpec((N_TABLES, tm, LANES), lambda i: (0, i % nt, 0)),
        ],
        out_specs=[
            pl.BlockSpec((tm, D), lambda i: (i, 0)),
            head_spec(N_HEADS),
            head_spec(N_HEADS),
            head_spec(N_HEADS // 2),
            pl.BlockSpec((tm, IDX_HEADS * IDX_DIM), lambda i: (i, 0)),
            pl.BlockSpec((tm, LANES), lambda i: (i, 0)),
            pl.BlockSpec((tm, LANES), lambda i: (i, 0)),
            pl.BlockSpec((IDX_HEADS, tm), lambda i: (0, i)),
        ],
        out_shape=[
            jax.ShapeDtypeStruct((T, D), BF16),
            jax.ShapeDtypeStruct((B, N_HEADS, S, LANES), BF16),
            jax.ShapeDtypeStruct((B, N_HEADS, S, LANES), BF16),
            jax.ShapeDtypeStruct((B, N_HEADS // 2, S, LANES), BF16),
            jax.ShapeDtypeStruct((T, IDX_HEADS * IDX_DIM), BF16),
            jax.ShapeDtypeStruct((T, LANES), BF16),
            jax.ShapeDtypeStruct((T, LANES), BF16),
            jax.ShapeDtypeStruct((IDX_HEADS, T), F32),
        ],
        scratch_shapes=[
            pltpu.VMEM((SMALL_COLS, D), BF16),
            pltpu.VMEM((N_HEADS * LANES, Q_LORA), BF16),
            pltpu.VMEM((IDX_HEADS * IDX_DIM, Q_LORA), BF16),
            pltpu.VMEM((IDX_HEADS * IDX_DIM, Q_LORA), BF16),
            pltpu.VMEM((N_HEADS * LANES, KV_LORA), BF16),
            pltpu.VMEM((N_HEADS * V_HEAD, KV_LORA), BF16),
        ],
        compiler_params=_cparams(("arbitrary",)),
        name="norm_a_proj",
    )(x2, g_mix, w_t, g_cq, g_ckv, wuq, wiq, wuk_t, wuv_t, tables)


def _uv_gate_kernel(n_ref, w_ref, *rest, n_z, sides):
    side_in, (out_ref, *side_out), w_bf = rest[:len(sides)], rest[len(sides):-1], rest[-1]
    j = pl.program_id(0)
    step = j * pl.num_programs(1) + pl.program_id(1)
    for side, src, dst in zip(sides, side_in, side_out):
        side.emit(step, src, dst)

    @pl.when(pl.program_id(1) == 0)
    def _():
        w_bf[...] = w_ref[...].astype(BF16)

    def proj():
        return _dot_nt(n_ref[...], w_bf[...])

    @pl.when(j < n_z)
    def _():
        out_ref[...] = _gelu_exact(proj()).astype(BF16)

    @pl.when(j >= n_z)
    def _():
        out_ref[...] = _sigmoid(proj()).astype(BF16)


def _uv_gate_proj(n, w_t, wa, wb, wo, *, tm, tn):
    T, D = n.shape
    n_z = (2 * SGU_WIDTH) // tn
    n_out = 2 * SGU_WIDTH + 2 * D
    nm = T // tm
    assert w_t.shape == (IN_SMALL + n_out, D) and IN_SMALL % 8 == 0 and tn % 8 == 0
    step_of = lambda j, i: j * nm + i
    a = _pow2_floor((n_out // tn) * nm // 4)
    sides = [_SideRound(wa, 0, a, step_of), _SideRound(wb, a, a, step_of), _SideRound(wo, 2 * a, 2 * a, step_of)]
    kern = functools.partial(_uv_gate_kernel, n_z=n_z, sides=sides)
    return pl.pallas_call(
        kern,
        grid=(n_out // tn, nm),
        in_specs=[
            pl.BlockSpec((tm, D), lambda j, i: (i, 0)),
            pl.BlockSpec((pl.Element(tn), pl.Element(D)), lambda j, i: ((IN_SMALL // 8 + j * (tn // 8)) * 8, 0)),
        ] + [s.spec for s in sides],
        out_specs=[pl.BlockSpec((tm, tn), lambda j, i: (i, j))] + [s.spec for s in sides],
        out_shape=[jax.ShapeDtypeStruct((T, n_out), BF16)] + [s.out_shape for s in sides],
        scratch_shapes=[pltpu.VMEM((tn, D), BF16)],
        compiler_params=_cparams(("arbitrary", "arbitrary")),
        name="uv_gate_proj",
    )(n, w_t, wa, wb, wo)


def _dsa_index_kernel(kilo_ref, kihi_ref, qi_ref, wt_ref, bias_ref, isc_ref, mm_ref, js_ref,
                      *, S, TQ, KCH, topk):
    j = pl.program_id(1)
    q0 = j * TQ
    nkeys = q0 + TQ
    qidx = q0 + lax.broadcasted_iota(jnp.int32, (1, TQ), 1)
    kf = float(topk)

    mm_ref[0:8, :] = jnp.full((8, TQ), jnp.inf, F32)
    mm_ref[8:16, :] = jnp.full((8, TQ), -jnp.inf, F32)
    for c in range(S // KCH):
        @pl.when(c * KCH < nkeys)
        def _(c=c):
            klo = kilo_ref[c * KCH:(c + 1) * KCH, :]
            khi = kihi_ref[c * KCH:(c + 1) * KCH, :]
            acc = jnp.zeros((KCH, TQ), F32)
            for hp in range(IDX_HEADS // 2):
                qp = qi_ref[:, hp * LANES:(hp + 1) * LANES]
                s0 = _dot_nt(klo, qp)
                s1 = _dot_nt(khi, qp)
                acc = acc + jnp.maximum(s0, 0.0) * wt_ref[2 * hp:2 * hp + 1, :]
                acc = acc + jnp.maximum(s1, 0.0) * wt_ref[2 * hp + 1:2 * hp + 2, :]
            kidx = c * KCH + lax.broadcasted_iota(jnp.int32, (KCH, TQ), 0)
            causal = kidx <= qidx
            isc_ref[c * KCH:(c + 1) * KCH, :] = jnp.where(causal, acc, -jnp.inf)
            lo_part = jnp.where(causal, acc, jnp.inf).reshape(KCH // 8, 8, TQ).min(axis=0)
            hi_part = jnp.where(causal, acc, -jnp.inf).reshape(KCH // 8, 8, TQ).max(axis=0)
            mm_ref[0:8, :] = jnp.minimum(mm_ref[0:8, :], lo_part)
            mm_ref[8:16, :] = jnp.maximum(mm_ref[8:16, :], hi_part)

    def select(nch):
        def count(pred):
            groups = []
            for lg in range(TQ // LANES):
                lanes = slice(lg * LANES, (lg + 1) * LANES)
                acc = jnp.zeros((COUNT_CHAINS, 8, LANES), F32)
                for c in range(nch):
                    ones = jnp.where(pred(isc_ref[c * TQ:(c + 1) * TQ, lanes], c * TQ, lanes), 1.0, 0.0)
                    acc = acc + ones.reshape(COUNT_CHAINS, TQ // (8 * COUNT_CHAINS), 8, LANES).sum(axis=1)
                groups.append(acc.sum(axis=0).sum(axis=0, keepdims=True))
            return jnp.concatenate(groups, axis=1)

        row_min = mm_ref[0:8, :].min(axis=0, keepdims=True)
        row_max = mm_ref[8:16, :].max(axis=0, keepdims=True)
        full = (qidx + 1) <= topk
        c_max = count(lambda blk, k0, lanes: blk >= row_max[:, lanes])
        exact0 = c_max == kf
        tie0 = c_max > kf
        settled0 = full | exact0 | tie0
        lo0 = jnp.where(full, -F32_MAX, jnp.where(settled0, row_max, row_min))
        hi0 = jnp.where(full, -F32_MAX, jnp.where(tie0, jnp.inf, row_max))
        act0 = jnp.where(settled0, 0.0, 1.0)

        def step(lo, hi, act):
            mid = lo * 0.5 + hi * 0.5
            inside = (mid > lo) & (mid < hi)
            cnt = count(lambda blk, k0, lanes: blk >= mid[:, lanes])
            upd = (act > 0.0) & inside
            found = upd & (cnt == kf)
            lo2 = jnp.where(upd & (cnt >= kf), mid, lo)
            hi2 = jnp.where(upd & (cnt <= kf), mid, hi)
            return lo2, hi2, jnp.where(upd & jnp.logical_not(found), 1.0, 0.0)

        def any_active(act):
            return (jnp.max(act) > 0.0).astype(jnp.int32)

        def body(st):
            lo, hi, act, _ = st
            for _ in range(STEPS_PER_CHECK):
                lo, hi, act = step(lo, hi, act)
            return lo, hi, act, any_active(act)

        lo, hi, _, _ = lax.while_loop(lambda st: st[3] > 0, body, (lo0, hi0, act0, any_active(act0)))

        tie = lo < hi
        js_ref[0:8, :] = jnp.full((8, TQ), -1.0, F32)

        @pl.when(jnp.max(jnp.where(tie, 1.0, 0.0)) > 0.0)
        def _():
            need = kf - count(lambda blk, k0, lanes: blk >= hi[:, lanes])

            def kpos(k0):
                return (k0 + lax.broadcasted_iota(jnp.int32, (TQ, LANES), 0)).astype(F32)

            def tie_step(_, st):
                ilo, ihi = st
                imid = jnp.floor((ilo + ihi) * 0.5)
                cnt = count(lambda blk, k0, lanes: (blk >= lo[:, lanes]) & (blk < hi[:, lanes])
                            & (kpos(k0) <= imid[:, lanes]))
                ge = cnt >= need
                return jnp.where(ge, ilo, imid), jnp.where(ge, imid, ihi)

            nsteps = int(math.ceil(math.log2(S))) + 1
            _, ihi = lax.fori_loop(0, nsteps, tie_step,
                                   (jnp.full((1, TQ), -1.0, F32), jnp.full((1, TQ), S - 1.0, F32)))
            js_ref[0:1, :] = jnp.where(tie, ihi, -1.0)

        jstar = js_ref[0:1, :]

        for c in range(S // TQ):
            if c < nch:
                blk = isc_ref[c * TQ:(c + 1) * TQ, :]
                kpos_c = (c * TQ + lax.broadcasted_iota(jnp.int32, (TQ, TQ), 0)).astype(F32)
                sel = (blk >= hi) | ((blk >= lo) & (kpos_c <= jstar))
                bias_ref[:, c * TQ:(c + 1) * TQ] = jnp.where(sel, 0.0, NEG_BIAS).T
            else:
                bias_ref[:, c * TQ:(c + 1) * TQ] = jnp.full((TQ, TQ), NEG_BIAS, F32)

    for jj in range(S // TQ):
        @pl.when(j == jj)
        def _(jj=jj):
            select(jj + 1)


def _dsa_index(ki_lo, ki_hi, qi, wt, *, B, S, TQ, KCH, topk):
    T = B * S
    nq = S // TQ
    kern = functools.partial(_dsa_index_kernel, S=S, TQ=TQ, KCH=KCH, topk=topk)
    return pl.pallas_call(
        kern,
        grid=(B, nq),
        in_specs=[
            pl.BlockSpec((S, LANES), lambda b, j: (b, 0)),
            pl.BlockSpec((S, LANES), lambda b, j: (b, 0)),
            pl.BlockSpec((TQ, IDX_HEADS * IDX_DIM), lambda b, j: (b * nq + j, 0)),
            pl.BlockSpec((IDX_HEADS, TQ), lambda b, j: (0, b * nq + j)),
        ],
        out_specs=pl.BlockSpec((TQ, S), lambda b, j: (b * nq + j, 0)),
        out_shape=jax.ShapeDtypeStruct((T, S), F32),
        scratch_shapes=[
            pltpu.VMEM((S, TQ), F32),
            pltpu.VMEM((16, TQ), F32),
            pltpu.VMEM((16, TQ), F32),
        ],
        compiler_params=_cparams(("parallel", "arbitrary")),
        name="dsa_index",
    )(ki_lo, ki_hi, qi, wt)


def _dsa_attn_kernel(q_ref, k_ref, v_ref, bias_ref, o_ref, *, S, TQ):
    j = pl.program_id(1)
    lane = lax.broadcasted_iota(jnp.int32, (TQ, LANES), 1)

    def variant(nk):
        ones = jnp.ones((nk, LANES), BF16)

        def pair(p, carry):
            vp = jnp.concatenate([v_ref[0, p, 0:nk, :], ones], axis=1)
            outs = []
            for e in range(2):
                h = 2 * p + e
                s = _dot_nt(q_ref[0, h], k_ref[0, h, 0:nk, :]) + bias_ref[:, 0:nk]
                m = s.max(axis=1, keepdims=True)
                pv = _dot(jnp.exp2(s - m).astype(BF16), vp)
                outs.append(pv[:, 0:LANES] * (1.0 / pv[:, LANES:2 * LANES]))
            o_ref[0, p] = jnp.where(lane < V_HEAD, outs[0], outs[1]).astype(BF16)
            return carry
        lax.fori_loop(0, N_HEADS // 2, pair, 0, unroll=4)

    for jj in range(S // TQ):
        @pl.when(j == jj)
        def _(jj=jj):
            variant((jj + 1) * TQ)


def _dsa_attn(q, k, v, bias, *, B, S, TQ):
    nq = S // TQ
    kern = functools.partial(_dsa_attn_kernel, S=S, TQ=TQ)
    return pl.pallas_call(
        kern,
        grid=(B, nq),
        in_specs=[
            pl.BlockSpec((1, N_HEADS, TQ, LANES), lambda b, j: (b, 0, j, 0)),
            pl.BlockSpec((1, N_HEADS, S, LANES), lambda b, j: (b, 0, 0, 0)),
            pl.BlockSpec((1, N_HEADS // 2, S, LANES), lambda b, j: (b, 0, 0, 0)),
            pl.BlockSpec((TQ, S), lambda b, j: (b * nq + j, 0)),
        ],
        out_specs=pl.BlockSpec((1, N_HEADS // 2, TQ, LANES), lambda b, j: (b, 0, j, 0)),
        out_shape=jax.ShapeDtypeStruct((B, N_HEADS // 2, S, LANES), BF16),
        compiler_params=_cparams(("parallel", "arbitrary")),
        name="dsa_attn",
    )(q, k, v, bias)


def _branch_out_kernel(o_ref, z_ref, ga_ref, gb_ref, x_ref, gs_ref, ws_ref, bt_ref, gf_ref,
                       wa_bf, wb_bf, wo_bf, h_ref, n_ref, y_scr, *, tm):
    row = lax.broadcasted_iota(jnp.int32, (SGU_CHUNK, SGU_CHUNK), 0)
    col = lax.broadcasted_iota(jnp.int32, (SGU_CHUNK, SGU_CHUNK), 1)
    tril = col <= row
    w = [jnp.where(tril, ws_ref[g], 0.0).astype(BF16) for g in range(SGU_GROUPS)]
    for cc in range(tm // SGU_CHUNK):
        rows = slice(cc * SGU_CHUNK, (cc + 1) * SGU_CHUNK)
        vn = _rms(z_ref[rows, SGU_WIDTH:2 * SGU_WIDTH].astype(F32), gs_ref[...]).astype(BF16)
        for g in range(SGU_GROUPS):
            cols = slice(g * SGU_GROUP_DIM, (g + 1) * SGU_GROUP_DIM)
            mixed = _dot(w[g], vn[:, cols]) + bt_ref[:, g:g + 1]
            y_scr[rows, cols] = (z_ref[rows, cols].astype(F32) * mixed).astype(BF16)

    o_a = jnp.concatenate([o_ref[0, p] for p in range(N_HEADS // 2)], axis=1)
    ya = _dot(o_a, wa_bf[...])
    yb = _dot(y_scr[...], wb_bf[...])
    merged = (ga_ref[...].astype(F32) * ya + gb_ref[...].astype(F32) * yb).astype(BF16)
    h = x_ref[...] + _dot(merged, wo_bf[...])
    h_ref[...] = h
    n_ref[...] = _rms(h, gf_ref[...]).astype(BF16)


def _branch_out(o, zg, x2, g_sgu, w_spatial, b_t, g_ffn, wa, wb, wo, *, B, S, tm):
    T, D = x2.shape
    nt = S // tm
    assert 2 * SGU_WIDTH == D and wa.shape == wb.shape == (SGU_WIDTH, D) and wo.shape == (D, D)
    row = lambda c: pl.BlockSpec((tm, D), lambda i: (i, c))
    const = lambda shape: pl.BlockSpec(shape, lambda i: (0,) * len(shape))
    resident = lambda w: pl.BlockSpec(w.shape, lambda i: (0, 0), pipeline_mode=pl.Buffered(1))
    kern = functools.partial(_branch_out_kernel, tm=tm)
    return pl.pallas_call(
        kern,
        grid=(T // tm,),
        in_specs=[
            pl.BlockSpec((1, N_HEADS // 2, tm, LANES), lambda i: (i // nt, 0, i % nt, 0)),
            row(0), row(1), row(2),
            row(0),
            const((1, SGU_WIDTH)),
            const((SGU_GROUPS, SGU_CHUNK, SGU_CHUNK)),
            const((SGU_CHUNK, SGU_GROUPS)),
            const((1, D)),
            resident(wa), resident(wb), resident(wo),
        ],
        out_specs=[pl.BlockSpec((tm, D), lambda i: (i, 0)), pl.BlockSpec((tm, D), lambda i: (i, 0))],
        out_shape=[jax.ShapeDtypeStruct((T, D), F32), jax.ShapeDtypeStruct((T, D), BF16)],
        scratch_shapes=[pltpu.VMEM((tm, SGU_WIDTH), BF16)],
        compiler_params=_cparams(("parallel",)),
        name="branch_out",
    )(o, zg, zg, zg, x2, g_sgu, w_spatial, b_t, g_ffn, wa, wb, wo)


def _ffn_up_kernel(n_ref, wg_ref, wu_ref, wd_ref, a_ref, wd_bf_ref, wg_bf, wu_bf):
    @pl.when(pl.program_id(1) == 0)
    def _():
        wg_bf[...] = wg_ref[...].astype(BF16)
        wu_bf[...] = wu_ref[...].astype(BF16)

    wd_bf_ref[...] = wd_ref[...].astype(BF16)
    n = n_ref[...]
    g = _dot(n, wg_bf[...])
    u = _dot(n, wu_bf[...])
    a_ref[...] = (g * _sigmoid(g) * u).astype(BF16)


def _ffn_up(n2, w_gu, w_down, *, tm, tn):
    T, D = n2.shape
    d_ff = w_down.shape[0]
    nn, nm = d_ff // tn, T // tm
    slab = d_ff // (nn * nm)
    assert slab * nn * nm == d_ff and slab % 16 == 0
    return pl.pallas_call(
        _ffn_up_kernel,
        grid=(nn, nm),
        in_specs=[
            pl.BlockSpec((tm, D), lambda j, i: (i, 0)),
            pl.BlockSpec((D, tn), lambda j, i: (0, j)),
            pl.BlockSpec((D, tn), lambda j, i: (0, nn + j)),
            pl.BlockSpec((slab, D), lambda j, i: (j * nm + i, 0)),
        ],
        out_specs=[
            pl.BlockSpec((tm, tn), lambda j, i: (i, j)),
            pl.BlockSpec((slab, D), lambda j, i: (j * nm + i, 0)),
        ],
        out_shape=[jax.ShapeDtypeStruct((T, d_ff), BF16), jax.ShapeDtypeStruct((d_ff, D), BF16)],
        scratch_shapes=[pltpu.VMEM((D, tn), BF16), pltpu.VMEM((D, tn), BF16)],
        compiler_params=_cparams(("arbitrary", "arbitrary")),
        name="ffn_up",
    )(n2, w_gu, w_gu, w_down)


def _ffn_down_kernel(a_ref, w_ref, h1_ref, wg_ref, wp_ref, h2_ref, wg_bf_ref, wp_bf_ref, *, sides):
    step = pl.program_id(0) * pl.num_programs(1) + pl.program_id(1)
    for side, src, dst in zip(sides, (wg_ref, wp_ref), (wg_bf_ref, wp_bf_ref)):
        side.emit(step, src, dst)
    h2_ref[...] = h1_ref[...] + _dot(a_ref[...], w_ref[...])


def _ffn_down(act, w_down_bf, h1, w_pg, w_pp, *, tm, tn):
    T, D = h1.shape
    d_ff = act.shape[1]
    nn, nm = D // tn, T // tm
    step_of = lambda j, i: j * nm + i
    a = _pow2_floor(nn * nm // 2)
    b = min(a, w_pp.shape[0] // 16)
    sides = [_SideRound(w_pg, 0, a, step_of), _SideRound(w_pp, a, b, step_of)]
    kern = functools.partial(_ffn_down_kernel, sides=sides)
    return pl.pallas_call(
        kern,
        grid=(nn, nm),
        in_specs=[
            pl.BlockSpec((tm, d_ff), lambda j, i: (i, 0)),
            pl.BlockSpec((d_ff, tn), lambda j, i: (0, j)),
            pl.BlockSpec((tm, tn), lambda j, i: (i, j)),
        ] + [s.spec for s in sides],
        out_specs=[pl.BlockSpec((tm, tn), lambda j, i: (i, j))] + [s.spec for s in sides],
        out_shape=[jax.ShapeDtypeStruct((T, D), F32)] + [s.out_shape for s in sides],
        compiler_params=_cparams(("arbitrary", "arbitrary")),
        name="ffn_down",
    )(act, w_down_bf, h1, w_pg, w_pp)


def _ple_final_kernel(h2_ref, p_ref, gp_ref, gf_ref, wg_bf, wp_bf, out_ref):
    h2 = h2_ref[...]
    gate = _sigmoid(_dot(_rms(h2, gp_ref[...]).astype(BF16), wg_bf[...]))
    pp = _dot(p_ref[...].astype(BF16), wp_bf[...])
    out_ref[...] = _rms(h2 + gate * pp, gf_ref[...])


def _ple_final(h2, p2, w_pg, w_pp, g_ple, g_final, *, tm):
    T, D = h2.shape
    P = p2.shape[1]
    resident = lambda w: pl.BlockSpec(w.shape, lambda i: (0, 0), pipeline_mode=pl.Buffered(1))
    return pl.pallas_call(
        _ple_final_kernel,
        grid=(T // tm,),
        in_specs=[
            pl.BlockSpec((tm, D), lambda i: (i, 0)),
            pl.BlockSpec((tm, P), lambda i: (i, 0)),
            pl.BlockSpec((1, D), lambda i: (0, 0)),
            pl.BlockSpec((1, D), lambda i: (0, 0)),
            resident(w_pg), resident(w_pp),
        ],
        out_specs=pl.BlockSpec((tm, D), lambda i: (i, 0)),
        out_shape=jax.ShapeDtypeStruct((T, D), F32),
        compiler_params=_cparams(("parallel",)),
        name="ple_final",
    )(h2, p2, g_ple, g_final, w_pg, w_pp)


def _lane_tables(S):
    assert QK_ROPE == IDX_ROPE == 32
    scale = (QK_NOPE + QK_ROPE) ** -0.5 * math.log2(math.e)
    a = np.zeros((N_TABLES, LANES), np.float32)
    bc = np.zeros_like(a)
    bs = np.zeros_like(a)
    a[TAB_Q, 0:QK_NOPE] = scale
    bc[TAB_Q, 64:96] = scale
    bs[TAB_Q + 1, 64:80] = -scale
    bs[TAB_Q + 1, 80:96] = scale
    for o in (0, IDX_DIM):
        bc[TAB_QI, o:o + 32] = 1.0
        a[TAB_QI, o + 32:o + 64] = 1.0
        bs[TAB_QI + 1, o:o + 32] = 1.0
    bc[TAB_KI, 0:32] = 1.0
    a[TAB_KI, 32:64] = 1.0
    bs[TAB_KI + 1, 0:16] = -1.0
    bs[TAB_KI + 2, 16:32] = 1.0
    bc[TAB_KR, 0:32] = 1.0
    bs[TAB_KR + 1, 0:16] = -1.0
    bs[TAB_KR + 2, 16:32] = 1.0
    inv = ROPE_THETA ** (-jnp.arange(0, QK_ROPE, 2, dtype=F32) / QK_ROPE)
    ang = jnp.arange(S, dtype=F32)[:, None] * jnp.tile(inv, LANES // inv.shape[0])[None, :]
    return a[:, None, :] + bc[:, None, :] * jnp.cos(ang)[None] + bs[:, None, :] * jnp.sin(ang)[None]


def _tiles(S):
    return dict(
        attn_q=256,
        index_keys=min(512, S),
        a_proj_rows=min(256, S),
        uv_gate_rows=min(1024, S), uv_gate_cols=1024,
        branch_out_rows=min(256, S),
        ffn_up_rows=min(1024, S), ffn_up_cols=512,
        ffn_down_rows=min(512, S), ffn_down_cols=1024,
        ple_rows=min(512, S),
    )


def kernel(x, p, g_mix, w_in, g_cq, g_ckv, w_uq, w_uk, w_uv, w_iq, w_a_proj, g_sgu, w_spatial,
           b_spatial, w_b_proj, w_o, g_ffn, w_gu, w_down, g_ple, w_ple_gate, w_ple_proj, g_final):
    B, S, D = x.shape
    T = B * S
    depth = w_in.shape[0]
    topk = min(TOPK_MAX, S // 4)
    t = _tiles(S)
    tables = _lane_tables(S)

    h = x.reshape(T, D)
    for i in range(depth):
        w_t = jnp.transpose(w_in[i])
        wuk_t = jnp.transpose(w_uk[i], (1, 2, 0)).reshape(N_HEADS * QK_NOPE, KV_LORA)
        wuv_t = jnp.transpose(w_uv[i], (1, 2, 0)).reshape(N_HEADS * V_HEAD, KV_LORA)
        n, q, k, v, qi, ki_lo, ki_hi, wt = _a_proj(h, g_mix[i][None], w_t, g_cq[i][None], g_ckv[i][None],
                                                    w_uq[i], w_iq[i], wuk_t, wuv_t,
                                                    tables, B=B, S=S, tm=t["a_proj_rows"])
        zg, wa_bf, wb_bf, wo_bf = _uv_gate_proj(n, w_t, w_a_proj[i], w_b_proj[i], w_o[i],
                                                tm=t["uv_gate_rows"], tn=t["uv_gate_cols"])
        bias = _dsa_index(ki_lo, ki_hi, qi, wt, B=B, S=S, TQ=t["attn_q"], KCH=t["index_keys"], topk=topk)
        o = _dsa_attn(q, k, v, bias, B=B, S=S, TQ=t["attn_q"])
        h1, n2 = _branch_out(o, zg, h, g_sgu[i][None], w_spatial[i], jnp.transpose(b_spatial[i]),
                             g_ffn[i][None], wa_bf, wb_bf, wo_bf, B=B, S=S, tm=t["branch_out_rows"])
        act, w_down_bf = _ffn_up(n2, w_gu[i], w_down[i], tm=t["ffn_up_rows"], tn=t["ffn_up_cols"])
        h2, wpg_bf, wpp_bf = _ffn_down(act, w_down_bf, h1, w_ple_gate[i], w_ple_proj[i],
                                       tm=t["ffn_down_rows"], tn=t["ffn_down_cols"])
        assert depth == 1
        h = _ple_final(h2, p[i].reshape(T, -1), wpg_bf, wpp_bf, g_ple[i][None], g_final[None], tm=t["ple_rows"])
    return h.reshape(B, S, D)
```

```python
import functools
import math

import numpy as np
import jax
import jax.numpy as jnp
from jax import lax
from jax.experimental import pallas as pl
from jax.experimental.pallas import tpu as pltpu

F32 = jnp.float32
BF16 = jnp.bfloat16

N_HEADS = 16
QK_NOPE = 64
QK_ROPE = 32
V_HEAD = 64
Q_LORA = 512
KV_LORA = 256
IDX_HEADS = 16
IDX_DIM = 64
IDX_ROPE = 32
TOPK_MAX = 256
SGU_CHUNK = 128
SGU_GROUPS = 8
SGU_GROUP_DIM = 128
SGU_WIDTH = SGU_GROUPS * SGU_GROUP_DIM
ROPE_THETA = 10000.0
EPS = 1e-6

LANES = 128
IN_SMALL = Q_LORA + KV_LORA + QK_ROPE + IDX_DIM + IDX_HEADS
SMALL_COLS = 1024
COL_CKV = Q_LORA
COL_MISC = Q_LORA + KV_LORA
MISC_KIDX = QK_ROPE
MISC_WIDX = QK_ROPE + IDX_DIM
VMEM_LIMIT_BYTES = 56 * 1024 * 1024
NEG_BIAS = -1e30
F32_MAX = 3.4028234663852886e38
STEPS_PER_CHECK = 4
COUNT_CHAINS = 4


def _cparams(semantics):
    return pltpu.CompilerParams(dimension_semantics=semantics, vmem_limit_bytes=VMEM_LIMIT_BYTES)


def _rms(x, g):
    return x * lax.rsqrt(jnp.mean(x * x, axis=-1, keepdims=True) + EPS) * g


def _dot(a, b):
    return jnp.dot(a, b, preferred_element_type=F32)


def _dot_nt(a, b):
    return lax.dot_general(a, b, (((1,), (1,)), ((), ())), preferred_element_type=F32)


def _gelu_exact(x):
    return 0.5 * x * (1.0 + lax.erf(x * (1.0 / math.sqrt(2.0))))


def _sigmoid(x):
    return 0.5 * jnp.tanh(0.5 * x) + 0.5


class _SideRound:
    def __init__(self, w, first, steps, step_of):
        rows, cols = w.shape
        slab = rows // steps
        assert slab * steps == rows and slab % 16 == 0
        self.first, self.steps = first, steps
        index = lambda *g: (jnp.clip(step_of(*g) - first, 0, steps - 1), 0)
        self.spec = pl.BlockSpec((slab, cols), index)
        self.out_shape = jax.ShapeDtypeStruct((rows, cols), BF16)

    def emit(self, step, src_ref, dst_ref):
        @pl.when((step >= self.first) & (step < self.first + self.steps))
        def _():
            dst_ref[...] = src_ref[...].astype(BF16)


def _pow2_floor(n):
    return 1 << (max(int(n), 1).bit_length() - 1)


TAB_Q = 0
TAB_QI = 2
TAB_KI = 4
TAB_KR = 7
N_TABLES = 10


def _rope_block(x, cos, sin_a, sin_b):
    return x * cos + pltpu.roll(x, LANES - 16, 1) * sin_a + pltpu.roll(x, 16, 1) * sin_b


def _a_proj_kernel(x_ref, gmix_ref, ws_ref, gcq_ref, gckv_ref, wuq_ref, wiq_ref, wuk_ref, wuv_ref, tab_ref,
                   n_ref, q_ref, k_ref, v_ref, qi_ref, kilo_ref, kihi_ref, wt_ref,
                   ws_bf, wq_bf, wiq_bf, wiqr_bf, wk_bf, wv_bf):
    @pl.when(pl.program_id(0) == 0)
    def _():
        ws_bf[...] = ws_ref[...].astype(BF16)
        wv_bf[...] = wuv_ref[...].astype(BF16)
        wuq_t = wuq_ref[...].T
        wiq_t = wiq_ref[...].T
        wiq_bf[...] = wiq_t.astype(BF16)
        r1 = QK_ROPE // 2
        hd = QK_NOPE + QK_ROPE
        for h in range(N_HEADS):
            src, dst = h * hd, h * LANES
            wq_bf[dst:dst + hd, :] = wuq_t[src:src + hd, :].astype(BF16)
            wq_bf[dst + hd:dst + hd + r1, :] = wuq_t[src + QK_NOPE + r1:src + hd, :].astype(BF16)
            wq_bf[dst + hd + r1:dst + LANES, :] = wuq_t[src + QK_NOPE:src + QK_NOPE + r1, :].astype(BF16)
            wk_bf[dst:dst + QK_NOPE, :] = wuk_ref[h * QK_NOPE:(h + 1) * QK_NOPE, :].astype(BF16)
            wk_bf[dst + QK_NOPE:dst + LANES, :] = jnp.zeros((LANES - QK_NOPE, wk_bf.shape[1]), BF16)
        ri = IDX_ROPE // 2
        for h in range(IDX_HEADS):
            o = h * IDX_DIM
            wiqr_bf[o:o + ri, :] = (-wiq_t[o + ri:o + 2 * ri, :]).astype(BF16)
            wiqr_bf[o + ri:o + 2 * ri, :] = wiq_t[o:o + ri, :].astype(BF16)
            wiqr_bf[o + 2 * ri:o + IDX_DIM, :] = jnp.zeros((IDX_DIM - 2 * ri, wiqr_bf.shape[1]), BF16)

    n = _rms(x_ref[...], gmix_ref[...]).astype(BF16)
    n_ref[...] = n
    small = _dot_nt(n, ws_bf[...])
    c_q = _rms(small[:, 0:Q_LORA], gcq_ref[...]).astype(BF16)
    c_kv = _rms(small[:, COL_CKV:COL_CKV + KV_LORA], gckv_ref[...]).astype(BF16)

    q = _dot_nt(c_q, wq_bf[...])
    cq, sq = tab_ref[TAB_Q], tab_ref[TAB_Q + 1]
    for h in range(N_HEADS):
        blk = q[:, h * LANES:(h + 1) * LANES]
        q_ref[0, h] = (blk * cq + pltpu.roll(blk, LANES - QK_ROPE, 1) * sq).astype(BF16)

    ci, si = tab_ref[TAB_QI], tab_ref[TAB_QI + 1]
    qi = _dot_nt(c_q, wiq_bf[...])
    qir = _dot_nt(c_q, wiqr_bf[...])
    for hp in range(IDX_HEADS // 2):
        cols = slice(hp * LANES, (hp + 1) * LANES)
        qi_ref[:, cols] = (qi[:, cols] * ci + qir[:, cols] * si).astype(BF16)

    misc = small[:, COL_MISC:COL_MISC + LANES]
    ki_lo = _rope_block(pltpu.roll(misc, LANES - MISC_KIDX, 1),
                        tab_ref[TAB_KI], tab_ref[TAB_KI + 1], tab_ref[TAB_KI + 2])
    kilo_ref[...] = ki_lo.astype(BF16)
    kihi_ref[...] = pltpu.roll(ki_lo, IDX_DIM, 1).astype(BF16)

    k_rope = pltpu.roll(_rope_block(misc, tab_ref[TAB_KR], tab_ref[TAB_KR + 1], tab_ref[TAB_KR + 2]),
                        QK_NOPE, 1)
    k_nope = _dot_nt(c_kv, wk_bf[...])
    for h in range(N_HEADS):
        k_ref[0, h] = (k_nope[:, h * LANES:(h + 1) * LANES] + k_rope).astype(BF16)

    v = _dot_nt(c_kv, wv_bf[...])
    for p in range(N_HEADS // 2):
        v_ref[0, p] = v[:, p * LANES:(p + 1) * LANES].astype(BF16)

    w_scale = IDX_HEADS ** -0.5 * IDX_DIM ** -0.5
    wt_ref[...] = misc.T[MISC_WIDX:MISC_WIDX + IDX_HEADS, :] * w_scale


def _a_proj(x2, g_mix, w_t, g_cq, g_ckv, wuq, wiq, wuk_t, wuv_t, tables, *, B, S, tm):
    T, D = x2.shape
    nt = S // tm
    const2 = lambda i: (0, 0)
    once = lambda w: pl.BlockSpec(w.shape, const2, pipeline_mode=pl.Buffered(1))
    head_spec = lambda nh: pl.BlockSpec((1, nh, tm, LANES), lambda i: (i // nt, 0, i % nt, 0))
    return pl.pallas_call(
        _a_proj_kernel,
        grid=(T // tm,),
        in_specs=[
            pl.BlockSpec((tm, D), lambda i: (i, 0)),
            pl.BlockSpec((1, D), const2),
            pl.BlockSpec((SMALL_COLS, D), const2, pipeline_mode=pl.Buffered(1)),
            pl.BlockSpec((1, Q_LORA), const2),
            pl.BlockSpec((1, KV_LORA), const2),
            once(wuq), once(wiq), once(wuk_t), once(wuv_t),
            pl.BlockSpec((N_TABLES, tm, LANES), lambda i: (0, i % nt, 0)),
        ],
        out_specs=[
            pl.BlockSpec((tm, D), lambda i: (i, 0)),
            head_spec(N_HEADS),
            head_spec(N_HEADS),
            head_spec(N_HEADS // 2),
            pl.BlockSpec((tm, IDX_HEADS * IDX_DIM), lambda i: (i, 0)),
            pl.BlockSpec((tm, LANES), lambda i: (i, 0)),
            pl.BlockSpec((tm, LANES), lambda i: (i, 0)),
            pl.BlockSpec((IDX_HEADS, tm), lambda i: (0, i)),
        ],
        out_shape=[
            jax.ShapeDtypeStruct((T, D), BF16),
            jax.ShapeDtypeStruct((B, N_HEADS, S, LANES), BF16),
            jax.ShapeDtypeStruct((B, N_HEADS, S, LANES), BF16),
            jax.ShapeDtypeStruct((B, N_HEADS // 2, S, LANES), BF16),
            jax.ShapeDtypeStruct((T, IDX_HEADS * IDX_DIM), BF16),
            jax.ShapeDtypeStruct((T, LANES), BF16),
            jax.ShapeDtypeStruct((T, LANES), BF16),
            jax.ShapeDtypeStruct((IDX_HEADS, T), F32),
        ],
        scratch_shapes=[
            pltpu.VMEM((SMALL_COLS, D), BF16),
            pltpu.VMEM((N_HEADS * LANES, Q_LORA), BF16),
            pltpu.VMEM((IDX_HEADS * IDX_DIM, Q_LORA), BF16),
            pltpu.VMEM((IDX_HEADS * IDX_DIM, Q_LORA), BF16),
            pltpu.VMEM((N_HEADS * LANES, KV_LORA), BF16),
            pltpu.VMEM((N_HEADS * V_HEAD, KV_LORA), BF16),
        ],
        compiler_params=_cparams(("arbitrary",)),
        name="norm_a_proj",
    )(x2, g_mix, w_t, g_cq, g_ckv, wuq, wiq, wuk_t, wuv_t, tables)


def _uv_gate_kernel(n_ref, w_ref, *rest, n_z, sides):
    side_in, (out_ref, *side_out), w_bf = rest[:len(sides)], rest[len(sides):-1], rest[-1]
    j = pl.program_id(0)
    step = j * pl.num_programs(1) + pl.program_id(1)
    for side, src, dst in zip(sides, side_in, side_out):
        side.emit(step, src, dst)

    @pl.when(pl.program_id(1) == 0)
    def _():
        w_bf[...] = w_ref[...].astype(BF16)

    def proj():
        return _dot_nt(n_ref[...], w_bf[...])

    @pl.when(j < n_z)
    def _():
        out_ref[...] = _gelu_exact(proj()).astype(BF16)

    @pl.when(j >= n_z)
    def _():
        out_ref[...] = _sigmoid(proj()).astype(BF16)


def _uv_gate_proj(n, w_t, wa, wb, wo, *, tm, tn):
    T, D = n.shape
    n_z = (2 * SGU_WIDTH) // tn
    n_out = 2 * SGU_WIDTH + 2 * D
    nm = T // tm
    assert w_t.shape == (IN_SMALL + n_out, D) and IN_SMALL % 8 == 0 and tn % 8 == 0
    step_of = lambda j, i: j * nm + i
    a = _pow2_floor((n_out // tn) * nm // 4)
    sides = [_SideRound(wa, 0, a, step_of), _SideRound(wb, a, a, step_of), _SideRound(wo, 2 * a, 2 * a, step_of)]
    kern = functools.partial(_uv_gate_kernel, n_z=n_z, sides=sides)
    return pl.pallas_call(
        kern,
        grid=(n_out // tn, nm),
        in_specs=[
            pl.BlockSpec((tm, D), lambda j, i: (i, 0)),
            pl.BlockSpec((pl.Element(tn), pl.Element(D)), lambda j, i: ((IN_SMALL // 8 + j * (tn // 8)) * 8, 0)),
        ] + [s.spec for s in sides],
        out_specs=[pl.BlockSpec((tm, tn), lambda j, i: (i, j))] + [s.spec for s in sides],
        out_shape=[jax.ShapeDtypeStruct((T, n_out), BF16)] + [s.out_shape for s in sides],
        scratch_shapes=[pltpu.VMEM((tn, D), BF16)],
        compiler_params=_cparams(("arbitrary", "arbitrary")),
        name="uv_gate_proj",
    )(n, w_t, wa, wb, wo)


def _dsa_index_kernel(kilo_ref, kihi_ref, qi_ref, wt_ref, bias_ref, isc_ref, mm_ref, js_ref,
                      *, S, TQ, KCH, topk):
    j = pl.program_id(1)
    q0 = j * TQ
    nkeys = q0 + TQ
    qidx = q0 + lax.broadcasted_iota(jnp.int32, (1, TQ), 1)
    kf = float(topk)

    mm_ref[0:8, :] = jnp.full((8, TQ), jnp.inf, F32)
    mm_ref[8:16, :] = jnp.full((8, TQ), -jnp.inf, F32)
    for c in range(S // KCH):
        @pl.when(c * KCH < nkeys)
        def _(c=c):
            klo = kilo_ref[c * KCH:(c + 1) * KCH, :]
            khi = kihi_ref[c * KCH:(c + 1) * KCH, :]
            acc = jnp.zeros((KCH, TQ), F32)
            for hp in range(IDX_HEADS // 2):
                qp = qi_ref[:, hp * LANES:(hp + 1) * LANES]
                s0 = _dot_nt(klo, qp)
                s1 = _dot_nt(khi, qp)
                acc = acc + jnp.maximum(s0, 0.0) * wt_ref[2 * hp:2 * hp + 1, :]
                acc = acc + jnp.maximum(s1, 0.0) * wt_ref[2 * hp + 1:2 * hp + 2, :]
            kidx = c * KCH + lax.broadcasted_iota(jnp.int32, (KCH, TQ), 0)
            causal = kidx <= qidx
            isc_ref[c * KCH:(c + 1) * KCH, :] = jnp.where(causal, acc, -jnp.inf)
            lo_part = jnp.where(causal, acc, jnp.inf).reshape(KCH // 8, 8, TQ).min(axis=0)
            hi_part = jnp.where(causal, acc, -jnp.inf).reshape(KCH // 8, 8, TQ).max(axis=0)
            mm_ref[0:8, :] = jnp.minimum(mm_ref[0:8, :], lo_part)
            mm_ref[8:16, :] = jnp.maximum(mm_ref[8:16, :], hi_part)

    def select(nch):
        def count(pred):
            groups = []
            for lg in range(TQ // LANES):
                lanes = slice(lg * LANES, (lg + 1) * LANES)
                acc = jnp.zeros((COUNT_CHAINS, 8, LANES), F32)
                for c in range(nch):
                    ones = jnp.where(pred(isc_ref[c * TQ:(c + 1) * TQ, lanes], c * TQ, lanes), 1.0, 0.0)
                    acc = acc + ones.reshape(COUNT_CHAINS, TQ // (8 * COUNT_CHAINS), 8, LANES).sum(axis=1)
                groups.append(acc.sum(axis=0).sum(axis=0, keepdims=True))
            return jnp.concatenate(groups, axis=1)

        row_min = mm_ref[0:8, :].min(axis=0, keepdims=True)
        row_max = mm_ref[8:16, :].max(axis=0, keepdims=True)
        full = (qidx + 1) <= topk
        c_max = count(lambda blk, k0, lanes: blk >= row_max[:, lanes])
        exact0 = c_max == kf
        tie0 = c_max > kf
        settled0 = full | exact0 | tie0
        lo0 = jnp.where(full, -F32_MAX, jnp.where(settled0, row_max, row_min))
        hi0 = jnp.where(full, -F32_MAX, jnp.where(tie0, jnp.inf, row_max))
        act0 = jnp.where(settled0, 0.0, 1.0)

        def step(lo, hi, act):
            mid = lo * 0.5 + hi * 0.5
            inside = (mid > lo) & (mid < hi)
            cnt = count(lambda blk, k0, lanes: blk >= mid[:, lanes])
            upd = (act > 0.0) & inside
            found = upd & (cnt == kf)
            lo2 = jnp.where(upd & (cnt >= kf), mid, lo)
            hi2 = jnp.where(upd & (cnt <= kf), mid, hi)
            return lo2, hi2, jnp.where(upd & jnp.logical_not(found), 1.0, 0.0)

        def any_active(act):
            return (jnp.max(act) > 0.0).astype(jnp.int32)

        def body(st):
            lo, hi, act, _ = st
            for _ in range(STEPS_PER_CHECK):
                lo, hi, act = step(lo, hi, act)
            return lo, hi, act, any_active(act)

        lo, hi, _, _ = lax.while_loop(lambda st: st[3] > 0, body, (lo0, hi0, act0, any_active(act0)))

        tie = lo < hi
        js_ref[0:8, :] = jnp.full((8, TQ), -1.0, F32)

        @pl.when(jnp.max(jnp.where(tie, 1.0, 0.0)) > 0.0)
        def _():
            need = kf - count(lambda blk, k0, lanes: blk >= hi[:, lanes])

            def kpos(k0):
                return (k0 + lax.broadcasted_iota(jnp.int32, (TQ, LANES), 0)).astype(F32)

            def tie_step(_, st):
                ilo, ihi = st
                imid = jnp.floor((ilo + ihi) * 0.5)
                cnt = count(lambda blk, k0, lanes: (blk >= lo[:, lanes]) & (blk < hi[:, lanes])
                            & (kpos(k0) <= imid[:, lanes]))
                ge = cnt >= need
                return jnp.where(ge, ilo, imid), jnp.where(ge, imid, ihi)

            nsteps = int(math.ceil(math.log2(S))) + 1
            _, ihi = lax.fori_loop(0, nsteps, tie_step,
                                   (jnp.full((1, TQ), -1.0, F32), jnp.full((1, TQ), S - 1.0, F32)))
            js_ref[0:1, :] = jnp.where(tie, ihi, -1.0)

        jstar = js_ref[0:1, :]

        for c in range(S // TQ):
            if c < nch:
                blk = isc_ref[c * TQ:(c + 1) * TQ, :]
                kpos_c = (c * TQ + lax.broadcasted_iota(jnp.int32, (TQ, TQ), 0)).astype(F32)
                sel = (blk >= hi) | ((blk >= lo) & (kpos_c <= jstar))
                bias_ref[:, c * TQ:(c + 1) * TQ] = jnp.where(sel, 0.0, NEG_BIAS).T
            else:
                bias_ref[:, c * TQ:(c + 1) * TQ] = jnp.full((TQ, TQ), NEG_BIAS, F32)

    for jj in range(S // TQ):
        @pl.when(j == jj)
        def _(jj=jj):
            select(jj + 1)


def _dsa_index(ki_lo, ki_hi, qi, wt, *, B, S, TQ, KCH, topk):
    T = B * S
    nq = S // TQ
    kern = functools.partial(_dsa_index_kernel, S=S, TQ=TQ, KCH=KCH, topk=topk)
    return pl.pallas_call(
        kern,
        grid=(B, nq),
        in_specs=[
            pl.BlockSpec((S, LANES), lambda b, j: (b, 0)),
            pl.BlockSpec((S, LANES), lambda b, j: (b, 0)),
            pl.BlockSpec((TQ, IDX_HEADS * IDX_DIM), lambda b, j: (b * nq + j, 0)),
            pl.BlockSpec((IDX_HEADS, TQ), lambda b, j: (0, b * nq + j)),
        ],
        out_specs=pl.BlockSpec((TQ, S), lambda b, j: (b * nq + j, 0)),
        out_shape=jax.ShapeDtypeStruct((T, S), F32),
        scratch_shapes=[
            pltpu.VMEM((S, TQ), F32),
            pltpu.VMEM((16, TQ), F32),
            pltpu.VMEM((16, TQ), F32),
        ],
        compiler_params=_cparams(("parallel", "arbitrary")),
        name="dsa_index",
    )(ki_lo, ki_hi, qi, wt)


def _dsa_attn_kernel(q_ref, k_ref, v_ref, bias_ref, o_ref, *, S, TQ):
    j = pl.program_id(1)
    lane = lax.broadcasted_iota(jnp.int32, (TQ, LANES), 1)

    def variant(nk):
        ones = jnp.ones((nk, LANES), BF16)

        def pair(p, carry):
            vp = jnp.concatenate([v_ref[0, p, 0:nk, :], ones], axis=1)
            outs = []
            for e in range(2):
                h = 2 * p + e
                s = _dot_nt(q_ref[0, h], k_ref[0, h, 0:nk, :]) + bias_ref[:, 0:nk]
                m = s.max(axis=1, keepdims=True)
                pv = _dot(jnp.exp2(s - m).astype(BF16), vp)
                outs.append(pv[:, 0:LANES] * (1.0 / pv[:, LANES:2 * LANES]))
            o_ref[0, p] = jnp.where(lane < V_HEAD, outs[0], outs[1]).astype(BF16)
            return carry
        lax.fori_loop(0, N_HEADS // 2, pair, 0, unroll=4)

    for jj in range(S // TQ):
        @pl.when(j == jj)
        def _(jj=jj):
            variant((jj + 1) * TQ)


def _dsa_attn(q, k, v, bias, *, B, S, TQ):
    nq = S // TQ
    kern = functools.partial(_dsa_attn_kernel, S=S, TQ=TQ)
    return pl.pallas_call(
        kern,
        grid=(B, nq),
        in_specs=[
            pl.BlockSpec((1, N_HEADS, TQ, LANES), lambda b, j: (b, 0, j, 0)),
            pl.BlockSpec((1, N_HEADS, S, LANES), lambda b, j: (b, 0, 0, 0)),
            pl.BlockSpec((1, N_HEADS // 2, S, LANES), lambda b, j: (b, 0, 0, 0)),
            pl.BlockSpec((TQ, S), lambda b, j: (b * nq + j, 0)),
        ],
        out_specs=pl.BlockSpec((1, N_HEADS // 2, TQ, LANES), lambda b, j: (b, 0, j, 0)),
        out_shape=jax.ShapeDtypeStruct((B, N_HEADS // 2, S, LANES), BF16),
        compiler_params=_cparams(("parallel", "arbitrary")),
        name="dsa_attn",
    )(q, k, v, bias)


def _branch_out_kernel(o_ref, z_ref, ga_ref, gb_ref, x_ref, gs_ref, ws_ref, b_ref, gf_ref,
                       wa_bf, wb_bf, wo_bf, h_ref, n_ref, y_scr, *, tm):
    row = lax.broadcasted_iota(jnp.int32, (SGU_CHUNK, SGU_CHUNK), 0)
    col = lax.broadcasted_iota(jnp.int32, (SGU_CHUNK, SGU_CHUNK), 1)
    tril = col <= row
    w = [jnp.where(tril, ws_ref[g], 0.0).astype(BF16) for g in range(SGU_GROUPS)]
    b_t = jnp.transpose(b_ref[...])
    for cc in range(tm // SGU_CHUNK):
        rows = slice(cc * SGU_CHUNK, (cc + 1) * SGU_CHUNK)
        vn = _rms(z_ref[rows, SGU_WIDTH:2 * SGU_WIDTH].astype(F32), gs_ref[...]).astype(BF16)
        for g in range(SGU_GROUPS):
            cols = slice(g * SGU_GROUP_DIM, (g + 1) * SGU_GROUP_DIM)
            mixed = _dot(w[g], vn[:, cols]) + b_t[:, g:g + 1]
            y_scr[rows, cols] = (z_ref[rows, cols].astype(F32) * mixed).astype(BF16)

    o_a = jnp.concatenate([o_ref[0, p] for p in range(N_HEADS // 2)], axis=1)
    ya = _dot(o_a, wa_bf[...])
    yb = _dot(y_scr[...], wb_bf[...])
    merged = (ga_ref[...].astype(F32) * ya + gb_ref[...].astype(F32) * yb).astype(BF16)
    h = x_ref[...] + _dot(merged, wo_bf[...])
    h_ref[...] = h
    n_ref[...] = _rms(h, gf_ref[...]).astype(BF16)


def _branch_out(o, zg, x2, g_sgu, w_spatial, b_spatial, g_ffn, wa, wb, wo, *, B, S, tm):
    T, D = x2.shape
    nt = S // tm
    assert 2 * SGU_WIDTH == D and wa.shape == wb.shape == (SGU_WIDTH, D) and wo.shape == (D, D)
    row = lambda c: pl.BlockSpec((tm, D), lambda i: (i, c))
    const = lambda shape: pl.BlockSpec(shape, lambda i: (0,) * len(shape))
    resident = lambda w: pl.BlockSpec(w.shape, lambda i: (0, 0), pipeline_mode=pl.Buffered(1))
    kern = functools.partial(_branch_out_kernel, tm=tm)
    return pl.pallas_call(
        kern,
        grid=(T // tm,),
        in_specs=[
            pl.BlockSpec((1, N_HEADS // 2, tm, LANES), lambda i: (i // nt, 0, i % nt, 0)),
            row(0), row(1), row(2),
            row(0),
            const((1, SGU_WIDTH)),
            const((SGU_GROUPS, SGU_CHUNK, SGU_CHUNK)),
            const((SGU_GROUPS, SGU_CHUNK)),
            const((1, D)),
            resident(wa), resident(wb), resident(wo),
        ],
        out_specs=[pl.BlockSpec((tm, D), lambda i: (i, 0)), pl.BlockSpec((tm, D), lambda i: (i, 0))],
        out_shape=[jax.ShapeDtypeStruct((T, D), F32), jax.ShapeDtypeStruct((T, D), BF16)],
        scratch_shapes=[pltpu.VMEM((tm, SGU_WIDTH), BF16)],
        compiler_params=_cparams(("parallel",)),
        name="branch_out",
    )(o, zg, zg, zg, x2, g_sgu, w_spatial, b_spatial, g_ffn, wa, wb, wo)


def _ffn_up_kernel(n_ref, wg_ref, wu_ref, wd_ref, a_ref, wd_bf_ref, wg_bf, wu_bf):
    @pl.when(pl.program_id(1) == 0)
    def _():
        wg_bf[...] = wg_ref[...].astype(BF16)
        wu_bf[...] = wu_ref[...].astype(BF16)

    wd_bf_ref[...] = wd_ref[...].astype(BF16)
    n = n_ref[...]
    g = _dot(n, wg_bf[...])
    u = _dot(n, wu_bf[...])
    a_ref[...] = (g * _sigmoid(g) * u).astype(BF16)


def _ffn_up(n2, w_gu, w_down, *, tm, tn):
    T, D = n2.shape
    d_ff = w_down.shape[0]
    nn, nm = d_ff // tn, T // tm
    slab = d_ff // (nn * nm)
    assert slab * nn * nm == d_ff and slab % 16 == 0
    return pl.pallas_call(
        _ffn_up_kernel,
        grid=(nn, nm),
        in_specs=[
            pl.BlockSpec((tm, D), lambda j, i: (i, 0)),
            pl.BlockSpec((D, tn), lambda j, i: (0, j)),
            pl.BlockSpec((D, tn), lambda j, i: (0, nn + j)),
            pl.BlockSpec((slab, D), lambda j, i: (j * nm + i, 0)),
        ],
        out_specs=[
            pl.BlockSpec((tm, tn), lambda j, i: (i, j)),
            pl.BlockSpec((slab, D), lambda j, i: (j * nm + i, 0)),
        ],
        out_shape=[jax.ShapeDtypeStruct((T, d_ff), BF16), jax.ShapeDtypeStruct((d_ff, D), BF16)],
        scratch_shapes=[pltpu.VMEM((D, tn), BF16), pltpu.VMEM((D, tn), BF16)],
        compiler_params=_cparams(("arbitrary", "arbitrary")),
        name="ffn_up",
    )(n2, w_gu, w_gu, w_down)


def _ffn_down_kernel(a_ref, w_ref, h1_ref, wg_ref, wp_ref, h2_ref, wg_bf_ref, wp_bf_ref, *, sides):
    step = pl.program_id(0) * pl.num_programs(1) + pl.program_id(1)
    for side, src, dst in zip(sides, (wg_ref, wp_ref), (wg_bf_ref, wp_bf_ref)):
        side.emit(step, src, dst)
    h2_ref[...] = h1_ref[...] + _dot(a_ref[...], w_ref[...])


def _ffn_down(act, w_down_bf, h1, w_pg, w_pp, *, tm, tn):
    T, D = h1.shape
    d_ff = act.shape[1]
    nn, nm = D // tn, T // tm
    step_of = lambda j, i: j * nm + i
    a = _pow2_floor(nn * nm // 2)
    b = min(a, w_pp.shape[0] // 16)
    sides = [_SideRound(w_pg, 0, a, step_of), _SideRound(w_pp, a, b, step_of)]
    kern = functools.partial(_ffn_down_kernel, sides=sides)
    return pl.pallas_call(
        kern,
        grid=(nn, nm),
        in_specs=[
            pl.BlockSpec((tm, d_ff), lambda j, i: (i, 0)),
            pl.BlockSpec((d_ff, tn), lambda j, i: (0, j)),
            pl.BlockSpec((tm, tn), lambda j, i: (i, j)),
        ] + [s.spec for s in sides],
        out_specs=[pl.BlockSpec((tm, tn), lambda j, i: (i, j))] + [s.spec for s in sides],
        out_shape=[jax.ShapeDtypeStruct((T, D), F32)] + [s.out_shape for s in sides],
        compiler_params=_cparams(("arbitrary", "arbitrary")),
        name="ffn_down",
    )(act, w_down_bf, h1, w_pg, w_pp)


def _ple_final_kernel(h2_ref, p_ref, gp_ref, gf_ref, wg_bf, wp_bf, out_ref):
    h2 = h2_ref[...]
    gate = _sigmoid(_dot(_rms(h2, gp_ref[...]).astype(BF16), wg_bf[...]))
    pp = _dot(p_ref[...].astype(BF16), wp_bf[...])
    out_ref[...] = _rms(h2 + gate * pp, gf_ref[...])


def _ple_final(h2, p2, w_pg, w_pp, g_ple, g_final, *, tm):
    T, D = h2.shape
    P = p2.shape[1]
    resident = lambda w: pl.BlockSpec(w.shape, lambda i: (0, 0), pipeline_mode=pl.Buffered(1))
    return pl.pallas_call(
        _ple_final_kernel,
        grid=(T // tm,),
        in_specs=[
            pl.BlockSpec((tm, D), lambda i: (i, 0)),
            pl.BlockSpec((tm, P), lambda i: (i, 0)),
            pl.BlockSpec((1, D), lambda i: (0, 0)),
            pl.BlockSpec((1, D), lambda i: (0, 0)),
            resident(w_pg), resident(w_pp),
        ],
        out_specs=pl.BlockSpec((tm, D), lambda i: (i, 0)),
        out_shape=jax.ShapeDtypeStruct((T, D), F32),
        compiler_params=_cparams(("parallel",)),
        name="ple_final",
    )(h2, p2, g_ple, g_final, w_pg, w_pp)


def _lane_tables(S):
    assert QK_ROPE == IDX_ROPE == 32
    scale = (QK_NOPE + QK_ROPE) ** -0.5 * math.log2(math.e)
    a = np.zeros((N_TABLES, LANES), np.float64)
    bc = np.zeros_like(a)
    bs = np.zeros_like(a)
    a[TAB_Q, 0:QK_NOPE] = scale
    bc[TAB_Q, 64:96] = scale
    bs[TAB_Q + 1, 64:80] = -scale
    bs[TAB_Q + 1, 80:96] = scale
    for o in (0, IDX_DIM):
        bc[TAB_QI, o:o + 32] = 1.0
        a[TAB_QI, o + 32:o + 64] = 1.0
        bs[TAB_QI + 1, o:o + 32] = 1.0
    bc[TAB_KI, 0:32] = 1.0
    a[TAB_KI, 32:64] = 1.0
    bs[TAB_KI + 1, 0:16] = -1.0
    bs[TAB_KI + 2, 16:32] = 1.0
    bc[TAB_KR, 0:32] = 1.0
    bs[TAB_KR + 1, 0:16] = -1.0
    bs[TAB_KR + 2, 16:32] = 1.0
    inv = ROPE_THETA ** (-np.arange(0, QK_ROPE, 2, dtype=np.float64) / QK_ROPE)
    ang = np.arange(S, dtype=np.float64)[:, None] * np.tile(inv, LANES // inv.shape[0])[None, :]
    tables = a[:, None, :] + bc[:, None, :] * np.cos(ang)[None] + bs[:, None, :] * np.sin(ang)[None]
    return jnp.asarray(tables.astype(np.float32))


def _tiles(S):
    return dict(
        attn_q=256,
        index_keys=min(512, S),
        a_proj_rows=min(256, S),
        uv_gate_rows=min(1024, S), uv_gate_cols=1024,
        branch_out_rows=min(256, S),
        ffn_up_rows=min(1024, S), ffn_up_cols=512,
        ffn_down_rows=min(512, S), ffn_down_cols=1024,
        ple_rows=min(512, S),
    )


def kernel(x, p, g_mix, w_in, g_cq, g_ckv, w_uq, w_uk, w_uv, w_iq, w_a_proj, g_sgu, w_spatial,
           b_spatial, w_b_proj, w_o, g_ffn, w_gu, w_down, g_ple, w_ple_gate, w_ple_proj, g_final):
    B, S, D = x.shape
    T = B * S
    depth = w_in.shape[0]
    topk = min(TOPK_MAX, S // 4)
    t = _tiles(S)
    tables = _lane_tables(S)

    h = x.reshape(T, D)
    for i in range(depth):
        w_t = jnp.transpose(w_in[i])
        wuk_t = jnp.transpose(w_uk[i], (1, 2, 0)).reshape(N_HEADS * QK_NOPE, KV_LORA)
        wuv_t = jnp.transpose(w_uv[i], (1, 2, 0)).reshape(N_HEADS * V_HEAD, KV_LORA)
        n, q, k, v, qi, ki_lo, ki_hi, wt = _a_proj(h, g_mix[i][None], w_t, g_cq[i][None], g_ckv[i][None],
                                                    w_uq[i], w_iq[i], wuk_t, wuv_t,
                                                    tables, B=B, S=S, tm=t["a_proj_rows"])
        zg, wa_bf, wb_bf, wo_bf = _uv_gate_proj(n, w_t, w_a_proj[i], w_b_proj[i], w_o[i],
                                                tm=t["uv_gate_rows"], tn=t["uv_gate_cols"])
        bias = _dsa_index(ki_lo, ki_hi, qi, wt, B=B, S=S, TQ=t["attn_q"], KCH=t["index_keys"], topk=topk)
        o = _dsa_attn(q, k, v, bias, B=B, S=S, TQ=t["attn_q"])
        h1, n2 = _branch_out(o, zg, h, g_sgu[i][None], w_spatial[i], b_spatial[i],
                             g_ffn[i][None], wa_bf, wb_bf, wo_bf, B=B, S=S, tm=t["branch_out_rows"])
        act, w_down_bf = _ffn_up(n2, w_gu[i], w_down[i], tm=t["ffn_up_rows"], tn=t["ffn_up_cols"])
        h2, wpg_bf, wpp_bf = _ffn_down(act, w_down_bf, h1, w_ple_gate[i], w_ple_proj[i],
                                       tm=t["ffn_down_rows"], tn=t["ffn_down_cols"])
        assert depth == 1
        h = _ple_final(h2, p[i].reshape(T, -1), wpg_bf, wpp_bf, g_ple[i][None], g_final[None], tm=t["ple_rows"])
    return h.reshape(B, S, D)
```

```python
import functools
import math

import numpy as np
import jax
import jax.numpy as jnp
from jax import lax
from jax.experimental import pallas as pl
from jax.experimental.pallas import tpu as pltpu

F32 = jnp.float32
BF16 = jnp.bfloat16

N_HEADS = 16
QK_NOPE = 64
QK_ROPE = 32
V_HEAD = 64
Q_LORA = 512
KV_LORA = 256
IDX_HEADS = 16
IDX_DIM = 64
IDX_ROPE = 32
TOPK_MAX = 256
SGU_CHUNK = 128
SGU_GROUPS = 8
SGU_GROUP_DIM = 128
SGU_WIDTH = SGU_GROUPS * SGU_GROUP_DIM
ROPE_THETA = 10000.0
EPS = 1e-6

LANES = 128
IN_SMALL = Q_LORA + KV_LORA + QK_ROPE + IDX_DIM + IDX_HEADS
SMALL_COLS = 1024
COL_CKV = Q_LORA
COL_MISC = Q_LORA + KV_LORA
MISC_KIDX = QK_ROPE
MISC_WIDX = QK_ROPE + IDX_DIM
VMEM_LIMIT_BYTES = 56 * 1024 * 1024
BRANCH_OUT_VMEM_LIMIT_BYTES = 60 * 1024 * 1024
NEG_BIAS = -1e30
F32_MAX = 3.4028234663852886e38
STEPS_PER_CHECK = 4
COUNT_CHAINS = 4


def _cparams(semantics, vmem_limit_bytes=VMEM_LIMIT_BYTES):
    return pltpu.CompilerParams(dimension_semantics=semantics, vmem_limit_bytes=vmem_limit_bytes)


def _rms(x, g):
    return x * lax.rsqrt(jnp.mean(x * x, axis=-1, keepdims=True) + EPS) * g


def _dot(a, b):
    return jnp.dot(a, b, preferred_element_type=F32)


def _dot_nt(a, b):
    return lax.dot_general(a, b, (((1,), (1,)), ((), ())), preferred_element_type=F32)


def _gelu_exact(x):
    return 0.5 * x * (1.0 + lax.erf(x * (1.0 / math.sqrt(2.0))))


def _sigmoid(x):
    return 0.5 * jnp.tanh(0.5 * x) + 0.5


class _SideRound:
    def __init__(self, w, first, steps, step_of):
        rows, cols = w.shape
        slab = rows // steps
        assert slab * steps == rows and slab % 16 == 0
        self.first, self.steps = first, steps
        index = lambda *g: (jnp.clip(step_of(*g) - first, 0, steps - 1), 0)
        self.spec = pl.BlockSpec((slab, cols), index)
        self.out_shape = jax.ShapeDtypeStruct((rows, cols), BF16)

    def emit(self, step, src_ref, dst_ref):
        @pl.when((step >= self.first) & (step < self.first + self.steps))
        def _():
            dst_ref[...] = src_ref[...].astype(BF16)


def _pow2_floor(n):
    return 1 << (max(int(n), 1).bit_length() - 1)


TAB_Q = 0
TAB_QI = 2
TAB_KI = 4
TAB_KR = 7
N_TABLES = 10


def _rope_block(x, cos, sin_a, sin_b):
    return x * cos + pltpu.roll(x, LANES - 16, 1) * sin_a + pltpu.roll(x, 16, 1) * sin_b


def _a_proj_kernel(x_ref, gmix_ref, ws_ref, gcq_ref, gckv_ref, wuq_ref, wiq_ref, wuk_ref, wuv_ref, tab_ref,
                   n_ref, q_ref, k_ref, v_ref, qi_ref, kilo_ref, kihi_ref, wt_ref,
                   ws_bf, wq_bf, wiq_bf, wiqr_bf, wk_bf, wv_bf):
    @pl.when(pl.program_id(0) == 0)
    def _():
        ws_bf[...] = ws_ref[...].astype(BF16)
        wv_bf[...] = wuv_ref[...].astype(BF16)
        wuq_t = wuq_ref[...].T
        wiq_t = wiq_ref[...].T
        wiq_bf[...] = wiq_t.astype(BF16)
        r1 = QK_ROPE // 2
        hd = QK_NOPE + QK_ROPE
        for h in range(N_HEADS):
            src, dst = h * hd, h * LANES
            wq_bf[dst:dst + hd, :] = wuq_t[src:src + hd, :].astype(BF16)
            wq_bf[dst + hd:dst + hd + r1, :] = wuq_t[src + QK_NOPE + r1:src + hd, :].astype(BF16)
            wq_bf[dst + hd + r1:dst + LANES, :] = wuq_t[src + QK_NOPE:src + QK_NOPE + r1, :].astype(BF16)
            wk_bf[dst:dst + QK_NOPE, :] = wuk_ref[h * QK_NOPE:(h + 1) * QK_NOPE, :].astype(BF16)
            wk_bf[dst + QK_NOPE:dst + LANES, :] = jnp.zeros((LANES - QK_NOPE, wk_bf.shape[1]), BF16)
        ri = IDX_ROPE // 2
        for h in range(IDX_HEADS):
            o = h * IDX_DIM
            wiqr_bf[o:o + ri, :] = (-wiq_t[o + ri:o + 2 * ri, :]).astype(BF16)
            wiqr_bf[o + ri:o + 2 * ri, :] = wiq_t[o:o + ri, :].astype(BF16)
            wiqr_bf[o + 2 * ri:o + IDX_DIM, :] = jnp.zeros((IDX_DIM - 2 * ri, wiqr_bf.shape[1]), BF16)

    n = _rms(x_ref[...], gmix_ref[...]).astype(BF16)
    n_ref[...] = n
    small = _dot_nt(n, ws_bf[...])
    c_q = _rms(small[:, 0:Q_LORA], gcq_ref[...]).astype(BF16)
    c_kv = _rms(small[:, COL_CKV:COL_CKV + KV_LORA], gckv_ref[...]).astype(BF16)

    q = _dot_nt(c_q, wq_bf[...])
    cq, sq = tab_ref[TAB_Q], tab_ref[TAB_Q + 1]
    for h in range(N_HEADS):
        blk = q[:, h * LANES:(h + 1) * LANES]
        q_ref[0, h] = (blk * cq + pltpu.roll(blk, LANES - QK_ROPE, 1) * sq).astype(BF16)

    ci, si = tab_ref[TAB_QI], tab_ref[TAB_QI + 1]
    qi = _dot_nt(c_q, wiq_bf[...])
    qir = _dot_nt(c_q, wiqr_bf[...])
    for hp in range(IDX_HEADS // 2):
        cols = slice(hp * LANES, (hp + 1) * LANES)
        qi_ref[:, cols] = (qi[:, cols] * ci + qir[:, cols] * si).astype(BF16)

    misc = small[:, COL_MISC:COL_MISC + LANES]
    ki_lo = _rope_block(pltpu.roll(misc, LANES - MISC_KIDX, 1),
                        tab_ref[TAB_KI], tab_ref[TAB_KI + 1], tab_ref[TAB_KI + 2])
    kilo_ref[...] = ki_lo.astype(BF16)
    kihi_ref[...] = pltpu.roll(ki_lo, IDX_DIM, 1).astype(BF16)

    k_rope = pltpu.roll(_rope_block(misc, tab_ref[TAB_KR], tab_ref[TAB_KR + 1], tab_ref[TAB_KR + 2]),
                        QK_NOPE, 1)
    k_nope = _dot_nt(c_kv, wk_bf[...])
    for h in range(N_HEADS):
        k_ref[0, h] = (k_nope[:, h * LANES:(h + 1) * LANES] + k_rope).astype(BF16)

    v = _dot_nt(c_kv, wv_bf[...])
    for p in range(N_HEADS // 2):
        v_ref[0, p] = v[:, p * LANES:(p + 1) * LANES].astype(BF16)

    w_scale = IDX_HEADS ** -0.5 * IDX_DIM ** -0.5
    wt_ref[...] = misc.T[MISC_WIDX:MISC_WIDX + IDX_HEADS, :] * w_scale


def _a_proj(x2, g_mix, w_t, g_cq, g_ckv, wuq, wiq, wuk_t, wuv_t, tables, *, B, S, tm):
    T, D = x2.shape
    nt = S // tm
    const2 = lambda i: (0, 0)
    once = lambda w: pl.BlockSpec(w.shape, const2, pipeline_mode=pl.Buffered(1))
    head_spec = lambda nh: pl.BlockSpec((1, nh, tm, LANES), lambda i: (i // nt, 0, i % nt, 0))
    return pl.pallas_call(
        _a_proj_kernel,
        grid=(T // tm,),
        in_specs=[
            pl.BlockSpec((tm, D), lambda i: (i, 0)),
            pl.BlockSpec((1, D), const2),
            pl.BlockSpec((SMALL_COLS, D), const2, pipeline_mode=pl.Buffered(1)),
            pl.BlockSpec((1, Q_LORA), const2),
            pl.BlockSpec((1, KV_LORA), const2),
            once(wuq), once(wiq), once(wuk_t), once(wuv_t),
            pl.BlockSpec((N_TABLES, tm, LANES), lambda i: (0, i % nt, 0)),
        ],
        out_specs=[
            pl.BlockSpec((tm, D), lambda i: (i, 0)),
            head_spec(N_HEADS),
            head_spec(N_HEADS),
            head_spec(N_HEADS // 2),
            pl.BlockSpec((tm, IDX_HEADS * IDX_DIM), lambda i: (i, 0)),
            pl.BlockSpec((tm, LANES), lambda i: (i, 0)),
            pl.BlockSpec((tm, LANES), lambda i: (i, 0)),
            pl.BlockSpec((IDX_HEADS, tm), lambda i: (0, i)),
        ],
        out_shape=[
            jax.ShapeDtypeStruct((T, D), BF16),
            jax.ShapeDtypeStruct((B, N_HEADS, S, LANES), BF16),
            jax.ShapeDtypeStruct((B, N_HEADS, S, LANES), BF16),
            jax.ShapeDtypeStruct((B, N_HEADS // 2, S, LANES), BF16),
            jax.ShapeDtypeStruct((T, IDX_HEADS * IDX_DIM), BF16),
            jax.ShapeDtypeStruct((T, LANES), BF16),
            jax.ShapeDtypeStruct((T, LANES), BF16),
            jax.ShapeDtypeStruct((IDX_HEADS, T), F32),
        ],
        scratch_shapes=[
            pltpu.VMEM((SMALL_COLS, D), BF16),
            pltpu.VMEM((N_HEADS * LANES, Q_LORA), BF16),
            pltpu.VMEM((IDX_HEADS * IDX_DIM, Q_LORA), BF16),
            pltpu.VMEM((IDX_HEADS * IDX_DIM, Q_LORA), BF16),
            pltpu.VMEM((N_HEADS * LANES, KV_LORA), BF16),
            pltpu.VMEM((N_HEADS * V_HEAD, KV_LORA), BF16),
        ],
        compiler_params=_cparams(("arbitrary",)),
        name="norm_a_proj",
    )(x2, g_mix, w_t, g_cq, g_ckv, wuq, wiq, wuk_t, wuv_t, tables)


def _uv_gate_kernel(n_ref, w_ref, *rest, n_z, sides):
    side_in, (out_ref, *side_out), w_bf = rest[:len(sides)], rest[len(sides):-1], rest[-1]
    j = pl.program_id(0)
    step = j * pl.num_programs(1) + pl.program_id(1)
    for side, src, dst in zip(sides, side_in, side_out):
        side.emit(step, src, dst)

    @pl.when(pl.program_id(1) == 0)
    def _():
        w_bf[...] = w_ref[...].astype(BF16)

    def proj():
        return _dot_nt(n_ref[...], w_bf[...])

    @pl.when(j < n_z)
    def _():
        out_ref[...] = _gelu_exact(proj()).astype(BF16)

    @pl.when(j >= n_z)
    def _():
        out_ref[...] = _sigmoid(proj()).astype(BF16)


def _uv_gate_proj(n, w_t, wa, wb, wo, *, tm, tn):
    T, D = n.shape
    n_z = (2 * SGU_WIDTH) // tn
    n_out = 2 * SGU_WIDTH + 2 * D
    nm = T // tm
    assert w_t.shape == (IN_SMALL + n_out, D) and IN_SMALL % 8 == 0 and tn % 8 == 0
    step_of = lambda j, i: j * nm + i
    a = _pow2_floor((n_out // tn) * nm // 4)
    sides = [_SideRound(wa, 0, a, step_of), _SideRound(wb, a, a, step_of), _SideRound(wo, 2 * a, 2 * a, step_of)]
    kern = functools.partial(_uv_gate_kernel, n_z=n_z, sides=sides)
    return pl.pallas_call(
        kern,
        grid=(n_out // tn, nm),
        in_specs=[
            pl.BlockSpec((tm, D), lambda j, i: (i, 0)),
            pl.BlockSpec((pl.Element(tn), pl.Element(D)), lambda j, i: ((IN_SMALL // 8 + j * (tn // 8)) * 8, 0)),
        ] + [s.spec for s in sides],
        out_specs=[pl.BlockSpec((tm, tn), lambda j, i: (i, j))] + [s.spec for s in sides],
        out_shape=[jax.ShapeDtypeStruct((T, n_out), BF16)] + [s.out_shape for s in sides],
        scratch_shapes=[pltpu.VMEM((tn, D), BF16)],
        compiler_params=_cparams(("arbitrary", "arbitrary")),
        name="uv_gate_proj",
    )(n, w_t, wa, wb, wo)


def _dsa_index_kernel(kilo_ref, kihi_ref, qi_ref, wt_ref, bias_ref, isc_ref, mm_ref, js_ref,
                      *, S, TQ, KCH, topk):
    j = pl.program_id(1)
    q0 = j * TQ
    nkeys = q0 + TQ
    qidx = q0 + lax.broadcasted_iota(jnp.int32, (1, TQ), 1)
    kf = float(topk)

    mm_ref[0:8, :] = jnp.full((8, TQ), jnp.inf, F32)
    mm_ref[8:16, :] = jnp.full((8, TQ), -jnp.inf, F32)
    for c in range(S // KCH):
        @pl.when(c * KCH < nkeys)
        def _(c=c):
            klo = kilo_ref[c * KCH:(c + 1) * KCH, :]
            khi = kihi_ref[c * KCH:(c + 1) * KCH, :]
            acc = jnp.zeros((KCH, TQ), F32)
            for hp in range(IDX_HEADS // 2):
                qp = qi_ref[:, hp * LANES:(hp + 1) * LANES]
                s0 = _dot_nt(klo, qp)
                s1 = _dot_nt(khi, qp)
                acc = acc + jnp.maximum(s0, 0.0) * wt_ref[2 * hp:2 * hp + 1, :]
                acc = acc + jnp.maximum(s1, 0.0) * wt_ref[2 * hp + 1:2 * hp + 2, :]
            kidx = c * KCH + lax.broadcasted_iota(jnp.int32, (KCH, TQ), 0)
            causal = kidx <= qidx
            isc_ref[c * KCH:(c + 1) * KCH, :] = jnp.where(causal, acc, -jnp.inf)
            lo_part = jnp.where(causal, acc, jnp.inf).reshape(KCH // 8, 8, TQ).min(axis=0)
            hi_part = jnp.where(causal, acc, -jnp.inf).reshape(KCH // 8, 8, TQ).max(axis=0)
            mm_ref[0:8, :] = jnp.minimum(mm_ref[0:8, :], lo_part)
            mm_ref[8:16, :] = jnp.maximum(mm_ref[8:16, :], hi_part)

    def select(nch):
        def count(pred):
            groups = []
            for lg in range(TQ // LANES):
                lanes = slice(lg * LANES, (lg + 1) * LANES)
                acc = jnp.zeros((COUNT_CHAINS, 8, LANES), F32)
                for c in range(nch):
                    ones = jnp.where(pred(isc_ref[c * TQ:(c + 1) * TQ, lanes], c * TQ, lanes), 1.0, 0.0)
                    acc = acc + ones.reshape(COUNT_CHAINS, TQ // (8 * COUNT_CHAINS), 8, LANES).sum(axis=1)
                groups.append(acc.sum(axis=0).sum(axis=0, keepdims=True))
            return jnp.concatenate(groups, axis=1)

        row_min = mm_ref[0:8, :].min(axis=0, keepdims=True)
        row_max = mm_ref[8:16, :].max(axis=0, keepdims=True)
        full = (qidx + 1) <= topk
        c_max = count(lambda blk, k0, lanes: blk >= row_max[:, lanes])
        exact0 = c_max == kf
        tie0 = c_max > kf
        settled0 = full | exact0 | tie0
        lo0 = jnp.where(full, -F32_MAX, jnp.where(settled0, row_max, row_min))
        hi0 = jnp.where(full, -F32_MAX, jnp.where(tie0, jnp.inf, row_max))
        act0 = jnp.where(settled0, 0.0, 1.0)

        def step(lo, hi, act):
            mid = lo * 0.5 + hi * 0.5
            inside = (mid > lo) & (mid < hi)
            cnt = count(lambda blk, k0, lanes: blk >= mid[:, lanes])
            upd = (act > 0.0) & inside
            found = upd & (cnt == kf)
            lo2 = jnp.where(upd & (cnt >= kf), mid, lo)
            hi2 = jnp.where(upd & (cnt <= kf), mid, hi)
            return lo2, hi2, jnp.where(upd & jnp.logical_not(found), 1.0, 0.0)

        def any_active(act):
            return (jnp.max(act) > 0.0).astype(jnp.int32)

        def body(st):
            lo, hi, act, _ = st
            for _ in range(STEPS_PER_CHECK):
                lo, hi, act = step(lo, hi, act)
            return lo, hi, act, any_active(act)

        lo, hi, _, _ = lax.while_loop(lambda st: st[3] > 0, body, (lo0, hi0, act0, any_active(act0)))

        tie = lo < hi
        js_ref[0:8, :] = jnp.full((8, TQ), -1.0, F32)

        @pl.when(jnp.max(jnp.where(tie, 1.0, 0.0)) > 0.0)
        def _():
            need = kf - count(lambda blk, k0, lanes: blk >= hi[:, lanes])

            def kpos(k0):
                return (k0 + lax.broadcasted_iota(jnp.int32, (TQ, LANES), 0)).astype(F32)

            def tie_step(_, st):
                ilo, ihi = st
                imid = jnp.floor((ilo + ihi) * 0.5)
                cnt = count(lambda blk, k0, lanes: (blk >= lo[:, lanes]) & (blk < hi[:, lanes])
                            & (kpos(k0) <= imid[:, lanes]))
                ge = cnt >= need
                return jnp.where(ge, ilo, imid), jnp.where(ge, imid, ihi)

            nsteps = int(math.ceil(math.log2(S))) + 1
            _, ihi = lax.fori_loop(0, nsteps, tie_step,
                                   (jnp.full((1, TQ), -1.0, F32), jnp.full((1, TQ), S - 1.0, F32)))
            js_ref[0:1, :] = jnp.where(tie, ihi, -1.0)

        jstar = js_ref[0:1, :]

        for c in range(S // TQ):
            if c < nch:
                blk = isc_ref[c * TQ:(c + 1) * TQ, :]
                kpos_c = (c * TQ + lax.broadcasted_iota(jnp.int32, (TQ, TQ), 0)).astype(F32)
                sel = (blk >= hi) | ((blk >= lo) & (kpos_c <= jstar))
                bias_ref[:, c * TQ:(c + 1) * TQ] = jnp.where(sel, 0.0, NEG_BIAS).T
            else:
                bias_ref[:, c * TQ:(c + 1) * TQ] = jnp.full((TQ, TQ), NEG_BIAS, F32)

    for jj in range(S // TQ):
        @pl.when(j == jj)
        def _(jj=jj):
            select(jj + 1)


def _dsa_index(ki_lo, ki_hi, qi, wt, *, B, S, TQ, KCH, topk):
    T = B * S
    nq = S // TQ
    kern = functools.partial(_dsa_index_kernel, S=S, TQ=TQ, KCH=KCH, topk=topk)
    return pl.pallas_call(
        kern,
        grid=(B, nq),
        in_specs=[
            pl.BlockSpec((S, LANES), lambda b, j: (b, 0)),
            pl.BlockSpec((S, LANES), lambda b, j: (b, 0)),
            pl.BlockSpec((TQ, IDX_HEADS * IDX_DIM), lambda b, j: (b * nq + j, 0)),
            pl.BlockSpec((IDX_HEADS, TQ), lambda b, j: (0, b * nq + j)),
        ],
        out_specs=pl.BlockSpec((TQ, S), lambda b, j: (b * nq + j, 0)),
        out_shape=jax.ShapeDtypeStruct((T, S), F32),
        scratch_shapes=[
            pltpu.VMEM((S, TQ), F32),
            pltpu.VMEM((16, TQ), F32),
            pltpu.VMEM((16, TQ), F32),
        ],
        compiler_params=_cparams(("parallel", "arbitrary")),
        name="dsa_index",
    )(ki_lo, ki_hi, qi, wt)


def _dsa_attn_kernel(q_ref, k_ref, v_ref, bias_ref, o_ref, *, S, TQ):
    j = pl.program_id(1)
    lane = lax.broadcasted_iota(jnp.int32, (TQ, LANES), 1)

    def variant(nk):
        ones = jnp.ones((nk, LANES), BF16)

        def pair(p, carry):
            vp = jnp.concatenate([v_ref[0, p, 0:nk, :], ones], axis=1)
            outs = []
            for e in range(2):
                h = 2 * p + e
                s = _dot_nt(q_ref[0, h], k_ref[0, h, 0:nk, :]) + bias_ref[:, 0:nk]
                m = s.max(axis=1, keepdims=True)
                pv = _dot(jnp.exp2(s - m).astype(BF16), vp)
                outs.append(pv[:, 0:LANES] * (1.0 / pv[:, LANES:2 * LANES]))
            o_ref[0, p] = jnp.where(lane < V_HEAD, outs[0], outs[1]).astype(BF16)
            return carry
        lax.fori_loop(0, N_HEADS // 2, pair, 0, unroll=4)

    for jj in range(S // TQ):
        @pl.when(j == jj)
        def _(jj=jj):
            variant((jj + 1) * TQ)


def _dsa_attn(q, k, v, bias, *, B, S, TQ):
    nq = S // TQ
    kern = functools.partial(_dsa_attn_kernel, S=S, TQ=TQ)
    return pl.pallas_call(
        kern,
        grid=(B, nq),
        in_specs=[
            pl.BlockSpec((1, N_HEADS, TQ, LANES), lambda b, j: (b, 0, j, 0)),
            pl.BlockSpec((1, N_HEADS, S, LANES), lambda b, j: (b, 0, 0, 0)),
            pl.BlockSpec((1, N_HEADS // 2, S, LANES), lambda b, j: (b, 0, 0, 0)),
            pl.BlockSpec((TQ, S), lambda b, j: (b * nq + j, 0)),
        ],
        out_specs=pl.BlockSpec((1, N_HEADS // 2, TQ, LANES), lambda b, j: (b, 0, j, 0)),
        out_shape=jax.ShapeDtypeStruct((B, N_HEADS // 2, S, LANES), BF16),
        compiler_params=_cparams(("parallel", "arbitrary")),
        name="dsa_attn",
    )(q, k, v, bias)


def _branch_out_kernel(o_ref, z_ref, ga_ref, gb_ref, x_ref, gs_ref, ws_ref, b_ref, gf_ref,
                       wa_bf, wb_bf, wo_bf, h_ref, n_ref, y_scr, *, tm):
    row = lax.broadcasted_iota(jnp.int32, (SGU_CHUNK, SGU_CHUNK), 0)
    col = lax.broadcasted_iota(jnp.int32, (SGU_CHUNK, SGU_CHUNK), 1)
    tril = col <= row
    w = [jnp.where(tril, ws_ref[g], 0.0).astype(BF16) for g in range(SGU_GROUPS)]
    b_t = jnp.transpose(b_ref[...])
    for cc in range(tm // SGU_CHUNK):
        rows = slice(cc * SGU_CHUNK, (cc + 1) * SGU_CHUNK)
        vn = _rms(z_ref[rows, SGU_WIDTH:2 * SGU_WIDTH].astype(F32), gs_ref[...]).astype(BF16)
        for g in range(SGU_GROUPS):
            cols = slice(g * SGU_GROUP_DIM, (g + 1) * SGU_GROUP_DIM)
            mixed = _dot(w[g], vn[:, cols]) + b_t[:, g:g + 1]
            y_scr[rows, cols] = (z_ref[rows, cols].astype(F32) * mixed).astype(BF16)

    o_a = jnp.concatenate([o_ref[0, p] for p in range(N_HEADS // 2)], axis=1)
    ya = _dot(o_a, wa_bf[...])
    yb = _dot(y_scr[...], wb_bf[...])
    merged = (ga_ref[...].astype(F32) * ya + gb_ref[...].astype(F32) * yb).astype(BF16)
    h = x_ref[...] + _dot(merged, wo_bf[...])
    h_ref[...] = h
    n_ref[...] = _rms(h, gf_ref[...]).astype(BF16)


def _branch_out(o, zg, x2, g_sgu, w_spatial, b_spatial, g_ffn, wa, wb, wo, *, B, S, tm):
    T, D = x2.shape
    nt = S // tm
    assert 2 * SGU_WIDTH == D and wa.shape == wb.shape == (SGU_WIDTH, D) and wo.shape == (D, D)
    row = lambda c: pl.BlockSpec((tm, D), lambda i: (i, c))
    const = lambda shape: pl.BlockSpec(shape, lambda i: (0,) * len(shape))
    resident = lambda w: pl.BlockSpec(w.shape, lambda i: (0, 0), pipeline_mode=pl.Buffered(1))
    kern = functools.partial(_branch_out_kernel, tm=tm)
    return pl.pallas_call(
        kern,
        grid=(T // tm,),
        in_specs=[
            pl.BlockSpec((1, N_HEADS // 2, tm, LANES), lambda i: (i // nt, 0, i % nt, 0)),
            row(0), row(1), row(2),
            row(0),
            const((1, SGU_WIDTH)),
            const((SGU_GROUPS, SGU_CHUNK, SGU_CHUNK)),
            const((SGU_GROUPS, SGU_CHUNK)),
            const((1, D)),
            resident(wa), resident(wb), resident(wo),
        ],
        out_specs=[pl.BlockSpec((tm, D), lambda i: (i, 0)), pl.BlockSpec((tm, D), lambda i: (i, 0))],
        out_shape=[jax.ShapeDtypeStruct((T, D), F32), jax.ShapeDtypeStruct((T, D), BF16)],
        scratch_shapes=[pltpu.VMEM((tm, SGU_WIDTH), BF16)],
        compiler_params=_cparams(("parallel",), BRANCH_OUT_VMEM_LIMIT_BYTES),
        name="branch_out",
    )(o, zg, zg, zg, x2, g_sgu, w_spatial, b_spatial, g_ffn, wa, wb, wo)


def _ffn_up_kernel(n_ref, wg_ref, wu_ref, wd_ref, a_ref, wd_bf_ref, wg_bf, wu_bf):
    @pl.when(pl.program_id(1) == 0)
    def _():
        wg_bf[...] = wg_ref[...].astype(BF16)
        wu_bf[...] = wu_ref[...].astype(BF16)

    wd_bf_ref[...] = wd_ref[...].astype(BF16)
    n = n_ref[...]
    g = _dot(n, wg_bf[...])
    u = _dot(n, wu_bf[...])
    a_ref[...] = (g * _sigmoid(g) * u).astype(BF16)


def _ffn_up(n2, w_gu, w_down, *, tm, tn):
    T, D = n2.shape
    d_ff = w_down.shape[0]
    nn, nm = d_ff // tn, T // tm
    slab = d_ff // (nn * nm)
    assert slab * nn * nm == d_ff and slab % 16 == 0
    return pl.pallas_call(
        _ffn_up_kernel,
        grid=(nn, nm),
        in_specs=[
            pl.BlockSpec((tm, D), lambda j, i: (i, 0)),
            pl.BlockSpec((D, tn), lambda j, i: (0, j)),
            pl.BlockSpec((D, tn), lambda j, i: (0, nn + j)),
            pl.BlockSpec((slab, D), lambda j, i: (j * nm + i, 0)),
        ],
        out_specs=[
            pl.BlockSpec((tm, tn), lambda j, i: (i, j)),
            pl.BlockSpec((slab, D), lambda j, i: (j * nm + i, 0)),
        ],
        out_shape=[jax.ShapeDtypeStruct((T, d_ff), BF16), jax.ShapeDtypeStruct((d_ff, D), BF16)],
        scratch_shapes=[pltpu.VMEM((D, tn), BF16), pltpu.VMEM((D, tn), BF16)],
        compiler_params=_cparams(("arbitrary", "arbitrary")),
        name="ffn_up",
    )(n2, w_gu, w_gu, w_down)


def _ffn_down_kernel(a_ref, w_ref, h1_ref, wg_ref, wp_ref, h2_ref, wg_bf_ref, wp_bf_ref, *, sides):
    step = pl.program_id(0) * pl.num_programs(1) + pl.program_id(1)
    for side, src, dst in zip(sides, (wg_ref, wp_ref), (wg_bf_ref, wp_bf_ref)):
        side.emit(step, src, dst)
    h2_ref[...] = h1_ref[...] + _dot(a_ref[...], w_ref[...])


def _ffn_down(act, w_down_bf, h1, w_pg, w_pp, *, tm, tn):
    T, D = h1.shape
    d_ff = act.shape[1]
    nn, nm = D // tn, T // tm
    step_of = lambda j, i: j * nm + i
    a = _pow2_floor(nn * nm // 2)
    b = min(a, w_pp.shape[0] // 16)
    sides = [_SideRound(w_pg, 0, a, step_of), _SideRound(w_pp, a, b, step_of)]
    kern = functools.partial(_ffn_down_kernel, sides=sides)
    return pl.pallas_call(
        kern,
        grid=(nn, nm),
        in_specs=[
            pl.BlockSpec((tm, d_ff), lambda j, i: (i, 0)),
            pl.BlockSpec((d_ff, tn), lambda j, i: (0, j)),
            pl.BlockSpec((tm, tn), lambda j, i: (i, j)),
        ] + [s.spec for s in sides],
        out_specs=[pl.BlockSpec((tm, tn), lambda j, i: (i, j))] + [s.spec for s in sides],
        out_shape=[jax.ShapeDtypeStruct((T, D), F32)] + [s.out_shape for s in sides],
        compiler_params=_cparams(("arbitrary", "arbitrary")),
        name="ffn_down",
    )(act, w_down_bf, h1, w_pg, w_pp)


def _ple_final_kernel(h2_ref, p_ref, gp_ref, gf_ref, wg_bf, wp_bf, out_ref):
    h2 = h2_ref[...]
    gate = _sigmoid(_dot(_rms(h2, gp_ref[...]).astype(BF16), wg_bf[...]))
    pp = _dot(p_ref[...].astype(BF16), wp_bf[...])
    out_ref[...] = _rms(h2 + gate * pp, gf_ref[...])


def _ple_final(h2, p2, w_pg, w_pp, g_ple, g_final, *, tm):
    T, D = h2.shape
    P = p2.shape[1]
    resident = lambda w: pl.BlockSpec(w.shape, lambda i: (0, 0), pipeline_mode=pl.Buffered(1))
    return pl.pallas_call(
        _ple_final_kernel,
        grid=(T // tm,),
        in_specs=[
            pl.BlockSpec((tm, D), lambda i: (i, 0)),
            pl.BlockSpec((tm, P), lambda i: (i, 0)),
            pl.BlockSpec((1, D), lambda i: (0, 0)),
            pl.BlockSpec((1, D), lambda i: (0, 0)),
            resident(w_pg), resident(w_pp),
        ],
        out_specs=pl.BlockSpec((tm, D), lambda i: (i, 0)),
        out_shape=jax.ShapeDtypeStruct((T, D), F32),
        compiler_params=_cparams(("parallel",)),
        name="ple_final",
    )(h2, p2, g_ple, g_final, w_pg, w_pp)


def _lane_tables(S):
    assert QK_ROPE == IDX_ROPE == 32
    scale = (QK_NOPE + QK_ROPE) ** -0.5 * math.log2(math.e)
    a = np.zeros((N_TABLES, LANES), np.float64)
    bc = np.zeros_like(a)
    bs = np.zeros_like(a)
    a[TAB_Q, 0:QK_NOPE] = scale
    bc[TAB_Q, 64:96] = scale
    bs[TAB_Q + 1, 64:80] = -scale
    bs[TAB_Q + 1, 80:96] = scale
    for o in (0, IDX_DIM):
        bc[TAB_QI, o:o + 32] = 1.0
        a[TAB_QI, o + 32:o + 64] = 1.0
        bs[TAB_QI + 1, o:o + 32] = 1.0
    bc[TAB_KI, 0:32] = 1.0
    a[TAB_KI, 32:64] = 1.0
    bs[TAB_KI + 1, 0:16] = -1.0
    bs[TAB_KI + 2, 16:32] = 1.0
    bc[TAB_KR, 0:32] = 1.0
    bs[TAB_KR + 1, 0:16] = -1.0
    bs[TAB_KR + 2, 16:32] = 1.0
    inv = ROPE_THETA ** (-np.arange(0, QK_ROPE, 2, dtype=np.float64) / QK_ROPE)
    ang = np.arange(S, dtype=np.float64)[:, None] * np.tile(inv, LANES // inv.shape[0])[None, :]
    tables = a[:, None, :] + bc[:, None, :] * np.cos(ang)[None] + bs[:, None, :] * np.sin(ang)[None]
    return jnp.asarray(tables.astype(np.float32))


def _tiles(S):
    return dict(
        attn_q=256,
        index_keys=min(512, S),
        a_proj_rows=min(256, S),
        uv_gate_rows=min(1024, S), uv_gate_cols=1024,
        branch_out_rows=min(512, S),
        ffn_up_rows=min(1024, S), ffn_up_cols=512,
        ffn_down_rows=min(512, S), ffn_down_cols=1024,
        ple_rows=min(512, S),
    )


def kernel(x, p, g_mix, w_in, g_cq, g_ckv, w_uq, w_uk, w_uv, w_iq, w_a_proj, g_sgu, w_spatial,
           b_spatial, w_b_proj, w_o, g_ffn, w_gu, w_down, g_ple, w_ple_gate, w_ple_proj, g_final):
    B, S, D = x.shape
    T = B * S
    depth = w_in.shape[0]
    topk = min(TOPK_MAX, S // 4)
    t = _tiles(S)
    tables = _lane_tables(S)

    h = x.reshape(T, D)
    for i in range(depth):
        w_t = jnp.transpose(w_in[i])
        wuk_t = jnp.transpose(w_uk[i], (1, 2, 0)).reshape(N_HEADS * QK_NOPE, KV_LORA)
        wuv_t = jnp.transpose(w_uv[i], (1, 2, 0)).reshape(N_HEADS * V_HEAD, KV_LORA)
        n, q, k, v, qi, ki_lo, ki_hi, wt = _a_proj(h, g_mix[i][None], w_t, g_cq[i][None], g_ckv[i][None],
                                                    w_uq[i], w_iq[i], wuk_t, wuv_t,
                                                    tables, B=B, S=S, tm=t["a_proj_rows"])
        zg, wa_bf, wb_bf, wo_bf = _uv_gate_proj(n, w_t, w_a_proj[i], w_b_proj[i], w_o[i],
                                                tm=t["uv_gate_rows"], tn=t["uv_gate_cols"])
        bias = _dsa_index(ki_lo, ki_hi, qi, wt, B=B, S=S, TQ=t["attn_q"], KCH=t["index_keys"], topk=topk)
        o = _dsa_attn(q, k, v, bias, B=B, S=S, TQ=t["attn_q"])
        h1, n2 = _branch_out(o, zg, h, g_sgu[i][None], w_spatial[i], b_spatial[i],
                             g_ffn[i][None], wa_bf, wb_bf, wo_bf, B=B, S=S, tm=t["branch_out_rows"])
        act, w_down_bf = _ffn_up(n2, w_gu[i], w_down[i], tm=t["ffn_up_rows"], tn=t["ffn_up_cols"])
        h2, wpg_bf, wpp_bf = _ffn_down(act, w_down_bf, h1, w_ple_gate[i], w_ple_proj[i],
                                       tm=t["ffn_down_rows"], tn=t["ffn_down_cols"])
        assert depth == 1
        h = _ple_final(h2, p[i].reshape(T, -1), wpg_bf, wpp_bf, g_ple[i][None], g_final[None], tm=t["ple_rows"])
    return h.reshape(B, S, D)
```

```python
import functools
import math

import numpy as np
import jax
import jax.numpy as jnp
from jax import lax
from jax.experimental import pallas as pl
from jax.experimental.pallas import tpu as pltpu

F32 = jnp.float32
BF16 = jnp.bfloat16

N_HEADS = 16
QK_NOPE = 64
QK_ROPE = 32
V_HEAD = 64
Q_LORA = 512
KV_LORA = 256
IDX_HEADS = 16
IDX_DIM = 64
IDX_ROPE = 32
TOPK_MAX = 256
SGU_CHUNK = 128
SGU_GROUPS = 8
SGU_GROUP_DIM = 128
SGU_WIDTH = SGU_GROUPS * SGU_GROUP_DIM
ROPE_THETA = 10000.0
EPS = 1e-6

LANES = 128
IN_SMALL = Q_LORA + KV_LORA + QK_ROPE + IDX_DIM + IDX_HEADS
SMALL_COLS = 1024
COL_CKV = Q_LORA
COL_MISC = Q_LORA + KV_LORA
MISC_KIDX = QK_ROPE
MISC_WIDX = QK_ROPE + IDX_DIM
VMEM_LIMIT_BYTES = 56 * 1024 * 1024
BRANCH_OUT_VMEM_LIMIT_BYTES = 60 * 1024 * 1024
NEG_BIAS = -1e30
F32_MAX = 3.4028234663852886e38
STEPS_PER_CHECK = 4
COUNT_CHAINS = 4


def _cparams(semantics, vmem_limit_bytes=VMEM_LIMIT_BYTES):
    return pltpu.CompilerParams(dimension_semantics=semantics, vmem_limit_bytes=vmem_limit_bytes)


def _rms(x, g):
    return x * lax.rsqrt(jnp.mean(x * x, axis=-1, keepdims=True) + EPS) * g


def _dot(a, b):
    return jnp.dot(a, b, preferred_element_type=F32)


def _dot_nt(a, b):
    return lax.dot_general(a, b, (((1,), (1,)), ((), ())), preferred_element_type=F32)


def _gelu_exact(x):
    return 0.5 * x * (1.0 + lax.erf(x * (1.0 / math.sqrt(2.0))))


def _sigmoid(x):
    return 0.5 * jnp.tanh(0.5 * x) + 0.5


class _SideRound:
    def __init__(self, w, first, steps, step_of):
        rows, cols = w.shape
        slab = rows // steps
        assert slab * steps == rows and slab % 16 == 0
        self.first, self.steps = first, steps
        index = lambda *g: (jnp.clip(step_of(*g) - first, 0, steps - 1), 0)
        self.spec = pl.BlockSpec((slab, cols), index)
        self.out_shape = jax.ShapeDtypeStruct((rows, cols), BF16)

    def emit(self, step, src_ref, dst_ref):
        @pl.when((step >= self.first) & (step < self.first + self.steps))
        def _():
            dst_ref[...] = src_ref[...].astype(BF16)


def _pow2_floor(n):
    return 1 << (max(int(n), 1).bit_length() - 1)


TAB_Q = 0
TAB_QI = 2
TAB_KI = 4
TAB_KR = 7
N_TABLES = 10


def _rope_block(x, cos, sin_a, sin_b):
    return x * cos + pltpu.roll(x, LANES - 16, 1) * sin_a + pltpu.roll(x, 16, 1) * sin_b


def _a_proj_kernel(x_ref, gmix_ref, ws_ref, gcq_ref, gckv_ref, wuq_ref, wiq_ref, wuk_ref, wuv_ref, tab_ref,
                   n_ref, q_ref, k_ref, v_ref, qi_ref, kilo_ref, kihi_ref, wt_ref,
                   ws_bf, wq_bf, wiq_bf, wiqr_bf, wk_bf, wv_bf):
    @pl.when(pl.program_id(0) == 0)
    def _():
        ws_bf[...] = ws_ref[...].astype(BF16)
        wv_bf[...] = wuv_ref[...].astype(BF16)
        wuq_t = wuq_ref[...].T
        wiq_t = wiq_ref[...].T
        wiq_bf[...] = wiq_t.astype(BF16)
        r1 = QK_ROPE // 2
        hd = QK_NOPE + QK_ROPE
        for h in range(N_HEADS):
            src, dst = h * hd, h * LANES
            wq_bf[dst:dst + hd, :] = wuq_t[src:src + hd, :].astype(BF16)
            wq_bf[dst + hd:dst + hd + r1, :] = wuq_t[src + QK_NOPE + r1:src + hd, :].astype(BF16)
            wq_bf[dst + hd + r1:dst + LANES, :] = wuq_t[src + QK_NOPE:src + QK_NOPE + r1, :].astype(BF16)
            wk_bf[dst:dst + QK_NOPE, :] = wuk_ref[h * QK_NOPE:(h + 1) * QK_NOPE, :].astype(BF16)
            wk_bf[dst + QK_NOPE:dst + LANES, :] = jnp.zeros((LANES - QK_NOPE, wk_bf.shape[1]), BF16)
        ri = IDX_ROPE // 2
        for h in range(IDX_HEADS):
            o = h * IDX_DIM
            wiqr_bf[o:o + ri, :] = (-wiq_t[o + ri:o + 2 * ri, :]).astype(BF16)
            wiqr_bf[o + ri:o + 2 * ri, :] = wiq_t[o:o + ri, :].astype(BF16)
            wiqr_bf[o + 2 * ri:o + IDX_DIM, :] = jnp.zeros((IDX_DIM - 2 * ri, wiqr_bf.shape[1]), BF16)

    n = _rms(x_ref[...], gmix_ref[...]).astype(BF16)
    n_ref[...] = n
    small = _dot_nt(n, ws_bf[...])
    c_q = _rms(small[:, 0:Q_LORA], gcq_ref[...]).astype(BF16)
    c_kv = _rms(small[:, COL_CKV:COL_CKV + KV_LORA], gckv_ref[...]).astype(BF16)

    q = _dot_nt(c_q, wq_bf[...])
    cq, sq = tab_ref[TAB_Q], tab_ref[TAB_Q + 1]
    for h in range(N_HEADS):
        blk = q[:, h * LANES:(h + 1) * LANES]
        q_ref[0, h] = (blk * cq + pltpu.roll(blk, LANES - QK_ROPE, 1) * sq).astype(BF16)

    ci, si = tab_ref[TAB_QI], tab_ref[TAB_QI + 1]
    qi = _dot_nt(c_q, wiq_bf[...])
    qir = _dot_nt(c_q, wiqr_bf[...])
    for hp in range(IDX_HEADS // 2):
        cols = slice(hp * LANES, (hp + 1) * LANES)
        qi_ref[:, cols] = (qi[:, cols] * ci + qir[:, cols] * si).astype(BF16)

    misc = small[:, COL_MISC:COL_MISC + LANES]
    ki_lo = _rope_block(pltpu.roll(misc, LANES - MISC_KIDX, 1),
                        tab_ref[TAB_KI], tab_ref[TAB_KI + 1], tab_ref[TAB_KI + 2])
    kilo_ref[...] = ki_lo.astype(BF16)
    kihi_ref[...] = pltpu.roll(ki_lo, IDX_DIM, 1).astype(BF16)

    k_rope = pltpu.roll(_rope_block(misc, tab_ref[TAB_KR], tab_ref[TAB_KR + 1], tab_ref[TAB_KR + 2]),
                        QK_NOPE, 1)
    k_nope = _dot_nt(c_kv, wk_bf[...])
    for h in range(N_HEADS):
        k_ref[0, h] = (k_nope[:, h * LANES:(h + 1) * LANES] + k_rope).astype(BF16)

    v = _dot_nt(c_kv, wv_bf[...])
    for p in range(N_HEADS // 2):
        v_ref[0, p] = v[:, p * LANES:(p + 1) * LANES].astype(BF16)

    w_scale = IDX_HEADS ** -0.5 * IDX_DIM ** -0.5
    wt_ref[...] = misc.T[MISC_WIDX:MISC_WIDX + IDX_HEADS, :] * w_scale


def _a_proj(x2, g_mix, w_t, g_cq, g_ckv, wuq, wiq, wuk_t, wuv_t, tables, *, B, S, tm):
    T, D = x2.shape
    nt = S // tm
    const2 = lambda i: (0, 0)
    once = lambda w: pl.BlockSpec(w.shape, const2, pipeline_mode=pl.Buffered(1))
    head_spec = lambda nh: pl.BlockSpec((1, nh, tm, LANES), lambda i: (i // nt, 0, i % nt, 0))
    return pl.pallas_call(
        _a_proj_kernel,
        grid=(T // tm,),
        in_specs=[
            pl.BlockSpec((tm, D), lambda i: (i, 0)),
            pl.BlockSpec((1, D), const2),
            pl.BlockSpec((SMALL_COLS, D), const2, pipeline_mode=pl.Buffered(1)),
            pl.BlockSpec((1, Q_LORA), const2),
            pl.BlockSpec((1, KV_LORA), const2),
            once(wuq), once(wiq), once(wuk_t), once(wuv_t),
            pl.BlockSpec((N_TABLES, tm, LANES), lambda i: (0, i % nt, 0)),
        ],
        out_specs=[
            pl.BlockSpec((tm, D), lambda i: (i, 0)),
            head_spec(N_HEADS),
            head_spec(N_HEADS),
            head_spec(N_HEADS // 2),
            pl.BlockSpec((tm, IDX_HEADS * IDX_DIM), lambda i: (i, 0)),
            pl.BlockSpec((tm, LANES), lambda i: (i, 0)),
            pl.BlockSpec((tm, LANES), lambda i: (i, 0)),
            pl.BlockSpec((IDX_HEADS, tm), lambda i: (0, i)),
        ],
        out_shape=[
            jax.ShapeDtypeStruct((T, D), BF16),
            jax.ShapeDtypeStruct((B, N_HEADS, S, LANES), BF16),
            jax.ShapeDtypeStruct((B, N_HEADS, S, LANES), BF16),
            jax.ShapeDtypeStruct((B, N_HEADS // 2, S, LANES), BF16),
            jax.ShapeDtypeStruct((T, IDX_HEADS * IDX_DIM), BF16),
            jax.ShapeDtypeStruct((T, LANES), BF16),
            jax.ShapeDtypeStruct((T, LANES), BF16),
            jax.ShapeDtypeStruct((IDX_HEADS, T), F32),
        ],
        scratch_shapes=[
            pltpu.VMEM((SMALL_COLS, D), BF16),
            pltpu.VMEM((N_HEADS * LANES, Q_LORA), BF16),
            pltpu.VMEM((IDX_HEADS * IDX_DIM, Q_LORA), BF16),
            pltpu.VMEM((IDX_HEADS * IDX_DIM, Q_LORA), BF16),
            pltpu.VMEM((N_HEADS * LANES, KV_LORA), BF16),
            pltpu.VMEM((N_HEADS * V_HEAD, KV_LORA), BF16),
        ],
        compiler_params=_cparams(("arbitrary",)),
        name="norm_a_proj",
    )(x2, g_mix, w_t, g_cq, g_ckv, wuq, wiq, wuk_t, wuv_t, tables)


def _uv_gate_kernel(n_ref, w_ref, *rest, n_z, sides):
    side_in, (out_ref, *side_out), w_bf = rest[:len(sides)], rest[len(sides):-1], rest[-1]
    j = pl.program_id(0)
    step = j * pl.num_programs(1) + pl.program_id(1)
    for side, src, dst in zip(sides, side_in, side_out):
        side.emit(step, src, dst)

    @pl.when(pl.program_id(1) == 0)
    def _():
        w_bf[...] = w_ref[...].astype(BF16)

    def proj():
        return _dot_nt(n_ref[...], w_bf[...])

    @pl.when(j < n_z)
    def _():
        out_ref[...] = _gelu_exact(proj()).astype(BF16)

    @pl.when(j >= n_z)
    def _():
        out_ref[...] = _sigmoid(proj()).astype(BF16)


def _uv_gate_proj(n, w_t, wa, wb, wo, *, tm, tn):
    T, D = n.shape
    n_z = (2 * SGU_WIDTH) // tn
    n_out = 2 * SGU_WIDTH + 2 * D
    nm = T // tm
    assert w_t.shape == (IN_SMALL + n_out, D) and IN_SMALL % 8 == 0 and tn % 8 == 0
    step_of = lambda j, i: j * nm + i
    a = _pow2_floor((n_out // tn) * nm // 4)
    sides = [_SideRound(wa, 0, a, step_of), _SideRound(wb, a, a, step_of), _SideRound(wo, 2 * a, 2 * a, step_of)]
    kern = functools.partial(_uv_gate_kernel, n_z=n_z, sides=sides)
    return pl.pallas_call(
        kern,
        grid=(n_out // tn, nm),
        in_specs=[
            pl.BlockSpec((tm, D), lambda j, i: (i, 0)),
            pl.BlockSpec((pl.Element(tn), pl.Element(D)), lambda j, i: ((IN_SMALL // 8 + j * (tn // 8)) * 8, 0)),
        ] + [s.spec for s in sides],
        out_specs=[pl.BlockSpec((tm, tn), lambda j, i: (i, j))] + [s.spec for s in sides],
        out_shape=[jax.ShapeDtypeStruct((T, n_out), BF16)] + [s.out_shape for s in sides],
        scratch_shapes=[pltpu.VMEM((tn, D), BF16)],
        compiler_params=_cparams(("arbitrary", "arbitrary")),
        name="uv_gate_proj",
    )(n, w_t, wa, wb, wo)


def _dsa_index_kernel(kilo_ref, kihi_ref, qi_ref, wt_ref, bias_ref, isc_ref, mm_ref, js_ref,
                      *, S, TQ, KCH, topk):
    j = pl.program_id(1)
    q0 = j * TQ
    nkeys = q0 + TQ
    qidx = q0 + lax.broadcasted_iota(jnp.int32, (1, TQ), 1)
    kf = float(topk)

    mm_ref[0:8, :] = jnp.full((8, TQ), jnp.inf, F32)
    mm_ref[8:16, :] = jnp.full((8, TQ), -jnp.inf, F32)
    for c in range(S // KCH):
        @pl.when(c * KCH < nkeys)
        def _(c=c):
            klo = kilo_ref[c * KCH:(c + 1) * KCH, :]
            khi = kihi_ref[c * KCH:(c + 1) * KCH, :]
            acc = jnp.zeros((KCH, TQ), F32)
            for hp in range(IDX_HEADS // 2):
                qp = qi_ref[:, hp * LANES:(hp + 1) * LANES]
                s0 = _dot_nt(klo, qp)
                s1 = _dot_nt(khi, qp)
                acc = acc + jnp.maximum(s0, 0.0) * wt_ref[2 * hp:2 * hp + 1, :]
                acc = acc + jnp.maximum(s1, 0.0) * wt_ref[2 * hp + 1:2 * hp + 2, :]
            kidx = c * KCH + lax.broadcasted_iota(jnp.int32, (KCH, TQ), 0)
            causal = kidx <= qidx
            isc_ref[c * KCH:(c + 1) * KCH, :] = jnp.where(causal, acc, -jnp.inf)
            lo_part = jnp.where(causal, acc, jnp.inf).reshape(KCH // 8, 8, TQ).min(axis=0)
            hi_part = jnp.where(causal, acc, -jnp.inf).reshape(KCH // 8, 8, TQ).max(axis=0)
            mm_ref[0:8, :] = jnp.minimum(mm_ref[0:8, :], lo_part)
            mm_ref[8:16, :] = jnp.maximum(mm_ref[8:16, :], hi_part)

    def select(nch):
        def count(pred):
            groups = []
            for lg in range(TQ // LANES):
                lanes = slice(lg * LANES, (lg + 1) * LANES)
                acc = jnp.zeros((COUNT_CHAINS, 8, LANES), F32)
                for c in range(nch):
                    ones = jnp.where(pred(isc_ref[c * TQ:(c + 1) * TQ, lanes], c * TQ, lanes), 1.0, 0.0)
                    acc = acc + ones.reshape(COUNT_CHAINS, TQ // (8 * COUNT_CHAINS), 8, LANES).sum(axis=1)
                groups.append(acc.sum(axis=0).sum(axis=0, keepdims=True))
            return jnp.concatenate(groups, axis=1)

        row_min = mm_ref[0:8, :].min(axis=0, keepdims=True)
        row_max = mm_ref[8:16, :].max(axis=0, keepdims=True)
        full = (qidx + 1) <= topk
        c_max = count(lambda blk, k0, lanes: blk >= row_max[:, lanes])
        exact0 = c_max == kf
        tie0 = c_max > kf
        settled0 = full | exact0 | tie0
        lo0 = jnp.where(full, -F32_MAX, jnp.where(settled0, row_max, row_min))
        hi0 = jnp.where(full, -F32_MAX, jnp.where(tie0, jnp.inf, row_max))
        act0 = jnp.where(settled0, 0.0, 1.0)

        def step(lo, hi, act):
            mid = lo * 0.5 + hi * 0.5
            inside = (mid > lo) & (mid < hi)
            cnt = count(lambda blk, k0, lanes: blk >= mid[:, lanes])
            upd = (act > 0.0) & inside
            found = upd & (cnt == kf)
            lo2 = jnp.where(upd & (cnt >= kf), mid, lo)
            hi2 = jnp.where(upd & (cnt <= kf), mid, hi)
            return lo2, hi2, jnp.where(upd & jnp.logical_not(found), 1.0, 0.0)

        def any_active(act):
            return (jnp.max(act) > 0.0).astype(jnp.int32)

        def body(st):
            lo, hi, act, _ = st
            for _ in range(STEPS_PER_CHECK):
                lo, hi, act = step(lo, hi, act)
            return lo, hi, act, any_active(act)

        lo, hi, _, _ = lax.while_loop(lambda st: st[3] > 0, body, (lo0, hi0, act0, any_active(act0)))

        tie = lo < hi
        js_ref[0:8, :] = jnp.full((8, TQ), -1.0, F32)

        @pl.when(jnp.max(jnp.where(tie, 1.0, 0.0)) > 0.0)
        def _():
            need = kf - count(lambda blk, k0, lanes: blk >= hi[:, lanes])

            def kpos(k0):
                return (k0 + lax.broadcasted_iota(jnp.int32, (TQ, LANES), 0)).astype(F32)

            def tie_step(_, st):
                ilo, ihi = st
                imid = jnp.floor((ilo + ihi) * 0.5)
                cnt = count(lambda blk, k0, lanes: (blk >= lo[:, lanes]) & (blk < hi[:, lanes])
                            & (kpos(k0) <= imid[:, lanes]))
                ge = cnt >= need
                return jnp.where(ge, ilo, imid), jnp.where(ge, imid, ihi)

            nsteps = int(math.ceil(math.log2(S))) + 1
            _, ihi = lax.fori_loop(0, nsteps, tie_step,
                                   (jnp.full((1, TQ), -1.0, F32), jnp.full((1, TQ), S - 1.0, F32)))
            js_ref[0:1, :] = jnp.where(tie, ihi, -1.0)

        jstar = js_ref[0:1, :]

        for c in range(S // TQ):
            if c < nch:
                blk = isc_ref[c * TQ:(c + 1) * TQ, :]
                kpos_c = (c * TQ + lax.broadcasted_iota(jnp.int32, (TQ, TQ), 0)).astype(F32)
                sel = (blk >= hi) | ((blk >= lo) & (kpos_c <= jstar))
                bias_ref[:, c * TQ:(c + 1) * TQ] = jnp.where(sel, 0.0, NEG_BIAS).T
            else:
                bias_ref[:, c * TQ:(c + 1) * TQ] = jnp.full((TQ, TQ), NEG_BIAS, F32)

    for jj in range(S // TQ):
        @pl.when(j == jj)
        def _(jj=jj):
            select(jj + 1)


def _dsa_index(ki_lo, ki_hi, qi, wt, *, B, S, TQ, KCH, topk):
    T = B * S
    nq = S // TQ
    kern = functools.partial(_dsa_index_kernel, S=S, TQ=TQ, KCH=KCH, topk=topk)
    return pl.pallas_call(
        kern,
        grid=(B, nq),
        in_specs=[
            pl.BlockSpec((S, LANES), lambda b, j: (b, 0)),
            pl.BlockSpec((S, LANES), lambda b, j: (b, 0)),
            pl.BlockSpec((TQ, IDX_HEADS * IDX_DIM), lambda b, j: (b * nq + j, 0)),
            pl.BlockSpec((IDX_HEADS, TQ), lambda b, j: (0, b * nq + j)),
        ],
        out_specs=pl.BlockSpec((TQ, S), lambda b, j: (b * nq + j, 0)),
        out_shape=jax.ShapeDtypeStruct((T, S), F32),
        scratch_shapes=[
            pltpu.VMEM((S, TQ), F32),
            pltpu.VMEM((16, TQ), F32),
            pltpu.VMEM((16, TQ), F32),
        ],
        compiler_params=_cparams(("parallel", "arbitrary")),
        name="dsa_index",
    )(ki_lo, ki_hi, qi, wt)


def _dsa_attn_kernel(q_ref, k_ref, v_ref, bias_ref, o_ref, *, S, TQ):
    j = pl.program_id(1)
    lane = lax.broadcasted_iota(jnp.int32, (TQ, LANES), 1)

    def variant(nk):
        ones = jnp.ones((nk, LANES), BF16)

        def pair(p, carry):
            vp = jnp.concatenate([v_ref[0, p, 0:nk, :], ones], axis=1)
            outs = []
            for e in range(2):
                h = 2 * p + e
                s = _dot_nt(q_ref[0, h], k_ref[0, h, 0:nk, :]) + bias_ref[:, 0:nk]
                m = s.max(axis=1, keepdims=True)
                pv = _dot(jnp.exp2(s - m).astype(BF16), vp)
                outs.append(pv[:, 0:LANES] * (1.0 / pv[:, LANES:2 * LANES]))
            o_ref[0, p] = jnp.where(lane < V_HEAD, outs[0], outs[1]).astype(BF16)
            return carry
        lax.fori_loop(0, N_HEADS // 2, pair, 0, unroll=4)

    for jj in range(S // TQ):
        @pl.when(j == jj)
        def _(jj=jj):
            variant((jj + 1) * TQ)


def _dsa_attn(q, k, v, bias, *, B, S, TQ):
    nq = S // TQ
    kern = functools.partial(_dsa_attn_kernel, S=S, TQ=TQ)
    return pl.pallas_call(
        kern,
        grid=(B, nq),
        in_specs=[
            pl.BlockSpec((1, N_HEADS, TQ, LANES), lambda b, j: (b, 0, j, 0)),
            pl.BlockSpec((1, N_HEADS, S, LANES), lambda b, j: (b, 0, 0, 0)),
            pl.BlockSpec((1, N_HEADS // 2, S, LANES), lambda b, j: (b, 0, 0, 0)),
            pl.BlockSpec((TQ, S), lambda b, j: (b * nq + j, 0)),
        ],
        out_specs=pl.BlockSpec((1, N_HEADS // 2, TQ, LANES), lambda b, j: (b, 0, j, 0)),
        out_shape=jax.ShapeDtypeStruct((B, N_HEADS // 2, S, LANES), BF16),
        compiler_params=_cparams(("parallel", "arbitrary")),
        name="dsa_attn",
    )(q, k, v, bias)


def _branch_out_kernel(o_ref, z_ref, ga_ref, gb_ref, x_ref, gs_ref, ws_ref, b_ref, gf_ref,
                       wa_bf, wb_bf, wo_bf, h_ref, n_ref, y_scr, *, tm):
    row = lax.broadcasted_iota(jnp.int32, (SGU_CHUNK, SGU_CHUNK), 0)
    col = lax.broadcasted_iota(jnp.int32, (SGU_CHUNK, SGU_CHUNK), 1)
    tril = col <= row
    w = [jnp.where(tril, ws_ref[g], 0.0).astype(BF16) for g in range(SGU_GROUPS)]
    b_t = jnp.transpose(b_ref[...])
    for cc in range(tm // SGU_CHUNK):
        rows = slice(cc * SGU_CHUNK, (cc + 1) * SGU_CHUNK)
        vn = _rms(z_ref[rows, SGU_WIDTH:2 * SGU_WIDTH].astype(F32), gs_ref[...]).astype(BF16)
        for g in range(SGU_GROUPS):
            cols = slice(g * SGU_GROUP_DIM, (g + 1) * SGU_GROUP_DIM)
            mixed = _dot(w[g], vn[:, cols]) + b_t[:, g:g + 1]
            y_scr[rows, cols] = (z_ref[rows, cols].astype(F32) * mixed).astype(BF16)

    o_a = jnp.concatenate([o_ref[0, p] for p in range(N_HEADS // 2)], axis=1)
    ya = _dot(o_a, wa_bf[...])
    yb = _dot(y_scr[...], wb_bf[...])
    merged = (ga_ref[...].astype(F32) * ya + gb_ref[...].astype(F32) * yb).astype(BF16)
    h = x_ref[...] + _dot(merged, wo_bf[...])
    h_ref[...] = h
    n_ref[...] = _rms(h, gf_ref[...]).astype(BF16)


def _branch_out(o, zg, x2, g_sgu, w_spatial, b_spatial, g_ffn, wa, wb, wo, *, B, S, tm):
    T, D = x2.shape
    nt = S // tm
    assert 2 * SGU_WIDTH == D and wa.shape == wb.shape == (SGU_WIDTH, D) and wo.shape == (D, D)
    row = lambda c: pl.BlockSpec((tm, D), lambda i: (i, c))
    const = lambda shape: pl.BlockSpec(shape, lambda i: (0,) * len(shape))
    resident = lambda w: pl.BlockSpec(w.shape, lambda i: (0, 0), pipeline_mode=pl.Buffered(1))
    kern = functools.partial(_branch_out_kernel, tm=tm)
    return pl.pallas_call(
        kern,
        grid=(T // tm,),
        in_specs=[
            pl.BlockSpec((1, N_HEADS // 2, tm, LANES), lambda i: (i // nt, 0, i % nt, 0)),
            row(0), row(1), row(2),
            row(0),
            const((1, SGU_WIDTH)),
            const((SGU_GROUPS, SGU_CHUNK, SGU_CHUNK)),
            const((SGU_GROUPS, SGU_CHUNK)),
            const((1, D)),
            resident(wa), resident(wb), resident(wo),
        ],
        out_specs=[pl.BlockSpec((tm, D), lambda i: (i, 0)), pl.BlockSpec((tm, D), lambda i: (i, 0))],
        out_shape=[jax.ShapeDtypeStruct((T, D), F32), jax.ShapeDtypeStruct((T, D), BF16)],
        scratch_shapes=[pltpu.VMEM((tm, SGU_WIDTH), BF16)],
        compiler_params=_cparams(("parallel",), BRANCH_OUT_VMEM_LIMIT_BYTES),
        name="branch_out",
    )(o, zg, zg, zg, x2, g_sgu, w_spatial, b_spatial, g_ffn, wa, wb, wo)


def _ffn_up_kernel(n_ref, wg_ref, wu_ref, wd_ref, a_ref, wd_bf_ref, wg_bf, wu_bf):
    @pl.when(pl.program_id(1) == 0)
    def _():
        wg_bf[...] = wg_ref[...].astype(BF16)
        wu_bf[...] = wu_ref[...].astype(BF16)

    wd_bf_ref[...] = wd_ref[...].astype(BF16)
    n = n_ref[...]
    g = _dot(n, wg_bf[...])
    u = _dot(n, wu_bf[...])
    a_ref[...] = (g * _sigmoid(g) * u).astype(BF16)


def _ffn_up(n2, w_gu, w_down, *, tm, tn):
    T, D = n2.shape
    d_ff = w_down.shape[0]
    nn, nm = d_ff // tn, T // tm
    slab = d_ff // (nn * nm)
    assert slab * nn * nm == d_ff and slab % 16 == 0
    return pl.pallas_call(
        _ffn_up_kernel,
        grid=(nn, nm),
        in_specs=[
            pl.BlockSpec((tm, D), lambda j, i: (i, 0)),
            pl.BlockSpec((D, tn), lambda j, i: (0, j)),
            pl.BlockSpec((D, tn), lambda j, i: (0, nn + j)),
            pl.BlockSpec((slab, D), lambda j, i: (j * nm + i, 0)),
        ],
        out_specs=[
            pl.BlockSpec((tm, tn), lambda j, i: (i, j)),
            pl.BlockSpec((slab, D), lambda j, i: (j * nm + i, 0)),
        ],
        out_shape=[jax.ShapeDtypeStruct((T, d_ff), BF16), jax.ShapeDtypeStruct((d_ff, D), BF16)],
        scratch_shapes=[pltpu.VMEM((D, tn), BF16), pltpu.VMEM((D, tn), BF16)],
        compiler_params=_cparams(("arbitrary", "arbitrary")),
        name="ffn_up",
    )(n2, w_gu, w_gu, w_down)


def _ffn_down_kernel(a_ref, w_ref, h1_ref, wg_ref, wp_ref, h2_ref, wg_bf_ref, wp_bf_ref, *, sides):
    step = pl.program_id(0) * pl.num_programs(1) + pl.program_id(1)
    for side, src, dst in zip(sides, (wg_ref, wp_ref), (wg_bf_ref, wp_bf_ref)):
        side.emit(step, src, dst)
    h2_ref[...] = h1_ref[...] + _dot(a_ref[...], w_ref[...])


def _ffn_down(act, w_down_bf, h1, w_pg, w_pp, *, tm, tn):
    T, D = h1.shape
    d_ff = act.shape[1]
    nn, nm = D // tn, T // tm
    step_of = lambda j, i: j * nm + i
    a = _pow2_floor(nn * nm // 2)
    b = min(a, w_pp.shape[0] // 16)
    sides = [_SideRound(w_pg, 0, a, step_of), _SideRound(w_pp, a, b, step_of)]
    kern = functools.partial(_ffn_down_kernel, sides=sides)
    return pl.pallas_call(
        kern,
        grid=(nn, nm),
        in_specs=[
            pl.BlockSpec((tm, d_ff), lambda j, i: (i, 0)),
            pl.BlockSpec((d_ff, tn), lambda j, i: (0, j)),
            pl.BlockSpec((tm, tn), lambda j, i: (i, j)),
        ] + [s.spec for s in sides],
        out_specs=[pl.BlockSpec((tm, tn), lambda j, i: (i, j))] + [s.spec for s in sides],
        out_shape=[jax.ShapeDtypeStruct((T, D), F32)] + [s.out_shape for s in sides],
        compiler_params=_cparams(("arbitrary", "arbitrary")),
        name="ffn_down",
    )(act, w_down_bf, h1, w_pg, w_pp)


def _ple_final_kernel(h2_ref, p_ref, gp_ref, gf_ref, wg_bf, wp_bf, out_ref):
    h2 = h2_ref[...]
    gate = _sigmoid(_dot(_rms(h2, gp_ref[...]).astype(BF16), wg_bf[...]))
    pp = _dot(p_ref[...].astype(BF16), wp_bf[...])
    out_ref[...] = _rms(h2 + gate * pp, gf_ref[...])


def _ple_final(h2, p2, w_pg, w_pp, g_ple, g_final, *, tm):
    T, D = h2.shape
    P = p2.shape[1]
    resident = lambda w: pl.BlockSpec(w.shape, lambda i: (0, 0), pipeline_mode=pl.Buffered(1))
    return pl.pallas_call(
        _ple_final_kernel,
        grid=(T // tm,),
        in_specs=[
            pl.BlockSpec((tm, D), lambda i: (i, 0)),
            pl.BlockSpec((tm, P), lambda i: (i, 0)),
            pl.BlockSpec((1, D), lambda i: (0, 0)),
            pl.BlockSpec((1, D), lambda i: (0, 0)),
            resident(w_pg), resident(w_pp),
        ],
        out_specs=pl.BlockSpec((tm, D), lambda i: (i, 0)),
        out_shape=jax.ShapeDtypeStruct((T, D), F32),
        compiler_params=_cparams(("parallel",)),
        name="ple_final",
    )(h2, p2, g_ple, g_final, w_pg, w_pp)


def _lane_tables(S):
    assert QK_ROPE == IDX_ROPE == 32
    scale = (QK_NOPE + QK_ROPE) ** -0.5 * math.log2(math.e)
    a = np.zeros((N_TABLES, LANES), np.float64)
    bc = np.zeros_like(a)
    bs = np.zeros_like(a)
    a[TAB_Q, 0:QK_NOPE] = scale
    bc[TAB_Q, 64:96] = scale
    bs[TAB_Q + 1, 64:80] = -scale
    bs[TAB_Q + 1, 80:96] = scale
    for o in (0, IDX_DIM):
        bc[TAB_QI, o:o + 32] = 1.0
        a[TAB_QI, o + 32:o + 64] = 1.0
        bs[TAB_QI + 1, o:o + 32] = 1.0
    bc[TAB_KI, 0:32] = 1.0
    a[TAB_KI, 32:64] = 1.0
    bs[TAB_KI + 1, 0:16] = -1.0
    bs[TAB_KI + 2, 16:32] = 1.0
    bc[TAB_KR, 0:32] = 1.0
    bs[TAB_KR + 1, 0:16] = -1.0
    bs[TAB_KR + 2, 16:32] = 1.0
    inv = ROPE_THETA ** (-np.arange(0, QK_ROPE, 2, dtype=np.float64) / QK_ROPE)
    ang = np.arange(S, dtype=np.float64)[:, None] * np.tile(inv, LANES // inv.shape[0])[None, :]
    tables = a[:, None, :] + bc[:, None, :] * np.cos(ang)[None] + bs[:, None, :] * np.sin(ang)[None]
    return jnp.asarray(tables.astype(np.float32))


def _tiles(S):
    return dict(
        attn_q=256,
        index_keys=min(512, S),
        a_proj_rows=min(256, S),
        uv_gate_rows=min(1024, S), uv_gate_cols=1024,
        branch_out_rows=min(512, S),
        ffn_up_rows=min(1024, S), ffn_up_cols=512,
        ffn_down_rows=min(512, S), ffn_down_cols=1024,
        ple_rows=min(1024, S),
    )


def kernel(x, p, g_mix, w_in, g_cq, g_ckv, w_uq, w_uk, w_uv, w_iq, w_a_proj, g_sgu, w_spatial,
           b_spatial, w_b_proj, w_o, g_ffn, w_gu, w_down, g_ple, w_ple_gate, w_ple_proj, g_final):
    B, S, D = x.shape
    T = B * S
    depth = w_in.shape[0]
    topk = min(TOPK_MAX, S // 4)
    t = _tiles(S)
    tables = _lane_tables(S)

    h = x.reshape(T, D)
    for i in range(depth):
        w_t = jnp.transpose(w_in[i])
        wuk_t = jnp.transpose(w_uk[i], (1, 2, 0)).reshape(N_HEADS * QK_NOPE, KV_LORA)
        wuv_t = jnp.transpose(w_uv[i], (1, 2, 0)).reshape(N_HEADS * V_HEAD, KV_LORA)
        n, q, k, v, qi, ki_lo, ki_hi, wt = _a_proj(h, g_mix[i][None], w_t, g_cq[i][None], g_ckv[i][None],
                                                    w_uq[i], w_iq[i], wuk_t, wuv_t,
                                                    tables, B=B, S=S, tm=t["a_proj_rows"])
        zg, wa_bf, wb_bf, wo_bf = _uv_gate_proj(n, w_t, w_a_proj[i], w_b_proj[i], w_o[i],
                                                tm=t["uv_gate_rows"], tn=t["uv_gate_cols"])
        bias = _dsa_index(ki_lo, ki_hi, qi, wt, B=B, S=S, TQ=t["attn_q"], KCH=t["index_keys"], topk=topk)
        o = _dsa_attn(q, k, v, bias, B=B, S=S, TQ=t["attn_q"])
        h1, n2 = _branch_out(o, zg, h, g_sgu[i][None], w_spatial[i], b_spatial[i],
                             g_ffn[i][None], wa_bf, wb_bf, wo_bf, B=B, S=S, tm=t["branch_out_rows"])
        act, w_down_bf = _ffn_up(n2, w_gu[i], w_down[i], tm=t["ffn_up_rows"], tn=t["ffn_up_cols"])
        h2, wpg_bf, wpp_bf = _ffn_down(act, w_down_bf, h1, w_ple_gate[i], w_ple_proj[i],
                                       tm=t["ffn_down_rows"], tn=t["ffn_down_cols"])
        assert depth == 1
        h = _ple_final(h2, p[i].reshape(T, -1), wpg_bf, wpp_bf, g_ple[i][None], g_final[None], tm=t["ple_rows"])
    return h.reshape(B, S, D)
```

```python
import functools
import math

import numpy as np
import jax
import jax.numpy as jnp
from jax import lax
from jax.experimental import pallas as pl
from jax.experimental.pallas import tpu as pltpu

F32 = jnp.float32
BF16 = jnp.bfloat16

N_HEADS = 16
QK_NOPE = 64
QK_ROPE = 32
V_HEAD = 64
Q_LORA = 512
KV_LORA = 256
IDX_HEADS = 16
IDX_DIM = 64
IDX_ROPE = 32
TOPK_MAX = 256
SGU_CHUNK = 128
SGU_GROUPS = 8
SGU_GROUP_DIM = 128
SGU_WIDTH = SGU_GROUPS * SGU_GROUP_DIM
ROPE_THETA = 10000.0
EPS = 1e-6

LANES = 128
IN_SMALL = Q_LORA + KV_LORA + QK_ROPE + IDX_DIM + IDX_HEADS
SMALL_COLS = 1024
COL_CKV = Q_LORA
COL_MISC = Q_LORA + KV_LORA
MISC_KIDX = QK_ROPE
MISC_WIDX = QK_ROPE + IDX_DIM
VMEM_LIMIT_BYTES = 56 * 1024 * 1024
BRANCH_OUT_VMEM_LIMIT_BYTES = 60 * 1024 * 1024
NEG_BIAS = -1e30
F32_MAX = 3.4028234663852886e38
STEPS_PER_CHECK = 4
COUNT_CHAINS = 4


def _cparams(semantics, vmem_limit_bytes=VMEM_LIMIT_BYTES):
    return pltpu.CompilerParams(dimension_semantics=semantics, vmem_limit_bytes=vmem_limit_bytes)


def _rms(x, g):
    return x * lax.rsqrt(jnp.mean(x * x, axis=-1, keepdims=True) + EPS) * g


def _dot(a, b):
    return jnp.dot(a, b, preferred_element_type=F32)


def _dot_nt(a, b):
    return lax.dot_general(a, b, (((1,), (1,)), ((), ())), preferred_element_type=F32)


def _gelu_exact(x):
    return 0.5 * x * (1.0 + lax.erf(x * (1.0 / math.sqrt(2.0))))


def _sigmoid(x):
    return 0.5 * jnp.tanh(0.5 * x) + 0.5


class _SideRound:
    def __init__(self, w, first, steps, step_of):
        rows, cols = w.shape
        slab = rows // steps
        assert slab * steps == rows and slab % 16 == 0
        self.first, self.steps = first, steps
        index = lambda *g: (jnp.clip(step_of(*g) - first, 0, steps - 1), 0)
        self.spec = pl.BlockSpec((slab, cols), index)
        self.out_shape = jax.ShapeDtypeStruct((rows, cols), BF16)

    def emit(self, step, src_ref, dst_ref):
        @pl.when((step >= self.first) & (step < self.first + self.steps))
        def _():
            dst_ref[...] = src_ref[...].astype(BF16)


def _pow2_floor(n):
    return 1 << (max(int(n), 1).bit_length() - 1)


TAB_Q = 0
TAB_QI = 2
TAB_KI = 4
TAB_KR = 7
N_TABLES = 10


def _rope_block(x, cos, sin_a, sin_b):
    return x * cos + pltpu.roll(x, LANES - 16, 1) * sin_a + pltpu.roll(x, 16, 1) * sin_b


def _a_proj_kernel(x_ref, gmix_ref, ws_ref, gcq_ref, gckv_ref, wuq_ref, wiq_ref, wuk_ref, wuv_ref, tab_ref,
                   n_ref, q_ref, k_ref, v_ref, qi_ref, kilo_ref, kihi_ref, wt_ref,
                   ws_bf, wq_bf, wiq_bf, wiqr_bf, wk_bf, wv_bf):
    @pl.when(pl.program_id(0) == 0)
    def _():
        ws_bf[...] = ws_ref[...].astype(BF16)
        wv_bf[...] = wuv_ref[...].astype(BF16)
        wuq_t = wuq_ref[...].T
        wiq_t = wiq_ref[...].T
        wiq_bf[...] = wiq_t.astype(BF16)
        r1 = QK_ROPE // 2
        hd = QK_NOPE + QK_ROPE
        for h in range(N_HEADS):
            src, dst = h * hd, h * LANES
            wq_bf[dst:dst + hd, :] = wuq_t[src:src + hd, :].astype(BF16)
            wq_bf[dst + hd:dst + hd + r1, :] = wuq_t[src + QK_NOPE + r1:src + hd, :].astype(BF16)
            wq_bf[dst + hd + r1:dst + LANES, :] = wuq_t[src + QK_NOPE:src + QK_NOPE + r1, :].astype(BF16)
            wk_bf[dst:dst + QK_NOPE, :] = wuk_ref[h * QK_NOPE:(h + 1) * QK_NOPE, :].astype(BF16)
            wk_bf[dst + QK_NOPE:dst + LANES, :] = jnp.zeros((LANES - QK_NOPE, wk_bf.shape[1]), BF16)
        ri = IDX_ROPE // 2
        for h in range(IDX_HEADS):
            o = h * IDX_DIM
            wiqr_bf[o:o + ri, :] = (-wiq_t[o + ri:o + 2 * ri, :]).astype(BF16)
            wiqr_bf[o + ri:o + 2 * ri, :] = wiq_t[o:o + ri, :].astype(BF16)
            wiqr_bf[o + 2 * ri:o + IDX_DIM, :] = jnp.zeros((IDX_DIM - 2 * ri, wiqr_bf.shape[1]), BF16)

    n = _rms(x_ref[...], gmix_ref[...]).astype(BF16)
    n_ref[...] = n
    small = _dot_nt(n, ws_bf[...])
    c_q = _rms(small[:, 0:Q_LORA], gcq_ref[...]).astype(BF16)
    c_kv = _rms(small[:, COL_CKV:COL_CKV + KV_LORA], gckv_ref[...]).astype(BF16)

    q = _dot_nt(c_q, wq_bf[...])
    cq, sq = tab_ref[TAB_Q], tab_ref[TAB_Q + 1]
    for h in range(N_HEADS):
        blk = q[:, h * LANES:(h + 1) * LANES]
        q_ref[0, h] = (blk * cq + pltpu.roll(blk, LANES - QK_ROPE, 1) * sq).astype(BF16)

    ci, si = tab_ref[TAB_QI], tab_ref[TAB_QI + 1]
    qi = _dot_nt(c_q, wiq_bf[...])
    qir = _dot_nt(c_q, wiqr_bf[...])
    for hp in range(IDX_HEADS // 2):
        cols = slice(hp * LANES, (hp + 1) * LANES)
        qi_ref[:, cols] = (qi[:, cols] * ci + qir[:, cols] * si).astype(BF16)

    misc = small[:, COL_MISC:COL_MISC + LANES]
    ki_lo = _rope_block(pltpu.roll(misc, LANES - MISC_KIDX, 1),
                        tab_ref[TAB_KI], tab_ref[TAB_KI + 1], tab_ref[TAB_KI + 2])
    kilo_ref[...] = ki_lo.astype(BF16)
    kihi_ref[...] = pltpu.roll(ki_lo, IDX_DIM, 1).astype(BF16)

    k_rope = pltpu.roll(_rope_block(misc, tab_ref[TAB_KR], tab_ref[TAB_KR + 1], tab_ref[TAB_KR + 2]),
                        QK_NOPE, 1)
    k_nope = _dot_nt(c_kv, wk_bf[...])
    for h in range(N_HEADS):
        k_ref[0, h] = (k_nope[:, h * LANES:(h + 1) * LANES] + k_rope).astype(BF16)

    v = _dot_nt(c_kv, wv_bf[...])
    for p in range(N_HEADS // 2):
        v_ref[0, p] = v[:, p * LANES:(p + 1) * LANES].astype(BF16)

    w_scale = IDX_HEADS ** -0.5 * IDX_DIM ** -0.5
    wt_ref[...] = misc.T[MISC_WIDX:MISC_WIDX + IDX_HEADS, :] * w_scale


def _a_proj(x2, g_mix, w_t, g_cq, g_ckv, wuq, wiq, wuk_t, wuv_t, tables, *, B, S, tm):
    T, D = x2.shape
    nt = S // tm
    const2 = lambda i: (0, 0)
    once = lambda w: pl.BlockSpec(w.shape, const2, pipeline_mode=pl.Buffered(1))
    head_spec = lambda nh: pl.BlockSpec((1, nh, tm, LANES), lambda i: (i // nt, 0, i % nt, 0))
    return pl.pallas_call(
        _a_proj_kernel,
        grid=(T // tm,),
        in_specs=[
            pl.BlockSpec((tm, D), lambda i: (i, 0)),
            pl.BlockSpec((1, D), const2),
            pl.BlockSpec((SMALL_COLS, D), const2, pipeline_mode=pl.Buffered(1)),
            pl.BlockSpec((1, Q_LORA), const2),
            pl.BlockSpec((1, KV_LORA), const2),
            once(wuq), once(wiq), once(wuk_t), once(wuv_t),
            pl.BlockSpec((N_TABLES, tm, LANES), lambda i: (0, i % nt, 0)),
        ],
        out_specs=[
            pl.BlockSpec((tm, D), lambda i: (i, 0)),
            head_spec(N_HEADS),
            head_spec(N_HEADS),
            head_spec(N_HEADS // 2),
            pl.BlockSpec((tm, IDX_HEADS * IDX_DIM), lambda i: (i, 0)),
            pl.BlockSpec((tm, LANES), lambda i: (i, 0)),
            pl.BlockSpec((tm, LANES), lambda i: (i, 0)),
            pl.BlockSpec((IDX_HEADS, tm), lambda i: (0, i)),
        ],
        out_shape=[
            jax.ShapeDtypeStruct((T, D), BF16),
            jax.ShapeDtypeStruct((B, N_HEADS, S, LANES), BF16),
            jax.ShapeDtypeStruct((B, N_HEADS, S, LANES), BF16),
            jax.ShapeDtypeStruct((B, N_HEADS // 2, S, LANES), BF16),
            jax.ShapeDtypeStruct((T, IDX_HEADS * IDX_DIM), BF16),
            jax.ShapeDtypeStruct((T, LANES), BF16),
            jax.ShapeDtypeStruct((T, LANES), BF16),
            jax.ShapeDtypeStruct((IDX_HEADS, T), F32),
        ],
        scratch_shapes=[
            pltpu.VMEM((SMALL_COLS, D), BF16),
            pltpu.VMEM((N_HEADS * LANES, Q_LORA), BF16),
            pltpu.VMEM((IDX_HEADS * IDX_DIM, Q_LORA), BF16),
            pltpu.VMEM((IDX_HEADS * IDX_DIM, Q_LORA), BF16),
            pltpu.VMEM((N_HEADS * LANES, KV_LORA), BF16),
            pltpu.VMEM((N_HEADS * V_HEAD, KV_LORA), BF16),
        ],
        compiler_params=_cparams(("arbitrary",)),
        name="norm_a_proj",
    )(x2, g_mix, w_t, g_cq, g_ckv, wuq, wiq, wuk_t, wuv_t, tables)


def _uv_gate_kernel(n_ref, w_ref, *rest, n_z, sides):
    side_in, (out_ref, *side_out), w_bf = rest[:len(sides)], rest[len(sides):-1], rest[-1]
    j = pl.program_id(0)
    step = j * pl.num_programs(1) + pl.program_id(1)
    for side, src, dst in zip(sides, side_in, side_out):
        side.emit(step, src, dst)

    @pl.when(pl.program_id(1) == 0)
    def _():
        w_bf[...] = w_ref[...].astype(BF16)

    def proj():
        return _dot_nt(n_ref[...], w_bf[...])

    @pl.when(j < n_z)
    def _():
        out_ref[...] = _gelu_exact(proj()).astype(BF16)

    @pl.when(j >= n_z)
    def _():
        out_ref[...] = _sigmoid(proj()).astype(BF16)


def _uv_gate_proj(n, w_t, wa, wb, wo, *, tm, tn):
    T, D = n.shape
    n_z = (2 * SGU_WIDTH) // tn
    n_out = 2 * SGU_WIDTH + 2 * D
    nm = T // tm
    assert w_t.shape == (IN_SMALL + n_out, D) and IN_SMALL % 8 == 0 and tn % 8 == 0
    step_of = lambda j, i: j * nm + i
    a = _pow2_floor((n_out // tn) * nm // 4)
    sides = [_SideRound(wa, 0, a, step_of), _SideRound(wb, a, a, step_of), _SideRound(wo, 2 * a, 2 * a, step_of)]
    kern = functools.partial(_uv_gate_kernel, n_z=n_z, sides=sides)
    return pl.pallas_call(
        kern,
        grid=(n_out // tn, nm),
        in_specs=[
            pl.BlockSpec((tm, D), lambda j, i: (i, 0)),
            pl.BlockSpec((pl.Element(tn), pl.Element(D)), lambda j, i: ((IN_SMALL // 8 + j * (tn // 8)) * 8, 0)),
        ] + [s.spec for s in sides],
        out_specs=[pl.BlockSpec((tm, tn), lambda j, i: (i, j))] + [s.spec for s in sides],
        out_shape=[jax.ShapeDtypeStruct((T, n_out), BF16)] + [s.out_shape for s in sides],
        scratch_shapes=[pltpu.VMEM((tn, D), BF16)],
        compiler_params=_cparams(("arbitrary", "arbitrary")),
        name="uv_gate_proj",
    )(n, w_t, wa, wb, wo)


def _dsa_index_kernel(kilo_ref, kihi_ref, qi_ref, wt_ref, bias_ref, isc_ref, mm_ref, js_ref,
                      *, S, TQ, KCH, topk):
    j = pl.program_id(1)
    q0 = j * TQ
    nkeys = q0 + TQ
    qidx = q0 + lax.broadcasted_iota(jnp.int32, (1, TQ), 1)
    kf = float(topk)

    mm_ref[0:8, :] = jnp.full((8, TQ), jnp.inf, F32)
    mm_ref[8:16, :] = jnp.full((8, TQ), -jnp.inf, F32)
    for c in range(S // KCH):
        @pl.when(c * KCH < nkeys)
        def _(c=c):
            klo = kilo_ref[c * KCH:(c + 1) * KCH, :]
            khi = kihi_ref[c * KCH:(c + 1) * KCH, :]
            acc = jnp.zeros((KCH, TQ), F32)
            for hp in range(IDX_HEADS // 2):
                qp = qi_ref[:, hp * LANES:(hp + 1) * LANES]
                s0 = _dot_nt(klo, qp)
                s1 = _dot_nt(khi, qp)
                acc = acc + jnp.maximum(s0, 0.0) * wt_ref[2 * hp:2 * hp + 1, :]
                acc = acc + jnp.maximum(s1, 0.0) * wt_ref[2 * hp + 1:2 * hp + 2, :]
            kidx = c * KCH + lax.broadcasted_iota(jnp.int32, (KCH, TQ), 0)
            causal = kidx <= qidx
            isc_ref[c * KCH:(c + 1) * KCH, :] = jnp.where(causal, acc, -jnp.inf)
            lo_part = jnp.where(causal, acc, jnp.inf).reshape(KCH // 8, 8, TQ).min(axis=0)
            hi_part = jnp.where(causal, acc, -jnp.inf).reshape(KCH // 8, 8, TQ).max(axis=0)
            mm_ref[0:8, :] = jnp.minimum(mm_ref[0:8, :], lo_part)
            mm_ref[8:16, :] = jnp.maximum(mm_ref[8:16, :], hi_part)

    def select(nch):
        def count(pred):
            groups = []
            for lg in range(TQ // LANES):
                lanes = slice(lg * LANES, (lg + 1) * LANES)
                acc = jnp.zeros((COUNT_CHAINS, 8, LANES), F32)
                for c in range(nch):
                    ones = jnp.where(pred(isc_ref[c * TQ:(c + 1) * TQ, lanes], c * TQ, lanes), 1.0, 0.0)
                    acc = acc + ones.reshape(COUNT_CHAINS, TQ // (8 * COUNT_CHAINS), 8, LANES).sum(axis=1)
                groups.append(acc.sum(axis=0).sum(axis=0, keepdims=True))
            return jnp.concatenate(groups, axis=1)

        row_min = mm_ref[0:8, :].min(axis=0, keepdims=True)
        row_max = mm_ref[8:16, :].max(axis=0, keepdims=True)
        full = (qidx + 1) <= topk
        c_max = count(lambda blk, k0, lanes: blk >= row_max[:, lanes])
        exact0 = c_max == kf
        tie0 = c_max > kf
        settled0 = full | exact0 | tie0
        lo0 = jnp.where(full, -F32_MAX, jnp.where(settled0, row_max, row_min))
        hi0 = jnp.where(full, -F32_MAX, jnp.where(tie0, jnp.inf, row_max))
        act0 = jnp.where(settled0, 0.0, 1.0)

        def step(lo, hi, act):
            mid = lo * 0.5 + hi * 0.5
            inside = (mid > lo) & (mid < hi)
            cnt = count(lambda blk, k0, lanes: blk >= mid[:, lanes])
            upd = (act > 0.0) & inside
            found = upd & (cnt == kf)
            lo2 = jnp.where(upd & (cnt >= kf), mid, lo)
            hi2 = jnp.where(upd & (cnt <= kf), mid, hi)
            return lo2, hi2, jnp.where(upd & jnp.logical_not(found), 1.0, 0.0)

        def any_active(act):
            return (jnp.max(act) > 0.0).astype(jnp.int32)

        def body(st):
            lo, hi, act, _ = st
            for _ in range(STEPS_PER_CHECK):
                lo, hi, act = step(lo, hi, act)
            return lo, hi, act, any_active(act)

        lo, hi, _, _ = lax.while_loop(lambda st: st[3] > 0, body, (lo0, hi0, act0, any_active(act0)))

        tie = lo < hi
        js_ref[0:8, :] = jnp.full((8, TQ), -1.0, F32)

        @pl.when(jnp.max(jnp.where(tie, 1.0, 0.0)) > 0.0)
        def _():
            need = kf - count(lambda blk, k0, lanes: blk >= hi[:, lanes])

            def kpos(k0):
                return (k0 + lax.broadcasted_iota(jnp.int32, (TQ, LANES), 0)).astype(F32)

            def tie_step(_, st):
                ilo, ihi = st
                imid = jnp.floor((ilo + ihi) * 0.5)
                cnt = count(lambda blk, k0, lanes: (blk >= lo[:, lanes]) & (blk < hi[:, lanes])
                            & (kpos(k0) <= imid[:, lanes]))
                ge = cnt >= need
                return jnp.where(ge, ilo, imid), jnp.where(ge, imid, ihi)

            nsteps = int(math.ceil(math.log2(S))) + 1
            _, ihi = lax.fori_loop(0, nsteps, tie_step,
                                   (jnp.full((1, TQ), -1.0, F32), jnp.full((1, TQ), S - 1.0, F32)))
            js_ref[0:1, :] = jnp.where(tie, ihi, -1.0)

        jstar = js_ref[0:1, :]

        for c in range(S // TQ):
            if c < nch:
                blk = isc_ref[c * TQ:(c + 1) * TQ, :]
                kpos_c = (c * TQ + lax.broadcasted_iota(jnp.int32, (TQ, TQ), 0)).astype(F32)
                sel = (blk >= hi) | ((blk >= lo) & (kpos_c <= jstar))
                bias_ref[:, c * TQ:(c + 1) * TQ] = jnp.where(sel, 0.0, NEG_BIAS).T
            else:
                bias_ref[:, c * TQ:(c + 1) * TQ] = jnp.full((TQ, TQ), NEG_BIAS, F32)

    for jj in range(S // TQ):
        @pl.when(j == jj)
        def _(jj=jj):
            select(jj + 1)


def _dsa_index(ki_lo, ki_hi, qi, wt, *, B, S, TQ, KCH, topk):
    T = B * S
    nq = S // TQ
    kern = functools.partial(_dsa_index_kernel, S=S, TQ=TQ, KCH=KCH, topk=topk)
    return pl.pallas_call(
        kern,
        grid=(B, nq),
        in_specs=[
            pl.BlockSpec((S, LANES), lambda b, j: (b, 0)),
            pl.BlockSpec((S, LANES), lambda b, j: (b, 0)),
            pl.BlockSpec((TQ, IDX_HEADS * IDX_DIM), lambda b, j: (b * nq + j, 0)),
            pl.BlockSpec((IDX_HEADS, TQ), lambda b, j: (0, b * nq + j)),
        ],
        out_specs=pl.BlockSpec((TQ, S), lambda b, j: (b * nq + j, 0)),
        out_shape=jax.ShapeDtypeStruct((T, S), F32),
        scratch_shapes=[
            pltpu.VMEM((S, TQ), F32),
            pltpu.VMEM((16, TQ), F32),
            pltpu.VMEM((16, TQ), F32),
        ],
        compiler_params=_cparams(("parallel", "arbitrary")),
        name="dsa_index",
    )(ki_lo, ki_hi, qi, wt)


def _dsa_attn_kernel(q_ref, k_ref, v_ref, bias_ref, o_ref, *, S, TQ):
    j = pl.program_id(1)
    lane = lax.broadcasted_iota(jnp.int32, (TQ, LANES), 1)

    def variant(nk):
        ones = jnp.ones((nk, LANES), BF16)

        def pair(p, carry):
            vp = jnp.concatenate([v_ref[0, p, 0:nk, :], ones], axis=1)
            outs = []
            for e in range(2):
                h = 2 * p + e
                s = _dot_nt(q_ref[0, h], k_ref[0, h, 0:nk, :]) + bias_ref[:, 0:nk]
                m = s.max(axis=1, keepdims=True)
                pv = _dot(jnp.exp2(s - m).astype(BF16), vp)
                outs.append(pv[:, 0:LANES] * (1.0 / pv[:, LANES:2 * LANES]))
            o_ref[0, p] = jnp.where(lane < V_HEAD, outs[0], outs[1]).astype(BF16)
            return carry
        lax.fori_loop(0, N_HEADS // 2, pair, 0, unroll=4)

    for jj in range(S // TQ):
        @pl.when(j == jj)
        def _(jj=jj):
            variant((jj + 1) * TQ)


def _dsa_attn(q, k, v, bias, *, B, S, TQ):
    nq = S // TQ
    kern = functools.partial(_dsa_attn_kernel, S=S, TQ=TQ)
    return pl.pallas_call(
        kern,
        grid=(B, nq),
        in_specs=[
            pl.BlockSpec((1, N_HEADS, TQ, LANES), lambda b, j: (b, 0, j, 0)),
            pl.BlockSpec((1, N_HEADS, S, LANES), lambda b, j: (b, 0, 0, 0)),
            pl.BlockSpec((1, N_HEADS // 2, S, LANES), lambda b, j: (b, 0, 0, 0)),
            pl.BlockSpec((TQ, S), lambda b, j: (b * nq + j, 0)),
        ],
        out_specs=pl.BlockSpec((1, N_HEADS // 2, TQ, LANES), lambda b, j: (b, 0, j, 0)),
        out_shape=jax.ShapeDtypeStruct((B, N_HEADS // 2, S, LANES), BF16),
        compiler_params=_cparams(("parallel", "arbitrary")),
        name="dsa_attn",
    )(q, k, v, bias)


def _branch_out_kernel(o_ref, z_ref, ga_ref, gb_ref, x_ref, gs_ref, ws_ref, b_ref, gf_ref,
                       wa_bf, wb_bf, wo_bf, h_ref, n_ref, y_scr, *, tm):
    row = lax.broadcasted_iota(jnp.int32, (SGU_CHUNK, SGU_CHUNK), 0)
    col = lax.broadcasted_iota(jnp.int32, (SGU_CHUNK, SGU_CHUNK), 1)
    tril = col <= row
    w = [jnp.where(tril, ws_ref[g], 0.0).astype(BF16) for g in range(SGU_GROUPS)]
    b_t = jnp.transpose(b_ref[...])
    for cc in range(tm // SGU_CHUNK):
        rows = slice(cc * SGU_CHUNK, (cc + 1) * SGU_CHUNK)
        vn = _rms(z_ref[rows, SGU_WIDTH:2 * SGU_WIDTH].astype(F32), gs_ref[...]).astype(BF16)
        for g in range(SGU_GROUPS):
            cols = slice(g * SGU_GROUP_DIM, (g + 1) * SGU_GROUP_DIM)
            mixed = _dot(w[g], vn[:, cols]) + b_t[:, g:g + 1]
            y_scr[rows, cols] = (z_ref[rows, cols].astype(F32) * mixed).astype(BF16)

    o_a = jnp.concatenate([o_ref[0, p] for p in range(N_HEADS // 2)], axis=1)
    ya = _dot(o_a, wa_bf[...])
    yb = _dot(y_scr[...], wb_bf[...])
    merged = (ga_ref[...].astype(F32) * ya + gb_ref[...].astype(F32) * yb).astype(BF16)
    h = x_ref[...] + _dot(merged, wo_bf[...])
    h_ref[...] = h
    n_ref[...] = _rms(h, gf_ref[...]).astype(BF16)


def _branch_out(o, zg, x2, g_sgu, w_spatial, b_spatial, g_ffn, wa, wb, wo, *, B, S, tm):
    T, D = x2.shape
    nt = S // tm
    assert 2 * SGU_WIDTH == D and wa.shape == wb.shape == (SGU_WIDTH, D) and wo.shape == (D, D)
    row = lambda c: pl.BlockSpec((tm, D), lambda i: (i, c))
    const = lambda shape: pl.BlockSpec(shape, lambda i: (0,) * len(shape))
    resident = lambda w: pl.BlockSpec(w.shape, lambda i: (0, 0), pipeline_mode=pl.Buffered(1))
    kern = functools.partial(_branch_out_kernel, tm=tm)
    return pl.pallas_call(
        kern,
        grid=(T // tm,),
        in_specs=[
            pl.BlockSpec((1, N_HEADS // 2, tm, LANES), lambda i: (i // nt, 0, i % nt, 0)),
            row(0), row(1), row(2),
            row(0),
            const((1, SGU_WIDTH)),
            const((SGU_GROUPS, SGU_CHUNK, SGU_CHUNK)),
            const((SGU_GROUPS, SGU_CHUNK)),
            const((1, D)),
            resident(wa), resident(wb), resident(wo),
        ],
        out_specs=[pl.BlockSpec((tm, D), lambda i: (i, 0)), pl.BlockSpec((tm, D), lambda i: (i, 0))],
        out_shape=[jax.ShapeDtypeStruct((T, D), F32), jax.ShapeDtypeStruct((T, D), BF16)],
        scratch_shapes=[pltpu.VMEM((tm, SGU_WIDTH), BF16)],
        compiler_params=_cparams(("parallel",), BRANCH_OUT_VMEM_LIMIT_BYTES),
        name="branch_out",
    )(o, zg, zg, zg, x2, g_sgu, w_spatial, b_spatial, g_ffn, wa, wb, wo)


def _ffn_up_kernel(n_ref, wg_ref, wu_ref, wd_ref, a_ref, wd_bf_ref, wg_bf, wu_bf):
    @pl.when(pl.program_id(1) == 0)
    def _():
        wg_bf[...] = wg_ref[...].astype(BF16)
        wu_bf[...] = wu_ref[...].astype(BF16)

    wd_bf_ref[...] = wd_ref[...].astype(BF16)
    n = n_ref[...]
    g = _dot(n, wg_bf[...])
    u = _dot(n, wu_bf[...])
    a_ref[...] = (g * _sigmoid(g) * u).astype(BF16)


def _ffn_up(n2, w_gu, w_down, *, tm, tn):
    T, D = n2.shape
    d_ff = w_down.shape[0]
    nn, nm = d_ff // tn, T // tm
    slab = d_ff // (nn * nm)
    assert slab * nn * nm == d_ff and slab % 16 == 0
    return pl.pallas_call(
        _ffn_up_kernel,
        grid=(nn, nm),
        in_specs=[
            pl.BlockSpec((tm, D), lambda j, i: (i, 0)),
            pl.BlockSpec((D, tn), lambda j, i: (0, j)),
            pl.BlockSpec((D, tn), lambda j, i: (0, nn + j)),
            pl.BlockSpec((slab, D), lambda j, i: (j * nm + i, 0)),
        ],
        out_specs=[
            pl.BlockSpec((tm, tn), lambda j, i: (i, j)),
            pl.BlockSpec((slab, D), lambda j, i: (j * nm + i, 0)),
        ],
        out_shape=[jax.ShapeDtypeStruct((T, d_ff), BF16), jax.ShapeDtypeStruct((d_ff, D), BF16)],
        scratch_shapes=[pltpu.VMEM((D, tn), BF16), pltpu.VMEM((D, tn), BF16)],
        compiler_params=_cparams(("arbitrary", "arbitrary")),
        name="ffn_up",
    )(n2, w_gu, w_gu, w_down)


def _ffn_down_kernel(a_ref, w_ref, h1_ref, wg_ref, wp_ref, h2_ref, wg_bf_ref, wp_bf_ref, *, sides):
    step = pl.program_id(0) * pl.num_programs(1) + pl.program_id(1)
    for side, src, dst in zip(sides, (wg_ref, wp_ref), (wg_bf_ref, wp_bf_ref)):
        side.emit(step, src, dst)
    h2_ref[...] = h1_ref[...] + _dot(a_ref[...], w_ref[...])


def _ffn_down(act, w_down_bf, h1, w_pg, w_pp, *, tm, tn):
    T, D = h1.shape
    d_ff = act.shape[1]
    nn, nm = D // tn, T // tm
    step_of = lambda j, i: j * nm + i
    a = _pow2_floor(nn * nm // 2)
    b = min(a, w_pp.shape[0] // 16)
    sides = [_SideRound(w_pg, 0, a, step_of), _SideRound(w_pp, a, b, step_of)]
    kern = functools.partial(_ffn_down_kernel, sides=sides)
    return pl.pallas_call(
        kern,
        grid=(nn, nm),
        in_specs=[
            pl.BlockSpec((tm, d_ff), lambda j, i: (i, 0)),
            pl.BlockSpec((d_ff, tn), lambda j, i: (0, j)),
            pl.BlockSpec((tm, tn), lambda j, i: (i, j)),
        ] + [s.spec for s in sides],
        out_specs=[pl.BlockSpec((tm, tn), lambda j, i: (i, j))] + [s.spec for s in sides],
        out_shape=[jax.ShapeDtypeStruct((T, D), F32)] + [s.out_shape for s in sides],
        compiler_params=_cparams(("arbitrary", "arbitrary")),
        name="ffn_down",
    )(act, w_down_bf, h1, w_pg, w_pp)


def _ple_final_kernel(h2_ref, p_ref, gp_ref, gf_ref, wg_bf, wp_bf, out_ref):
    h2 = h2_ref[...]
    gate = _sigmoid(_dot(_rms(h2, gp_ref[...]).astype(BF16), wg_bf[...]))
    pp = _dot(p_ref[...].astype(BF16), wp_bf[...])
    out_ref[...] = _rms(h2 + gate * pp, gf_ref[...])


def _ple_final(h2, p2, w_pg, w_pp, g_ple, g_final, *, tm):
    T, D = h2.shape
    P = p2.shape[1]
    resident = lambda w: pl.BlockSpec(w.shape, lambda i: (0, 0), pipeline_mode=pl.Buffered(1))
    return pl.pallas_call(
        _ple_final_kernel,
        grid=(T // tm,),
        in_specs=[
            pl.BlockSpec((tm, D), lambda i: (i, 0)),
            pl.BlockSpec((tm, P), lambda i: (i, 0)),
            pl.BlockSpec((1, D), lambda i: (0, 0)),
            pl.BlockSpec((1, D), lambda i: (0, 0)),
            resident(w_pg), resident(w_pp),
        ],
        out_specs=pl.BlockSpec((tm, D), lambda i: (i, 0)),
        out_shape=jax.ShapeDtypeStruct((T, D), F32),
        compiler_params=_cparams(("parallel",)),
        name="ple_final",
    )(h2, p2, g_ple, g_final, w_pg, w_pp)


def _lane_tables(S):
    assert QK_ROPE == IDX_ROPE == 32
    scale = (QK_NOPE + QK_ROPE) ** -0.5 * math.log2(math.e)
    a = np.zeros((N_TABLES, LANES), np.float64)
    bc = np.zeros_like(a)
    bs = np.zeros_like(a)
    a[TAB_Q, 0:QK_NOPE] = scale
    bc[TAB_Q, 64:96] = scale
    bs[TAB_Q + 1, 64:80] = -scale
    bs[TAB_Q + 1, 80:96] = scale
    for o in (0, IDX_DIM):
        bc[TAB_QI, o:o + 32] = 1.0
        a[TAB_QI, o + 32:o + 64] = 1.0
        bs[TAB_QI + 1, o:o + 32] = 1.0
    bc[TAB_KI, 0:32] = 1.0
    a[TAB_KI, 32:64] = 1.0
    bs[TAB_KI + 1, 0:16] = -1.0
    bs[TAB_KI + 2, 16:32] = 1.0
    bc[TAB_KR, 0:32] = 1.0
    bs[TAB_KR + 1, 0:16] = -1.0
    bs[TAB_KR + 2, 16:32] = 1.0
    inv = ROPE_THETA ** (-np.arange(0, QK_ROPE, 2, dtype=np.float64) / QK_ROPE)
    ang = np.arange(S, dtype=np.float64)[:, None] * np.tile(inv, LANES // inv.shape[0])[None, :]
    tables = a[:, None, :] + bc[:, None, :] * np.cos(ang)[None] + bs[:, None, :] * np.sin(ang)[None]
    return jnp.asarray(tables.astype(np.float32))


def _tiles(S):
    return dict(
        attn_q=256,
        index_keys=min(512, S),
        a_proj_rows=min(256, S),
        uv_gate_rows=min(1024, S), uv_gate_cols=1024,
        branch_out_rows=min(512, S),
        ffn_up_rows=min(1024, S), ffn_up_cols=512,
        ffn_down_rows=min(512, S), ffn_down_cols=1024,
        ple_rows=min(512, S),
    )


def kernel(x, p, g_mix, w_in, g_cq, g_ckv, w_uq, w_uk, w_uv, w_iq, w_a_proj, g_sgu, w_spatial,
           b_spatial, w_b_proj, w_o, g_ffn, w_gu, w_down, g_ple, w_ple_gate, w_ple_proj, g_final):
    B, S, D = x.shape
    T = B * S
    depth = w_in.shape[0]
    topk = min(TOPK_MAX, S // 4)
    t = _tiles(S)
    tables = _lane_tables(S)

    h = x.reshape(T, D)
    for i in range(depth):
        w_t = jnp.transpose(w_in[i])
        wuk_t = jnp.transpose(w_uk[i], (1, 2, 0)).reshape(N_HEADS * QK_NOPE, KV_LORA)
        wuv_t = jnp.transpose(w_uv[i], (1, 2, 0)).reshape(N_HEADS * V_HEAD, KV_LORA)
        n, q, k, v, qi, ki_lo, ki_hi, wt = _a_proj(h, g_mix[i][None], w_t, g_cq[i][None], g_ckv[i][None],
                                                    w_uq[i], w_iq[i], wuk_t, wuv_t,
                                                    tables, B=B, S=S, tm=t["a_proj_rows"])
        zg, wa_bf, wb_bf, wo_bf = _uv_gate_proj(n, w_t, w_a_proj[i], w_b_proj[i], w_o[i],
                                                tm=t["uv_gate_rows"], tn=t["uv_gate_cols"])
        bias = _dsa_index(ki_lo, ki_hi, qi, wt, B=B, S=S, TQ=t["attn_q"], KCH=t["index_keys"], topk=topk)
        o = _dsa_attn(q, k, v, bias, B=B, S=S, TQ=t["attn_q"])
        h1, n2 = _branch_out(o, zg, h, g_sgu[i][None], w_spatial[i], b_spatial[i],
                             g_ffn[i][None], wa_bf, wb_bf, wo_bf, B=B, S=S, tm=t["branch_out_rows"])
        act, w_down_bf = _ffn_up(n2, w_gu[i], w_down[i], tm=t["ffn_up_rows"], tn=t["ffn_up_cols"])
        h2, wpg_bf, wpp_bf = _ffn_down(act, w_down_bf, h1, w_ple_gate[i], w_ple_proj[i],
                                       tm=t["ffn_down_rows"], tn=t["ffn_down_cols"])
        assert depth == 1
        h = _ple_final(h2, p[i].reshape(T, -1), wpg_bf, wpp_bf, g_ple[i][None], g_final[None], tm=t["ple_rows"])
    return h.reshape(B, S, D)
```
